```python
import jax, jax.numpy as jnp
from jax import lax
import numpy as np

D_MODEL = 2048
BATCH = 8
SEQ = 8192
DEPTH = 1

GRID_W = 64
CTX_LEN = 256
N_MOD = 6
SSD_HEAD_DIM = 64
SSD_HEADS = 32
SSD_WIDTH = SSD_HEADS * SSD_HEAD_DIM
SSD_GROUPS = 4
SSD_HPG = SSD_HEADS // SSD_GROUPS
SSD_STATE = 128
SSD_BC = SSD_GROUPS * SSD_STATE
SSD_CHUNK = 128
CONV_W = 4
CONV_PAD_LEFT = 2
LRU_WIDTH = D_MODEL
LRU_HEADS = 16
LRU_HEAD_DIM = LRU_WIDTH // LRU_HEADS
LRU_C = 8.0
FFN_HIDDEN = ((8 * D_MODEL + 2) // 3 + 255) // 256 * 256
IN_SIZES = (SSD_WIDTH, SSD_WIDTH, SSD_BC, SSD_BC, 2 * SSD_HEADS, LRU_WIDTH, LRU_WIDTH)
IN_COLS = sum(IN_SIZES)
IN_SPLITS = tuple(int(s) for s in np.cumsum(IN_SIZES)[:-1])
EPS = 1e-6

kernel_name = "hybrid_ssd_rglru_prefix_dit_block"


def rmsnorm(t, g):
    tf = t.astype(jnp.float32)
    tf = tf * lax.rsqrt(jnp.mean(tf * tf, axis=-1, keepdims=True) + EPS)
    return (tf * g.astype(jnp.float32)).astype(t.dtype)


def modulate(h, shift, scale):
    return h * (1 + scale) + shift


def flip(t):
    return jnp.flip(t, axis=1)


def dwconv_centred(t, w, bias):
    L = t.shape[1]
    tp = jnp.pad(t, ((0, 0), (CONV_PAD_LEFT, CONV_W - 1 - CONV_PAD_LEFT), (0, 0)))
    out = bias
    for k in range(CONV_W):
        out = out + w[k] * tp[:, k:k + L]
    return out


def to_col_major(t, rows):
    b_, L, C = t.shape
    return t.reshape(b_, rows, GRID_W, C).transpose(0, 2, 1, 3).reshape(b_, L, C)


def to_row_major(t, rows):
    b_, L, C = t.shape
    return t.reshape(b_, GRID_W, rows, C).transpose(0, 2, 1, 3).reshape(b_, L, C)


def _ssd_chunks(x, dt, A, B):
    b_, L, G, E, P = x.shape
    nc = L // SSD_CHUNK
    x = x.reshape(b_, nc, SSD_CHUNK, G, E, P)
    dt = dt.reshape(b_, nc, SSD_CHUNK, G, E)
    B = B.reshape(b_, nc, SSD_CHUNK, G, SSD_STATE)
    xdt = x * dt[..., None]
    a_cum = jnp.cumsum(dt * A, axis=2)
    decay_to_end = jnp.exp(a_cum[:, :, -1:] - a_cum)
    states = jnp.einsum('bcqgn,bcqge,bcqgep->bcgepn', B, decay_to_end, xdt)
    chunk_decay = jnp.exp(a_cum[:, :, -1])
    return xdt, B, a_cum, states, chunk_decay


def _carry_states(states, chunk_decay, h0):
    def step(h, inp):
        s, d = inp
        return d[..., None, None] * h + s, h
    final, entering = lax.scan(step, h0, (jnp.moveaxis(states, 1, 0), jnp.moveaxis(chunk_decay, 1, 0)))
    return jnp.moveaxis(entering, 0, 1), final


def ssd_scan(x, dt, A, B, C, h0):
    b_, L, G, E, P = x.shape
    xdt, Bc, a_cum, states, chunk_decay = _ssd_chunks(x, dt, A, B)
    entering, final = _carry_states(states, chunk_decay, h0)
    Cc = C.reshape(Bc.shape)
    idx = jnp.arange(SSD_CHUNK)
    lower = (idx[:, None] >= idx[None, :])[None, None, :, :, None, None]
    seg = a_cum[:, :, :, None] - a_cum[:, :, None, :]
    Lmat = jnp.exp(jnp.where(lower, seg, -jnp.inf))
    CB = jnp.einsum('bcign,bcjgn->bcijg', Cc, Bc)
    y_diag = jnp.einsum('bcijg,bcijge,bcjgep->bcigep', CB, Lmat, xdt)
    y_off = jnp.einsum('bcign,bcige,bcgepn->bcigep', Cc, jnp.exp(a_cum), entering)
    return (y_diag + y_off).reshape(b_, L, G, E, P), final


def ssd_final_state(x, dt, A, B, h0):
    _, _, _, states, chunk_decay = _ssd_chunks(x, dt, A, B)
    _, final = _carry_states(states, chunk_decay, h0)
    return final


def ssd_inputs(xs, bs, cs, dt_raw, lp):
    f32 = jnp.float32
    xbc = jax.nn.silu(dwconv_centred(jnp.concatenate([xs, bs, cs], axis=-1), lp['ssd_conv_w'], lp['ssd_conv_b']))
    xs, bs, cs = jnp.split(xbc, [SSD_WIDTH, SSD_WIDTH + SSD_BC], axis=-1)
    b_, L = xs.shape[:2]
    xs = xs.reshape(b_, L, SSD_GROUPS, SSD_HPG, SSD_HEAD_DIM).astype(f32)
    bs = bs.reshape(b_, L, SSD_GROUPS, SSD_STATE).astype(f32)
    cs = cs.reshape(b_, L, SSD_GROUPS, SSD_STATE).astype(f32)
    dt = jax.nn.softplus(dt_raw.astype(f32).reshape(b_, L, 2, SSD_HEADS) + lp['ssd_dt_bias'].astype(f32))
    dt = dt.reshape(b_, L, 2, SSD_GROUPS, SSD_HPG)
    A = -jnp.exp(lp['ssd_a_log'].astype(f32)).reshape(2, SSD_GROUPS, SSD_HPG)
    return xs, bs, cs, dt, A


def lru_coeffs(xr, w_a, b_a, w_x, b_x, lam):
    b_, L, W = xr.shape
    xh = xr.reshape(b_, L, LRU_HEADS, LRU_HEAD_DIM)
    f32 = jnp.float32
    r = jax.nn.sigmoid(jnp.einsum('blhi,hij->blhj', xh, w_a.astype(f32)).reshape(b_, L, W) + b_a.astype(f32))
    i = jax.nn.sigmoid(jnp.einsum('blhi,hij->blhj', xh, w_x.astype(f32)).reshape(b_, L, W) + b_x.astype(f32))
    log_a = -LRU_C * r * jax.nn.softplus(-lam.astype(f32))
    a = jnp.exp(log_a)
    return a, jnp.sqrt(-jnp.expm1(2 * log_a)) * (i * xr)


def linear_scan(a, u, h0):
    u = u.at[:, 0].add(a[:, 0] * h0)
    def comb(lhs, rhs):
        return lhs[0] * rhs[0], rhs[0] * lhs[1] + rhs[1]
    _, h = lax.associative_scan(comb, (a, u), axis=1)
    return h


def lru_bidir(xr_seq, lp, h0_f, h0_b):
    a_f, u_f = lru_coeffs(xr_seq, lp['lru_w_a'][0], lp['lru_b_a'][0], lp['lru_w_x'][0], lp['lru_b_x'][0], lp['lru_lambda'][0])
    a_b, u_b = lru_coeffs(xr_seq, lp['lru_w_a'][1], lp['lru_b_a'][1], lp['lru_w_x'][1], lp['lru_b_x'][1], lp['lru_lambda'][1])
    h_f = linear_scan(a_f, u_f, h0_f)
    h_b_rev = linear_scan(flip(a_b), flip(u_b), h0_b)
    return h_f + flip(h_b_rev), h_f[:, -1], h_b_rev[:, -1]


def mixer_full(h, lp, init, rows):
    z, xs, bs, cs, dt_raw, xr, yr = jnp.split(h @ lp['w_in'], IN_SPLITS, axis=-1)
    b_, L = h.shape[:2]
    xsh, bsh, csh, dt, A = ssd_inputs(xs, bs, cs, dt_raw, lp)
    y_f, s_f = ssd_scan(xsh, dt[:, :, 0], A[0], bsh, csh, init[0])
    y_b, s_b = ssd_scan(flip(xsh), flip(dt[:, :, 1]), A[1], flip(bsh), flip(csh), init[1])
    d_skip = lp['ssd_d'].astype(jnp.float32).reshape(SSD_GROUPS, SSD_HPG)[..., None]
    y = (y_f + flip(y_b) + d_skip * xsh).reshape(b_, L, SSD_WIDTH)
    y = rmsnorm(y * jax.nn.silu(z.astype(jnp.float32)), lp['ssd_norm'])
    o_s = y.astype(h.dtype) @ lp['w_out_ssd']
    xr_seq = xr if rows is None else to_col_major(xr, rows)
    xr_seq = dwconv_centred(xr_seq, lp['lru_conv_w'], lp['lru_conv_b']).astype(jnp.float32)
    r_out, f_f, f_b = lru_bidir(xr_seq, lp, init[2], init[3])
    if rows is not None:
        r_out = to_row_major(r_out, rows)
    o_r = (r_out * jax.nn.gelu(yr.astype(jnp.float32))).astype(h.dtype) @ lp['w_out_lru']
    g_s, g_r = jnp.split(jax.nn.sigmoid(h @ lp['w_gate'] + lp['b_gate']), 2, axis=-1)
    out = (g_s * o_s + g_r * o_r) @ lp['w_o']
    return out, (s_f, s_b, f_f, f_b)


def mixer_states(hc, lp):
    _, xs, bs, cs, dt_raw, xr, _ = jnp.split(hc @ lp['w_in'], IN_SPLITS, axis=-1)
    b_ = hc.shape[0]
    xsh, bsh, _, dt, A = ssd_inputs(xs, bs, cs, dt_raw, lp)
    s0 = jnp.zeros((b_, SSD_GROUPS, SSD_HPG, SSD_HEAD_DIM, SSD_STATE), jnp.float32)
    s_f = ssd_final_state(xsh, dt[:, :, 0], A[0], bsh, s0)
    s_b = ssd_final_state(flip(xsh), flip(dt[:, :, 1]), A[1], flip(bsh), s0)
    xr_seq = dwconv_centred(xr, lp['lru_conv_w'], lp['lru_conv_b']).astype(jnp.float32)
    h0 = jnp.zeros((b_, LRU_WIDTH), jnp.float32)
    _, f_f, f_b = lru_bidir(xr_seq, lp, h0, h0)
    return (s_f, s_b, f_f, f_b)


def zero_states(b_):
    s0 = jnp.zeros((b_, SSD_GROUPS, SSD_HPG, SSD_HEAD_DIM, SSD_STATE), jnp.float32)
    h0 = jnp.zeros((b_, LRU_WIDTH), jnp.float32)
    return (s0, s0, h0, h0)


def swiglu(h, w13, w2):
    g, u = jnp.split(h @ w13, 2, axis=-1)
    return (jax.nn.silu(g) * u) @ w2


def _fwd_setup_inputs(seed: int = 0) -> dict:
    key = jax.random.key(seed)
    ks = jax.random.split(key, 30)
    D = D_MODEL
    f32 = jnp.float32

    def nrm(k, shape, fan_in, s=1.0):
        return jax.random.normal(k, shape, f32) * (s * fan_in ** -0.5)

    def small(k, shape, s=0.02):
        return jax.random.normal(k, shape, f32) * s

    dt0 = jnp.exp(jax.random.uniform(ks[10], (DEPTH, 2, SSD_HEADS), f32, np.log(1e-3), np.log(1e-1)))
    a_target = jax.random.uniform(ks[21], (DEPTH, 2, LRU_WIDTH), f32, 0.9, 0.999)
    sig_lam = a_target ** (1.0 / LRU_C)
    return {
        "x": jax.random.normal(ks[0], (BATCH, SEQ, D), f32),
        "c": jax.random.normal(ks[1], (BATCH, D), f32),
        "ctx": jax.random.normal(ks[2], (BATCH, CTX_LEN, D), f32),
        "c_ctx": jax.random.normal(ks[3], (D,), f32),
        "w_ada": nrm(ks[4], (DEPTH, D, N_MOD * D), D, 0.5),
        "b_ada": small(ks[5], (DEPTH, N_MOD * D)),
        "norm_mix": 1.0 + small(ks[6], (DEPTH, D)),
        "norm_ffn": 1.0 + small(ks[7], (DEPTH, D)),
        "w_in": nrm(ks[8], (DEPTH, D, IN_COLS), D),
        "ssd_conv_w": nrm(ks[9], (DEPTH, CONV_W, SSD_WIDTH + 2 * SSD_BC), CONV_W),
        "ssd_conv_b": small(ks[11], (DEPTH, SSD_WIDTH + 2 * SSD_BC)),
        "ssd_dt_bias": dt0 + jnp.log(-jnp.expm1(-dt0)),
        "ssd_a_log": jnp.log(jax.random.uniform(ks[12], (DEPTH, 2, SSD_HEADS), f32, 1.0, 16.0)),
        "ssd_d": 1.0 + small(ks[13], (DEPTH, SSD_HEADS)),
        "ssd_norm": 1.0 + small(ks[14], (DEPTH, SSD_WIDTH)),
        "w_out_ssd": nrm(ks[15], (DEPTH, SSD_WIDTH, D), SSD_WIDTH),
        "lru_conv_w": nrm(ks[16], (DEPTH, CONV_W, LRU_WIDTH), CONV_W),
        "lru_conv_b": small(ks[17], (DEPTH, LRU_WIDTH)),
        "lru_w_a": nrm(ks[18], (DEPTH, 2, LRU_HEADS, LRU_HEAD_DIM, LRU_HEAD_DIM), LRU_HEAD_DIM),
        "lru_b_a": small(ks[19], (DEPTH, 2, LRU_WIDTH)),
        "lru_w_x": nrm(ks[20], (DEPTH, 2, LRU_HEADS, LRU_HEAD_DIM, LRU_HEAD_DIM), LRU_HEAD_DIM),
        "lru_b_x": small(ks[22], (DEPTH, 2, LRU_WIDTH)),
        "lru_lambda": jnp.log(sig_lam) - jnp.log1p(-sig_lam),
        "w_out_lru": nrm(ks[23], (DEPTH, LRU_WIDTH, D), LRU_WIDTH),
        "w_gate": nrm(ks[24], (DEPTH, D, 2 * D), D),
        "b_gate": small(ks[25], (DEPTH, 2 * D)),
        "w_o": nrm(ks[26], (DEPTH, D, D), D),
        "ffn_w13": nrm(ks[27], (DEPTH, D, 2 * FFN_HIDDEN), D),
        "ffn_w2": nrm(ks[28], (DEPTH, FFN_HIDDEN, D), FFN_HIDDEN),
        "final_norm": 1.0 + small(ks[29], (D,)),
    }


def _fwd_reference(x, c, ctx, c_ctx, w_ada, b_ada, norm_mix, norm_ffn, w_in, ssd_conv_w, ssd_conv_b,
              ssd_dt_bias, ssd_a_log, ssd_d, ssd_norm, w_out_ssd, lru_conv_w, lru_conv_b,
              lru_w_a, lru_b_a, lru_w_x, lru_b_x, lru_lambda, w_out_lru, w_gate, b_gate, w_o,
              ffn_w13, ffn_w2, final_norm):
    rows = x.shape[1] // GRID_W
    for l in range(DEPTH):
        lp = dict(w_in=w_in[l], ssd_conv_w=ssd_conv_w[l], ssd_conv_b=ssd_conv_b[l],
                  ssd_dt_bias=ssd_dt_bias[l], ssd_a_log=ssd_a_log[l], ssd_d=ssd_d[l],
                  ssd_norm=ssd_norm[l], w_out_ssd=w_out_ssd[l], lru_conv_w=lru_conv_w[l],
                  lru_conv_b=lru_conv_b[l], lru_w_a=lru_w_a[l], lru_b_a=lru_b_a[l],
                  lru_w_x=lru_w_x[l], lru_b_x=lru_b_x[l], lru_lambda=lru_lambda[l],
                  w_out_lru=w_out_lru[l], w_gate=w_gate[l], b_gate=b_gate[l], w_o=w_o[l])
        mod = jax.nn.silu(c) @ w_ada[l] + b_ada[l]
        sh_m, sc_m, g_m, sh_f, sc_f, g_f = jnp.split(mod[:, None, :], N_MOD, axis=-1)
        mod_c = jax.nn.silu(c_ctx) @ w_ada[l] + b_ada[l]
        csh_m, csc_m, cg_m, csh_f, csc_f, cg_f = jnp.split(mod_c, N_MOD, axis=-1)
        hc = modulate(rmsnorm(ctx, norm_mix[l]), csh_m, csc_m)
        if l + 1 < DEPTH:
            mix_c, ctx_states = mixer_full(hc, lp, zero_states(ctx.shape[0]), None)
            ctx = ctx + cg_m * mix_c
            ctx = ctx + cg_f * swiglu(modulate(rmsnorm(ctx, norm_ffn[l]), csh_f, csc_f), ffn_w13[l], ffn_w2[l])
        else:
            ctx_states = mixer_states(hc, lp)
        h = modulate(rmsnorm(x, norm_mix[l]), sh_m, sc_m)
        mix, _ = mixer_full(h, lp, ctx_states, rows)
        x = x + g_m * mix
        h = modulate(rmsnorm(x, norm_ffn[l]), sh_f, sc_f)
        x = x + g_f * swiglu(h, ffn_w13[l], ffn_w2[l])
    return rmsnorm(x, final_norm)


import jax as _jax
import jax.numpy as _jnp

TWIN_FORMAT = 'train_step'
FWD_PARAMS = ['x', 'c', 'ctx', 'c_ctx', 'w_ada', 'b_ada', 'norm_mix', 'norm_ffn', 'w_in', 'ssd_conv_w', 'ssd_conv_b', 'ssd_dt_bias', 'ssd_a_log', 'ssd_d', 'ssd_norm', 'w_out_ssd', 'lru_conv_w', 'lru_conv_b', 'lru_w_a', 'lru_b_a', 'lru_w_x', 'lru_b_x', 'lru_lambda', 'w_out_lru', 'w_gate', 'b_gate', 'w_o', 'ffn_w13', 'ffn_w2', 'final_norm']
TWIN_WEIGHTS = ['c_ctx', 'w_ada', 'b_ada', 'norm_mix', 'norm_ffn', 'w_in', 'ssd_conv_w', 'ssd_conv_b', 'ssd_dt_bias', 'ssd_a_log', 'ssd_d', 'ssd_norm', 'w_out_ssd', 'lru_conv_w', 'lru_conv_b', 'lru_w_a', 'lru_b_a', 'lru_w_x', 'lru_b_x', 'lru_lambda', 'w_out_lru', 'w_gate', 'b_gate', 'w_o', 'ffn_w13', 'ffn_w2', 'final_norm']
TWIN_DIFF_INPUT = 'x'
TWIN_INPUTS = ['x', 'c', 'ctx', 'c_ctx', 'w_ada', 'b_ada', 'norm_mix', 'norm_ffn', 'w_in', 'ssd_conv_w', 'ssd_conv_b', 'ssd_dt_bias', 'ssd_a_log', 'ssd_d', 'ssd_norm', 'w_out_ssd', 'lru_conv_w', 'lru_conv_b', 'lru_w_a', 'lru_b_a', 'lru_w_x', 'lru_b_x', 'lru_lambda', 'w_out_lru', 'w_gate', 'b_gate', 'w_o', 'ffn_w13', 'ffn_w2', 'final_norm', 'loss_target', 'm_c_ctx', 'm_w_ada', 'm_b_ada', 'm_norm_mix', 'm_norm_ffn', 'm_w_in', 'm_ssd_conv_w', 'm_ssd_conv_b', 'm_ssd_dt_bias', 'm_ssd_a_log', 'm_ssd_d', 'm_ssd_norm', 'm_w_out_ssd', 'm_lru_conv_w', 'm_lru_conv_b', 'm_lru_w_a', 'm_lru_b_a', 'm_lru_w_x', 'm_lru_b_x', 'm_lru_lambda', 'm_w_out_lru', 'm_w_gate', 'm_b_gate', 'm_w_o', 'm_ffn_w13', 'm_ffn_w2', 'm_final_norm', 'v_c_ctx', 'v_w_ada', 'v_b_ada', 'v_norm_mix', 'v_norm_ffn', 'v_w_in', 'v_ssd_conv_w', 'v_ssd_conv_b', 'v_ssd_dt_bias', 'v_ssd_a_log', 'v_ssd_d', 'v_ssd_norm', 'v_w_out_ssd', 'v_lru_conv_w', 'v_lru_conv_b', 'v_lru_w_a', 'v_lru_b_a', 'v_lru_w_x', 'v_lru_b_x', 'v_lru_lambda', 'v_w_out_lru', 'v_w_gate', 'v_b_gate', 'v_w_o', 'v_ffn_w13', 'v_ffn_w2', 'v_final_norm']
TWIN_OUTPUTS = ['loss', 'grad_x', 'grad_c_ctx', 'grad_w_ada', 'grad_b_ada', 'grad_norm_mix', 'grad_norm_ffn', 'grad_w_in', 'grad_ssd_conv_w', 'grad_ssd_conv_b', 'grad_ssd_dt_bias', 'grad_ssd_a_log', 'grad_ssd_d', 'grad_ssd_norm', 'grad_w_out_ssd', 'grad_lru_conv_w', 'grad_lru_conv_b', 'grad_lru_w_a', 'grad_lru_b_a', 'grad_lru_w_x', 'grad_lru_b_x', 'grad_lru_lambda', 'grad_w_out_lru', 'grad_w_gate', 'grad_b_gate', 'grad_w_o', 'grad_ffn_w13', 'grad_ffn_w2', 'grad_final_norm', 'delta_c_ctx', 'delta_w_ada', 'delta_b_ada', 'delta_norm_mix', 'delta_norm_ffn', 'delta_w_in', 'delta_ssd_conv_w', 'delta_ssd_conv_b', 'delta_ssd_dt_bias', 'delta_ssd_a_log', 'delta_ssd_d', 'delta_ssd_norm', 'delta_w_out_ssd', 'delta_lru_conv_w', 'delta_lru_conv_b', 'delta_lru_w_a', 'delta_lru_b_a', 'delta_lru_w_x', 'delta_lru_b_x', 'delta_lru_lambda', 'delta_w_out_lru', 'delta_w_gate', 'delta_b_gate', 'delta_w_o', 'delta_ffn_w13', 'delta_ffn_w2', 'delta_final_norm', 'new_m_c_ctx', 'new_m_w_ada', 'new_m_b_ada', 'new_m_norm_mix', 'new_m_norm_ffn', 'new_m_w_in', 'new_m_ssd_conv_w', 'new_m_ssd_conv_b', 'new_m_ssd_dt_bias', 'new_m_ssd_a_log', 'new_m_ssd_d', 'new_m_ssd_norm', 'new_m_w_out_ssd', 'new_m_lru_conv_w', 'new_m_lru_conv_b', 'new_m_lru_w_a', 'new_m_lru_b_a', 'new_m_lru_w_x', 'new_m_lru_b_x', 'new_m_lru_lambda', 'new_m_w_out_lru', 'new_m_w_gate', 'new_m_b_gate', 'new_m_w_o', 'new_m_ffn_w13', 'new_m_ffn_w2', 'new_m_final_norm', 'new_v_c_ctx', 'new_v_w_ada', 'new_v_b_ada', 'new_v_norm_mix', 'new_v_norm_ffn', 'new_v_w_in', 'new_v_ssd_conv_w', 'new_v_ssd_conv_b', 'new_v_ssd_dt_bias', 'new_v_ssd_a_log', 'new_v_ssd_d', 'new_v_ssd_norm', 'new_v_w_out_ssd', 'new_v_lru_conv_w', 'new_v_lru_conv_b', 'new_v_lru_w_a', 'new_v_lru_b_a', 'new_v_lru_w_x', 'new_v_lru_b_x', 'new_v_lru_lambda', 'new_v_w_out_lru', 'new_v_w_gate', 'new_v_b_gate', 'new_v_w_o', 'new_v_ffn_w13', 'new_v_ffn_w2', 'new_v_final_norm']
TWIN_LEAF_KINDS = {'loss': 'loss', 'grad_x': 'grad_x', 'grad_c_ctx': 'grad_w', 'grad_w_ada': 'grad_w', 'grad_b_ada': 'grad_w', 'grad_norm_mix': 'grad_w', 'grad_norm_ffn': 'grad_w', 'grad_w_in': 'grad_w', 'grad_ssd_conv_w': 'grad_w', 'grad_ssd_conv_b': 'grad_w', 'grad_ssd_dt_bias': 'grad_w', 'grad_ssd_a_log': 'grad_w', 'grad_ssd_d': 'grad_w', 'grad_ssd_norm': 'grad_w', 'grad_w_out_ssd': 'grad_w', 'grad_lru_conv_w': 'grad_w', 'grad_lru_conv_b': 'grad_w', 'grad_lru_w_a': 'grad_w', 'grad_lru_b_a': 'grad_w', 'grad_lru_w_x': 'grad_w', 'grad_lru_b_x': 'grad_w', 'grad_lru_lambda': 'grad_w', 'grad_w_out_lru': 'grad_w', 'grad_w_gate': 'grad_w', 'grad_b_gate': 'grad_w', 'grad_w_o': 'grad_w', 'grad_ffn_w13': 'grad_w', 'grad_ffn_w2': 'grad_w', 'grad_final_norm': 'grad_w', 'delta_c_ctx': 'delta_w', 'delta_w_ada': 'delta_w', 'delta_b_ada': 'delta_w', 'delta_norm_mix': 'delta_w', 'delta_norm_ffn': 'delta_w', 'delta_w_in': 'delta_w', 'delta_ssd_conv_w': 'delta_w', 'delta_ssd_conv_b': 'delta_w', 'delta_ssd_dt_bias': 'delta_w', 'delta_ssd_a_log': 'delta_w', 'delta_ssd_d': 'delta_w', 'delta_ssd_norm': 'delta_w', 'delta_w_out_ssd': 'delta_w', 'delta_lru_conv_w': 'delta_w', 'delta_lru_conv_b': 'delta_w', 'delta_lru_w_a': 'delta_w', 'delta_lru_b_a': 'delta_w', 'delta_lru_w_x': 'delta_w', 'delta_lru_b_x': 'delta_w', 'delta_lru_lambda': 'delta_w', 'delta_w_out_lru': 'delta_w', 'delta_w_gate': 'delta_w', 'delta_b_gate': 'delta_w', 'delta_w_o': 'delta_w', 'delta_ffn_w13': 'delta_w', 'delta_ffn_w2': 'delta_w', 'delta_final_norm': 'delta_w', 'new_m_c_ctx': 'new_m', 'new_m_w_ada': 'new_m', 'new_m_b_ada': 'new_m', 'new_m_norm_mix': 'new_m', 'new_m_norm_ffn': 'new_m', 'new_m_w_in': 'new_m', 'new_m_ssd_conv_w': 'new_m', 'new_m_ssd_conv_b': 'new_m', 'new_m_ssd_dt_bias': 'new_m', 'new_m_ssd_a_log': 'new_m', 'new_m_ssd_d': 'new_m', 'new_m_ssd_norm': 'new_m', 'new_m_w_out_ssd': 'new_m', 'new_m_lru_conv_w': 'new_m', 'new_m_lru_conv_b': 'new_m', 'new_m_lru_w_a': 'new_m', 'new_m_lru_b_a': 'new_m', 'new_m_lru_w_x': 'new_m', 'new_m_lru_b_x': 'new_m', 'new_m_lru_lambda': 'new_m', 'new_m_w_out_lru': 'new_m', 'new_m_w_gate': 'new_m', 'new_m_b_gate': 'new_m', 'new_m_w_o': 'new_m', 'new_m_ffn_w13': 'new_m', 'new_m_ffn_w2': 'new_m', 'new_m_final_norm': 'new_m', 'new_v_c_ctx': 'new_v', 'new_v_w_ada': 'new_v', 'new_v_b_ada': 'new_v', 'new_v_norm_mix': 'new_v', 'new_v_norm_ffn': 'new_v', 'new_v_w_in': 'new_v', 'new_v_ssd_conv_w': 'new_v', 'new_v_ssd_conv_b': 'new_v', 'new_v_ssd_dt_bias': 'new_v', 'new_v_ssd_a_log': 'new_v', 'new_v_ssd_d': 'new_v', 'new_v_ssd_norm': 'new_v', 'new_v_w_out_ssd': 'new_v', 'new_v_lru_conv_w': 'new_v', 'new_v_lru_conv_b': 'new_v', 'new_v_lru_w_a': 'new_v', 'new_v_lru_b_a': 'new_v', 'new_v_lru_w_x': 'new_v', 'new_v_lru_b_x': 'new_v', 'new_v_lru_lambda': 'new_v', 'new_v_w_out_lru': 'new_v', 'new_v_w_gate': 'new_v', 'new_v_b_gate': 'new_v', 'new_v_w_o': 'new_v', 'new_v_ffn_w13': 'new_v', 'new_v_ffn_w2': 'new_v', 'new_v_final_norm': 'new_v'}


def _forward(args):
    return _fwd_reference(*[args[k] for k in FWD_PARAMS])


def _output_shape():
    def fwd():
        inp = _fwd_setup_inputs(0)
        return _fwd_reference(*[inp[k] for k in FWD_PARAMS])
    out = _jax.eval_shape(fwd)
    return out.shape, out.dtype

N_MICROBATCH = 1
ADAM_LR = 0.001
ADAM_B1 = 0.9
ADAM_B2 = 0.999
ADAM_EPS = 1e-08
ADAM_WD = 0.01
ADAM_STEP = 10
PER_EXAMPLE_BATCH_AXIS = {'x': 0, 'c': 0, 'ctx': 0, 'loss_target': 0}
SHARED_INPUTS = []
_WEIGHT_DTYPES = {'c_ctx': _jnp.float32, 'w_ada': _jnp.float32, 'b_ada': _jnp.float32, 'norm_mix': _jnp.float32, 'norm_ffn': _jnp.float32, 'w_in': _jnp.float32, 'ssd_conv_w': _jnp.float32, 'ssd_conv_b': _jnp.float32, 'ssd_dt_bias': _jnp.float32, 'ssd_a_log': _jnp.float32, 'ssd_d': _jnp.float32, 'ssd_norm': _jnp.float32, 'w_out_ssd': _jnp.float32, 'lru_conv_w': _jnp.float32, 'lru_conv_b': _jnp.float32, 'lru_w_a': _jnp.float32, 'lru_b_a': _jnp.float32, 'lru_w_x': _jnp.float32, 'lru_b_x': _jnp.float32, 'lru_lambda': _jnp.float32, 'w_out_lru': _jnp.float32, 'w_gate': _jnp.float32, 'b_gate': _jnp.float32, 'w_o': _jnp.float32, 'ffn_w13': _jnp.float32, 'ffn_w2': _jnp.float32, 'final_norm': _jnp.float32}
MOMENT_SCALE = {'c_ctx': 1.369670e-02, 'w_ada': 8.290800e-02, 'b_ada': 1.467695e-01, 'norm_mix': 6.606334e-02, 'norm_ffn': 3.385954e-02, 'w_in': 3.989637e-02, 'ssd_conv_w': 1.761464e-02, 'ssd_conv_b': 2.242102e-02, 'ssd_dt_bias': 3.005964e-02, 'ssd_a_log': 7.198197e-02, 'ssd_d': 8.034696e-02, 'ssd_norm': 2.017816e-02, 'w_out_ssd': 1.958175e-02, 'lru_conv_w': 6.018489e-02, 'lru_conv_b': 1.777538e-01, 'lru_w_a': 3.353825e-03, 'lru_b_a': 4.956211e-03, 'lru_w_x': 6.796482e-03, 'lru_b_x': 1.195432e-02, 'lru_lambda': 1.267696e-02, 'w_out_lru': 6.052865e-02, 'w_gate': 1.749911e-02, 'b_gate': 1.680045e-02, 'w_o': 6.150465e-02, 'ffn_w13': 1.515644e-02, 'ffn_w2': 2.471311e-02, 'final_norm': 3.209160e+01}


def _to_microbatches(a, axis):
    t = _jnp.moveaxis(a, axis, 0)
    t = t.reshape((N_MICROBATCH, t.shape[0] // N_MICROBATCH) + t.shape[1:])
    return _jnp.moveaxis(t, 1, axis + 1)


def setup_inputs(seed: int = 0) -> dict:
    inp = _fwd_setup_inputs(seed)
    key = _jax.random.fold_in(_jax.random.key(seed), 7919)
    shape, _ = _output_shape()
    out = dict(inp)
    out["loss_target"] = _jax.random.normal(_jax.random.fold_in(key, 0), shape, _jnp.float32)
    for i, name in enumerate(TWIN_WEIGHTS):
        w = inp[name].astype(_jnp.float32)
        if MOMENT_SCALE is None:
            s = _jnp.sqrt(_jnp.mean(_jnp.square(w)) + 1e-30)
        else:
            s = MOMENT_SCALE[name]
        km, kv = _jax.random.split(_jax.random.fold_in(key, i + 1))
        out[name] = w
        out["m_" + name] = s * _jax.random.normal(km, w.shape, _jnp.float32)
        out["v_" + name] = (s * s) * _jax.random.uniform(kv, w.shape, _jnp.float32, 0.5, 1.5)
    if N_MICROBATCH > 1:
        for name, axis in PER_EXAMPLE_BATCH_AXIS.items():
            out[name] = _to_microbatches(out[name], axis)
    return {'x': out['x'], 'c': out['c'], 'ctx': out['ctx'], 'c_ctx': out['c_ctx'], 'w_ada': out['w_ada'], 'b_ada': out['b_ada'], 'norm_mix': out['norm_mix'], 'norm_ffn': out['norm_ffn'], 'w_in': out['w_in'], 'ssd_conv_w': out['ssd_conv_w'], 'ssd_conv_b': out['ssd_conv_b'], 'ssd_dt_bias': out['ssd_dt_bias'], 'ssd_a_log': out['ssd_a_log'], 'ssd_d': out['ssd_d'], 'ssd_norm': out['ssd_norm'], 'w_out_ssd': out['w_out_ssd'], 'lru_conv_w': out['lru_conv_w'], 'lru_conv_b': out['lru_conv_b'], 'lru_w_a': out['lru_w_a'], 'lru_b_a': out['lru_b_a'], 'lru_w_x': out['lru_w_x'], 'lru_b_x': out['lru_b_x'], 'lru_lambda': out['lru_lambda'], 'w_out_lru': out['w_out_lru'], 'w_gate': out['w_gate'], 'b_gate': out['b_gate'], 'w_o': out['w_o'], 'ffn_w13': out['ffn_w13'], 'ffn_w2': out['ffn_w2'], 'final_norm': out['final_norm'], 'loss_target': out['loss_target'], 'm_c_ctx': out['m_c_ctx'], 'm_w_ada': out['m_w_ada'], 'm_b_ada': out['m_b_ada'], 'm_norm_mix': out['m_norm_mix'], 'm_norm_ffn': out['m_norm_ffn'], 'm_w_in': out['m_w_in'], 'm_ssd_conv_w': out['m_ssd_conv_w'], 'm_ssd_conv_b': out['m_ssd_conv_b'], 'm_ssd_dt_bias': out['m_ssd_dt_bias'], 'm_ssd_a_log': out['m_ssd_a_log'], 'm_ssd_d': out['m_ssd_d'], 'm_ssd_norm': out['m_ssd_norm'], 'm_w_out_ssd': out['m_w_out_ssd'], 'm_lru_conv_w': out['m_lru_conv_w'], 'm_lru_conv_b': out['m_lru_conv_b'], 'm_lru_w_a': out['m_lru_w_a'], 'm_lru_b_a': out['m_lru_b_a'], 'm_lru_w_x': out['m_lru_w_x'], 'm_lru_b_x': out['m_lru_b_x'], 'm_lru_lambda': out['m_lru_lambda'], 'm_w_out_lru': out['m_w_out_lru'], 'm_w_gate': out['m_w_gate'], 'm_b_gate': out['m_b_gate'], 'm_w_o': out['m_w_o'], 'm_ffn_w13': out['m_ffn_w13'], 'm_ffn_w2': out['m_ffn_w2'], 'm_final_norm': out['m_final_norm'], 'v_c_ctx': out['v_c_ctx'], 'v_w_ada': out['v_w_ada'], 'v_b_ada': out['v_b_ada'], 'v_norm_mix': out['v_norm_mix'], 'v_norm_ffn': out['v_norm_ffn'], 'v_w_in': out['v_w_in'], 'v_ssd_conv_w': out['v_ssd_conv_w'], 'v_ssd_conv_b': out['v_ssd_conv_b'], 'v_ssd_dt_bias': out['v_ssd_dt_bias'], 'v_ssd_a_log': out['v_ssd_a_log'], 'v_ssd_d': out['v_ssd_d'], 'v_ssd_norm': out['v_ssd_norm'], 'v_w_out_ssd': out['v_w_out_ssd'], 'v_lru_conv_w': out['v_lru_conv_w'], 'v_lru_conv_b': out['v_lru_conv_b'], 'v_lru_w_a': out['v_lru_w_a'], 'v_lru_b_a': out['v_lru_b_a'], 'v_lru_w_x': out['v_lru_w_x'], 'v_lru_b_x': out['v_lru_b_x'], 'v_lru_lambda': out['v_lru_lambda'], 'v_w_out_lru': out['v_w_out_lru'], 'v_w_gate': out['v_w_gate'], 'v_b_gate': out['v_b_gate'], 'v_w_o': out['v_w_o'], 'v_ffn_w13': out['v_ffn_w13'], 'v_ffn_w2': out['v_ffn_w2'], 'v_final_norm': out['v_final_norm']}


def _loss(weights, diff, rest, loss_target):
    with _jax.named_scope("forward"):
        args = {**rest, TWIN_DIFF_INPUT: diff, **{k: w.astype(_WEIGHT_DTYPES[k]) for k, w in weights.items()}}
        y = _forward(args)
    with _jax.named_scope("loss_head"):
        err = _jnp.square(y.astype(_jnp.float32) - loss_target)
        return 0.5 * _jnp.sum(_jnp.mean(err, axis=-1)) if err.ndim else 0.5 * err


def _adamw(w, g, m, v):
    m = ADAM_B1 * m + (1.0 - ADAM_B1) * g
    v = ADAM_B2 * v + (1.0 - ADAM_B2) * _jnp.square(g)
    m_hat = m / (1.0 - ADAM_B1 ** ADAM_STEP)
    v_hat = v / (1.0 - ADAM_B2 ** ADAM_STEP)
    delta = -ADAM_LR * (m_hat / (_jnp.sqrt(v_hat) + ADAM_EPS) + ADAM_WD * w)
    return delta, m, v


def reference(x, c, ctx, c_ctx, w_ada, b_ada, norm_mix, norm_ffn, w_in, ssd_conv_w, ssd_conv_b, ssd_dt_bias, ssd_a_log, ssd_d, ssd_norm, w_out_ssd, lru_conv_w, lru_conv_b, lru_w_a, lru_b_a, lru_w_x, lru_b_x, lru_lambda, w_out_lru, w_gate, b_gate, w_o, ffn_w13, ffn_w2, final_norm, loss_target, m_c_ctx, m_w_ada, m_b_ada, m_norm_mix, m_norm_ffn, m_w_in, m_ssd_conv_w, m_ssd_conv_b, m_ssd_dt_bias, m_ssd_a_log, m_ssd_d, m_ssd_norm, m_w_out_ssd, m_lru_conv_w, m_lru_conv_b, m_lru_w_a, m_lru_b_a, m_lru_w_x, m_lru_b_x, m_lru_lambda, m_w_out_lru, m_w_gate, m_b_gate, m_w_o, m_ffn_w13, m_ffn_w2, m_final_norm, v_c_ctx, v_w_ada, v_b_ada, v_norm_mix, v_norm_ffn, v_w_in, v_ssd_conv_w, v_ssd_conv_b, v_ssd_dt_bias, v_ssd_a_log, v_ssd_d, v_ssd_norm, v_w_out_ssd, v_lru_conv_w, v_lru_conv_b, v_lru_w_a, v_lru_b_a, v_lru_w_x, v_lru_b_x, v_lru_lambda, v_w_out_lru, v_w_gate, v_b_gate, v_w_o, v_ffn_w13, v_ffn_w2, v_final_norm):
    given = dict(x=x, c=c, ctx=ctx, c_ctx=c_ctx, w_ada=w_ada, b_ada=b_ada, norm_mix=norm_mix, norm_ffn=norm_ffn, w_in=w_in, ssd_conv_w=ssd_conv_w, ssd_conv_b=ssd_conv_b, ssd_dt_bias=ssd_dt_bias, ssd_a_log=ssd_a_log, ssd_d=ssd_d, ssd_norm=ssd_norm, w_out_ssd=w_out_ssd, lru_conv_w=lru_conv_w, lru_conv_b=lru_conv_b, lru_w_a=lru_w_a, lru_b_a=lru_b_a, lru_w_x=lru_w_x, lru_b_x=lru_b_x, lru_lambda=lru_lambda, w_out_lru=w_out_lru, w_gate=w_gate, b_gate=b_gate, w_o=w_o, ffn_w13=ffn_w13, ffn_w2=ffn_w2, final_norm=final_norm, loss_target=loss_target, m_c_ctx=m_c_ctx, m_w_ada=m_w_ada, m_b_ada=m_b_ada, m_norm_mix=m_norm_mix, m_norm_ffn=m_norm_ffn, m_w_in=m_w_in, m_ssd_conv_w=m_ssd_conv_w, m_ssd_conv_b=m_ssd_conv_b, m_ssd_dt_bias=m_ssd_dt_bias, m_ssd_a_log=m_ssd_a_log, m_ssd_d=m_ssd_d, m_ssd_norm=m_ssd_norm, m_w_out_ssd=m_w_out_ssd, m_lru_conv_w=m_lru_conv_w, m_lru_conv_b=m_lru_conv_b, m_lru_w_a=m_lru_w_a, m_lru_b_a=m_lru_b_a, m_lru_w_x=m_lru_w_x, m_lru_b_x=m_lru_b_x, m_lru_lambda=m_lru_lambda, m_w_out_lru=m_w_out_lru, m_w_gate=m_w_gate, m_b_gate=m_b_gate, m_w_o=m_w_o, m_ffn_w13=m_ffn_w13, m_ffn_w2=m_ffn_w2, m_final_norm=m_final_norm, v_c_ctx=v_c_ctx, v_w_ada=v_w_ada, v_b_ada=v_b_ada, v_norm_mix=v_norm_mix, v_norm_ffn=v_norm_ffn, v_w_in=v_w_in, v_ssd_conv_w=v_ssd_conv_w, v_ssd_conv_b=v_ssd_conv_b, v_ssd_dt_bias=v_ssd_dt_bias, v_ssd_a_log=v_ssd_a_log, v_ssd_d=v_ssd_d, v_ssd_norm=v_ssd_norm, v_w_out_ssd=v_w_out_ssd, v_lru_conv_w=v_lru_conv_w, v_lru_conv_b=v_lru_conv_b, v_lru_w_a=v_lru_w_a, v_lru_b_a=v_lru_b_a, v_lru_w_x=v_lru_w_x, v_lru_b_x=v_lru_b_x, v_lru_lambda=v_lru_lambda, v_w_out_lru=v_w_out_lru, v_w_gate=v_w_gate, v_b_gate=v_b_gate, v_w_o=v_w_o, v_ffn_w13=v_ffn_w13, v_ffn_w2=v_ffn_w2, v_final_norm=v_final_norm)
    weights = {n: given[n] for n in TWIN_WEIGHTS}
    shared = {n: given[n] for n in SHARED_INPUTS}
    per_example = {n: given[n] for n in ['x', 'c', 'ctx']}
    grad_fn = _jax.value_and_grad(_loss, argnums=(0, 1))

    def one_microbatch(ex, loss_target):
        ex = dict(ex)
        diff = ex.pop(TWIN_DIFF_INPUT)
        return grad_fn(weights, diff, {**shared, **ex}, loss_target)

    if N_MICROBATCH == 1:
        loss, (grad_w, grad_x) = one_microbatch(per_example, given["loss_target"])
    else:
        def body(carry, xs):
            loss_sum, grad_sum = carry
            l_k, (gw_k, gx_k) = one_microbatch(xs[0], xs[1])
            with _jax.named_scope("update"):
                return (loss_sum + l_k, _jax.tree.map(_jnp.add, grad_sum, gw_k)), gx_k

        init = (_jnp.zeros((), _jnp.float32), _jax.tree.map(_jnp.zeros_like, weights))
        (loss, grad_w), grad_x = _jax.lax.scan(body, init, (per_example, given["loss_target"]))
    with _jax.named_scope("update"):
        delta_w, new_m, new_v = {}, {}, {}
        for n in TWIN_WEIGHTS:
            delta_w[n], new_m[n], new_v[n] = _adamw(weights[n], grad_w[n], given["m_" + n], given["v_" + n])
    return (loss, grad_x, *[grad_w[n] for n in TWIN_WEIGHTS], *[delta_w[n] for n in TWIN_WEIGHTS],
            *[new_m[n] for n in TWIN_WEIGHTS], *[new_v[n] for n in TWIN_WEIGHTS])
```

```python
import functools

import jax
import jax.numpy as jnp
from jax import lax
from jax.experimental import pallas as pl
from jax.experimental.pallas import tpu as pltpu

F32 = jnp.float32
BF16 = jnp.bfloat16
MESH = pl.DeviceIdType.MESH

V7X_VMEM_LIMIT_BYTES = 56 * 1024 * 1024
LANES = 128
SUBLANES = 8

HEAD_DIM = 64
SSD_STATE = 128
SSD_GROUPS = 4
SSD_CHUNK = 128
GRID_W = 64
LRU_HEAD = 128
LRU_C = 8.0
EPS = 1e-6
NEG = -1e30

ADAM_LR = 0.001
ADAM_B1 = 0.9
ADAM_B2 = 0.999
ADAM_EPS = 1e-08
ADAM_WD = 0.01
ADAM_STEP = 10


def _cparams(sem=None, **kw):
    if sem is not None:
        kw["dimension_semantics"] = sem
    return pltpu.CompilerParams(vmem_limit_bytes=V7X_VMEM_LIMIT_BYTES, **kw)


_DN = {"nn": (((1,), (0,)), ((), ())), "nt": (((1,), (1,)), ((), ())), "tn": (((0,), (0,)), ((), ()))}


def _raw_dot(a, b, form, precision=None):
    return lax.dot_general(a, b, _DN[form], precision=precision, preferred_element_type=F32)


def _dot_bwd_forms(form):
    return {"nn": (("g", "b", "nt"), ("a", "g", "tn")),
            "nt": (("g", "b", "nn"), ("g", "a", "tn")),
            "tn": (("b", "g", "nt"), ("a", "g", "nn"))}[form]


def _make_dot(rounding):
    @functools.partial(jax.custom_vjp, nondiff_argnums=(2,))
    def dot(a, b, form):
        if rounding is None:
            return _raw_dot(a, b, form, precision=lax.Precision.HIGHEST)
        return _raw_dot(a.astype(rounding), b.astype(rounding), form)

    def fwd(a, b, form):
        return dot(a, b, form), (a, b)

    def bwd(form, res, g):
        a, b = res
        ops = {"a": a, "b": b, "g": g}
        (l1, r1, f1), (l2, r2, f2) = _dot_bwd_forms(form)
        return dot(ops[l1], ops[r1], f1).astype(a.dtype), dot(ops[l2], ops[r2], f2).astype(b.dtype)

    dot.defvjp(fwd, bwd)
    return dot


bdot = _make_dot(BF16)
hdot = _make_dot(None)


def _sigmoid(v):
    return 1.0 / (1.0 + jnp.exp(-v))


def _silu(v):
    return v * _sigmoid(v)


def _softplus(v):
    return jnp.maximum(v, 0.0) + jnp.log1p(jnp.exp(-jnp.abs(v)))


def _gelu_tanh(v):
    return 0.5 * v * (1.0 + jnp.tanh(0.7978845608028654 * (v + 0.044715 * v * v * v)))


def _neg_expm1(v):
    poly = -v * (1.0 + v * (0.5 + v * (1.0 / 6.0 + v * (1.0 / 24.0 + v * (1.0 / 120.0)))))
    return jnp.where(v > -0.05, poly, 1.0 - jnp.exp(v))


def _rms(t, gain):
    return t * lax.rsqrt(jnp.mean(t * t, axis=-1, keepdims=True) + EPS) * gain


def _pick(dim, want):
    t = min(dim, want)
    while dim % t:
        t //= 2
    assert t >= 1
    return t


def matmul(a, b, form, out_dtype, name, tm=512, tn=512, tk=1024, add=None):
    if form == "nn":
        (m, k), (k2, n) = a.shape, b.shape
    elif form == "nt":
        (m, k), (n, k2) = a.shape, b.shape
    else:
        (k, m), (k2, n) = a.shape, b.shape
    assert k == k2, (a.shape, b.shape, form)
    tm, tn, tk = _pick(m, tm), _pick(n, tn), _pick(k, tk)
    nk = k // tk

    def body(*refs):
        if add is None:
            a_ref, b_ref, o_ref, acc = refs
        else:
            a_ref, b_ref, c_ref, o_ref, acc = refs
        kk = pl.program_id(2)

        @pl.when(kk == 0)
        def _():
            acc[...] = jnp.zeros_like(acc) if add is None else c_ref[...]

        acc[...] += _raw_dot(a_ref[...], b_ref[...], form)

        @pl.when(kk == nk - 1)
        def _():
            o_ref[...] = acc[...].astype(o_ref.dtype)

    a_spec = pl.BlockSpec((tk, tm), lambda i, j, kk: (kk, i)) if form == "tn" else pl.BlockSpec((tm, tk), lambda i, j, kk: (i, kk))
    b_spec = pl.BlockSpec((tn, tk), lambda i, j, kk: (j, kk)) if form == "nt" else pl.BlockSpec((tk, tn), lambda i, j, kk: (kk, j))
    in_specs = [a_spec, b_spec]
    args = [a, b]
    if add is not None:
        in_specs.append(pl.BlockSpec((tm, tn), lambda i, j, kk: (i, j)))
        args.append(add)
    return pl.pallas_call(
        body, name=name, grid=(m // tm, n // tn, nk), in_specs=in_specs,
        out_specs=pl.BlockSpec((tm, tn), lambda i, j, kk: (i, j)),
        out_shape=jax.ShapeDtypeStruct((m, n), out_dtype),
        scratch_shapes=[pltpu.VMEM((tm, tn), F32)],
        compiler_params=_cparams(("parallel", "parallel", "arbitrary")),
    )(*args)


class Cols:
    def __init__(self, arr, w, j):
        assert (j + 1) * w <= arr.shape[1]
        self.arr, self.w, self.j = arr, w, j


def _row_width(x, ncol):
    return x.w if isinstance(x, Cols) else x.shape[1] // ncol


def _row_spec(x, tb, ncol):
    if isinstance(x, Cols):
        j0 = x.j
        return x.arr, pl.BlockSpec((tb, x.w), lambda j, i: (i, j0 + j))
    return x, pl.BlockSpec((tb, x.shape[1] // ncol), lambda j, i: (i, j))


def _par_spec(shape, ncol):
    if ncol == 1:
        return pl.BlockSpec(shape, lambda j, i, nd=len(shape): (0,) * nd)
    assert len(shape) == 3 and shape[0] == ncol, shape
    return pl.BlockSpec((1,) + tuple(shape[1:]), lambda j, i: (j, 0, 0))


def _par_value(ref, ncol):
    return ref[...] if ncol == 1 else ref[0]


def rowwise(fn, rows, pars, outs, name, tb=256, sub=16, ncol=1):
    m = (rows[0].arr if isinstance(rows[0], Cols) else rows[0]).shape[0]
    tb = _pick(m, tb)
    sub = min(sub, tb)
    nr, npar = len(rows), len(pars)

    def body(*refs):
        row_refs, par_refs, out_refs = refs[:nr], refs[nr:nr + npar], refs[nr + npar:]
        pv = [_par_value(p, ncol) for p in par_refs]

        def step(i, carry):
            sl = pl.ds(pl.multiple_of(i * sub, sub), sub)
            res = fn(*[r[sl, :].astype(F32) for r in row_refs], *pv)
            for o, v in zip(out_refs, res):
                o[sl, :] = v.astype(o.dtype)
            return carry

        lax.fori_loop(0, tb // sub, step, 0)

    arrs, specs = zip(*[_row_spec(r, tb, ncol) for r in rows])
    return pl.pallas_call(
        body, name=name, grid=(ncol, m // tb),
        in_specs=list(specs) + [_par_spec(p.shape, ncol) for p in pars],
        out_specs=[pl.BlockSpec((tb, w), lambda j, i: (i, j)) for w, _ in outs],
        out_shape=[jax.ShapeDtypeStruct((m, w * ncol), dt) for w, dt in outs],
        compiler_params=_cparams(("parallel", "parallel")),
    )(*arrs, *pars)


def rowwise_vjp(fn, rows, pars, cots, drow_dtypes, name, tb=256, sub=16, out_sums=(), ncol=1):
    m = (rows[0].arr if isinstance(rows[0], Cols) else rows[0]).shape[0]
    tb = _pick(m, tb)
    sub = min(sub, tb)
    cots = [list(c) if isinstance(c, (list, tuple)) else [c] for c in cots]
    flat = [c for group in cots for c in group]
    nr, npar, nc = len(rows), len(pars), len(flat)
    want = [i for i, d in enumerate(drow_dtypes) if d is not None]
    assert ncol == 1 or not out_sums

    def body(*refs):
        row_refs, par_refs = refs[:nr], refs[nr:nr + npar]
        cot_refs = list(refs[nr + npar:nr + npar + nc])
        drow_refs = refs[nr + npar + nc:nr + npar + nc + len(want)]
        acc_refs = refs[nr + npar + nc + len(want):]
        pv = [_par_value(p, ncol) for p in par_refs]

        def step(i, acc):
            sl = pl.ds(pl.multiple_of(i * sub, sub), sub)
            rv = [r[sl, :].astype(F32) for r in row_refs]
            res, vjp = jax.vjp(lambda rr, pp: tuple(fn(*rr, *pp)), rv, pv)
            ct, at = [], 0
            for group in cots:
                ct.append(sum(c[sl, :].astype(F32) for c in cot_refs[at:at + len(group)]))
                at += len(group)
            d_rows, d_pars = vjp(tuple(ct))
            for o, idx in zip(drow_refs, want):
                o[sl, :] = d_rows[idx].astype(o.dtype)
            sums = [jnp.sum(res[k], axis=0, keepdims=True) for k, _ in out_sums]
            return tuple(a + d for a, d in zip(acc, list(d_pars) + sums))

        init = tuple(jnp.zeros(p.shape, F32) for p in pv) + tuple(jnp.zeros((1, w), F32) for _, w in out_sums)
        acc = lax.fori_loop(0, tb // sub, step, init)

        @pl.when(pl.program_id(1) == 0)
        def _():
            for o in acc_refs:
                o[...] = jnp.zeros_like(o)

        for o, a in zip(acc_refs, acc):
            if ncol == 1:
                o[...] += a
            else:
                o[0] += a

    arrs, specs = zip(*[_row_spec(r, tb, ncol) for r in rows])
    carrs, cspecs = zip(*[_row_spec(c, tb, ncol) for c in flat])
    widths = [_row_width(r, ncol) for r in rows]
    acc_shapes = [tuple(p.shape) for p in pars] + [(1, w) for _, w in out_sums]
    return pl.pallas_call(
        body, name=name, grid=(ncol, m // tb),
        in_specs=list(specs) + [_par_spec(p.shape, ncol) for p in pars] + list(cspecs),
        out_specs=[pl.BlockSpec((tb, widths[i]), lambda j, i_: (i_, j)) for i in want]
        + [_par_spec(s, ncol) for s in acc_shapes],
        out_shape=[jax.ShapeDtypeStruct((m, widths[i] * ncol), drow_dtypes[i]) for i in want]
        + [jax.ShapeDtypeStruct(s, F32) for s in acc_shapes],
        compiler_params=_cparams(("parallel", "arbitrary")),
    )(*arrs, *pars, *carrs)


def _col(v, c):
    lane = lax.broadcasted_iota(jnp.int32, v.shape, 1)
    return jnp.sum(jnp.where(lane == c, v, 0.0), axis=1, keepdims=True)


def _row(v, r):
    sub = lax.broadcasted_iota(jnp.int32, v.shape, 0)
    return jnp.sum(jnp.where(sub == r, v, 0.0), axis=0, keepdims=True)


def _ssd_chunk(xs, dt, bm, cm, hs, a_row, rev, col0):
    q = dt.shape[0]
    ii = lax.broadcasted_iota(jnp.int32, (q, q), 0)
    jj = lax.broadcasted_iota(jnp.int32, (q, q), 1)
    keep = (jj >= ii) if rev else (jj <= ii)
    tri = keep.astype(F32)
    dta = dt * a_row
    a_cum = hdot(tri, dta, "nn")
    tri_t = ((jj <= ii) if rev else (jj >= ii)).astype(F32)
    a_cum_t = hdot(dta, tri_t, "tn")
    cb = bdot(cm, bm, "nt")
    last = 0 if rev else q - 1
    ys, hn = [], []
    for e, (x, h) in enumerate(zip(xs, hs)):
        c = col0 + e
        ac = _col(a_cum, c)
        seg = ac - _row(a_cum_t, c)
        lm = jnp.exp(jnp.where(keep, seg, NEG))
        xdt = x * _col(dt, c)
        y_diag = bdot(cb * lm, xdt, "nn")
        y_off = bdot(cm * jnp.exp(ac), h, "nt")
        ys.append(y_diag + y_off)
        tot = _row(ac, last)
        states = bdot(xdt, bm * jnp.exp(tot - ac), "tn")
        hn.append(jnp.exp(tot) * h + states)
    return ys, hn


def _ssd_specs(nc, hpg, w_ssd, rev_order):
    q = SSD_CHUNK
    wx = hpg * HEAD_DIM
    boff = w_ssd // LANES

    def cidx(c):
        return nc - 1 - c if rev_order else c

    x_spec = pl.BlockSpec((q, wx), lambda g, c: (cidx(c), g))
    dt_spec = pl.BlockSpec((q, LANES), lambda g, c: (cidx(c), g))
    b_spec = pl.BlockSpec((q, LANES), lambda g, c: (cidx(c), boff + g))
    c_spec = pl.BlockSpec((q, LANES), lambda g, c: (cidx(c), boff + SSD_GROUPS + g))
    a_spec = pl.BlockSpec((1, SUBLANES, LANES), lambda g, c: (g, 0, 0))
    st_spec = pl.BlockSpec((1, wx, LANES), lambda g, c: (g, 0, 0))
    ent_spec = pl.BlockSpec((1, 1, wx, LANES), lambda g, c: (g, cidx(c), 0, 0))
    return x_spec, dt_spec, b_spec, c_spec, a_spec, st_spec, ent_spec


def ssd_fwd(xbc, dtg, a_g, h0, rev, w_ssd, name):
    L = xbc.shape[0]
    nc = L // SSD_CHUNK
    hpg = w_ssd // (SSD_GROUPS * HEAD_DIM)
    wx = hpg * HEAD_DIM
    col0 = hpg if rev else 0
    x_spec, dt_spec, b_spec, c_spec, a_spec, st_spec, ent_spec = _ssd_specs(nc, hpg, w_ssd, rev)

    def body(x_ref, dt_ref, b_ref, c_ref, a_ref, h0_ref, y_ref, ent_ref, fin_ref, hs):
        c = pl.program_id(1)

        @pl.when(c == 0)
        def _():
            hs[...] = h0_ref[0]

        ent_ref[0, 0] = hs[...]
        xs = [x_ref[:, e * HEAD_DIM:(e + 1) * HEAD_DIM] for e in range(hpg)]
        hin = [hs[e * HEAD_DIM:(e + 1) * HEAD_DIM, :] for e in range(hpg)]
        ys, hn = _ssd_chunk(xs, dt_ref[...], b_ref[...], c_ref[...], hin, a_ref[0, 0:1, :], rev, col0)
        for e in range(hpg):
            y_ref[:, e * HEAD_DIM:(e + 1) * HEAD_DIM] = ys[e]
            hs[e * HEAD_DIM:(e + 1) * HEAD_DIM, :] = hn[e]
        fin_ref[0] = hs[...]

    return pl.pallas_call(
        body, name=name, grid=(SSD_GROUPS, nc),
        in_specs=[x_spec, dt_spec, b_spec, c_spec, a_spec, st_spec],
        out_specs=[x_spec, ent_spec, st_spec],
        out_shape=[jax.ShapeDtypeStruct((L, w_ssd), F32),
                   jax.ShapeDtypeStruct((SSD_GROUPS, nc, wx, LANES), F32),
                   jax.ShapeDtypeStruct((SSD_GROUPS, wx, LANES), F32)],
        scratch_shapes=[pltpu.VMEM((wx, LANES), F32)],
        compiler_params=_cparams(("parallel", "arbitrary")),
    )(xbc, dtg, xbc, xbc, a_g, h0)


def ssd_bwd(xbc, dtg, a_g, ent, dy, dfin, rev, w_ssd, name):
    L = xbc.shape[0]
    nc = L // SSD_CHUNK
    hpg = w_ssd // (SSD_GROUPS * HEAD_DIM)
    wx = hpg * HEAD_DIM
    col0 = hpg if rev else 0
    x_spec, dt_spec, b_spec, c_spec, a_spec, st_spec, ent_spec = _ssd_specs(nc, hpg, w_ssd, not rev)
    bc_out = pl.BlockSpec((SSD_CHUNK, LANES), lambda g, c: ((c if rev else nc - 1 - c), g))

    def body(x_ref, dt_ref, b_ref, c_ref, a_ref, ent_ref, dy_ref, dfin_ref,
             dx_ref, ddt_ref, db_ref, dc_ref, da_ref, dh0_ref, dhs):
        c = pl.program_id(1)

        @pl.when(c == 0)
        def _():
            dhs[...] = dfin_ref[0]
            da_ref[...] = jnp.zeros_like(da_ref)

        xs = [x_ref[:, e * HEAD_DIM:(e + 1) * HEAD_DIM] for e in range(hpg)]
        hin = [ent_ref[0, 0, e * HEAD_DIM:(e + 1) * HEAD_DIM, :] for e in range(hpg)]
        _, vjp = jax.vjp(lambda *a: _ssd_chunk(*a, rev, col0), xs, dt_ref[...], b_ref[...], c_ref[...], hin, a_ref[0, 0:1, :])
        dys = [dy_ref[:, e * HEAD_DIM:(e + 1) * HEAD_DIM] for e in range(hpg)]
        dhn = [dhs[e * HEAD_DIM:(e + 1) * HEAD_DIM, :] for e in range(hpg)]
        dxs, ddt, db, dc, dh, da = vjp((dys, dhn))
        for e in range(hpg):
            dx_ref[:, e * HEAD_DIM:(e + 1) * HEAD_DIM] = dxs[e]
            dhs[e * HEAD_DIM:(e + 1) * HEAD_DIM, :] = dh[e]
        ddt_ref[...] = ddt
        db_ref[...] = db
        dc_ref[...] = dc
        da_ref[0, 0:1, :] += da
        dh0_ref[0] = dhs[...]

    return pl.pallas_call(
        body, name=name, grid=(SSD_GROUPS, nc),
        in_specs=[x_spec, dt_spec, b_spec, c_spec, a_spec, ent_spec, x_spec, st_spec],
        out_specs=[x_spec, dt_spec, bc_out, bc_out, a_spec, st_spec],
        out_shape=[jax.ShapeDtypeStruct((L, w_ssd), F32),
                   jax.ShapeDtypeStruct((L, SSD_GROUPS * LANES), F32),
                   jax.ShapeDtypeStruct((L, SSD_GROUPS * LANES), F32),
                   jax.ShapeDtypeStruct((L, SSD_GROUPS * LANES), F32),
                   jax.ShapeDtypeStruct((SSD_GROUPS, SUBLANES, LANES), F32),
                   jax.ShapeDtypeStruct((SSD_GROUPS, wx, LANES), F32)],
        scratch_shapes=[pltpu.VMEM((wx, LANES), F32)],
        compiler_params=_cparams(("parallel", "arbitrary")),
    )(xbc, dtg, xbc, xbc, a_g, ent, dy, dfin)


CONV_W = 4
CONV_LEFT = 2
CONV_CB = 128
CONV_ROWS = 256
HALO = SUBLANES


def _conv_window(ref, r, rows, nch, L):
    s = pl.multiple_of(r * rows, rows)
    cur = ref[pl.ds(s, rows), :]
    sp = pl.multiple_of(jnp.maximum(s - HALO, 0), HALO)
    sn = pl.multiple_of(jnp.minimum(s + rows, L - HALO), HALO)
    prev = jnp.where(r > 0, ref[pl.ds(sp, HALO), :], 0.0)
    nxt = jnp.where(r < nch - 1, ref[pl.ds(sn, HALO), :], 0.0)
    return jnp.concatenate([prev, cur, nxt], axis=0)


def _shifted(win, off, rows):
    n = win.shape[0]
    return pltpu.roll(win, (-off) % n, axis=0)[HALO:HALO + rows, :]


def _dsilu(p):
    s = _sigmoid(p)
    return s * (1.0 + p * (1.0 - s))


def conv_fwd(src, col_blk0, w, b, act, name):
    L, C = src.shape[0], w.shape[1]
    rows = min(CONV_ROWS, L)
    nch = L // rows

    def body(x_ref, w_ref, b_ref, o_ref):
        def chunk(r, carry):
            win = _conv_window(x_ref, r, rows, nch, L)
            pre = b_ref[...] + sum(w_ref[k:k + 1, :] * _shifted(win, k - CONV_LEFT, rows) for k in range(CONV_W))
            o_ref[pl.ds(pl.multiple_of(r * rows, rows), rows), :] = _silu(pre) if act else pre
            return carry

        lax.fori_loop(0, nch, chunk, 0)

    return pl.pallas_call(
        body, name=name, grid=(C // CONV_CB,),
        in_specs=[pl.BlockSpec((L, CONV_CB), lambda j: (0, col_blk0 + j)),
                  pl.BlockSpec((CONV_W, CONV_CB), lambda j: (0, j)),
                  pl.BlockSpec((1, CONV_CB), lambda j: (0, j))],
        out_specs=pl.BlockSpec((L, CONV_CB), lambda j: (0, j)),
        out_shape=jax.ShapeDtypeStruct((L, C), F32),
        compiler_params=_cparams(("parallel",)),
    )(src, w, b)


def conv_bwd(src, col_blk0, w, b, douts, act, name):
    L, C = src.shape[0], w.shape[1]
    rows = min(CONV_ROWS, L)
    nch = L // rows
    nd = len(douts)

    def body(*refs):
        x_ref, w_ref, b_ref = refs[:3]
        d_refs = refs[3:3 + nd]
        dx_ref, dw_ref, db_ref, dp = refs[3 + nd:]

        def pass1(r, acc):
            sl = pl.ds(pl.multiple_of(r * rows, rows), rows)
            win = _conv_window(x_ref, r, rows, nch, L)
            taps = [_shifted(win, k - CONV_LEFT, rows) for k in range(CONV_W)]
            dpre = sum(d[sl, :] for d in d_refs)
            if act:
                pre = b_ref[...] + sum(w_ref[k:k + 1, :] * taps[k] for k in range(CONV_W))
                dpre = dpre * _dsilu(pre)
            dp[sl, :] = dpre
            new = [acc[k] + jnp.sum(dpre * taps[k], axis=0, keepdims=True) for k in range(CONV_W)]
            return tuple(new) + (acc[CONV_W] + jnp.sum(dpre, axis=0, keepdims=True),)

        zero = jnp.zeros((1, CONV_CB), F32)
        acc = lax.fori_loop(0, nch, pass1, (zero,) * (CONV_W + 1))
        for k in range(CONV_W):
            dw_ref[k:k + 1, :] = acc[k]
        db_ref[...] = acc[CONV_W]

        def pass2(r, carry):
            win = _conv_window(dp, r, rows, nch, L)
            dx = sum(w_ref[k:k + 1, :] * _shifted(win, CONV_LEFT - k, rows) for k in range(CONV_W))
            dx_ref[pl.ds(pl.multiple_of(r * rows, rows), rows), :] = dx
            return carry

        lax.fori_loop(0, nch, pass2, 0)

    col = pl.BlockSpec((L, CONV_CB), lambda j: (0, j))
    return pl.pallas_call(
        body, name=name, grid=(C // CONV_CB,),
        in_specs=[pl.BlockSpec((L, CONV_CB), lambda j: (0, col_blk0 + j)),
                  pl.BlockSpec((CONV_W, CONV_CB), lambda j: (0, j)),
                  pl.BlockSpec((1, CONV_CB), lambda j: (0, j))] + [col] * nd,
        out_specs=[col, pl.BlockSpec((CONV_W, CONV_CB), lambda j: (0, j)), pl.BlockSpec((1, CONV_CB), lambda j: (0, j))],
        out_shape=[jax.ShapeDtypeStruct((L, C), F32), jax.ShapeDtypeStruct((CONV_W, C), F32), jax.ShapeDtypeStruct((1, C), F32)],
        scratch_shapes=[pltpu.VMEM((L, CONV_CB), F32)],
        compiler_params=_cparams(("parallel",)),
    )(src, w, b, *douts)


SCAN_ROWS = 512
SCAN_CB = 512


def _scan8(a, u, rev):
    sub = lax.broadcasted_iota(jnp.int32, a.shape, 0)
    for s in (1, 2, 4):
        if rev:
            ok = sub < SUBLANES - s
            a_sh, u_sh = pltpu.roll(a, SUBLANES - s, axis=0), pltpu.roll(u, SUBLANES - s, axis=0)
        else:
            ok = sub >= s
            a_sh, u_sh = pltpu.roll(a, s, axis=0), pltpu.roll(u, s, axis=0)
        u = a * jnp.where(ok, u_sh, 0.0) + u
        a = a * jnp.where(ok, a_sh, 1.0)
    return a, u


def _shift8(h, carry, rev):
    sub = lax.broadcasted_iota(jnp.int32, h.shape, 0)
    if rev:
        return jnp.where(sub == SUBLANES - 1, carry, pltpu.roll(h, SUBLANES - 1, axis=0))
    return jnp.where(sub == 0, carry, pltpu.roll(h, 1, axis=0))


def lru_scan(a, u, c0, rev, name, adjoint_of=None, add=None):
    L, C = a.shape
    tc, cb = _pick(L, SCAN_ROWS), _pick(C, SCAN_CB)
    nt = L // tc
    ng = tc // SUBLANES
    adj = adjoint_of is not None
    n_in = 3 + (1 if adj else 0) + (1 if add is not None else 0)

    def body(*refs):
        a_ref, u_ref, c0_ref = refs[:3]
        x_ref = refs[3] if (adj or add is not None) else None
        outs = refs[n_in:-1]
        st = refs[-1]

        @pl.when(pl.program_id(1) == 0)
        def _():
            st[...] = c0_ref[...]

        def group(i, carry):
            g = (ng - 1 - i) if rev else i
            sl = pl.ds(pl.multiple_of(g * SUBLANES, SUBLANES), SUBLANES)
            av, uv = a_ref[sl, :], u_ref[sl, :]
            pa, h = _scan8(av, av * uv if adj else uv, rev)
            h = pa * carry + h
            hs = _shift8(h, carry, rev)
            if adj:
                lam = uv + hs
                outs[0][sl, :] = lam
                outs[1][sl, :] = lam * x_ref[sl, :]
            else:
                outs[0][sl, :] = h
                outs[1][sl, :] = hs
                if add is not None:
                    outs[3][sl, :] = h + x_ref[sl, :]
            last = 0 if rev else SUBLANES - 1
            return jnp.broadcast_to(h[last:last + 1, :], h.shape)

        st[...] = lax.fori_loop(0, ng, group, st[...])
        outs[2][...] = st[...]

    def tmap(j, t):
        return ((nt - 1 - t) if rev else t, j)

    blk = pl.BlockSpec((tc, cb), tmap)
    vec = pl.BlockSpec((SUBLANES, cb), lambda j, t: (0, j))
    args = [a, u, c0] + ([adjoint_of] if adj else []) + ([add] if add is not None else [])
    n_big = 2 if (adj or add is None) else 3
    out_specs = [blk, blk, vec] + ([blk] if n_big == 3 else [])
    out_shape = [jax.ShapeDtypeStruct((L, C), F32), jax.ShapeDtypeStruct((L, C), F32), jax.ShapeDtypeStruct((SUBLANES, C), F32)]
    out_shape += [jax.ShapeDtypeStruct((L, C), F32)] if n_big == 3 else []
    return pl.pallas_call(
        body, name=name, grid=(C // cb, nt),
        in_specs=[blk, blk, vec] + [blk] * (n_in - 3),
        out_specs=out_specs, out_shape=out_shape,
        scratch_shapes=[pltpu.VMEM((SUBLANES, cb), F32)],
        compiler_params=_cparams(("parallel", "arbitrary")),
    )(*args)


def f_silu(v):
    return (_silu(v),)


def f_add_bias(v, b):
    return (v + b,)


def f_mod_in(x, gain, shift, scale):
    return _rms(x, gain) * (1.0 + scale) + shift, x


def f_dt(raw, bias):
    return (_softplus(raw + bias),)


def f_gnorm(yf, yb, xs, z, dexp, gain):
    return (_rms((yf + yb + dexp * xs) * _silu(z), gain),)


def f_gelu_gate(r, yr):
    return (r * _gelu_tanh(yr),)


def f_merge(gs, gr, o_s, o_r, bs, br):
    return (_sigmoid(gs + bs) * o_s + _sigmoid(gr + br) * o_r,)


def f_res_mod(x, out, gm, gain, shift, scale):
    x1 = x + gm * out
    return x1, _rms(x1, gain) * (1.0 + scale) + shift


def f_final(x1, f, t, gf, gain):
    e = _rms(x1 + gf * f, gain) - t
    return (jnp.broadcast_to(0.5 * jnp.mean(e * e, axis=-1, keepdims=True), (e.shape[0], LANES)),)


def _lru_coeffs(xh, wa, ba, wx, bx, lam):
    r = _sigmoid(bdot(xh, wa, "nn") + ba)
    i = _sigmoid(bdot(xh, wx, "nn") + bx)
    log_a = -LRU_C * r * _softplus(-lam)
    return jnp.exp(log_a), jnp.sqrt(_neg_expm1(2.0 * log_a)) * (i * xh)


def f_gates(xh, wa_f, ba_f, wx_f, bx_f, lam_f, wa_b, ba_b, wx_b, bx_b, lam_b):
    return _lru_coeffs(xh, wa_f, ba_f, wx_f, bx_f, lam_f) + _lru_coeffs(xh, wa_b, ba_b, wx_b, bx_b, lam_b)


def f_adam(w, g, m, v):
    m2 = ADAM_B1 * m + (1.0 - ADAM_B1) * g
    v2 = ADAM_B2 * v + (1.0 - ADAM_B2) * (g * g)
    m_hat = m2 / (1.0 - ADAM_B1 ** ADAM_STEP)
    v_hat = v2 / (1.0 - ADAM_B2 ** ADAM_STEP)
    return -ADAM_LR * (m_hat / (jnp.sqrt(v_hat) + ADAM_EPS) + ADAM_WD * w), m2, v2


def swiglu(gu, dact, name, tb=128, sub=16):
    L, f2 = gu.shape
    f = f2 // 2
    tb = _pick(L, tb)

    def body(*refs):
        gu_ref, o_ref = refs[0], refs[-1]

        def step(i, carry):
            sl = pl.ds(pl.multiple_of(i * sub, sub), sub)
            g, u = gu_ref[sl, :f], gu_ref[sl, f:]
            if dact is None:
                o_ref[sl, :] = (_silu(g) * u).astype(o_ref.dtype)
            else:
                d = refs[1][sl, :]
                o_ref[sl, :f] = (d * u * _dsilu(g)).astype(o_ref.dtype)
                o_ref[sl, f:] = (d * _silu(g)).astype(o_ref.dtype)
            return carry

        lax.fori_loop(0, tb // sub, step, 0)

    wout = f if dact is None else f2
    ins = [gu] + ([] if dact is None else [dact])
    return pl.pallas_call(
        body, name=name, grid=(L // tb,),
        in_specs=[pl.BlockSpec((tb, a.shape[1]), lambda i: (i, 0)) for a in ins],
        out_specs=pl.BlockSpec((tb, wout), lambda i: (i, 0)),
        out_shape=jax.ShapeDtypeStruct((L, wout), BF16),
        compiler_params=_cparams(("parallel",)),
    )(*ins)


N_DEV = 8
N_CHIP = 4
PACK_W = 1024
_HBM = pl.BlockSpec(memory_space=pltpu.HBM)


def _me():
    return lax.axis_index("x"), lax.axis_index("y"), lax.axis_index("c")


def _flip(pos, k):
    x, y, c = pos
    return (1 - x if k & 4 else x, 1 - y if k & 2 else y, 1 - c if k & 1 else c)


def _dev_index(pos):
    return 4 * pos[0] + 2 * pos[1] + pos[2]


def _chip_index(pos):
    return 2 * pos[0] + pos[1]


def _rcopy(src, dst, sems, k, to):
    send_sems, recv_sems = sems
    return pltpu.make_async_remote_copy(src_ref=src, dst_ref=dst, send_sem=send_sems.at[k], recv_sem=recv_sems.at[k],
                                        device_id=to, device_id_type=MESH)


def allgather8(v, name):
    def body(v_ref, out_ref, send_sems, recv_sems, local_sem):
        me = _me()
        sems = (send_sems, recv_sems)
        mine = out_ref.at[_dev_index(me)]
        local = pltpu.make_async_copy(v_ref, mine, local_sem)
        local.start()
        sends = [_rcopy(v_ref, mine, sems, k - 1, _flip(me, k)) for k in range(1, N_DEV)]
        for cp in sends:
            cp.start()
        for k in range(1, N_DEV):
            peer = _flip(me, k)
            _rcopy(v_ref, out_ref.at[_dev_index(peer)], sems, k - 1, peer).wait_recv()
        for cp in sends:
            cp.wait_send()
        local.wait()

    return pl.pallas_call(
        body, name=name, in_specs=[_HBM], out_specs=_HBM,
        out_shape=jax.ShapeDtypeStruct((N_DEV,) + v.shape, v.dtype),
        scratch_shapes=[pltpu.SemaphoreType.DMA((N_DEV - 1,)), pltpu.SemaphoreType.DMA((N_DEV - 1,)), pltpu.SemaphoreType.DMA],
    )(v)


def gather_chips(mine, name):
    R = mine.shape[0]
    half = R // 2

    def body(m_ref, out_ref, send_sems, recv_sems, local_sem):
        me = _me()
        x, y, c = me
        sems = (send_sems, recv_sems)
        sibling = _flip(me, 1)
        chips = [_flip(me, 4), _flip(me, 2), _flip(me, 6)]
        my_rows = pl.ds(pl.multiple_of(c * half, 16), half)
        sib_rows = pl.ds(pl.multiple_of((1 - c) * half, 16), half)
        local = pltpu.make_async_copy(m_ref, out_ref.at[_chip_index(me)], local_sem)
        local.start()
        first = [_rcopy(m_ref.at[my_rows], out_ref.at[_chip_index(me), my_rows], sems, j, p) for j, p in enumerate(chips)]
        for cp in first:
            cp.start()
        passed = []
        for j, p in enumerate(chips):
            got = out_ref.at[_chip_index(p), my_rows]
            _rcopy(got, got, sems, j, p).wait_recv()
            fwd = _rcopy(got, got, sems, 3 + j, sibling)
            fwd.start()
            passed.append(fwd)
        for j, p in enumerate(chips):
            got = out_ref.at[_chip_index(p), sib_rows]
            _rcopy(got, got, sems, 3 + j, sibling).wait_recv()
        for cp in first + passed:
            cp.wait_send()
        local.wait()

    return pl.pallas_call(
        body, name=name, in_specs=[_HBM], out_specs=_HBM,
        out_shape=jax.ShapeDtypeStruct((N_CHIP,) + mine.shape, mine.dtype),
        scratch_shapes=[pltpu.SemaphoreType.DMA((6,)), pltpu.SemaphoreType.DMA((6,)), pltpu.SemaphoreType.DMA],
    )(mine)


def scatter_halves(parts, name):
    _, R, C = parts.shape
    half = R // 2

    def body(p_ref, land_ref, send_sems, recv_sems, local_sem):
        me = _me()
        sems = (send_sems, recv_sems)
        slot = land_ref.at[_dev_index(me)]

        def piece(to):
            return p_ref.at[_chip_index(to), pl.ds(pl.multiple_of(to[2] * half, 16), half)]

        local = pltpu.make_async_copy(piece(me), slot, local_sem)
        local.start()
        sends = [_rcopy(piece(_flip(me, k)), slot, sems, k - 1, _flip(me, k)) for k in range(1, N_DEV)]
        for cp in sends:
            cp.start()
        for k in range(1, N_DEV):
            peer = _flip(me, k)
            got = land_ref.at[_dev_index(peer)]
            _rcopy(got, got, sems, k - 1, peer).wait_recv()
        for cp in sends:
            cp.wait_send()
        local.wait()

    return pl.pallas_call(
        body, name=name, in_specs=[_HBM], out_specs=_HBM,
        out_shape=jax.ShapeDtypeStruct((N_DEV, half, C), parts.dtype),
        scratch_shapes=[pltpu.SemaphoreType.DMA((N_DEV - 1,)), pltpu.SemaphoreType.DMA((N_DEV - 1,)), pltpu.SemaphoreType.DMA],
    )(parts)


def join_halves(mine, name):
    half, C = mine.shape

    def body(m_ref, out_ref, send_sems, recv_sems, local_sem):
        me = _me()
        c = me[2]
        sems = (send_sems, recv_sems)
        sibling = _flip(me, 1)
        my_rows = out_ref.at[pl.ds(pl.multiple_of(c * half, 8), half)]
        sib_rows = out_ref.at[pl.ds(pl.multiple_of((1 - c) * half, 8), half)]
        local = pltpu.make_async_copy(m_ref, my_rows, local_sem)
        local.start()
        send = _rcopy(m_ref, my_rows, sems, 0, sibling)
        send.start()
        _rcopy(sib_rows, sib_rows, sems, 0, sibling).wait_recv()
        send.wait_send()
        local.wait()

    return pl.pallas_call(
        body, name=name, in_specs=[_HBM], out_specs=_HBM,
        out_shape=jax.ShapeDtypeStruct((2 * half, C), mine.dtype),
        scratch_shapes=[pltpu.SemaphoreType.DMA((1,)), pltpu.SemaphoreType.DMA((1,)), pltpu.SemaphoreType.DMA],
    )(mine)


def sum_slots(a, name, tb=128):
    n, R, C = a.shape
    tb = _pick(R, tb)

    def body(a_ref, o_ref):
        acc = a_ref[0].astype(F32)
        for j in range(1, n):
            acc = acc + a_ref[j].astype(F32)
        o_ref[...] = acc

    return pl.pallas_call(
        body, name=name, grid=(R // tb,),
        in_specs=[pl.BlockSpec((n, tb, C), lambda i: (0, i, 0))],
        out_specs=pl.BlockSpec((tb, C), lambda i: (i, 0)),
        out_shape=jax.ShapeDtypeStruct((R, C), F32),
        compiler_params=_cparams(("parallel",)),
    )(a)


def _pack(arrays, dtype, row_align):
    flat = jnp.concatenate([a.astype(dtype).reshape(-1) for a in arrays])
    per = PACK_W * row_align
    n = -(-flat.shape[0] // per) * per
    return jnp.pad(flat, (0, n - flat.shape[0])).reshape(-1, PACK_W)


def _unpack(buf, shapes):
    flat = buf.reshape(-1)
    out, at = [], 0
    for s in shapes:
        n = 1
        for d in s:
            n *= d
        out.append(flat[at:at + n].reshape(s))
        at += n
    return out


def _to_col_major(t, rows):
    L, C = t.shape
    return t.reshape(rows, GRID_W, C).transpose(1, 0, 2).reshape(L, C)


def _to_row_major(t, rows):
    L, C = t.shape
    return t.reshape(GRID_W, rows, C).transpose(1, 0, 2).reshape(L, C)


def _heads_to_groups(v, hpg):
    lead = v.shape[:-1]
    t = jnp.moveaxis(v.reshape(lead + (2, SSD_GROUPS, hpg)), -3, -2).reshape(lead + (SSD_GROUPS, 2 * hpg))
    t = jnp.pad(t, [(0, 0)] * (len(lead) + 1) + [(0, LANES - 2 * hpg)])
    return t.reshape(lead + (SSD_GROUPS * LANES,))


def _groups_to_heads(t, hpg):
    lead = t.shape[:-1]
    t = t.reshape(lead + (SSD_GROUPS, LANES))[..., :2 * hpg].reshape(lead + (SSD_GROUPS, 2, hpg))
    return jnp.moveaxis(t, -2, -3).reshape(lead + (2 * SSD_GROUPS * hpg,))


def _r1(v):
    return v.reshape(1, -1)


class _Layout:
    def __init__(self, D, W, hpg):
        assert W == D and D % (SSD_GROUPS * LANES) == 0
        self.D, self.W, self.hpg = D, W, hpg
        self.GN = SSD_GROUPS * SSD_STATE
        self.xbc0 = 5 * D
        self.dt0 = 5 * D + W + 2 * self.GN
        self.width = self.dt0 + SSD_GROUPS * LANES
        self.xbc_blk = self.xbc0 // CONV_CB
        self.dt_blk = self.dt0 // (SSD_GROUPS * LANES)
        assert self.dt0 % (SSD_GROUPS * LANES) == 0


def _seq_forward(hb, w_all, sp, lay, tag, grid_rows, init):
    D, W = lay.D, lay.W
    gw = SSD_GROUPS * LANES
    pg = matmul(hb, w_all, "nn", F32, f"proj_{tag}")
    xbc = conv_fwd(pg, lay.xbc_blk, sp["ssd_conv_w"], sp["ssd_conv_b"], True, f"ssd_conv_{tag}")
    dtg = rowwise(f_dt, [Cols(pg, gw, lay.dt_blk)], [sp["dt_bias_g"]], [(gw, F32)], f"ssd_dt_{tag}")[0]
    yf, ent_f, fin_f = ssd_fwd(xbc, dtg, sp["a_g"], init["ssd_f"], False, W, f"ssd_scan_f_{tag}")
    yb, ent_b, fin_b = ssd_fwd(xbc, dtg, sp["a_g"], init["ssd_b"], True, W, f"ssd_scan_b_{tag}")
    xr = pg[:, D:2 * D]
    if grid_rows:
        xr = _to_col_major(xr, grid_rows)
    xc = conv_fwd(xr, 0, sp["lru_conv_w"], sp["lru_conv_b"], False, f"lru_conv_{tag}")
    a_f, u_f, a_b, u_b = rowwise(f_gates, [xc], sp["gate_pars"], [(LRU_HEAD, F32)] * 4, f"lru_gates_{tag}",
                                 tb=256, sub=256, ncol=D // LRU_HEAD)
    h_f, hs_f, fl_f = lru_scan(a_f, u_f, init["lru_f"], False, f"lru_scan_f_{tag}")
    _, hs_b, fl_b, r = lru_scan(a_b, u_b, init["lru_b"], True, f"lru_scan_b_{tag}", add=h_f)
    saved = dict(pg=pg, xbc=xbc, dtg=dtg, yf=yf, yb=yb, ent_f=ent_f, ent_b=ent_b, xr=xr, xc=xc,
                 a_f=a_f, a_b=a_b, hs_f=hs_f, hs_b=hs_b, r=r)
    return saved, dict(ssd_f=fin_f, ssd_b=fin_b, lru_f=fl_f, lru_b=fl_b)


def _seq_backward(sv, sp, lay, tag, grid_rows, dy, dxs_extra, d_r, dfin):
    D, W, GN = lay.D, lay.W, lay.GN
    gw = SSD_GROUPS * LANES
    pg = sv["pg"]
    dxf, ddt_f, db_f, dc_f, da_f, dh0_f = ssd_bwd(sv["xbc"], sv["dtg"], sp["a_g"], sv["ent_f"], dy, dfin["ssd_f"], False, W, f"ssd_scan_f_bwd_{tag}")
    dxb, ddt_b, db_b, dc_b, da_b, dh0_b = ssd_bwd(sv["xbc"], sv["dtg"], sp["a_g"], sv["ent_b"], dy, dfin["ssd_b"], True, W, f"ssd_scan_b_bwd_{tag}")
    cw, cb = sp["ssd_conv_w"], sp["ssd_conv_b"]
    d_xs, dw1, db1 = conv_bwd(pg, lay.xbc_blk, cw[:, :W], cb[:, :W], [dxf, dxb] + dxs_extra, True, f"ssd_conv_x_bwd_{tag}")
    d_b, dw2, db2 = conv_bwd(pg, lay.xbc_blk + W // CONV_CB, cw[:, W:W + GN], cb[:, W:W + GN], [db_f, db_b], True, f"ssd_conv_b_bwd_{tag}")
    d_c, dw3, db3 = conv_bwd(pg, lay.xbc_blk + (W + GN) // CONV_CB, cw[:, W + GN:], cb[:, W + GN:], [dc_f, dc_b], True, f"ssd_conv_c_bwd_{tag}")
    d_dtraw, d_dtbias = rowwise_vjp(f_dt, [Cols(pg, gw, lay.dt_blk)], [sp["dt_bias_g"]], [[ddt_f, ddt_b]], [F32], f"ssd_dt_bwd_{tag}")
    du_b, dab, dl0_b = lru_scan(sv["a_b"], d_r, dfin["lru_b"], False, f"lru_scan_b_bwd_{tag}", adjoint_of=sv["hs_b"])
    du_f, daf, dl0_f = lru_scan(sv["a_f"], d_r, dfin["lru_f"], True, f"lru_scan_f_bwd_{tag}", adjoint_of=sv["hs_f"])
    res = rowwise_vjp(f_gates, [sv["xc"]], sp["gate_pars"], [daf, du_f, dab, du_b], [F32], f"lru_gates_bwd_{tag}",
                      tb=256, sub=256, ncol=D // LRU_HEAD)
    d_xc, gate_grads = res[0], res[1:]
    d_xr, g_lcw, g_lcb = conv_bwd(sv["xr"], 0, sp["lru_conv_w"], sp["lru_conv_b"], [d_xc], False, f"lru_conv_bwd_{tag}")
    if grid_rows:
        d_xr = _to_row_major(d_xr, grid_rows)
    grads = dict(ssd_conv_w=jnp.concatenate([dw1, dw2, dw3], axis=1), ssd_conv_b=jnp.concatenate([db1, db2, db3], axis=1),
                 dt_bias_g=d_dtbias, a_g=da_f + da_b, lru_conv_w=g_lcw, lru_conv_b=g_lcb, gate_pars=list(gate_grads))
    pieces = dict(xr=d_xr, xs=d_xs, b=d_b, c=d_c, dt=d_dtraw)
    return pieces, grads, dict(ssd_f=dh0_f, ssd_b=dh0_b, lru_f=dl0_f, lru_b=dl0_b)


def _proj_cotangent(lay, L, pieces, dz=None, dyr=None, dgs=None, dgr=None):
    zero = jnp.zeros((L, lay.D), BF16)
    cols = [zero if dz is None else dz, pieces["xr"], zero if dyr is None else dyr, zero if dgs is None else dgs,
            zero if dgr is None else dgr, pieces["xs"], pieces["b"], pieces["c"], pieces["dt"]]
    return jnp.concatenate([c.astype(BF16) for c in cols], axis=1)


def _local_step(x, cvec, ctx, target, wb, ws):
    L, D = x.shape
    Lc = ctx.shape[0]
    nh = ws["ssd_d"].shape[0]
    hpg = nh // SSD_GROUPS
    W = nh * HEAD_DIM
    lay = _Layout(D, W, hpg)
    GN = lay.GN
    grid_rows = L // GRID_W
    nlh = D // LRU_HEAD

    w_in = wb["w_in"]
    o_dt, o_xr = 2 * W + 2 * GN, 2 * W + 2 * GN + 2 * nh
    w_all = jnp.concatenate([w_in[:, :W], w_in[:, o_xr:o_xr + D], w_in[:, o_xr + D:], wb["w_gate"], w_in[:, W:o_dt],
                             _heads_to_groups(w_in[:, o_dt:o_xr], hpg)], axis=1)
    a_neg = -jnp.exp(ws["ssd_a_log"])
    a_g = jnp.pad(_heads_to_groups(a_neg.reshape(-1), hpg).reshape(SSD_GROUPS, 1, LANES), [(0, 0), (0, SUBLANES - 1), (0, 0)])
    gate_pars = []
    for d in range(2):
        gate_pars += [ws["lru_w_a"][d], ws["lru_b_a"][d].reshape(nlh, 1, LRU_HEAD), ws["lru_w_x"][d],
                      ws["lru_b_x"][d].reshape(nlh, 1, LRU_HEAD), ws["lru_lambda"][d].reshape(nlh, 1, LRU_HEAD)]
    sp = dict(ssd_conv_w=ws["ssd_conv_w"], ssd_conv_b=_r1(ws["ssd_conv_b"]), dt_bias_g=_r1(_heads_to_groups(ws["ssd_dt_bias"].reshape(-1), hpg)),
              a_g=a_g, lru_conv_w=ws["lru_conv_w"], lru_conv_b=_r1(ws["lru_conv_b"]), gate_pars=gate_pars)
    dexp = _r1(jnp.repeat(ws["ssd_d"], HEAD_DIM))
    norm_mix, norm_ffn, ssd_norm, final_norm = _r1(ws["norm_mix"]), _r1(ws["norm_ffn"]), _r1(ws["ssd_norm"]), _r1(ws["final_norm"])
    bg_s, bg_r = _r1(ws["b_gate"][:D]), _r1(ws["b_gate"][D:])
    b_ada = _r1(ws["b_ada"])

    MR = 16
    cc = jnp.concatenate([cvec, _r1(ws["c_ctx"]), jnp.zeros((MR - 2, D), F32)], axis=0)
    s_cc = rowwise(f_silu, [cc], [], [(D, BF16)], "mod_silu", tb=MR)[0]
    modraw = matmul(s_cc, wb["w_ada"], "nn", F32, "mod_proj")
    mod = rowwise(f_add_bias, [modraw], [b_ada], [(6 * D, F32)], "mod_bias", tb=MR)[0]
    sh_m, sc_m, g_m, sh_f, sc_f, g_f = [mod[0:1, k * D:(k + 1) * D] for k in range(6)]
    csh_m, csc_m = mod[1:2, :D], mod[1:2, D:2 * D]

    zst = dict(ssd_f=jnp.zeros((SSD_GROUPS, hpg * HEAD_DIM, SSD_STATE), F32), ssd_b=jnp.zeros((SSD_GROUPS, hpg * HEAD_DIM, SSD_STATE), F32),
               lru_f=jnp.zeros((SUBLANES, D), F32), lru_b=jnp.zeros((SUBLANES, D), F32))
    hcb = rowwise(f_mod_in, [ctx], [norm_mix, csh_m, csc_m], [(D, BF16)], "norm_mix_ctx")[0]
    hb = rowwise(f_mod_in, [x], [norm_mix, sh_m, sc_m], [(D, BF16)], "norm_mix_x")[0]
    svc, fin_c = _seq_forward(hcb, w_all, sp, lay, "ctx", None, zst)
    svx, _ = _seq_forward(hb, w_all, sp, lay, "x", grid_rows, fin_c)
    pg = svx["pg"]
    r_rm = _to_row_major(svx["r"], grid_rows)
    gn_rows = [svx["yf"], svx["yb"], Cols(svx["xbc"], W, 0), Cols(pg, D, 0)]
    yn = rowwise(f_gnorm, gn_rows, [dexp, ssd_norm], [(W, BF16)], "ssd_gnorm")[0]
    o_s = matmul(yn, wb["w_out_ssd"], "nn", F32, "out_ssd")
    o_in = rowwise(f_gelu_gate, [r_rm, Cols(pg, D, 2)], [], [(D, BF16)], "lru_gelu")[0]
    o_r = matmul(o_in, wb["w_out_lru"], "nn", F32, "out_lru")
    mg_rows = [Cols(pg, D, 3), Cols(pg, D, 4), o_s, o_r]
    mixed = rowwise(f_merge, mg_rows, [bg_s, bg_r], [(D, BF16)], "merge")[0]
    out = matmul(mixed, wb["w_o"], "nn", F32, "out_proj")
    x1, h2 = rowwise(f_res_mod, [x, out], [g_m, norm_ffn, sh_f, sc_f], [(D, F32), (D, BF16)], "res_norm_ffn")
    gu = matmul(h2, wb["ffn_w13"], "nn", F32, "ffn_in")
    act = swiglu(gu, None, "ffn_act")
    f = matmul(act, wb["ffn_w2"], "nn", F32, "ffn_out")

    ones = jnp.full((L, LANES), 1.0 / LANES, F32)
    dx1a, df, d_gf, d_fnorm, lsum = rowwise_vjp(f_final, [x1, f, target], [g_f, final_norm], [ones], [F32, BF16, None],
                                                "final_loss", out_sums=[(0, LANES)])
    loss = lsum[0, 0]
    d_act = matmul(df, wb["ffn_w2"], "nt", F32, "ffn_out_dx")
    g_w2 = matmul(act, df, "tn", F32, "ffn_out_dw")
    d_gu = swiglu(gu, d_act, "ffn_act_bwd")
    dh2 = matmul(d_gu, wb["ffn_w13"], "nt", F32, "ffn_in_dx")
    g_w13 = matmul(h2, d_gu, "tn", F32, "ffn_in_dw")
    dxa, d_out, d_gm, d_nffn, d_shf, d_scf = rowwise_vjp(f_res_mod, [x, out], [g_m, norm_ffn, sh_f, sc_f], [dx1a, dh2],
                                                         [F32, BF16], "res_norm_ffn_bwd")
    d_mixed = matmul(d_out, wb["w_o"], "nt", F32, "out_proj_dx")
    g_wo = matmul(mixed, d_out, "tn", F32, "out_proj_dw")
    dgs, dgr, do_s, do_r, d_bgs, d_bgr = rowwise_vjp(f_merge, mg_rows, [bg_s, bg_r], [d_mixed], [BF16] * 4, "merge_bwd")
    d_yn = matmul(do_s, wb["w_out_ssd"], "nt", F32, "out_ssd_dx")
    g_wos = matmul(yn, do_s, "tn", F32, "out_ssd_dw")
    d_oin = matmul(do_r, wb["w_out_lru"], "nt", F32, "out_lru_dx")
    g_wol = matmul(o_in, do_r, "tn", F32, "out_lru_dw")
    d_r_rm, d_yr = rowwise_vjp(f_gelu_gate, [r_rm, Cols(pg, D, 2)], [], [d_oin], [F32, BF16], "lru_gelu_bwd")
    dy, dxs_skip, dz, d_dexp, d_ssdn = rowwise_vjp(f_gnorm, gn_rows, [dexp, ssd_norm], [d_yn], [F32, None, F32, BF16], "ssd_gnorm_bwd")
    zfin = dict(zst)
    px, gx, dst = _seq_backward(svx, sp, lay, "x", grid_rows, dy, [dxs_skip], _to_col_major(d_r_rm, grid_rows), zfin)
    pc, gc, _ = _seq_backward(svc, sp, lay, "ctx", None, jnp.zeros((Lc, W), F32), [], jnp.zeros((Lc, D), F32), dst)
    dpg_x = _proj_cotangent(lay, L, px, dz, d_yr, dgs, dgr)
    dpg_c = _proj_cotangent(lay, Lc, pc)
    g_wall = matmul(hb, dpg_x, "tn", F32, "proj_dw_x", add=matmul(hcb, dpg_c, "tn", F32, "proj_dw_ctx"))
    dh = matmul(dpg_x, w_all, "nt", F32, "proj_dx_x")
    dhc = matmul(dpg_c, w_all, "nt", F32, "proj_dx_ctx")
    grad_x, d_nmix_x, d_shm, d_scm = rowwise_vjp(f_mod_in, [x], [norm_mix, sh_m, sc_m], [dh, dxa], [F32], "norm_mix_x_bwd")
    d_nmix_c, d_cshm, d_cscm = rowwise_vjp(f_mod_in, [ctx], [norm_mix, csh_m, csc_m], [dhc, jnp.zeros((Lc, D), F32)], [None], "norm_mix_ctx_bwd")

    dmod = jnp.concatenate([jnp.concatenate([d_shm, d_scm, d_gm, d_shf, d_scf, d_gf], axis=1),
                            jnp.concatenate([d_cshm, d_cscm, jnp.zeros((1, 4 * D), F32)], axis=1),
                            jnp.zeros((MR - 2, 6 * D), F32)], axis=0)
    d_modraw, g_b_ada = rowwise_vjp(f_add_bias, [modraw], [b_ada], [dmod], [BF16], "mod_bias_bwd", tb=MR)
    g_w_ada = matmul(s_cc, d_modraw, "tn", F32, "mod_proj_dw")
    d_scc = matmul(d_modraw, wb["w_ada"], "nt", F32, "mod_proj_dx")
    d_cc = rowwise_vjp(f_silu, [cc], [], [d_scc], [F32], "mod_silu_bwd", tb=MR)[0]

    xb0 = lay.xbc0
    g_dt_cols = _groups_to_heads(g_wall[:, lay.dt0:], hpg)
    g_w_in = jnp.concatenate([g_wall[:, :W], g_wall[:, xb0:xb0 + W + 2 * GN], g_dt_cols, g_wall[:, D:2 * D], g_wall[:, 2 * D:3 * D]], axis=1)
    d_a = _groups_to_heads((gx["a_g"] + gc["a_g"])[:, 0, :].reshape(-1), hpg).reshape(2, nh)
    gg = [a + b for a, b in zip(gx["gate_pars"], gc["gate_pars"])]

    def gate(k, shape):
        return jnp.stack([gg[k].reshape(shape), gg[5 + k].reshape(shape)], axis=0)

    grads = dict(
        c_ctx=d_cc[1], w_ada=g_w_ada, b_ada=g_b_ada.reshape(-1), norm_mix=(d_nmix_x + d_nmix_c).reshape(-1), norm_ffn=d_nffn.reshape(-1),
        w_in=g_w_in, ssd_conv_w=gx["ssd_conv_w"] + gc["ssd_conv_w"], ssd_conv_b=(gx["ssd_conv_b"] + gc["ssd_conv_b"]).reshape(-1),
        ssd_dt_bias=_groups_to_heads((gx["dt_bias_g"] + gc["dt_bias_g"]).reshape(-1), hpg).reshape(2, nh),
        ssd_a_log=d_a * a_neg, ssd_d=d_dexp.reshape(nh, HEAD_DIM).sum(axis=1), ssd_norm=d_ssdn.reshape(-1), w_out_ssd=g_wos,
        lru_conv_w=gx["lru_conv_w"] + gc["lru_conv_w"], lru_conv_b=(gx["lru_conv_b"] + gc["lru_conv_b"]).reshape(-1),
        lru_w_a=gate(0, (nlh, LRU_HEAD, LRU_HEAD)), lru_b_a=gate(1, (D,)), lru_w_x=gate(2, (nlh, LRU_HEAD, LRU_HEAD)),
        lru_b_x=gate(3, (D,)), lru_lambda=gate(4, (D,)), w_out_lru=g_wol, w_gate=g_wall[:, 3 * D:5 * D],
        b_gate=jnp.concatenate([d_bgs, d_bgr], axis=1).reshape(-1), w_o=g_wo, ffn_w13=g_w13, ffn_w2=g_w2, final_norm=d_fnorm.reshape(-1))
    return loss, grad_x, grads


WEIGHTS = ["c_ctx", "w_ada", "b_ada", "norm_mix", "norm_ffn", "w_in", "ssd_conv_w", "ssd_conv_b", "ssd_dt_bias", "ssd_a_log",
           "ssd_d", "ssd_norm", "w_out_ssd", "lru_conv_w", "lru_conv_b", "lru_w_a", "lru_b_a", "lru_w_x", "lru_b_x", "lru_lambda",
           "w_out_lru", "w_gate", "b_gate", "w_o", "ffn_w13", "ffn_w2", "final_norm"]
BIG = ["w_ada", "w_in", "w_out_ssd", "w_out_lru", "w_gate", "w_o", "ffn_w13", "ffn_w2"]
BIG_BY_COLUMNS = {"w_ada", "w_in", "w_gate", "ffn_w13"}
SMALL_SHARDED = ["ssd_conv_w", "lru_conv_w", "lru_b_a", "lru_b_x", "lru_lambda"]
ADAM_ROWS = 64


def _whole_from_shards(stack, by_columns):
    if by_columns:
        return stack.transpose(1, 0, 2).reshape(stack.shape[1], -1)
    return stack.reshape(-1, stack.shape[2])


def _shards_from_whole(g, by_columns):
    if by_columns:
        return g.reshape(g.shape[0], N_CHIP, -1).transpose(1, 0, 2)
    return g.reshape(N_CHIP, -1, g.shape[1])


def _adam(w, g, m, v, name):
    shape = w.shape
    cols = shape[-1]
    w2, g2, m2, v2 = [t.reshape(-1, cols) for t in (w, g, m, v)]
    d, nm, nv = rowwise(f_adam, [w2, g2, m2, v2], [], [(cols, F32)] * 3, name, tb=ADAM_ROWS, sub=8)
    return d.reshape(shape), nm.reshape(shape), nv.reshape(shape)


def kernel(x, c, ctx, c_ctx, w_ada, b_ada, norm_mix, norm_ffn, w_in, ssd_conv_w, ssd_conv_b, ssd_dt_bias, ssd_a_log, ssd_d, ssd_norm, w_out_ssd, lru_conv_w, lru_conv_b, lru_w_a, lru_b_a, lru_w_x, lru_b_x, lru_lambda, w_out_lru, w_gate, b_gate, w_o, ffn_w13, ffn_w2, final_norm, loss_target, m_c_ctx, m_w_ada, m_b_ada, m_norm_mix, m_norm_ffn, m_w_in, m_ssd_conv_w, m_ssd_conv_b, m_ssd_dt_bias, m_ssd_a_log, m_ssd_d, m_ssd_norm, m_w_out_ssd, m_lru_conv_w, m_lru_conv_b, m_lru_w_a, m_lru_b_a, m_lru_w_x, m_lru_b_x, m_lru_lambda, m_w_out_lru, m_w_gate, m_b_gate, m_w_o, m_ffn_w13, m_ffn_w2, m_final_norm, v_c_ctx, v_w_ada, v_b_ada, v_norm_mix, v_norm_ffn, v_w_in, v_ssd_conv_w, v_ssd_conv_b, v_ssd_dt_bias, v_ssd_a_log, v_ssd_d, v_ssd_norm, v_w_out_ssd, v_lru_conv_w, v_lru_conv_b, v_lru_w_a, v_lru_b_a, v_lru_w_x, v_lru_b_x, v_lru_lambda, v_w_out_lru, v_w_gate, v_b_gate, v_w_o, v_ffn_w13, v_ffn_w2, v_final_norm):
    given = dict(locals())
    layered = {n for n in WEIGHTS if n not in ("c_ctx", "final_norm")}
    w_blk = {n: (given[n][0] if n in layered else given[n]) for n in WEIGHTS}
    chip = _chip_index(_me())

    big_shapes = [w_blk[n].shape for n in BIG]
    gathered = gather_chips(_pack([w_blk[n] for n in BIG], BF16, 32), "gather_weights")
    stacks = _unpack_stacked(gathered, big_shapes)
    wb = {n: _whole_from_shards(s, n in BIG_BY_COLUMNS) for n, s in zip(BIG, stacks)}

    sm_shapes = [w_blk[n].shape for n in SMALL_SHARDED]
    sm_all = allgather8(_pack([w_blk[n] for n in SMALL_SHARDED], F32, SUBLANES), "gather_small")[::2]
    ws = {n: w_blk[n] for n in WEIGHTS if n not in BIG}
    for n, s in zip(SMALL_SHARDED, _unpack_stacked(sm_all, sm_shapes)):
        ws[n] = jnp.concatenate([s[k] for k in range(N_CHIP)], axis=-1)

    loss, grad_x, grads = _local_step(x[0], c, ctx[0], loss_target[0], wb, ws)
    loss = lax.psum(loss, ("x", "y", "c"))

    parts = jnp.concatenate([_shards_from_whole(grads[n], n in BIG_BY_COLUMNS).astype(BF16).reshape(N_CHIP, -1, PACK_W) for n in BIG], axis=1)
    pad = gathered.shape[1] - parts.shape[1]
    parts = jnp.pad(parts, [(0, 0), (0, pad), (0, 0)])
    landed = scatter_halves(parts, "scatter_grads")
    reduced = join_halves(sum_slots(landed, "sum_grads"), "join_grads")
    g_shard = dict(zip(BIG, _unpack(reduced, big_shapes)))

    small = [n for n in WEIGHTS if n not in BIG]
    sm_g = sum_slots(allgather8(_pack([grads[n] for n in small], F32, SUBLANES), "gather_small_grads"), "sum_small_grads")
    for n, g in zip(small, _unpack(sm_g, [grads[n].shape for n in small])):
        if n in SMALL_SHARDED:
            per = g.shape[-1] // N_CHIP
            g = lax.dynamic_slice_in_dim(g, chip * per, per, axis=g.ndim - 1)
        g_shard[n] = g

    out_g, out_d, out_m, out_v = {}, {}, {}, {}
    for n in BIG:
        out_g[n] = g_shard[n]
        out_d[n], out_m[n], out_v[n] = _adam(w_blk[n], g_shard[n], given["m_" + n][0], given["v_" + n][0], "adamw_" + n)
    sm_blk_shapes = [w_blk[n].shape for n in small]
    packed = [_pack([t[n].reshape(w_blk[n].shape) for n in small], F32, ADAM_ROWS)
              for t in (w_blk, g_shard, {n: given["m_" + n] for n in small}, {n: given["v_" + n] for n in small})]
    res = _adam(*packed, "adamw_small")
    for tgt, buf in zip((out_d, out_m, out_v), res):
        tgt.update(zip(small, _unpack(buf, sm_blk_shapes)))
    for n in small:
        out_g[n] = g_shard[n].reshape(w_blk[n].shape)

    def full(n, t):
        return t.reshape(given[n].shape)

    return (loss, grad_x[None], *[full(n, out_g[n]) for n in WEIGHTS], *[full(n, out_d[n]) for n in WEIGHTS],
            *[full(n, out_m[n]) for n in WEIGHTS], *[full(n, out_v[n]) for n in WEIGHTS])


def _unpack_stacked(buf, shapes):
    k = buf.shape[0]
    flat = buf.reshape(k, -1)
    out, at = [], 0
    for s in shapes:
        n = 1
        for d in s:
            n *= d
        out.append(flat[:, at:at + n].reshape((k,) + tuple(s)))
        at += n
    return out
```

```python
import functools

import jax
import jax.numpy as jnp
from jax import lax
from jax.experimental import pallas as pl
from jax.experimental.pallas import tpu as pltpu

F32 = jnp.float32
BF16 = jnp.bfloat16
MESH = pl.DeviceIdType.MESH

V7X_VMEM_LIMIT_BYTES = 56 * 1024 * 1024
LANES = 128
SUBLANES = 8

HEAD_DIM = 64
SSD_STATE = 128
SSD_GROUPS = 4
SSD_CHUNK = 128
GRID_W = 64
LRU_HEAD = 128
LRU_C = 8.0
EPS = 1e-6
NEG = -1e30

ADAM_LR = 0.001
ADAM_B1 = 0.9
ADAM_B2 = 0.999
ADAM_EPS = 1e-08
ADAM_WD = 0.01
ADAM_STEP = 10


def _cparams(sem=None, **kw):
    if sem is not None:
        kw["dimension_semantics"] = sem
    return pltpu.CompilerParams(vmem_limit_bytes=V7X_VMEM_LIMIT_BYTES, **kw)


_DN = {"nn": (((1,), (0,)), ((), ())), "nt": (((1,), (1,)), ((), ())), "tn": (((0,), (0,)), ((), ()))}


def _raw_dot(a, b, form, precision=None):
    return lax.dot_general(a, b, _DN[form], precision=precision, preferred_element_type=F32)


def _dot_bwd_forms(form):
    return {"nn": (("g", "b", "nt"), ("a", "g", "tn")),
            "nt": (("g", "b", "nn"), ("g", "a", "tn")),
            "tn": (("b", "g", "nt"), ("a", "g", "nn"))}[form]


def _make_dot(rounding):
    @functools.partial(jax.custom_vjp, nondiff_argnums=(2,))
    def dot(a, b, form):
        if rounding is None:
            return _raw_dot(a, b, form, precision=lax.Precision.HIGHEST)
        return _raw_dot(a.astype(rounding), b.astype(rounding), form)

    def fwd(a, b, form):
        return dot(a, b, form), (a, b)

    def bwd(form, res, g):
        a, b = res
        ops = {"a": a, "b": b, "g": g}
        (l1, r1, f1), (l2, r2, f2) = _dot_bwd_forms(form)
        return dot(ops[l1], ops[r1], f1).astype(a.dtype), dot(ops[l2], ops[r2], f2).astype(b.dtype)

    dot.defvjp(fwd, bwd)
    return dot


bdot = _make_dot(BF16)
hdot = _make_dot(None)


def _sigmoid(v):
    return 1.0 / (1.0 + jnp.exp(-v))


def _silu(v):
    return v * _sigmoid(v)


def _softplus(v):
    return jnp.maximum(v, 0.0) + jnp.log1p(jnp.exp(-jnp.abs(v)))


def _gelu_tanh(v):
    return 0.5 * v * (1.0 + jnp.tanh(0.7978845608028654 * (v + 0.044715 * v * v * v)))


def _neg_expm1(v):
    poly = -v * (1.0 + v * (0.5 + v * (1.0 / 6.0 + v * (1.0 / 24.0 + v * (1.0 / 120.0)))))
    return jnp.where(v > -0.05, poly, 1.0 - jnp.exp(v))


def _rms(t, gain):
    return t * lax.rsqrt(jnp.mean(t * t, axis=-1, keepdims=True) + EPS) * gain


def _pick(dim, want, mult=1):
    if dim <= want:
        return dim
    for t in range(want - want % mult, 0, -mult):
        if dim % t == 0:
            return t
    t = min(dim, want)
    while dim % t:
        t //= 2
    return t


MM_TILE_M = 1024
MM_TILE_N = 1536
MM_VMEM_BYTES = 46 * 1024 * 1024


def matmul(a, b, form, out_dtype, name, tm=MM_TILE_M, tn=MM_TILE_N, tk=None, a_rows=None):
    if form == "nn":
        (m, k), (k2, n) = a.shape, b.shape
    elif form == "nt":
        (m, k), (n, k2) = a.shape, b.shape
    else:
        (k, m), (k2, n) = a.shape, b.shape
    assert k == k2, (a.shape, b.shape, form)
    row0 = 0
    if a_rows is not None:
        assert form != "tn"
        row0, m = a_rows
    tm, tn = _pick(m, tm, LANES), _pick(n, tn, LANES)
    assert row0 % tm == 0
    blk0 = row0 // tm
    if tk is None:
        fixed = tm * tn * (2 * jnp.dtype(out_dtype).itemsize + 4)
        tk = (MM_VMEM_BYTES - fixed) // (2 * a.dtype.itemsize * (tm + tn))
    tk = _pick(k, tk, LANES)
    nk = k // tk

    def body(a_ref, b_ref, o_ref, *acc):
        part = _raw_dot(a_ref[...], b_ref[...], form)
        if nk == 1:
            o_ref[...] = part.astype(o_ref.dtype)
            return
        kk = pl.program_id(2)

        @pl.when(kk == 0)
        def _():
            acc[0][...] = part

        @pl.when(kk > 0)
        def _():
            acc[0][...] += part

        @pl.when(kk == nk - 1)
        def _():
            o_ref[...] = acc[0][...].astype(o_ref.dtype)

    a_spec = pl.BlockSpec((tk, tm), lambda i, j, kk: (kk, i)) if form == "tn" else pl.BlockSpec((tm, tk), lambda i, j, kk: (blk0 + i, kk))
    b_spec = pl.BlockSpec((tn, tk), lambda i, j, kk: (j, kk)) if form == "nt" else pl.BlockSpec((tk, tn), lambda i, j, kk: (kk, j))
    return pl.pallas_call(
        body, name=name, grid=(m // tm, n // tn, nk), in_specs=[a_spec, b_spec],
        out_specs=pl.BlockSpec((tm, tn), lambda i, j, kk: (i, j)),
        out_shape=jax.ShapeDtypeStruct((m, n), out_dtype),
        scratch_shapes=[pltpu.VMEM((tm, tn), F32)] if nk > 1 else [],
        compiler_params=_cparams(("parallel", "parallel", "arbitrary")),
    )(a, b)


class Cols:
    def __init__(self, arr, w, j):
        assert (j + 1) * w <= arr.shape[1]
        self.arr, self.w, self.j = arr, w, j


def _row_width(x, ncol):
    return x.w if isinstance(x, Cols) else x.shape[1] // ncol


def _row_spec(x, tb, ncol):
    if isinstance(x, Cols):
        j0 = x.j
        return x.arr, pl.BlockSpec((tb, x.w), lambda j, i: (i, j0 + j))
    return x, pl.BlockSpec((tb, x.shape[1] // ncol), lambda j, i: (i, j))


def _par_spec(shape, ncol):
    if ncol == 1:
        return pl.BlockSpec(shape, lambda j, i, nd=len(shape): (0,) * nd)
    assert len(shape) == 3 and shape[0] == ncol, shape
    return pl.BlockSpec((1,) + tuple(shape[1:]), lambda j, i: (j, 0, 0))


def _par_value(ref, ncol):
    return ref[...] if ncol == 1 else ref[0]


def rowwise(fn, rows, pars, outs, name, tb=256, sub=16, ncol=1):
    m = (rows[0].arr if isinstance(rows[0], Cols) else rows[0]).shape[0]
    tb = _pick(m, tb)
    sub = min(sub, tb)
    nr, npar = len(rows), len(pars)

    def body(*refs):
        row_refs, par_refs, out_refs = refs[:nr], refs[nr:nr + npar], refs[nr + npar:]
        pv = [_par_value(p, ncol) for p in par_refs]

        def step(i, carry):
            sl = pl.ds(pl.multiple_of(i * sub, sub), sub)
            res = fn(*[r[sl, :].astype(F32) for r in row_refs], *pv)
            for o, v in zip(out_refs, res):
                o[sl, :] = v.astype(o.dtype)
            return carry

        lax.fori_loop(0, tb // sub, step, 0)

    arrs, specs = zip(*[_row_spec(r, tb, ncol) for r in rows])
    return pl.pallas_call(
        body, name=name, grid=(ncol, m // tb),
        in_specs=list(specs) + [_par_spec(p.shape, ncol) for p in pars],
        out_specs=[pl.BlockSpec((tb, w), lambda j, i: (i, j)) for w, _ in outs],
        out_shape=[jax.ShapeDtypeStruct((m, w * ncol), dt) for w, dt in outs],
        compiler_params=_cparams(("parallel", "parallel")),
    )(*arrs, *pars)


def rowwise_vjp(fn, rows, pars, cots, drow_dtypes, name, tb=256, sub=16, out_sums=(), ncol=1):
    m = (rows[0].arr if isinstance(rows[0], Cols) else rows[0]).shape[0]
    tb = _pick(m, tb)
    sub = min(sub, tb)
    cots = [list(c) if isinstance(c, (list, tuple)) else [c] for c in cots]
    flat = [c for group in cots for c in group]
    nr, npar, nc = len(rows), len(pars), len(flat)
    want = [i for i, d in enumerate(drow_dtypes) if d is not None]
    assert ncol == 1 or not out_sums

    def body(*refs):
        row_refs, par_refs = refs[:nr], refs[nr:nr + npar]
        cot_refs = list(refs[nr + npar:nr + npar + nc])
        drow_refs = refs[nr + npar + nc:nr + npar + nc + len(want)]
        acc_refs = refs[nr + npar + nc + len(want):]
        pv = [_par_value(p, ncol) for p in par_refs]

        def step(i, acc):
            sl = pl.ds(pl.multiple_of(i * sub, sub), sub)
            rv = [r[sl, :].astype(F32) for r in row_refs]
            res, vjp = jax.vjp(lambda rr, pp: tuple(fn(*rr, *pp)), rv, pv)
            ct, at = [], 0
            for group in cots:
                ct.append(sum(c[sl, :].astype(F32) for c in cot_refs[at:at + len(group)]))
                at += len(group)
            d_rows, d_pars = vjp(tuple(ct))
            for o, idx in zip(drow_refs, want):
                o[sl, :] = d_rows[idx].astype(o.dtype)
            sums = [jnp.sum(res[k], axis=0, keepdims=True) for k, _ in out_sums]
            return tuple(a + d for a, d in zip(acc, list(d_pars) + sums))

        init = tuple(jnp.zeros(p.shape, F32) for p in pv) + tuple(jnp.zeros((1, w), F32) for _, w in out_sums)
        acc = lax.fori_loop(0, tb // sub, step, init)

        @pl.when(pl.program_id(1) == 0)
        def _():
            for o in acc_refs:
                o[...] = jnp.zeros_like(o)

        for o, a in zip(acc_refs, acc):
            if ncol == 1:
                o[...] += a
            else:
                o[0] += a

    arrs, specs = zip(*[_row_spec(r, tb, ncol) for r in rows])
    carrs, cspecs = zip(*[_row_spec(c, tb, ncol) for c in flat])
    widths = [_row_width(r, ncol) for r in rows]
    acc_shapes = [tuple(p.shape) for p in pars] + [(1, w) for _, w in out_sums]
    return pl.pallas_call(
        body, name=name, grid=(ncol, m // tb),
        in_specs=list(specs) + [_par_spec(p.shape, ncol) for p in pars] + list(cspecs),
        out_specs=[pl.BlockSpec((tb, widths[i]), lambda j, i_: (i_, j)) for i in want]
        + [_par_spec(s, ncol) for s in acc_shapes],
        out_shape=[jax.ShapeDtypeStruct((m, widths[i] * ncol), drow_dtypes[i]) for i in want]
        + [jax.ShapeDtypeStruct(s, F32) for s in acc_shapes],
        compiler_params=_cparams(("parallel", "arbitrary")),
    )(*arrs, *pars, *carrs)


def _col(v, c):
    lane = lax.broadcasted_iota(jnp.int32, v.shape, 1)
    return jnp.sum(jnp.where(lane == c, v, 0.0), axis=1, keepdims=True)


def _row(v, r):
    sub = lax.broadcasted_iota(jnp.int32, v.shape, 0)
    return jnp.sum(jnp.where(sub == r, v, 0.0), axis=0, keepdims=True)


def _ssd_chunk(xs, dt, bm, cm, hs, a_row, rev, col0):
    q = dt.shape[0]
    ii = lax.broadcasted_iota(jnp.int32, (q, q), 0)
    jj = lax.broadcasted_iota(jnp.int32, (q, q), 1)
    keep = (jj >= ii) if rev else (jj <= ii)
    tri = keep.astype(F32)
    dta = dt * a_row
    a_cum = hdot(tri, dta, "nn")
    tri_t = ((jj <= ii) if rev else (jj >= ii)).astype(F32)
    a_cum_t = hdot(dta, tri_t, "tn")
    cb = bdot(cm, bm, "nt")
    last = 0 if rev else q - 1
    ys, hn = [], []
    for e, (x, h) in enumerate(zip(xs, hs)):
        c = col0 + e
        ac = _col(a_cum, c)
        seg = ac - _row(a_cum_t, c)
        lm = jnp.exp(jnp.where(keep, seg, NEG))
        xdt = x * _col(dt, c)
        y_diag = bdot(cb * lm, xdt, "nn")
        y_off = bdot(cm * jnp.exp(ac), h, "nt")
        ys.append(y_diag + y_off)
        tot = _row(ac, last)
        states = bdot(xdt, bm * jnp.exp(tot - ac), "tn")
        hn.append(jnp.exp(tot) * h + states)
    return ys, hn


def _ssd_specs(nc, hpg, w_ssd, rev_order):
    q = SSD_CHUNK
    wx = hpg * HEAD_DIM
    boff = w_ssd // LANES

    def cidx(c):
        return nc - 1 - c if rev_order else c

    x_spec = pl.BlockSpec((q, wx), lambda g, c: (cidx(c), g))
    dt_spec = pl.BlockSpec((q, LANES), lambda g, c: (cidx(c), g))
    b_spec = pl.BlockSpec((q, LANES), lambda g, c: (cidx(c), boff + g))
    c_spec = pl.BlockSpec((q, LANES), lambda g, c: (cidx(c), boff + SSD_GROUPS + g))
    a_spec = pl.BlockSpec((1, SUBLANES, LANES), lambda g, c: (g, 0, 0))
    st_spec = pl.BlockSpec((1, wx, LANES), lambda g, c: (g, 0, 0))
    ent_spec = pl.BlockSpec((1, 1, wx, LANES), lambda g, c: (g, cidx(c), 0, 0))
    return x_spec, dt_spec, b_spec, c_spec, a_spec, st_spec, ent_spec


def ssd_fwd(xbc, dtg, a_g, h0, rev, w_ssd, name):
    L = xbc.shape[0]
    nc = L // SSD_CHUNK
    hpg = w_ssd // (SSD_GROUPS * HEAD_DIM)
    wx = hpg * HEAD_DIM
    col0 = hpg if rev else 0
    x_spec, dt_spec, b_spec, c_spec, a_spec, st_spec, ent_spec = _ssd_specs(nc, hpg, w_ssd, rev)

    def body(x_ref, dt_ref, b_ref, c_ref, a_ref, h0_ref, y_ref, ent_ref, fin_ref, hs):
        c = pl.program_id(1)

        @pl.when(c == 0)
        def _():
            hs[...] = h0_ref[0]

        ent_ref[0, 0] = hs[...]
        xs = [x_ref[:, e * HEAD_DIM:(e + 1) * HEAD_DIM] for e in range(hpg)]
        hin = [hs[e * HEAD_DIM:(e + 1) * HEAD_DIM, :] for e in range(hpg)]
        ys, hn = _ssd_chunk(xs, dt_ref[...], b_ref[...], c_ref[...], hin, a_ref[0, 0:1, :], rev, col0)
        for e in range(hpg):
            y_ref[:, e * HEAD_DIM:(e + 1) * HEAD_DIM] = ys[e]
            hs[e * HEAD_DIM:(e + 1) * HEAD_DIM, :] = hn[e]
        fin_ref[0] = hs[...]

    return pl.pallas_call(
        body, name=name, grid=(SSD_GROUPS, nc),
        in_specs=[x_spec, dt_spec, b_spec, c_spec, a_spec, st_spec],
        out_specs=[x_spec, ent_spec, st_spec],
        out_shape=[jax.ShapeDtypeStruct((L, w_ssd), F32),
                   jax.ShapeDtypeStruct((SSD_GROUPS, nc, wx, LANES), F32),
                   jax.ShapeDtypeStruct((SSD_GROUPS, wx, LANES), F32)],
        scratch_shapes=[pltpu.VMEM((wx, LANES), F32)],
        compiler_params=_cparams(("parallel", "arbitrary")),
    )(xbc, dtg, xbc, xbc, a_g, h0)


def ssd_bwd(xbc, dtg, a_g, ent, dy, dfin, rev, w_ssd, name):
    L = xbc.shape[0]
    nc = L // SSD_CHUNK
    hpg = w_ssd // (SSD_GROUPS * HEAD_DIM)
    wx = hpg * HEAD_DIM
    col0 = hpg if rev else 0
    x_spec, dt_spec, b_spec, c_spec, a_spec, st_spec, ent_spec = _ssd_specs(nc, hpg, w_ssd, not rev)
    bc_out = pl.BlockSpec((SSD_CHUNK, LANES), lambda g, c: ((c if rev else nc - 1 - c), g))

    def body(x_ref, dt_ref, b_ref, c_ref, a_ref, ent_ref, dy_ref, dfin_ref,
             dx_ref, ddt_ref, db_ref, dc_ref, da_ref, dh0_ref, dhs):
        c = pl.program_id(1)

        @pl.when(c == 0)
        def _():
            dhs[...] = dfin_ref[0]
            da_ref[...] = jnp.zeros_like(da_ref)

        xs = [x_ref[:, e * HEAD_DIM:(e + 1) * HEAD_DIM] for e in range(hpg)]
        hin = [ent_ref[0, 0, e * HEAD_DIM:(e + 1) * HEAD_DIM, :] for e in range(hpg)]
        _, vjp = jax.vjp(lambda *a: _ssd_chunk(*a, rev, col0), xs, dt_ref[...], b_ref[...], c_ref[...], hin, a_ref[0, 0:1, :])
        dys = [dy_ref[:, e * HEAD_DIM:(e + 1) * HEAD_DIM] for e in range(hpg)]
        dhn = [dhs[e * HEAD_DIM:(e + 1) * HEAD_DIM, :] for e in range(hpg)]
        dxs, ddt, db, dc, dh, da = vjp((dys, dhn))
        for e in range(hpg):
            dx_ref[:, e * HEAD_DIM:(e + 1) * HEAD_DIM] = dxs[e]
            dhs[e * HEAD_DIM:(e + 1) * HEAD_DIM, :] = dh[e]
        ddt_ref[...] = ddt
        db_ref[...] = db
        dc_ref[...] = dc
        da_ref[0, 0:1, :] += da
        dh0_ref[0] = dhs[...]

    return pl.pallas_call(
        body, name=name, grid=(SSD_GROUPS, nc),
        in_specs=[x_spec, dt_spec, b_spec, c_spec, a_spec, ent_spec, x_spec, st_spec],
        out_specs=[x_spec, dt_spec, bc_out, bc_out, a_spec, st_spec],
        out_shape=[jax.ShapeDtypeStruct((L, w_ssd), F32),
                   jax.ShapeDtypeStruct((L, SSD_GROUPS * LANES), F32),
                   jax.ShapeDtypeStruct((L, SSD_GROUPS * LANES), F32),
                   jax.ShapeDtypeStruct((L, SSD_GROUPS * LANES), F32),
                   jax.ShapeDtypeStruct((SSD_GROUPS, SUBLANES, LANES), F32),
                   jax.ShapeDtypeStruct((SSD_GROUPS, wx, LANES), F32)],
        scratch_shapes=[pltpu.VMEM((wx, LANES), F32)],
        compiler_params=_cparams(("parallel", "arbitrary")),
    )(xbc, dtg, xbc, xbc, a_g, ent, dy, dfin)


CONV_W = 4
CONV_LEFT = 2
CONV_CB = 128
CONV_ROWS = 256
HALO = SUBLANES


def _conv_window(ref, r, rows, nch, L):
    s = pl.multiple_of(r * rows, rows)
    cur = ref[pl.ds(s, rows), :]
    sp = pl.multiple_of(jnp.maximum(s - HALO, 0), HALO)
    sn = pl.multiple_of(jnp.minimum(s + rows, L - HALO), HALO)
    prev = jnp.where(r > 0, ref[pl.ds(sp, HALO), :], 0.0)
    nxt = jnp.where(r < nch - 1, ref[pl.ds(sn, HALO), :], 0.0)
    return jnp.concatenate([prev, cur, nxt], axis=0)


def _shifted(win, off, rows):
    n = win.shape[0]
    return pltpu.roll(win, (-off) % n, axis=0)[HALO:HALO + rows, :]


def _dsilu(p):
    s = _sigmoid(p)
    return s * (1.0 + p * (1.0 - s))


def conv_fwd(src, col_blk0, w, b, act, name):
    L, C = src.shape[0], w.shape[1]
    rows = min(CONV_ROWS, L)
    nch = L // rows

    def body(x_ref, w_ref, b_ref, o_ref):
        def chunk(r, carry):
            win = _conv_window(x_ref, r, rows, nch, L)
            pre = b_ref[...] + sum(w_ref[k:k + 1, :] * _shifted(win, k - CONV_LEFT, rows) for k in range(CONV_W))
            o_ref[pl.ds(pl.multiple_of(r * rows, rows), rows), :] = _silu(pre) if act else pre
            return carry

        lax.fori_loop(0, nch, chunk, 0)

    return pl.pallas_call(
        body, name=name, grid=(C // CONV_CB,),
        in_specs=[pl.BlockSpec((L, CONV_CB), lambda j: (0, col_blk0 + j)),
                  pl.BlockSpec((CONV_W, CONV_CB), lambda j: (0, j)),
                  pl.BlockSpec((1, CONV_CB), lambda j: (0, j))],
        out_specs=pl.BlockSpec((L, CONV_CB), lambda j: (0, j)),
        out_shape=jax.ShapeDtypeStruct((L, C), F32),
        compiler_params=_cparams(("parallel",)),
    )(src, w, b)


def conv_bwd(src, col_blk0, w, b, douts, act, name):
    L, C = src.shape[0], w.shape[1]
    rows = min(CONV_ROWS, L)
    nch = L // rows
    nd = len(douts)

    def body(*refs):
        x_ref, w_ref, b_ref = refs[:3]
        d_refs = refs[3:3 + nd]
        dx_ref, dw_ref, db_ref, dp = refs[3 + nd:]

        def pass1(r, acc):
            sl = pl.ds(pl.multiple_of(r * rows, rows), rows)
            win = _conv_window(x_ref, r, rows, nch, L)
            taps = [_shifted(win, k - CONV_LEFT, rows) for k in range(CONV_W)]
            dpre = sum(d[sl, :] for d in d_refs)
            if act:
                pre = b_ref[...] + sum(w_ref[k:k + 1, :] * taps[k] for k in range(CONV_W))
                dpre = dpre * _dsilu(pre)
            dp[sl, :] = dpre
            new = [acc[k] + jnp.sum(dpre * taps[k], axis=0, keepdims=True) for k in range(CONV_W)]
            return tuple(new) + (acc[CONV_W] + jnp.sum(dpre, axis=0, keepdims=True),)

        zero = jnp.zeros((1, CONV_CB), F32)
        acc = lax.fori_loop(0, nch, pass1, (zero,) * (CONV_W + 1))
        for k in range(CONV_W):
            dw_ref[k:k + 1, :] = acc[k]
        db_ref[...] = acc[CONV_W]

        def pass2(r, carry):
            win = _conv_window(dp, r, rows, nch, L)
            dx = sum(w_ref[k:k + 1, :] * _shifted(win, CONV_LEFT - k, rows) for k in range(CONV_W))
            dx_ref[pl.ds(pl.multiple_of(r * rows, rows), rows), :] = dx
            return carry

        lax.fori_loop(0, nch, pass2, 0)

    col = pl.BlockSpec((L, CONV_CB), lambda j: (0, j))
    return pl.pallas_call(
        body, name=name, grid=(C // CONV_CB,),
        in_specs=[pl.BlockSpec((L, CONV_CB), lambda j: (0, col_blk0 + j)),
                  pl.BlockSpec((CONV_W, CONV_CB), lambda j: (0, j)),
                  pl.BlockSpec((1, CONV_CB), lambda j: (0, j))] + [col] * nd,
        out_specs=[col, pl.BlockSpec((CONV_W, CONV_CB), lambda j: (0, j)), pl.BlockSpec((1, CONV_CB), lambda j: (0, j))],
        out_shape=[jax.ShapeDtypeStruct((L, C), F32), jax.ShapeDtypeStruct((CONV_W, C), F32), jax.ShapeDtypeStruct((1, C), F32)],
        scratch_shapes=[pltpu.VMEM((L, CONV_CB), F32)],
        compiler_params=_cparams(("parallel",)),
    )(src, w, b, *douts)


SCAN_ROWS = 512
SCAN_CB = 512


def _scan8(a, u, rev):
    sub = lax.broadcasted_iota(jnp.int32, a.shape, 0)
    for s in (1, 2, 4):
        if rev:
            ok = sub < SUBLANES - s
            a_sh, u_sh = pltpu.roll(a, SUBLANES - s, axis=0), pltpu.roll(u, SUBLANES - s, axis=0)
        else:
            ok = sub >= s
            a_sh, u_sh = pltpu.roll(a, s, axis=0), pltpu.roll(u, s, axis=0)
        u = a * jnp.where(ok, u_sh, 0.0) + u
        a = a * jnp.where(ok, a_sh, 1.0)
    return a, u


def _shift8(h, carry, rev):
    sub = lax.broadcasted_iota(jnp.int32, h.shape, 0)
    if rev:
        return jnp.where(sub == SUBLANES - 1, carry, pltpu.roll(h, SUBLANES - 1, axis=0))
    return jnp.where(sub == 0, carry, pltpu.roll(h, 1, axis=0))


def lru_scan(a, u, c0, rev, name, adjoint_of=None, add=None):
    L, C = a.shape
    tc, cb = _pick(L, SCAN_ROWS), _pick(C, SCAN_CB)
    nt = L // tc
    ng = tc // SUBLANES
    adj = adjoint_of is not None
    n_in = 3 + (1 if adj else 0) + (1 if add is not None else 0)

    def body(*refs):
        a_ref, u_ref, c0_ref = refs[:3]
        x_ref = refs[3] if (adj or add is not None) else None
        outs = refs[n_in:-1]
        st = refs[-1]

        @pl.when(pl.program_id(1) == 0)
        def _():
            st[...] = c0_ref[...]

        def group(i, carry):
            g = (ng - 1 - i) if rev else i
            sl = pl.ds(pl.multiple_of(g * SUBLANES, SUBLANES), SUBLANES)
            av, uv = a_ref[sl, :], u_ref[sl, :]
            pa, h = _scan8(av, av * uv if adj else uv, rev)
            h = pa * carry + h
            hs = _shift8(h, carry, rev)
            if adj:
                lam = uv + hs
                outs[0][sl, :] = lam
                outs[1][sl, :] = lam * x_ref[sl, :]
            else:
                outs[0][sl, :] = h
                outs[1][sl, :] = hs
                if add is not None:
                    outs[3][sl, :] = h + x_ref[sl, :]
            last = 0 if rev else SUBLANES - 1
            return jnp.broadcast_to(h[last:last + 1, :], h.shape)

        st[...] = lax.fori_loop(0, ng, group, st[...])
        outs[2][...] = st[...]

    def tmap(j, t):
        return ((nt - 1 - t) if rev else t, j)

    blk = pl.BlockSpec((tc, cb), tmap)
    vec = pl.BlockSpec((SUBLANES, cb), lambda j, t: (0, j))
    args = [a, u, c0] + ([adjoint_of] if adj else []) + ([add] if add is not None else [])
    n_big = 2 if (adj or add is None) else 3
    out_specs = [blk, blk, vec] + ([blk] if n_big == 3 else [])
    out_shape = [jax.ShapeDtypeStruct((L, C), F32), jax.ShapeDtypeStruct((L, C), F32), jax.ShapeDtypeStruct((SUBLANES, C), F32)]
    out_shape += [jax.ShapeDtypeStruct((L, C), F32)] if n_big == 3 else []
    return pl.pallas_call(
        body, name=name, grid=(C // cb, nt),
        in_specs=[blk, blk, vec] + [blk] * (n_in - 3),
        out_specs=out_specs, out_shape=out_shape,
        scratch_shapes=[pltpu.VMEM((SUBLANES, cb), F32)],
        compiler_params=_cparams(("parallel", "arbitrary")),
    )(*args)


def f_silu(v):
    return (_silu(v),)


def f_add_bias(v, b):
    return (v + b,)


def f_mod_in(x, gain, shift, scale):
    return _rms(x, gain) * (1.0 + scale) + shift, x


def f_dt(raw, bias):
    return (_softplus(raw + bias),)


def f_gnorm(yf, yb, xs, z, dexp, gain):
    return (_rms((yf + yb + dexp * xs) * _silu(z), gain),)


def f_gelu_gate(r, yr):
    return (r * _gelu_tanh(yr),)


def f_merge(gs, gr, o_s, o_r, bs, br):
    return (_sigmoid(gs + bs) * o_s + _sigmoid(gr + br) * o_r,)


def f_res_mod(x, out, gm, gain, shift, scale):
    x1 = x + gm * out
    return x1, _rms(x1, gain) * (1.0 + scale) + shift


def f_final(x1, f, t, gf, gain):
    e = _rms(x1 + gf * f, gain) - t
    return (jnp.broadcast_to(0.5 * jnp.mean(e * e, axis=-1, keepdims=True), (e.shape[0], LANES)),)


def _lru_coeffs(xh, wa, ba, wx, bx, lam):
    r = _sigmoid(bdot(xh, wa, "nn") + ba)
    i = _sigmoid(bdot(xh, wx, "nn") + bx)
    log_a = -LRU_C * r * _softplus(-lam)
    return jnp.exp(log_a), jnp.sqrt(_neg_expm1(2.0 * log_a)) * (i * xh)


def f_gates(xh, wa_f, ba_f, wx_f, bx_f, lam_f, wa_b, ba_b, wx_b, bx_b, lam_b):
    return _lru_coeffs(xh, wa_f, ba_f, wx_f, bx_f, lam_f) + _lru_coeffs(xh, wa_b, ba_b, wx_b, bx_b, lam_b)


def f_adam(w, g, m, v):
    m2 = ADAM_B1 * m + (1.0 - ADAM_B1) * g
    v2 = ADAM_B2 * v + (1.0 - ADAM_B2) * (g * g)
    m_hat = m2 / (1.0 - ADAM_B1 ** ADAM_STEP)
    v_hat = v2 / (1.0 - ADAM_B2 ** ADAM_STEP)
    return -ADAM_LR * (m_hat / (jnp.sqrt(v_hat) + ADAM_EPS) + ADAM_WD * w), m2, v2


def swiglu(gu, dact, name, tb=128, sub=16):
    L, f2 = gu.shape
    f = f2 // 2
    tb = _pick(L, tb)

    def body(*refs):
        gu_ref, o_ref = refs[0], refs[-1]

        def step(i, carry):
            sl = pl.ds(pl.multiple_of(i * sub, sub), sub)
            g, u = gu_ref[sl, :f], gu_ref[sl, f:]
            if dact is None:
                o_ref[sl, :] = (_silu(g) * u).astype(o_ref.dtype)
            else:
                d = refs[1][sl, :]
                o_ref[sl, :f] = (d * u * _dsilu(g)).astype(o_ref.dtype)
                o_ref[sl, f:] = (d * _silu(g)).astype(o_ref.dtype)
            return carry

        lax.fori_loop(0, tb // sub, step, 0)

    wout = f if dact is None else f2
    ins = [gu] + ([] if dact is None else [dact])
    return pl.pallas_call(
        body, name=name, grid=(L // tb,),
        in_specs=[pl.BlockSpec((tb, a.shape[1]), lambda i: (i, 0)) for a in ins],
        out_specs=pl.BlockSpec((tb, wout), lambda i: (i, 0)),
        out_shape=jax.ShapeDtypeStruct((L, wout), BF16),
        compiler_params=_cparams(("parallel",)),
    )(*ins)


N_DEV = 8
N_CHIP = 4
PACK_W = 1024
_HBM = pl.BlockSpec(memory_space=pltpu.HBM)


def _me():
    return lax.axis_index("x"), lax.axis_index("y"), lax.axis_index("c")


def _flip(pos, k):
    x, y, c = pos
    return (1 - x if k & 4 else x, 1 - y if k & 2 else y, 1 - c if k & 1 else c)


def _dev_index(pos):
    return 4 * pos[0] + 2 * pos[1] + pos[2]


def _chip_index(pos):
    return 2 * pos[0] + pos[1]


def _rcopy(src, dst, sems, k, to):
    send_sems, recv_sems = sems
    return pltpu.make_async_remote_copy(src_ref=src, dst_ref=dst, send_sem=send_sems.at[k], recv_sem=recv_sems.at[k],
                                        device_id=to, device_id_type=MESH)


def allgather8(v, name):
    def body(v_ref, out_ref, send_sems, recv_sems, local_sem):
        me = _me()
        sems = (send_sems, recv_sems)
        mine = out_ref.at[_dev_index(me)]
        local = pltpu.make_async_copy(v_ref, mine, local_sem)
        local.start()
        sends = [_rcopy(v_ref, mine, sems, k - 1, _flip(me, k)) for k in range(1, N_DEV)]
        for cp in sends:
            cp.start()
        for k in range(1, N_DEV):
            peer = _flip(me, k)
            _rcopy(v_ref, out_ref.at[_dev_index(peer)], sems, k - 1, peer).wait_recv()
        for cp in sends:
            cp.wait_send()
        local.wait()

    return pl.pallas_call(
        body, name=name, in_specs=[_HBM], out_specs=_HBM,
        out_shape=jax.ShapeDtypeStruct((N_DEV,) + v.shape, v.dtype),
        scratch_shapes=[pltpu.SemaphoreType.DMA((N_DEV - 1,)), pltpu.SemaphoreType.DMA((N_DEV - 1,)), pltpu.SemaphoreType.DMA],
    )(v)


COPY_PIECES = 4
ROW_TILE_16BIT = 16


def _pieces(rows, n=COPY_PIECES, align=ROW_TILE_16BIT):
    while n > 1 and rows % (n * align):
        n -= 1
    return [(q * (rows // n), rows // n) for q in range(n)]


def gather_chips(mine, name):
    R = mine.shape[0]
    half = R // 2
    cuts = _pieces(half)
    nq = len(cuts)
    whole = _pieces(R, 2 * COPY_PIECES)

    def body(m_ref, out_ref, send_sems, recv_sems, local_sems):
        me = _me()
        c = me[2]
        sems = (send_sems, recv_sems)
        sibling = _flip(me, 1)
        chips = [_flip(me, 4), _flip(me, 2), _flip(me, 6)]

        def rows(core, q):
            return pl.ds(pl.multiple_of(core * half + cuts[q][0], ROW_TILE_16BIT), cuts[q][1])

        locals_ = [pltpu.make_async_copy(m_ref.at[pl.ds(s, n)], out_ref.at[_chip_index(me), pl.ds(s, n)], local_sems.at[i])
                   for i, (s, n) in enumerate(whole)]
        for cp in locals_:
            cp.start()
        first = [_rcopy(m_ref.at[rows(c, q)], out_ref.at[_chip_index(me), rows(c, q)], sems, j * nq + q, p)
                 for q in range(nq) for j, p in enumerate(chips)]
        for cp in first:
            cp.start()
        passed = []
        for q in range(nq):
            for j, p in enumerate(chips):
                got = out_ref.at[_chip_index(p), rows(c, q)]
                _rcopy(got, got, sems, j * nq + q, p).wait_recv()
                fwd = _rcopy(got, got, sems, (3 + j) * nq + q, sibling)
                fwd.start()
                passed.append(fwd)
        for q in range(nq):
            for j, p in enumerate(chips):
                got = out_ref.at[_chip_index(p), rows(1 - c, q)]
                _rcopy(got, got, sems, (3 + j) * nq + q, sibling).wait_recv()
        for cp in first + passed:
            cp.wait_send()
        for cp in locals_:
            cp.wait()

    return pl.pallas_call(
        body, name=name, in_specs=[_HBM], out_specs=_HBM,
        out_shape=jax.ShapeDtypeStruct((N_CHIP,) + mine.shape, mine.dtype),
        scratch_shapes=[pltpu.SemaphoreType.DMA((6 * nq,)), pltpu.SemaphoreType.DMA((6 * nq,)), pltpu.SemaphoreType.DMA((len(whole),))],
    )(mine)


def scatter_halves(parts, name):
    _, R, C = parts.shape
    half = R // 2
    cuts = _pieces(half, 2 * COPY_PIECES)

    def body(p_ref, land_ref, send_sems, recv_sems, local_sems):
        me = _me()
        sems = (send_sems, recv_sems)
        slot = land_ref.at[_dev_index(me)]

        def piece(to, s=0, n=half):
            return p_ref.at[_chip_index(to), pl.ds(pl.multiple_of(to[2] * half + s, ROW_TILE_16BIT), n)]

        locals_ = [pltpu.make_async_copy(piece(me, s, n), slot.at[pl.ds(s, n)], local_sems.at[i]) for i, (s, n) in enumerate(cuts)]
        for cp in locals_:
            cp.start()
        sends = [_rcopy(piece(_flip(me, k)), slot, sems, k - 1, _flip(me, k)) for k in range(1, N_DEV)]
        for cp in sends:
            cp.start()
        for k in range(1, N_DEV):
            peer = _flip(me, k)
            got = land_ref.at[_dev_index(peer)]
            _rcopy(got, got, sems, k - 1, peer).wait_recv()
        for cp in sends:
            cp.wait_send()
        for cp in locals_:
            cp.wait()

    return pl.pallas_call(
        body, name=name, in_specs=[_HBM], out_specs=_HBM,
        out_shape=jax.ShapeDtypeStruct((N_DEV, half, C), parts.dtype),
        scratch_shapes=[pltpu.SemaphoreType.DMA((N_DEV - 1,)), pltpu.SemaphoreType.DMA((N_DEV - 1,)), pltpu.SemaphoreType.DMA((len(cuts),))],
    )(parts)


def join_halves(mine, name):
    half, C = mine.shape
    cuts = _pieces(half, 2 * COPY_PIECES, SUBLANES)
    nq = len(cuts)

    def body(m_ref, out_ref, send_sems, recv_sems, local_sems):
        me = _me()
        c = me[2]
        sems = (send_sems, recv_sems)
        sibling = _flip(me, 1)

        def rows(core, q):
            return out_ref.at[pl.ds(pl.multiple_of(core * half + cuts[q][0], SUBLANES), cuts[q][1])]

        locals_ = [pltpu.make_async_copy(m_ref.at[pl.ds(s, n)], rows(c, q), local_sems.at[q]) for q, (s, n) in enumerate(cuts)]
        sends = [_rcopy(m_ref.at[pl.ds(s, n)], rows(c, q), sems, q, sibling) for q, (s, n) in enumerate(cuts)]
        for cp in locals_ + sends:
            cp.start()
        for q in range(nq):
            _rcopy(rows(1 - c, q), rows(1 - c, q), sems, q, sibling).wait_recv()
        for cp in sends:
            cp.wait_send()
        for cp in locals_:
            cp.wait()

    return pl.pallas_call(
        body, name=name, in_specs=[_HBM], out_specs=_HBM,
        out_shape=jax.ShapeDtypeStruct((2 * half, C), mine.dtype),
        scratch_shapes=[pltpu.SemaphoreType.DMA((nq,)), pltpu.SemaphoreType.DMA((nq,)), pltpu.SemaphoreType.DMA((nq,))],
    )(mine)


SUM_BLOCK_ELEMS = 256 * 1024


def sum_slots(a, name):
    n, R, C = a.shape
    tb = _pick(R, max(ROW_TILE_16BIT, SUM_BLOCK_ELEMS // C), ROW_TILE_16BIT)

    def body(a_ref, o_ref):
        acc = a_ref[0].astype(F32)
        for j in range(1, n):
            acc = acc + a_ref[j].astype(F32)
        o_ref[...] = acc

    return pl.pallas_call(
        body, name=name, grid=(R // tb,),
        in_specs=[pl.BlockSpec((n, tb, C), lambda i: (0, i, 0))],
        out_specs=pl.BlockSpec((tb, C), lambda i: (i, 0)),
        out_shape=jax.ShapeDtypeStruct((R, C), F32),
        compiler_params=_cparams(("parallel",)),
    )(a)


def _pack(arrays, dtype, row_align):
    flat = jnp.concatenate([a.astype(dtype).reshape(-1) for a in arrays])
    per = PACK_W * row_align
    n = -(-flat.shape[0] // per) * per
    return jnp.pad(flat, (0, n - flat.shape[0])).reshape(-1, PACK_W)


def _unpack(buf, shapes):
    flat = buf.reshape(-1)
    out, at = [], 0
    for s in shapes:
        n = 1
        for d in s:
            n *= d
        out.append(flat[at:at + n].reshape(s))
        at += n
    return out


def _to_col_major(t, rows):
    L, C = t.shape
    return t.reshape(rows, GRID_W, C).transpose(1, 0, 2).reshape(L, C)


def _to_row_major(t, rows):
    L, C = t.shape
    return t.reshape(GRID_W, rows, C).transpose(1, 0, 2).reshape(L, C)


def _heads_to_groups(v, hpg):
    lead = v.shape[:-1]
    t = jnp.moveaxis(v.reshape(lead + (2, SSD_GROUPS, hpg)), -3, -2).reshape(lead + (SSD_GROUPS, 2 * hpg))
    t = jnp.pad(t, [(0, 0)] * (len(lead) + 1) + [(0, LANES - 2 * hpg)])
    return t.reshape(lead + (SSD_GROUPS * LANES,))


def _groups_to_heads(t, hpg):
    lead = t.shape[:-1]
    t = t.reshape(lead + (SSD_GROUPS, LANES))[..., :2 * hpg].reshape(lead + (SSD_GROUPS, 2, hpg))
    return jnp.moveaxis(t, -2, -3).reshape(lead + (2 * SSD_GROUPS * hpg,))


def _r1(v):
    return v.reshape(1, -1)


class _Layout:
    def __init__(self, D, W, hpg):
        assert W == D and D % (SSD_GROUPS * LANES) == 0
        self.D, self.W, self.hpg = D, W, hpg
        self.GN = SSD_GROUPS * SSD_STATE
        self.xbc0 = 5 * D
        self.dt0 = 5 * D + W + 2 * self.GN
        self.width = self.dt0 + SSD_GROUPS * LANES
        self.xbc_blk = self.xbc0 // CONV_CB
        self.dt_blk = self.dt0 // (SSD_GROUPS * LANES)
        assert self.dt0 % (SSD_GROUPS * LANES) == 0


def _seq_forward(hb, w_all, sp, lay, tag, grid_rows, init):
    D, W = lay.D, lay.W
    gw = SSD_GROUPS * LANES
    pg = matmul(hb, w_all, "nn", F32, f"proj_{tag}")
    xbc = conv_fwd(pg, lay.xbc_blk, sp["ssd_conv_w"], sp["ssd_conv_b"], True, f"ssd_conv_{tag}")
    dtg = rowwise(f_dt, [Cols(pg, gw, lay.dt_blk)], [sp["dt_bias_g"]], [(gw, F32)], f"ssd_dt_{tag}")[0]
    yf, ent_f, fin_f = ssd_fwd(xbc, dtg, sp["a_g"], init["ssd_f"], False, W, f"ssd_scan_f_{tag}")
    yb, ent_b, fin_b = ssd_fwd(xbc, dtg, sp["a_g"], init["ssd_b"], True, W, f"ssd_scan_b_{tag}")
    xr = pg[:, D:2 * D]
    if grid_rows:
        xr = _to_col_major(xr, grid_rows)
    xc = conv_fwd(xr, 0, sp["lru_conv_w"], sp["lru_conv_b"], False, f"lru_conv_{tag}")
    a_f, u_f, a_b, u_b = rowwise(f_gates, [xc], sp["gate_pars"], [(LRU_HEAD, F32)] * 4, f"lru_gates_{tag}",
                                 tb=256, sub=256, ncol=D // LRU_HEAD)
    h_f, hs_f, fl_f = lru_scan(a_f, u_f, init["lru_f"], False, f"lru_scan_f_{tag}")
    _, hs_b, fl_b, r = lru_scan(a_b, u_b, init["lru_b"], True, f"lru_scan_b_{tag}", add=h_f)
    saved = dict(pg=pg, xbc=xbc, dtg=dtg, yf=yf, yb=yb, ent_f=ent_f, ent_b=ent_b, xr=xr, xc=xc,
                 a_f=a_f, a_b=a_b, hs_f=hs_f, hs_b=hs_b, r=r)
    return saved, dict(ssd_f=fin_f, ssd_b=fin_b, lru_f=fl_f, lru_b=fl_b)


def _seq_backward(sv, sp, lay, tag, grid_rows, dy, dxs_extra, d_r, dfin):
    D, W, GN = lay.D, lay.W, lay.GN
    gw = SSD_GROUPS * LANES
    pg = sv["pg"]
    dxf, ddt_f, db_f, dc_f, da_f, dh0_f = ssd_bwd(sv["xbc"], sv["dtg"], sp["a_g"], sv["ent_f"], dy, dfin["ssd_f"], False, W, f"ssd_scan_f_bwd_{tag}")
    dxb, ddt_b, db_b, dc_b, da_b, dh0_b = ssd_bwd(sv["xbc"], sv["dtg"], sp["a_g"], sv["ent_b"], dy, dfin["ssd_b"], True, W, f"ssd_scan_b_bwd_{tag}")
    cw, cb = sp["ssd_conv_w"], sp["ssd_conv_b"]
    d_xs, dw1, db1 = conv_bwd(pg, lay.xbc_blk, cw[:, :W], cb[:, :W], [dxf, dxb] + dxs_extra, True, f"ssd_conv_x_bwd_{tag}")
    d_b, dw2, db2 = conv_bwd(pg, lay.xbc_blk + W // CONV_CB, cw[:, W:W + GN], cb[:, W:W + GN], [db_f, db_b], True, f"ssd_conv_b_bwd_{tag}")
    d_c, dw3, db3 = conv_bwd(pg, lay.xbc_blk + (W + GN) // CONV_CB, cw[:, W + GN:], cb[:, W + GN:], [dc_f, dc_b], True, f"ssd_conv_c_bwd_{tag}")
    d_dtraw, d_dtbias = rowwise_vjp(f_dt, [Cols(pg, gw, lay.dt_blk)], [sp["dt_bias_g"]], [[ddt_f, ddt_b]], [F32], f"ssd_dt_bwd_{tag}")
    du_b, dab, dl0_b = lru_scan(sv["a_b"], d_r, dfin["lru_b"], False, f"lru_scan_b_bwd_{tag}", adjoint_of=sv["hs_b"])
    du_f, daf, dl0_f = lru_scan(sv["a_f"], d_r, dfin["lru_f"], True, f"lru_scan_f_bwd_{tag}", adjoint_of=sv["hs_f"])
    res = rowwise_vjp(f_gates, [sv["xc"]], sp["gate_pars"], [daf, du_f, dab, du_b], [F32], f"lru_gates_bwd_{tag}",
                      tb=256, sub=256, ncol=D // LRU_HEAD)
    d_xc, gate_grads = res[0], res[1:]
    d_xr, g_lcw, g_lcb = conv_bwd(sv["xr"], 0, sp["lru_conv_w"], sp["lru_conv_b"], [d_xc], False, f"lru_conv_bwd_{tag}")
    if grid_rows:
        d_xr = _to_row_major(d_xr, grid_rows)
    grads = dict(ssd_conv_w=jnp.concatenate([dw1, dw2, dw3], axis=1), ssd_conv_b=jnp.concatenate([db1, db2, db3], axis=1),
                 dt_bias_g=d_dtbias, a_g=da_f + da_b, lru_conv_w=g_lcw, lru_conv_b=g_lcb, gate_pars=list(gate_grads))
    pieces = dict(xr=d_xr, xs=d_xs, b=d_b, c=d_c, dt=d_dtraw)
    return pieces, grads, dict(ssd_f=dh0_f, ssd_b=dh0_b, lru_f=dl0_f, lru_b=dl0_b)


def _proj_cotangent(lay, px, pc, Lc, dz, dyr, dgs, dgr):
    zero = jnp.zeros((Lc, lay.D), BF16)
    cols = [(dz, zero), (px["xr"], pc["xr"]), (dyr, zero), (dgs, zero), (dgr, zero),
            (px["xs"], pc["xs"]), (px["b"], pc["b"]), (px["c"], pc["c"]), (px["dt"], pc["dt"])]
    return jnp.concatenate([jnp.concatenate([a.astype(BF16), b.astype(BF16)], axis=0) for a, b in cols], axis=1)


def _local_step(x, cvec, ctx, target, wb, ws):
    L, D = x.shape
    Lc = ctx.shape[0]
    nh = ws["ssd_d"].shape[0]
    hpg = nh // SSD_GROUPS
    W = nh * HEAD_DIM
    lay = _Layout(D, W, hpg)
    GN = lay.GN
    grid_rows = L // GRID_W
    nlh = D // LRU_HEAD

    w_in = wb["w_in"]
    o_dt, o_xr = 2 * W + 2 * GN, 2 * W + 2 * GN + 2 * nh
    w_all = jnp.concatenate([w_in[:, :W], w_in[:, o_xr:o_xr + D], w_in[:, o_xr + D:], wb["w_gate"], w_in[:, W:o_dt],
                             _heads_to_groups(w_in[:, o_dt:o_xr], hpg)], axis=1)
    a_neg = -jnp.exp(ws["ssd_a_log"])
    a_g = jnp.pad(_heads_to_groups(a_neg.reshape(-1), hpg).reshape(SSD_GROUPS, 1, LANES), [(0, 0), (0, SUBLANES - 1), (0, 0)])
    gate_pars = []
    for d in range(2):
        gate_pars += [ws["lru_w_a"][d], ws["lru_b_a"][d].reshape(nlh, 1, LRU_HEAD), ws["lru_w_x"][d],
                      ws["lru_b_x"][d].reshape(nlh, 1, LRU_HEAD), ws["lru_lambda"][d].reshape(nlh, 1, LRU_HEAD)]
    sp = dict(ssd_conv_w=ws["ssd_conv_w"], ssd_conv_b=_r1(ws["ssd_conv_b"]), dt_bias_g=_r1(_heads_to_groups(ws["ssd_dt_bias"].reshape(-1), hpg)),
              a_g=a_g, lru_conv_w=ws["lru_conv_w"], lru_conv_b=_r1(ws["lru_conv_b"]), gate_pars=gate_pars)
    dexp = _r1(jnp.repeat(ws["ssd_d"], HEAD_DIM))
    norm_mix, norm_ffn, ssd_norm, final_norm = _r1(ws["norm_mix"]), _r1(ws["norm_ffn"]), _r1(ws["ssd_norm"]), _r1(ws["final_norm"])
    bg_s, bg_r = _r1(ws["b_gate"][:D]), _r1(ws["b_gate"][D:])
    b_ada = _r1(ws["b_ada"])

    MR = 16
    cc = jnp.concatenate([cvec, _r1(ws["c_ctx"]), jnp.zeros((MR - 2, D), F32)], axis=0)
    s_cc = rowwise(f_silu, [cc], [], [(D, BF16)], "mod_silu", tb=MR)[0]
    modraw = matmul(s_cc, wb["w_ada"], "nn", F32, "mod_proj")
    mod = rowwise(f_add_bias, [modraw], [b_ada], [(6 * D, F32)], "mod_bias", tb=MR)[0]
    sh_m, sc_m, g_m, sh_f, sc_f, g_f = [mod[0:1, k * D:(k + 1) * D] for k in range(6)]
    csh_m, csc_m = mod[1:2, :D], mod[1:2, D:2 * D]

    zst = dict(ssd_f=jnp.zeros((SSD_GROUPS, hpg * HEAD_DIM, SSD_STATE), F32), ssd_b=jnp.zeros((SSD_GROUPS, hpg * HEAD_DIM, SSD_STATE), F32),
               lru_f=jnp.zeros((SUBLANES, D), F32), lru_b=jnp.zeros((SUBLANES, D), F32))
    hcb = rowwise(f_mod_in, [ctx], [norm_mix, csh_m, csc_m], [(D, BF16)], "norm_mix_ctx")[0]
    hb = rowwise(f_mod_in, [x], [norm_mix, sh_m, sc_m], [(D, BF16)], "norm_mix_x")[0]
    svc, fin_c = _seq_forward(hcb, w_all, sp, lay, "ctx", None, zst)
    svx, _ = _seq_forward(hb, w_all, sp, lay, "x", grid_rows, fin_c)
    pg = svx["pg"]
    r_rm = _to_row_major(svx["r"], grid_rows)
    gn_rows = [svx["yf"], svx["yb"], Cols(svx["xbc"], W, 0), Cols(pg, D, 0)]
    yn = rowwise(f_gnorm, gn_rows, [dexp, ssd_norm], [(W, BF16)], "ssd_gnorm")[0]
    o_s = matmul(yn, wb["w_out_ssd"], "nn", F32, "out_ssd")
    o_in = rowwise(f_gelu_gate, [r_rm, Cols(pg, D, 2)], [], [(D, BF16)], "lru_gelu")[0]
    o_r = matmul(o_in, wb["w_out_lru"], "nn", F32, "out_lru")
    mg_rows = [Cols(pg, D, 3), Cols(pg, D, 4), o_s, o_r]
    mixed = rowwise(f_merge, mg_rows, [bg_s, bg_r], [(D, BF16)], "merge")[0]
    out = matmul(mixed, wb["w_o"], "nn", F32, "out_proj")
    x1, h2 = rowwise(f_res_mod, [x, out], [g_m, norm_ffn, sh_f, sc_f], [(D, F32), (D, BF16)], "res_norm_ffn")
    gu = matmul(h2, wb["ffn_w13"], "nn", F32, "ffn_in")
    act = swiglu(gu, None, "ffn_act")
    f = matmul(act, wb["ffn_w2"], "nn", F32, "ffn_out")

    ones = jnp.full((L, LANES), 1.0 / LANES, F32)
    dx1a, df, d_gf, d_fnorm, lsum = rowwise_vjp(f_final, [x1, f, target], [g_f, final_norm], [ones], [F32, BF16, None],
                                                "final_loss", out_sums=[(0, LANES)])
    loss = lsum[0, 0]
    d_act = matmul(df, wb["ffn_w2"], "nt", F32, "ffn_out_dx")
    g_w2 = matmul(act, df, "tn", F32, "ffn_out_dw")
    d_gu = swiglu(gu, d_act, "ffn_act_bwd")
    dh2 = matmul(d_gu, wb["ffn_w13"], "nt", F32, "ffn_in_dx")
    g_w13 = matmul(h2, d_gu, "tn", F32, "ffn_in_dw")
    dxa, d_out, d_gm, d_nffn, d_shf, d_scf = rowwise_vjp(f_res_mod, [x, out], [g_m, norm_ffn, sh_f, sc_f], [dx1a, dh2],
                                                         [F32, BF16], "res_norm_ffn_bwd")
    d_mixed = matmul(d_out, wb["w_o"], "nt", F32, "out_proj_dx")
    g_wo = matmul(mixed, d_out, "tn", F32, "out_proj_dw")
    dgs, dgr, do_s, do_r, d_bgs, d_bgr = rowwise_vjp(f_merge, mg_rows, [bg_s, bg_r], [d_mixed], [BF16] * 4, "merge_bwd")
    d_yn = matmul(do_s, wb["w_out_ssd"], "nt", F32, "out_ssd_dx")
    g_wos = matmul(yn, do_s, "tn", F32, "out_ssd_dw")
    d_oin = matmul(do_r, wb["w_out_lru"], "nt", F32, "out_lru_dx")
    g_wol = matmul(o_in, do_r, "tn", F32, "out_lru_dw")
    d_r_rm, d_yr = rowwise_vjp(f_gelu_gate, [r_rm, Cols(pg, D, 2)], [], [d_oin], [F32, BF16], "lru_gelu_bwd")
    dy, dxs_skip, dz, d_dexp, d_ssdn = rowwise_vjp(f_gnorm, gn_rows, [dexp, ssd_norm], [d_yn], [F32, None, F32, BF16], "ssd_gnorm_bwd")
    zfin = dict(zst)
    px, gx, dst = _seq_backward(svx, sp, lay, "x", grid_rows, dy, [dxs_skip], _to_col_major(d_r_rm, grid_rows), zfin)
    pc, gc, _ = _seq_backward(svc, sp, lay, "ctx", None, jnp.zeros((Lc, W), F32), [], jnp.zeros((Lc, D), F32), dst)
    dpg = _proj_cotangent(lay, px, pc, Lc, dz, d_yr, dgs, dgr)
    g_wall = matmul(jnp.concatenate([hb, hcb], axis=0), dpg, "tn", F32, "proj_dw")
    dh = matmul(dpg, w_all, "nt", F32, "proj_dx_x", a_rows=(0, L))
    dhc = matmul(dpg, w_all, "nt", F32, "proj_dx_ctx", a_rows=(L, Lc))
    grad_x, d_nmix_x, d_shm, d_scm = rowwise_vjp(f_mod_in, [x], [norm_mix, sh_m, sc_m], [dh, dxa], [F32], "norm_mix_x_bwd")
    d_nmix_c, d_cshm, d_cscm = rowwise_vjp(f_mod_in, [ctx], [norm_mix, csh_m, csc_m], [dhc, jnp.zeros((Lc, D), F32)], [None], "norm_mix_ctx_bwd")

    dmod = jnp.concatenate([jnp.concatenate([d_shm, d_scm, d_gm, d_shf, d_scf, d_gf], axis=1),
                            jnp.concatenate([d_cshm, d_cscm, jnp.zeros((1, 4 * D), F32)], axis=1),
                            jnp.zeros((MR - 2, 6 * D), F32)], axis=0)
    d_modraw, g_b_ada = rowwise_vjp(f_add_bias, [modraw], [b_ada], [dmod], [BF16], "mod_bias_bwd", tb=MR)
    g_w_ada = matmul(s_cc, d_modraw, "tn", F32, "mod_proj_dw")
    d_scc = matmul(d_modraw, wb["w_ada"], "nt", F32, "mod_proj_dx")
    d_cc = rowwise_vjp(f_silu, [cc], [], [d_scc], [F32], "mod_silu_bwd", tb=MR)[0]

    xb0 = lay.xbc0
    g_dt_cols = _groups_to_heads(g_wall[:, lay.dt0:], hpg)
    g_w_in = jnp.concatenate([g_wall[:, :W], g_wall[:, xb0:xb0 + W + 2 * GN], g_dt_cols, g_wall[:, D:2 * D], g_wall[:, 2 * D:3 * D]], axis=1)
    d_a = _groups_to_heads((gx["a_g"] + gc["a_g"])[:, 0, :].reshape(-1), hpg).reshape(2, nh)
    gg = [a + b for a, b in zip(gx["gate_pars"], gc["gate_pars"])]

    def gate(k, shape):
        return jnp.stack([gg[k].reshape(shape), gg[5 + k].reshape(shape)], axis=0)

    grads = dict(
        c_ctx=d_cc[1], w_ada=g_w_ada, b_ada=g_b_ada.reshape(-1), norm_mix=(d_nmix_x + d_nmix_c).reshape(-1), norm_ffn=d_nffn.reshape(-1),
        w_in=g_w_in, ssd_conv_w=gx["ssd_conv_w"] + gc["ssd_conv_w"], ssd_conv_b=(gx["ssd_conv_b"] + gc["ssd_conv_b"]).reshape(-1),
        ssd_dt_bias=_groups_to_heads((gx["dt_bias_g"] + gc["dt_bias_g"]).reshape(-1), hpg).reshape(2, nh),
        ssd_a_log=d_a * a_neg, ssd_d=d_dexp.reshape(nh, HEAD_DIM).sum(axis=1), ssd_norm=d_ssdn.reshape(-1), w_out_ssd=g_wos,
        lru_conv_w=gx["lru_conv_w"] + gc["lru_conv_w"], lru_conv_b=(gx["lru_conv_b"] + gc["lru_conv_b"]).reshape(-1),
        lru_w_a=gate(0, (nlh, LRU_HEAD, LRU_HEAD)), lru_b_a=gate(1, (D,)), lru_w_x=gate(2, (nlh, LRU_HEAD, LRU_HEAD)),
        lru_b_x=gate(3, (D,)), lru_lambda=gate(4, (D,)), w_out_lru=g_wol, w_gate=g_wall[:, 3 * D:5 * D],
        b_gate=jnp.concatenate([d_bgs, d_bgr], axis=1).reshape(-1), w_o=g_wo, ffn_w13=g_w13, ffn_w2=g_w2, final_norm=d_fnorm.reshape(-1))
    return loss, grad_x, grads


WEIGHTS = ["c_ctx", "w_ada", "b_ada", "norm_mix", "norm_ffn", "w_in", "ssd_conv_w", "ssd_conv_b", "ssd_dt_bias", "ssd_a_log",
           "ssd_d", "ssd_norm", "w_out_ssd", "lru_conv_w", "lru_conv_b", "lru_w_a", "lru_b_a", "lru_w_x", "lru_b_x", "lru_lambda",
           "w_out_lru", "w_gate", "b_gate", "w_o", "ffn_w13", "ffn_w2", "final_norm"]
BIG = ["w_ada", "w_in", "w_out_ssd", "w_out_lru", "w_gate", "w_o", "ffn_w13", "ffn_w2"]
BIG_BY_COLUMNS = {"w_ada", "w_in", "w_gate", "ffn_w13"}
SMALL_SHARDED = ["ssd_conv_w", "lru_conv_w", "lru_b_a", "lru_b_x", "lru_lambda"]
ADAM_ROWS = 64


def _whole_from_shards(stack, by_columns):
    if by_columns:
        return stack.transpose(1, 0, 2).reshape(stack.shape[1], -1)
    return stack.reshape(-1, stack.shape[2])


def _shards_from_whole(g, by_columns):
    if by_columns:
        return g.reshape(g.shape[0], N_CHIP, -1).transpose(1, 0, 2)
    return g.reshape(N_CHIP, -1, g.shape[1])


def _adam(w, g, m, v, name):
    shape = w.shape
    cols = shape[-1]
    w2, g2, m2, v2 = [t.reshape(-1, cols) for t in (w, g, m, v)]
    d, nm, nv = rowwise(f_adam, [w2, g2, m2, v2], [], [(cols, F32)] * 3, name, tb=ADAM_ROWS, sub=8)
    return d.reshape(shape), nm.reshape(shape), nv.reshape(shape)


def kernel(x, c, ctx, c_ctx, w_ada, b_ada, norm_mix, norm_ffn, w_in, ssd_conv_w, ssd_conv_b, ssd_dt_bias, ssd_a_log, ssd_d, ssd_norm, w_out_ssd, lru_conv_w, lru_conv_b, lru_w_a, lru_b_a, lru_w_x, lru_b_x, lru_lambda, w_out_lru, w_gate, b_gate, w_o, ffn_w13, ffn_w2, final_norm, loss_target, m_c_ctx, m_w_ada, m_b_ada, m_norm_mix, m_norm_ffn, m_w_in, m_ssd_conv_w, m_ssd_conv_b, m_ssd_dt_bias, m_ssd_a_log, m_ssd_d, m_ssd_norm, m_w_out_ssd, m_lru_conv_w, m_lru_conv_b, m_lru_w_a, m_lru_b_a, m_lru_w_x, m_lru_b_x, m_lru_lambda, m_w_out_lru, m_w_gate, m_b_gate, m_w_o, m_ffn_w13, m_ffn_w2, m_final_norm, v_c_ctx, v_w_ada, v_b_ada, v_norm_mix, v_norm_ffn, v_w_in, v_ssd_conv_w, v_ssd_conv_b, v_ssd_dt_bias, v_ssd_a_log, v_ssd_d, v_ssd_norm, v_w_out_ssd, v_lru_conv_w, v_lru_conv_b, v_lru_w_a, v_lru_b_a, v_lru_w_x, v_lru_b_x, v_lru_lambda, v_w_out_lru, v_w_gate, v_b_gate, v_w_o, v_ffn_w13, v_ffn_w2, v_final_norm):
    given = dict(locals())
    layered = {n for n in WEIGHTS if n not in ("c_ctx", "final_norm")}
    w_blk = {n: (given[n][0] if n in layered else given[n]) for n in WEIGHTS}
    chip = _chip_index(_me())

    wb = {n: _whole_from_shards(gather_chips(w_blk[n].astype(BF16), "gather_" + n), n in BIG_BY_COLUMNS) for n in BIG}

    sm_shapes = [w_blk[n].shape for n in SMALL_SHARDED]
    sm_all = allgather8(_pack([w_blk[n] for n in SMALL_SHARDED], F32, ROW_TILE_16BIT), "gather_small")[::2]
    ws = {n: w_blk[n] for n in WEIGHTS if n not in BIG}
    for n, s in zip(SMALL_SHARDED, _unpack_stacked(sm_all, sm_shapes)):
        ws[n] = jnp.concatenate([s[k] for k in range(N_CHIP)], axis=-1)

    loss, grad_x, grads = _local_step(x[0], c, ctx[0], loss_target[0], wb, ws)
    loss = lax.psum(loss, ("x", "y", "c"))

    g_shard = {}
    for n in BIG:
        landed = scatter_halves(_shards_from_whole(grads[n], n in BIG_BY_COLUMNS).astype(BF16), "scatter_" + n)
        g_shard[n] = join_halves(sum_slots(landed, "sum_" + n), "join_" + n)

    small = [n for n in WEIGHTS if n not in BIG]
    sm_g = sum_slots(allgather8(_pack([grads[n] for n in small], F32, ROW_TILE_16BIT), "gather_small_grads"), "sum_small_grads")
    for n, g in zip(small, _unpack(sm_g, [grads[n].shape for n in small])):
        if n in SMALL_SHARDED:
            per = g.shape[-1] // N_CHIP
            g = lax.dynamic_slice_in_dim(g, chip * per, per, axis=g.ndim - 1)
        g_shard[n] = g

    out_g, out_d, out_m, out_v = {}, {}, {}, {}
    for n in BIG:
        out_g[n] = g_shard[n]
        out_d[n], out_m[n], out_v[n] = _adam(w_blk[n], g_shard[n], given["m_" + n][0], given["v_" + n][0], "adamw_" + n)
    sm_blk_shapes = [w_blk[n].shape for n in small]
    packed = [_pack([t[n].reshape(w_blk[n].shape) for n in small], F32, ADAM_ROWS)
              for t in (w_blk, g_shard, {n: given["m_" + n] for n in small}, {n: given["v_" + n] for n in small})]
    res = _adam(*packed, "adamw_small")
    for tgt, buf in zip((out_d, out_m, out_v), res):
        tgt.update(zip(small, _unpack(buf, sm_blk_shapes)))
    for n in small:
        out_g[n] = g_shard[n].reshape(w_blk[n].shape)

    def full(n, t):
        return t.reshape(given[n].shape)

    return (loss, grad_x[None], *[full(n, out_g[n]) for n in WEIGHTS], *[full(n, out_d[n]) for n in WEIGHTS],
            *[full(n, out_m[n]) for n in WEIGHTS], *[full(n, out_v[n]) for n in WEIGHTS])


def _unpack_stacked(buf, shapes):
    k = buf.shape[0]
    flat = buf.reshape(k, -1)
    out, at = [], 0
    for s in shapes:
        n = 1
        for d in s:
            n *= d
        out.append(flat[:, at:at + n].reshape((k,) + tuple(s)))
        at += n
    return out
```

```python
import functools

import jax
import jax.numpy as jnp
from jax import lax
from jax.experimental import pallas as pl
from jax.experimental.pallas import tpu as pltpu

F32 = jnp.float32
BF16 = jnp.bfloat16
MESH = pl.DeviceIdType.MESH

V7X_VMEM_LIMIT_BYTES = 56 * 1024 * 1024
LANES = 128
SUBLANES = 8

HEAD_DIM = 64
SSD_STATE = 128
SSD_GROUPS = 4
SSD_CHUNK = 128
GRID_W = 64
LRU_HEAD = 128
LRU_C = 8.0
EPS = 1e-6
NEG = -1e30

ADAM_LR = 0.001
ADAM_B1 = 0.9
ADAM_B2 = 0.999
ADAM_EPS = 1e-08
ADAM_WD = 0.01
ADAM_STEP = 10


def _cparams(sem=None, **kw):
    if sem is not None:
        kw["dimension_semantics"] = sem
    return pltpu.CompilerParams(vmem_limit_bytes=V7X_VMEM_LIMIT_BYTES, **kw)


_DN = {"nn": (((1,), (0,)), ((), ())), "nt": (((1,), (1,)), ((), ())), "tn": (((0,), (0,)), ((), ()))}


def _raw_dot(a, b, form, precision=None):
    return lax.dot_general(a, b, _DN[form], precision=precision, preferred_element_type=F32)


def _dot_bwd_forms(form):
    return {"nn": (("g", "b", "nt"), ("a", "g", "tn")),
            "nt": (("g", "b", "nn"), ("g", "a", "tn")),
            "tn": (("b", "g", "nt"), ("a", "g", "nn"))}[form]


def _make_dot(rounding):
    @functools.partial(jax.custom_vjp, nondiff_argnums=(2,))
    def dot(a, b, form):
        if rounding is None:
            return _raw_dot(a, b, form, precision=lax.Precision.HIGHEST)
        return _raw_dot(a.astype(rounding), b.astype(rounding), form)

    def fwd(a, b, form):
        return dot(a, b, form), (a, b)

    def bwd(form, res, g):
        a, b = res
        ops = {"a": a, "b": b, "g": g}
        (l1, r1, f1), (l2, r2, f2) = _dot_bwd_forms(form)
        return dot(ops[l1], ops[r1], f1).astype(a.dtype), dot(ops[l2], ops[r2], f2).astype(b.dtype)

    dot.defvjp(fwd, bwd)
    return dot


bdot = _make_dot(BF16)
hdot = _make_dot(None)


def _sigmoid(v):
    return 1.0 / (1.0 + jnp.exp(-v))


def _silu(v):
    return v * _sigmoid(v)


def _softplus(v):
    return jnp.maximum(v, 0.0) + jnp.log1p(jnp.exp(-jnp.abs(v)))


def _gelu_tanh(v):
    return 0.5 * v * (1.0 + jnp.tanh(0.7978845608028654 * (v + 0.044715 * v * v * v)))


def _neg_expm1(v):
    poly = -v * (1.0 + v * (0.5 + v * (1.0 / 6.0 + v * (1.0 / 24.0 + v * (1.0 / 120.0)))))
    return jnp.where(v > -0.05, poly, 1.0 - jnp.exp(v))


def _rms(t, gain):
    return t * lax.rsqrt(jnp.mean(t * t, axis=-1, keepdims=True) + EPS) * gain


def _pick(dim, want, mult=1):
    if dim <= want:
        return dim
    for t in range(want - want % mult, 0, -mult):
        if dim % t == 0:
            return t
    t = min(dim, want)
    while dim % t:
        t //= 2
    return t


MM_TILE_M = 1024
MM_TILE_N = 1536
MM_VMEM_BYTES = 46 * 1024 * 1024


def matmul(a, b, form, out_dtype, name, tm=MM_TILE_M, tn=MM_TILE_N, tk=None, a_rows=None, carry=None):
    if form == "nn":
        (m, k), (k2, n) = a.shape, b.shape
    elif form == "nt":
        (m, k), (n, k2) = a.shape, b.shape
    else:
        (k, m), (k2, n) = a.shape, b.shape
    assert k == k2, (a.shape, b.shape, form)
    row0 = 0
    if a_rows is not None:
        assert form != "tn"
        row0, m = a_rows
    tm, tn = _pick(m, tm, LANES), _pick(n, tn, LANES)
    assert row0 % tm == 0
    blk0 = row0 // tm
    if tk is None:
        fixed = tm * tn * (2 * jnp.dtype(out_dtype).itemsize + 4)
        tk = (MM_VMEM_BYTES - fixed) // (2 * a.dtype.itemsize * (tm + tn))
    tk = _pick(k, tk, LANES)
    nk = k // tk

    grid = (m // tm, n // tn, nk)
    n_ci, n_co = (len(carry.ins), len(carry.outs)) if carry else (0, 0)

    def body(a_ref, b_ref, *rest):
        c_ins, o_ref, c_outs = rest[:n_ci], rest[n_ci], rest[n_ci + 1:n_ci + 1 + n_co]
        scratch = rest[n_ci + 1 + n_co:]
        kk = pl.program_id(2)
        if carry:
            at_step = [pl.program_id(d) for d in range(3)]
            sems = tuple(scratch[-2:])
            pl.when((at_step[0] == 0) & (at_step[1] == 0) & (at_step[2] == 0))(lambda: carry.start(c_ins, c_outs, sems))
        part = _raw_dot(a_ref[...], b_ref[...], form)
        if nk == 1:
            o_ref[...] = part.astype(o_ref.dtype)
        else:
            acc = scratch[0]

            @pl.when(kk == 0)
            def _():
                acc[...] = part

            @pl.when(kk > 0)
            def _():
                acc[...] += part

            @pl.when(kk == nk - 1)
            def _():
                o_ref[...] = acc[...].astype(o_ref.dtype)

        if carry:
            last = (at_step[0] == grid[0] - 1) & (at_step[1] == grid[1] - 1) & (at_step[2] == grid[2] - 1)
            pl.when(last)(lambda: carry.finish(c_ins, c_outs, sems))

    a_spec = pl.BlockSpec((tk, tm), lambda i, j, kk: (kk, i)) if form == "tn" else pl.BlockSpec((tm, tk), lambda i, j, kk: (blk0 + i, kk))
    b_spec = pl.BlockSpec((tn, tk), lambda i, j, kk: (j, kk)) if form == "nt" else pl.BlockSpec((tk, tn), lambda i, j, kk: (kk, j))
    res = pl.pallas_call(
        body, name=name, grid=grid, in_specs=[a_spec, b_spec] + [_HBM] * n_ci,
        out_specs=[pl.BlockSpec((tm, tn), lambda i, j, kk: (i, j))] + [_HBM] * n_co,
        out_shape=[jax.ShapeDtypeStruct((m, n), out_dtype)] + (carry.outs if carry else []),
        scratch_shapes=([pltpu.VMEM((tm, tn), F32)] if nk > 1 else []) + (carry.sem_shapes() if carry else []),
        compiler_params=_cparams(("arbitrary",) * 3 if carry else ("parallel", "parallel", "arbitrary")),
    )(a, b, *(carry.ins if carry else []))
    return (res[0], list(res[1:])) if carry else res[0]


class Cols:
    def __init__(self, arr, w, j):
        assert (j + 1) * w <= arr.shape[1]
        self.arr, self.w, self.j = arr, w, j


def _row_width(x, ncol):
    return x.w if isinstance(x, Cols) else x.shape[1] // ncol


def _row_spec(x, tb, ncol):
    if isinstance(x, Cols):
        j0 = x.j
        return x.arr, pl.BlockSpec((tb, x.w), lambda j, i: (i, j0 + j))
    return x, pl.BlockSpec((tb, x.shape[1] // ncol), lambda j, i: (i, j))


def _par_spec(shape, ncol):
    if ncol == 1:
        return pl.BlockSpec(shape, lambda j, i, nd=len(shape): (0,) * nd)
    assert len(shape) == 3 and shape[0] == ncol, shape
    return pl.BlockSpec((1,) + tuple(shape[1:]), lambda j, i: (j, 0, 0))


def _par_value(ref, ncol):
    return ref[...] if ncol == 1 else ref[0]


def rowwise(fn, rows, pars, outs, name, tb=256, sub=16, ncol=1):
    m = (rows[0].arr if isinstance(rows[0], Cols) else rows[0]).shape[0]
    tb = _pick(m, tb)
    sub = min(sub, tb)
    nr, npar = len(rows), len(pars)

    def body(*refs):
        row_refs, par_refs, out_refs = refs[:nr], refs[nr:nr + npar], refs[nr + npar:]
        pv = [_par_value(p, ncol) for p in par_refs]

        def step(i, carry):
            sl = pl.ds(pl.multiple_of(i * sub, sub), sub)
            res = fn(*[r[sl, :].astype(F32) for r in row_refs], *pv)
            for o, v in zip(out_refs, res):
                o[sl, :] = v.astype(o.dtype)
            return carry

        lax.fori_loop(0, tb // sub, step, 0)

    arrs, specs = zip(*[_row_spec(r, tb, ncol) for r in rows])
    return pl.pallas_call(
        body, name=name, grid=(ncol, m // tb),
        in_specs=list(specs) + [_par_spec(p.shape, ncol) for p in pars],
        out_specs=[pl.BlockSpec((tb, w), lambda j, i: (i, j)) for w, _ in outs],
        out_shape=[jax.ShapeDtypeStruct((m, w * ncol), dt) for w, dt in outs],
        compiler_params=_cparams(("parallel", "parallel")),
    )(*arrs, *pars)


def rowwise_vjp(fn, rows, pars, cots, drow_dtypes, name, tb=256, sub=16, out_sums=(), ncol=1):
    m = (rows[0].arr if isinstance(rows[0], Cols) else rows[0]).shape[0]
    tb = _pick(m, tb)
    sub = min(sub, tb)
    cots = [list(c) if isinstance(c, (list, tuple)) else [c] for c in cots]
    flat = [c for group in cots for c in group]
    nr, npar, nc = len(rows), len(pars), len(flat)
    want = [i for i, d in enumerate(drow_dtypes) if d is not None]
    assert ncol == 1 or not out_sums

    def body(*refs):
        row_refs, par_refs = refs[:nr], refs[nr:nr + npar]
        cot_refs = list(refs[nr + npar:nr + npar + nc])
        drow_refs = refs[nr + npar + nc:nr + npar + nc + len(want)]
        acc_refs = refs[nr + npar + nc + len(want):]
        pv = [_par_value(p, ncol) for p in par_refs]

        def step(i, acc):
            sl = pl.ds(pl.multiple_of(i * sub, sub), sub)
            rv = [r[sl, :].astype(F32) for r in row_refs]
            res, vjp = jax.vjp(lambda rr, pp: tuple(fn(*rr, *pp)), rv, pv)
            ct, at = [], 0
            for group in cots:
                ct.append(sum(c[sl, :].astype(F32) for c in cot_refs[at:at + len(group)]))
                at += len(group)
            d_rows, d_pars = vjp(tuple(ct))
            for o, idx in zip(drow_refs, want):
                o[sl, :] = d_rows[idx].astype(o.dtype)
            sums = [jnp.sum(res[k], axis=0, keepdims=True) for k, _ in out_sums]
            return tuple(a + d for a, d in zip(acc, list(d_pars) + sums))

        init = tuple(jnp.zeros(p.shape, F32) for p in pv) + tuple(jnp.zeros((1, w), F32) for _, w in out_sums)
        acc = lax.fori_loop(0, tb // sub, step, init)

        @pl.when(pl.program_id(1) == 0)
        def _():
            for o in acc_refs:
                o[...] = jnp.zeros_like(o)

        for o, a in zip(acc_refs, acc):
            if ncol == 1:
                o[...] += a
            else:
                o[0] += a

    arrs, specs = zip(*[_row_spec(r, tb, ncol) for r in rows])
    carrs, cspecs = zip(*[_row_spec(c, tb, ncol) for c in flat])
    widths = [_row_width(r, ncol) for r in rows]
    acc_shapes = [tuple(p.shape) for p in pars] + [(1, w) for _, w in out_sums]
    return pl.pallas_call(
        body, name=name, grid=(ncol, m // tb),
        in_specs=list(specs) + [_par_spec(p.shape, ncol) for p in pars] + list(cspecs),
        out_specs=[pl.BlockSpec((tb, widths[i]), lambda j, i_: (i_, j)) for i in want]
        + [_par_spec(s, ncol) for s in acc_shapes],
        out_shape=[jax.ShapeDtypeStruct((m, widths[i] * ncol), drow_dtypes[i]) for i in want]
        + [jax.ShapeDtypeStruct(s, F32) for s in acc_shapes],
        compiler_params=_cparams(("parallel", "arbitrary")),
    )(*arrs, *pars, *carrs)


def _col(v, c):
    lane = lax.broadcasted_iota(jnp.int32, v.shape, 1)
    return jnp.sum(jnp.where(lane == c, v, 0.0), axis=1, keepdims=True)


def _row(v, r):
    sub = lax.broadcasted_iota(jnp.int32, v.shape, 0)
    return jnp.sum(jnp.where(sub == r, v, 0.0), axis=0, keepdims=True)


def _ssd_chunk(xs, dt, bm, cm, hs, a_row, rev, col0):
    q = dt.shape[0]
    ii = lax.broadcasted_iota(jnp.int32, (q, q), 0)
    jj = lax.broadcasted_iota(jnp.int32, (q, q), 1)
    keep = (jj >= ii) if rev else (jj <= ii)
    tri = keep.astype(F32)
    dta = dt * a_row
    a_cum = hdot(tri, dta, "nn")
    tri_t = ((jj <= ii) if rev else (jj >= ii)).astype(F32)
    a_cum_t = hdot(dta, tri_t, "tn")
    cb = bdot(cm, bm, "nt")
    last = 0 if rev else q - 1
    ys, hn = [], []
    for e, (x, h) in enumerate(zip(xs, hs)):
        c = col0 + e
        ac = _col(a_cum, c)
        seg = ac - _row(a_cum_t, c)
        lm = jnp.exp(jnp.where(keep, seg, NEG))
        xdt = x * _col(dt, c)
        y_diag = bdot(cb * lm, xdt, "nn")
        y_off = bdot(cm * jnp.exp(ac), h, "nt")
        ys.append(y_diag + y_off)
        tot = _row(ac, last)
        states = bdot(xdt, bm * jnp.exp(tot - ac), "tn")
        hn.append(jnp.exp(tot) * h + states)
    return ys, hn


def _ssd_specs(nc, hpg, w_ssd, rev_order):
    q = SSD_CHUNK
    wx = hpg * HEAD_DIM
    boff = w_ssd // LANES

    def cidx(c):
        return nc - 1 - c if rev_order else c

    x_spec = pl.BlockSpec((q, wx), lambda g, c: (cidx(c), g))
    dt_spec = pl.BlockSpec((q, LANES), lambda g, c: (cidx(c), g))
    b_spec = pl.BlockSpec((q, LANES), lambda g, c: (cidx(c), boff + g))
    c_spec = pl.BlockSpec((q, LANES), lambda g, c: (cidx(c), boff + SSD_GROUPS + g))
    a_spec = pl.BlockSpec((1, SUBLANES, LANES), lambda g, c: (g, 0, 0))
    st_spec = pl.BlockSpec((1, wx, LANES), lambda g, c: (g, 0, 0))
    ent_spec = pl.BlockSpec((1, 1, wx, LANES), lambda g, c: (g, cidx(c), 0, 0))
    return x_spec, dt_spec, b_spec, c_spec, a_spec, st_spec, ent_spec


def ssd_fwd(xbc, dtg, a_g, h0, rev, w_ssd, name):
    L = xbc.shape[0]
    nc = L // SSD_CHUNK
    hpg = w_ssd // (SSD_GROUPS * HEAD_DIM)
    wx = hpg * HEAD_DIM
    col0 = hpg if rev else 0
    x_spec, dt_spec, b_spec, c_spec, a_spec, st_spec, ent_spec = _ssd_specs(nc, hpg, w_ssd, rev)

    def body(x_ref, dt_ref, b_ref, c_ref, a_ref, h0_ref, y_ref, ent_ref, fin_ref, hs):
        c = pl.program_id(1)

        @pl.when(c == 0)
        def _():
            hs[...] = h0_ref[0]

        ent_ref[0, 0] = hs[...]
        xs = [x_ref[:, e * HEAD_DIM:(e + 1) * HEAD_DIM] for e in range(hpg)]
        hin = [hs[e * HEAD_DIM:(e + 1) * HEAD_DIM, :] for e in range(hpg)]
        ys, hn = _ssd_chunk(xs, dt_ref[...], b_ref[...], c_ref[...], hin, a_ref[0, 0:1, :], rev, col0)
        for e in range(hpg):
            y_ref[:, e * HEAD_DIM:(e + 1) * HEAD_DIM] = ys[e]
            hs[e * HEAD_DIM:(e + 1) * HEAD_DIM, :] = hn[e]
        fin_ref[0] = hs[...]

    return pl.pallas_call(
        body, name=name, grid=(SSD_GROUPS, nc),
        in_specs=[x_spec, dt_spec, b_spec, c_spec, a_spec, st_spec],
        out_specs=[x_spec, ent_spec, st_spec],
        out_shape=[jax.ShapeDtypeStruct((L, w_ssd), F32),
                   jax.ShapeDtypeStruct((SSD_GROUPS, nc, wx, LANES), F32),
                   jax.ShapeDtypeStruct((SSD_GROUPS, wx, LANES), F32)],
        scratch_shapes=[pltpu.VMEM((wx, LANES), F32)],
        compiler_params=_cparams(("parallel", "arbitrary")),
    )(xbc, dtg, xbc, xbc, a_g, h0)


def ssd_bwd(xbc, dtg, a_g, ent, dy, dfin, rev, w_ssd, name, carry=None):
    L = xbc.shape[0]
    nc = L // SSD_CHUNK
    hpg = w_ssd // (SSD_GROUPS * HEAD_DIM)
    wx = hpg * HEAD_DIM
    col0 = hpg if rev else 0
    x_spec, dt_spec, b_spec, c_spec, a_spec, st_spec, ent_spec = _ssd_specs(nc, hpg, w_ssd, not rev)
    bc_out = pl.BlockSpec((SSD_CHUNK, LANES), lambda g, c: ((c if rev else nc - 1 - c), g))
    n_ci, n_co = (len(carry.ins), len(carry.outs)) if carry else (0, 0)

    def body(x_ref, dt_ref, b_ref, c_ref, a_ref, ent_ref, dy_ref, dfin_ref, *rest):
        c_ins = rest[:n_ci]
        dx_ref, ddt_ref, db_ref, dc_ref, da_ref, dh0_ref = rest[n_ci:n_ci + 6]
        c_outs = rest[n_ci + 6:n_ci + 6 + n_co]
        dhs = rest[n_ci + 6 + n_co]
        sems = tuple(rest[n_ci + 7 + n_co:])
        g, c = pl.program_id(0), pl.program_id(1)
        if carry:
            pl.when((g == 0) & (c == 0))(lambda: carry.start(c_ins, c_outs, sems))

        @pl.when(c == 0)
        def _():
            dhs[...] = dfin_ref[0]
            da_ref[...] = jnp.zeros_like(da_ref)

        xs = [x_ref[:, e * HEAD_DIM:(e + 1) * HEAD_DIM] for e in range(hpg)]
        hin = [ent_ref[0, 0, e * HEAD_DIM:(e + 1) * HEAD_DIM, :] for e in range(hpg)]
        _, vjp = jax.vjp(lambda *a: _ssd_chunk(*a, rev, col0), xs, dt_ref[...], b_ref[...], c_ref[...], hin, a_ref[0, 0:1, :])
        dys = [dy_ref[:, e * HEAD_DIM:(e + 1) * HEAD_DIM] for e in range(hpg)]
        dhn = [dhs[e * HEAD_DIM:(e + 1) * HEAD_DIM, :] for e in range(hpg)]
        dxs, ddt, db, dc, dh, da = vjp((dys, dhn))
        for e in range(hpg):
            dx_ref[:, e * HEAD_DIM:(e + 1) * HEAD_DIM] = dxs[e]
            dhs[e * HEAD_DIM:(e + 1) * HEAD_DIM, :] = dh[e]
        ddt_ref[...] = ddt
        db_ref[...] = db
        dc_ref[...] = dc
        da_ref[0, 0:1, :] += da
        dh0_ref[0] = dhs[...]
        if carry:
            pl.when((g == SSD_GROUPS - 1) & (c == nc - 1))(lambda: carry.finish(c_ins, c_outs, sems))

    res = pl.pallas_call(
        body, name=name, grid=(SSD_GROUPS, nc),
        in_specs=[x_spec, dt_spec, b_spec, c_spec, a_spec, ent_spec, x_spec, st_spec] + [_HBM] * n_ci,
        out_specs=[x_spec, dt_spec, bc_out, bc_out, a_spec, st_spec] + [_HBM] * n_co,
        out_shape=[jax.ShapeDtypeStruct((L, w_ssd), F32),
                   jax.ShapeDtypeStruct((L, SSD_GROUPS * LANES), F32),
                   jax.ShapeDtypeStruct((L, SSD_GROUPS * LANES), F32),
                   jax.ShapeDtypeStruct((L, SSD_GROUPS * LANES), F32),
                   jax.ShapeDtypeStruct((SSD_GROUPS, SUBLANES, LANES), F32),
                   jax.ShapeDtypeStruct((SSD_GROUPS, wx, LANES), F32)] + (carry.outs if carry else []),
        scratch_shapes=[pltpu.VMEM((wx, LANES), F32)] + (carry.sem_shapes() if carry else []),
        compiler_params=_cparams(("arbitrary", "arbitrary") if carry else ("parallel", "arbitrary")),
    )(xbc, dtg, xbc, xbc, a_g, ent, dy, dfin, *(carry.ins if carry else []))
    return (list(res[:6]), list(res[6:])) if carry else res


CONV_W = 4
CONV_LEFT = 2
CONV_CB = 128
CONV_ROWS = 256
HALO = SUBLANES


def _conv_window(ref, r, rows, nch, L):
    s = pl.multiple_of(r * rows, rows)
    cur = ref[pl.ds(s, rows), :]
    sp = pl.multiple_of(jnp.maximum(s - HALO, 0), HALO)
    sn = pl.multiple_of(jnp.minimum(s + rows, L - HALO), HALO)
    prev = jnp.where(r > 0, ref[pl.ds(sp, HALO), :], 0.0)
    nxt = jnp.where(r < nch - 1, ref[pl.ds(sn, HALO), :], 0.0)
    return jnp.concatenate([prev, cur, nxt], axis=0)


def _shifted(win, off, rows):
    n = win.shape[0]
    return pltpu.roll(win, (-off) % n, axis=0)[HALO:HALO + rows, :]


def _dsilu(p):
    s = _sigmoid(p)
    return s * (1.0 + p * (1.0 - s))


def conv_fwd(src, col_blk0, w, b, act, name):
    L, C = src.shape[0], w.shape[1]
    rows = min(CONV_ROWS, L)
    nch = L // rows

    def body(x_ref, w_ref, b_ref, o_ref):
        def chunk(r, carry):
            win = _conv_window(x_ref, r, rows, nch, L)
            pre = b_ref[...] + sum(w_ref[k:k + 1, :] * _shifted(win, k - CONV_LEFT, rows) for k in range(CONV_W))
            o_ref[pl.ds(pl.multiple_of(r * rows, rows), rows), :] = _silu(pre) if act else pre
            return carry

        lax.fori_loop(0, nch, chunk, 0)

    return pl.pallas_call(
        body, name=name, grid=(C // CONV_CB,),
        in_specs=[pl.BlockSpec((L, CONV_CB), lambda j: (0, col_blk0 + j)),
                  pl.BlockSpec((CONV_W, CONV_CB), lambda j: (0, j)),
                  pl.BlockSpec((1, CONV_CB), lambda j: (0, j))],
        out_specs=pl.BlockSpec((L, CONV_CB), lambda j: (0, j)),
        out_shape=jax.ShapeDtypeStruct((L, C), F32),
        compiler_params=_cparams(("parallel",)),
    )(src, w, b)


def conv_bwd(src, col_blk0, w, b, douts, act, name):
    L, C = src.shape[0], w.shape[1]
    rows = min(CONV_ROWS, L)
    nch = L // rows
    nd = len(douts)

    def body(*refs):
        x_ref, w_ref, b_ref = refs[:3]
        d_refs = refs[3:3 + nd]
        dx_ref, dw_ref, db_ref, dp = refs[3 + nd:]

        def pass1(r, acc):
            sl = pl.ds(pl.multiple_of(r * rows, rows), rows)
            win = _conv_window(x_ref, r, rows, nch, L)
            taps = [_shifted(win, k - CONV_LEFT, rows) for k in range(CONV_W)]
            dpre = sum(d[sl, :] for d in d_refs)
            if act:
                pre = b_ref[...] + sum(w_ref[k:k + 1, :] * taps[k] for k in range(CONV_W))
                dpre = dpre * _dsilu(pre)
            dp[sl, :] = dpre
            new = [acc[k] + jnp.sum(dpre * taps[k], axis=0, keepdims=True) for k in range(CONV_W)]
            return tuple(new) + (acc[CONV_W] + jnp.sum(dpre, axis=0, keepdims=True),)

        zero = jnp.zeros((1, CONV_CB), F32)
        acc = lax.fori_loop(0, nch, pass1, (zero,) * (CONV_W + 1))
        for k in range(CONV_W):
            dw_ref[k:k + 1, :] = acc[k]
        db_ref[...] = acc[CONV_W]

        def pass2(r, carry):
            win = _conv_window(dp, r, rows, nch, L)
            dx = sum(w_ref[k:k + 1, :] * _shifted(win, CONV_LEFT - k, rows) for k in range(CONV_W))
            dx_ref[pl.ds(pl.multiple_of(r * rows, rows), rows), :] = dx
            return carry

        lax.fori_loop(0, nch, pass2, 0)

    col = pl.BlockSpec((L, CONV_CB), lambda j: (0, j))
    return pl.pallas_call(
        body, name=name, grid=(C // CONV_CB,),
        in_specs=[pl.BlockSpec((L, CONV_CB), lambda j: (0, col_blk0 + j)),
                  pl.BlockSpec((CONV_W, CONV_CB), lambda j: (0, j)),
                  pl.BlockSpec((1, CONV_CB), lambda j: (0, j))] + [col] * nd,
        out_specs=[col, pl.BlockSpec((CONV_W, CONV_CB), lambda j: (0, j)), pl.BlockSpec((1, CONV_CB), lambda j: (0, j))],
        out_shape=[jax.ShapeDtypeStruct((L, C), F32), jax.ShapeDtypeStruct((CONV_W, C), F32), jax.ShapeDtypeStruct((1, C), F32)],
        scratch_shapes=[pltpu.VMEM((L, CONV_CB), F32)],
        compiler_params=_cparams(("parallel",)),
    )(src, w, b, *douts)


SCAN_ROWS = 512
SCAN_CB = 512


def _scan8(a, u, rev):
    sub = lax.broadcasted_iota(jnp.int32, a.shape, 0)
    for s in (1, 2, 4):
        if rev:
            ok = sub < SUBLANES - s
            a_sh, u_sh = pltpu.roll(a, SUBLANES - s, axis=0), pltpu.roll(u, SUBLANES - s, axis=0)
        else:
            ok = sub >= s
            a_sh, u_sh = pltpu.roll(a, s, axis=0), pltpu.roll(u, s, axis=0)
        u = a * jnp.where(ok, u_sh, 0.0) + u
        a = a * jnp.where(ok, a_sh, 1.0)
    return a, u


def _shift8(h, carry, rev):
    sub = lax.broadcasted_iota(jnp.int32, h.shape, 0)
    if rev:
        return jnp.where(sub == SUBLANES - 1, carry, pltpu.roll(h, SUBLANES - 1, axis=0))
    return jnp.where(sub == 0, carry, pltpu.roll(h, 1, axis=0))


def lru_scan(a, u, c0, rev, name, adjoint_of=None, add=None):
    L, C = a.shape
    tc, cb = _pick(L, SCAN_ROWS), _pick(C, SCAN_CB)
    nt = L // tc
    ng = tc // SUBLANES
    adj = adjoint_of is not None
    n_in = 3 + (1 if adj else 0) + (1 if add is not None else 0)

    def body(*refs):
        a_ref, u_ref, c0_ref = refs[:3]
        x_ref = refs[3] if (adj or add is not None) else None
        outs = refs[n_in:-1]
        st = refs[-1]

        @pl.when(pl.program_id(1) == 0)
        def _():
            st[...] = c0_ref[...]

        def group(i, carry):
            g = (ng - 1 - i) if rev else i
            sl = pl.ds(pl.multiple_of(g * SUBLANES, SUBLANES), SUBLANES)
            av, uv = a_ref[sl, :], u_ref[sl, :]
            pa, h = _scan8(av, av * uv if adj else uv, rev)
            h = pa * carry + h
            hs = _shift8(h, carry, rev)
            if adj:
                lam = uv + hs
                outs[0][sl, :] = lam
                outs[1][sl, :] = lam * x_ref[sl, :]
            else:
                outs[0][sl, :] = h
                outs[1][sl, :] = hs
                if add is not None:
                    outs[3][sl, :] = h + x_ref[sl, :]
            last = 0 if rev else SUBLANES - 1
            return jnp.broadcast_to(h[last:last + 1, :], h.shape)

        st[...] = lax.fori_loop(0, ng, group, st[...])
        outs[2][...] = st[...]

    def tmap(j, t):
        return ((nt - 1 - t) if rev else t, j)

    blk = pl.BlockSpec((tc, cb), tmap)
    vec = pl.BlockSpec((SUBLANES, cb), lambda j, t: (0, j))
    args = [a, u, c0] + ([adjoint_of] if adj else []) + ([add] if add is not None else [])
    n_big = 2 if (adj or add is None) else 3
    out_specs = [blk, blk, vec] + ([blk] if n_big == 3 else [])
    out_shape = [jax.ShapeDtypeStruct((L, C), F32), jax.ShapeDtypeStruct((L, C), F32), jax.ShapeDtypeStruct((SUBLANES, C), F32)]
    out_shape += [jax.ShapeDtypeStruct((L, C), F32)] if n_big == 3 else []
    return pl.pallas_call(
        body, name=name, grid=(C // cb, nt),
        in_specs=[blk, blk, vec] + [blk] * (n_in - 3),
        out_specs=out_specs, out_shape=out_shape,
        scratch_shapes=[pltpu.VMEM((SUBLANES, cb), F32)],
        compiler_params=_cparams(("parallel", "arbitrary")),
    )(*args)


def f_silu(v):
    return (_silu(v),)


def f_add_bias(v, b):
    return (v + b,)


def f_mod_in(x, gain, shift, scale):
    return _rms(x, gain) * (1.0 + scale) + shift, x


def f_dt(raw, bias):
    return (_softplus(raw + bias),)


def f_gnorm(yf, yb, xs, z, dexp, gain):
    return (_rms((yf + yb + dexp * xs) * _silu(z), gain),)


def f_gelu_gate(r, yr):
    return (r * _gelu_tanh(yr),)


def f_merge(gs, gr, o_s, o_r, bs, br):
    return (_sigmoid(gs + bs) * o_s + _sigmoid(gr + br) * o_r,)


def f_res_mod(x, out, gm, gain, shift, scale):
    x1 = x + gm * out
    return x1, _rms(x1, gain) * (1.0 + scale) + shift


def f_final(x1, f, t, gf, gain):
    e = _rms(x1 + gf * f, gain) - t
    return (jnp.broadcast_to(0.5 * jnp.mean(e * e, axis=-1, keepdims=True), (e.shape[0], LANES)),)


def _lru_coeffs(xh, wa, ba, wx, bx, lam):
    r = _sigmoid(bdot(xh, wa, "nn") + ba)
    i = _sigmoid(bdot(xh, wx, "nn") + bx)
    log_a = -LRU_C * r * _softplus(-lam)
    return jnp.exp(log_a), jnp.sqrt(_neg_expm1(2.0 * log_a)) * (i * xh)


def f_gates(xh, wa_f, ba_f, wx_f, bx_f, lam_f, wa_b, ba_b, wx_b, bx_b, lam_b):
    return _lru_coeffs(xh, wa_f, ba_f, wx_f, bx_f, lam_f) + _lru_coeffs(xh, wa_b, ba_b, wx_b, bx_b, lam_b)


def f_adam(w, g, m, v):
    m2 = ADAM_B1 * m + (1.0 - ADAM_B1) * g
    v2 = ADAM_B2 * v + (1.0 - ADAM_B2) * (g * g)
    m_hat = m2 / (1.0 - ADAM_B1 ** ADAM_STEP)
    v_hat = v2 / (1.0 - ADAM_B2 ** ADAM_STEP)
    return -ADAM_LR * (m_hat / (jnp.sqrt(v_hat) + ADAM_EPS) + ADAM_WD * w), m2, v2


def swiglu(gu, dact, name, tb=128, sub=16):
    L, f2 = gu.shape
    f = f2 // 2
    tb = _pick(L, tb)

    def body(*refs):
        gu_ref, o_ref = refs[0], refs[-1]

        def step(i, carry):
            sl = pl.ds(pl.multiple_of(i * sub, sub), sub)
            g, u = gu_ref[sl, :f], gu_ref[sl, f:]
            if dact is None:
                o_ref[sl, :] = (_silu(g) * u).astype(o_ref.dtype)
            else:
                d = refs[1][sl, :]
                o_ref[sl, :f] = (d * u * _dsilu(g)).astype(o_ref.dtype)
                o_ref[sl, f:] = (d * _silu(g)).astype(o_ref.dtype)
            return carry

        lax.fori_loop(0, tb // sub, step, 0)

    wout = f if dact is None else f2
    ins = [gu] + ([] if dact is None else [dact])
    return pl.pallas_call(
        body, name=name, grid=(L // tb,),
        in_specs=[pl.BlockSpec((tb, a.shape[1]), lambda i: (i, 0)) for a in ins],
        out_specs=pl.BlockSpec((tb, wout), lambda i: (i, 0)),
        out_shape=jax.ShapeDtypeStruct((L, wout), BF16),
        compiler_params=_cparams(("parallel",)),
    )(*ins)


N_DEV = 8
N_CHIP = 4
PACK_W = 1024
_HBM = pl.BlockSpec(memory_space=pltpu.HBM)


def _me():
    return lax.axis_index("x"), lax.axis_index("y"), lax.axis_index("c")


def _flip(pos, k):
    x, y, c = pos
    return (1 - x if k & 4 else x, 1 - y if k & 2 else y, 1 - c if k & 1 else c)


def _dev_index(pos):
    return 4 * pos[0] + 2 * pos[1] + pos[2]


def _chip_index(pos):
    return 2 * pos[0] + pos[1]


def _rcopy(src, dst, sems, k, to):
    send_sems, recv_sems = sems
    return pltpu.make_async_remote_copy(src_ref=src, dst_ref=dst, send_sem=send_sems.at[k], recv_sem=recv_sems.at[k],
                                        device_id=to, device_id_type=MESH)


def allgather8(v, name):
    def body(v_ref, out_ref, send_sems, recv_sems):
        me = _me()
        sems = (send_sems, recv_sems)
        mine = out_ref.at[_dev_index(me)]
        sends = [_rcopy(v_ref, mine, sems, k - 1, _flip(me, k)) for k in range(1, N_DEV)]
        for cp in sends:
            cp.start()
        for k in range(1, N_DEV):
            peer = _flip(me, k)
            _rcopy(v_ref, out_ref.at[_dev_index(peer)], sems, k - 1, peer).wait_recv()
        for cp in sends:
            cp.wait_send()

    out = pl.pallas_call(
        body, name=name, in_specs=[_HBM], out_specs=_HBM,
        out_shape=jax.ShapeDtypeStruct((N_DEV,) + v.shape, v.dtype),
        scratch_shapes=[pltpu.SemaphoreType.DMA((N_DEV - 1,)), pltpu.SemaphoreType.DMA((N_DEV - 1,))],
    )(v)
    return lax.dynamic_update_slice(out, v[None], (_dev_index(_me()),) + (0,) * v.ndim)


COPY_PIECES = 4
ROW_TILE_16BIT = 16


def _pieces(rows, n=COPY_PIECES, align=ROW_TILE_16BIT):
    while n > 1 and rows % (n * align):
        n -= 1
    return [(q * (rows // n), rows // n) for q in range(n)]


class Exchange:
    def __init__(self, ins, outs, n_sems, start, finish):
        self.ins, self.outs, self.n_sems, self.start, self.finish = list(ins), list(outs), n_sems, start, finish

    def sem_shapes(self):
        return [pltpu.SemaphoreType.DMA((self.n_sems,)), pltpu.SemaphoreType.DMA((self.n_sems,))]


def run_exchange(ex, name):
    n_in, n_out = len(ex.ins), len(ex.outs)

    def body(*refs):
        ins, outs, sems = refs[:n_in], refs[n_in:n_in + n_out], tuple(refs[n_in + n_out:])
        ex.start(ins, outs, sems)
        ex.finish(ins, outs, sems)

    return pl.pallas_call(body, name=name, in_specs=[_HBM] * n_in, out_specs=[_HBM] * n_out, out_shape=ex.outs,
                          scratch_shapes=ex.sem_shapes())(*ex.ins)


def gather_exchange(shards):
    cuts = [_pieces(s.shape[0] // 2) for s in shards]
    base = [0]
    for cu in cuts:
        base.append(base[-1] + 6 * len(cu))

    def plan(ins, outs):
        me = _me()
        chips = [_flip(me, 4), _flip(me, 2), _flip(me, 6)]
        for w, cu in enumerate(cuts):
            half, nq = shards[w].shape[0] // 2, len(cu)
            for q, (s, n) in enumerate(cu):
                mine = pl.ds(pl.multiple_of(me[2] * half + s, ROW_TILE_16BIT), n)
                sibs = pl.ds(pl.multiple_of((1 - me[2]) * half + s, ROW_TILE_16BIT), n)
                for j, p in enumerate(chips):
                    yield w, p, mine, sibs, base[w] + j * nq + q, base[w] + (3 + j) * nq + q

    def start(ins, outs, sems):
        me = _me()
        for w, p, mine, _, k_ici, _ in plan(ins, outs):
            _rcopy(ins[w].at[mine], outs[w].at[_chip_index(me), mine], sems, k_ici, p).start()

    def finish(ins, outs, sems):
        me = _me()
        sibling = _flip(me, 1)
        for w, p, mine, _, k_ici, k_fwd in plan(ins, outs):
            got = outs[w].at[_chip_index(p), mine]
            _rcopy(got, got, sems, k_ici, p).wait_recv()
            _rcopy(got, got, sems, k_fwd, sibling).start()
        for w, p, _, sibs, _, k_fwd in plan(ins, outs):
            got = outs[w].at[_chip_index(p), sibs]
            _rcopy(got, got, sems, k_fwd, sibling).wait_recv()
        for w, p, mine, _, k_ici, k_fwd in plan(ins, outs):
            got = outs[w].at[_chip_index(p), mine]
            _rcopy(ins[w].at[mine], got, sems, k_ici, p).wait_send()
            _rcopy(got, got, sems, k_fwd, sibling).wait_send()

    outs = [jax.ShapeDtypeStruct((N_CHIP,) + s.shape, s.dtype) for s in shards]
    return Exchange(shards, outs, base[-1], start, finish)


def fill_own_slot(stack, mine):
    return lax.dynamic_update_slice(stack, mine[None], (_chip_index(_me()), 0, 0))


def scatter_exchange(parts):
    def plan(ins, outs):
        me = _me()
        for w in range(len(parts)):
            half = parts[w].shape[1] // 2
            for k in range(1, N_DEV):
                to = _flip(me, k)
                src = ins[w].at[_chip_index(to), pl.ds(pl.multiple_of(to[2] * half, ROW_TILE_16BIT), half)]
                yield src, outs[w].at[k - 1], (N_DEV - 1) * w + k - 1, to

    def start(ins, outs, sems):
        for src, dst, k, to in plan(ins, outs):
            _rcopy(src, dst, sems, k, to).start()

    def finish(ins, outs, sems):
        for src, dst, k, to in plan(ins, outs):
            _rcopy(dst, dst, sems, k, to).wait_recv()
        for src, dst, k, to in plan(ins, outs):
            _rcopy(src, dst, sems, k, to).wait_send()

    outs = [jax.ShapeDtypeStruct((N_DEV - 1, p.shape[1] // 2, p.shape[2]), p.dtype) for p in parts]
    return Exchange(parts, outs, (N_DEV - 1) * len(parts), start, finish)


def own_piece(parts):
    me = _me()
    half = parts.shape[1] // 2
    return lax.dynamic_slice(parts, (_chip_index(me), me[2] * half, 0), (1, half, parts.shape[2]))[0]


def join_halves(mine, name):
    half, C = mine.shape
    cuts = _pieces(half, 2 * COPY_PIECES, SUBLANES)
    nq = len(cuts)

    def body(m_ref, out_ref, send_sems, recv_sems):
        me = _me()
        sems = (send_sems, recv_sems)
        sibling = _flip(me, 1)
        sends = [_rcopy(m_ref.at[pl.ds(s, n)], out_ref.at[pl.ds(s, n)], sems, q, sibling) for q, (s, n) in enumerate(cuts)]
        for cp in sends:
            cp.start()
        for q, (s, n) in enumerate(cuts):
            got = out_ref.at[pl.ds(s, n)]
            _rcopy(got, got, sems, q, sibling).wait_recv()
        for cp in sends:
            cp.wait_send()

    other = pl.pallas_call(
        body, name=name, in_specs=[_HBM], out_specs=_HBM,
        out_shape=jax.ShapeDtypeStruct((half, C), mine.dtype),
        scratch_shapes=[pltpu.SemaphoreType.DMA((nq,)), pltpu.SemaphoreType.DMA((nq,))],
    )(mine)
    south = _me()[2] == 0
    return jnp.concatenate([jnp.where(south, mine, other), jnp.where(south, other, mine)], axis=0)


SUM_BLOCK_ELEMS = 256 * 1024


def sum_slots(a, name, first=None):
    n, R, C = a.shape
    tb = _pick(R, max(ROW_TILE_16BIT, SUM_BLOCK_ELEMS // C), ROW_TILE_16BIT)

    def body(a_ref, *rest):
        o_ref = rest[-1]
        acc = a_ref[0].astype(F32) if first is None else rest[0][...].astype(F32) + a_ref[0].astype(F32)
        for j in range(1, n):
            acc = acc + a_ref[j].astype(F32)
        o_ref[...] = acc

    row = pl.BlockSpec((tb, C), lambda i: (i, 0))
    return pl.pallas_call(
        body, name=name, grid=(R // tb,),
        in_specs=[pl.BlockSpec((n, tb, C), lambda i: (0, i, 0))] + ([] if first is None else [row]),
        out_specs=row,
        out_shape=jax.ShapeDtypeStruct((R, C), F32),
        compiler_params=_cparams(("parallel",)),
    )(*([a] if first is None else [a, first]))


def _pack(arrays, dtype, row_align):
    flat = jnp.concatenate([a.astype(dtype).reshape(-1) for a in arrays])
    per = PACK_W * row_align
    n = -(-flat.shape[0] // per) * per
    return jnp.pad(flat, (0, n - flat.shape[0])).reshape(-1, PACK_W)


def _unpack(buf, shapes):
    flat = buf.reshape(-1)
    out, at = [], 0
    for s in shapes:
        n = 1
        for d in s:
            n *= d
        out.append(flat[at:at + n].reshape(s))
        at += n
    return out


def _to_col_major(t, rows):
    L, C = t.shape
    return t.reshape(rows, GRID_W, C).transpose(1, 0, 2).reshape(L, C)


def _to_row_major(t, rows):
    L, C = t.shape
    return t.reshape(GRID_W, rows, C).transpose(1, 0, 2).reshape(L, C)


def _heads_to_groups(v, hpg):
    lead = v.shape[:-1]
    t = jnp.moveaxis(v.reshape(lead + (2, SSD_GROUPS, hpg)), -3, -2).reshape(lead + (SSD_GROUPS, 2 * hpg))
    t = jnp.pad(t, [(0, 0)] * (len(lead) + 1) + [(0, LANES - 2 * hpg)])
    return t.reshape(lead + (SSD_GROUPS * LANES,))


def _groups_to_heads(t, hpg):
    lead = t.shape[:-1]
    t = t.reshape(lead + (SSD_GROUPS, LANES))[..., :2 * hpg].reshape(lead + (SSD_GROUPS, 2, hpg))
    return jnp.moveaxis(t, -2, -3).reshape(lead + (2 * SSD_GROUPS * hpg,))


def _r1(v):
    return v.reshape(1, -1)


class _Layout:
    def __init__(self, D, W, hpg):
        assert W == D and D % (SSD_GROUPS * LANES) == 0
        self.D, self.W, self.hpg = D, W, hpg
        self.GN = SSD_GROUPS * SSD_STATE
        self.xbc0 = 5 * D
        self.dt0 = 5 * D + W + 2 * self.GN
        self.width = self.dt0 + SSD_GROUPS * LANES
        self.xbc_blk = self.xbc0 // CONV_CB
        self.dt_blk = self.dt0 // (SSD_GROUPS * LANES)
        assert self.dt0 % (SSD_GROUPS * LANES) == 0


def _seq_forward(hb, w_all, sp, lay, tag, grid_rows, init, carry=None):
    D, W = lay.D, lay.W
    gw = SSD_GROUPS * LANES
    pg = matmul(hb, w_all, "nn", F32, f"proj_{tag}", carry=carry)
    pg, carried = pg if carry else (pg, [])
    xbc = conv_fwd(pg, lay.xbc_blk, sp["ssd_conv_w"], sp["ssd_conv_b"], True, f"ssd_conv_{tag}")
    dtg = rowwise(f_dt, [Cols(pg, gw, lay.dt_blk)], [sp["dt_bias_g"]], [(gw, F32)], f"ssd_dt_{tag}")[0]
    yf, ent_f, fin_f = ssd_fwd(xbc, dtg, sp["a_g"], init["ssd_f"], False, W, f"ssd_scan_f_{tag}")
    yb, ent_b, fin_b = ssd_fwd(xbc, dtg, sp["a_g"], init["ssd_b"], True, W, f"ssd_scan_b_{tag}")
    xr = pg[:, D:2 * D]
    if grid_rows:
        xr = _to_col_major(xr, grid_rows)
    xc = conv_fwd(xr, 0, sp["lru_conv_w"], sp["lru_conv_b"], False, f"lru_conv_{tag}")
    a_f, u_f, a_b, u_b = rowwise(f_gates, [xc], sp["gate_pars"], [(LRU_HEAD, F32)] * 4, f"lru_gates_{tag}",
                                 tb=256, sub=256, ncol=D // LRU_HEAD)
    h_f, hs_f, fl_f = lru_scan(a_f, u_f, init["lru_f"], False, f"lru_scan_f_{tag}")
    _, hs_b, fl_b, r = lru_scan(a_b, u_b, init["lru_b"], True, f"lru_scan_b_{tag}", add=h_f)
    saved = dict(pg=pg, xbc=xbc, dtg=dtg, yf=yf, yb=yb, ent_f=ent_f, ent_b=ent_b, xr=xr, xc=xc,
                 a_f=a_f, a_b=a_b, hs_f=hs_f, hs_b=hs_b, r=r)
    return saved, dict(ssd_f=fin_f, ssd_b=fin_b, lru_f=fl_f, lru_b=fl_b), carried


def _seq_backward(sv, sp, lay, tag, grid_rows, dy, dxs_extra, d_r, dfin, carries=(None, None)):
    D, W, GN = lay.D, lay.W, lay.GN
    gw = SSD_GROUPS * LANES
    pg = sv["pg"]
    res_f = ssd_bwd(sv["xbc"], sv["dtg"], sp["a_g"], sv["ent_f"], dy, dfin["ssd_f"], False, W, f"ssd_scan_f_bwd_{tag}", carry=carries[0])
    res_b = ssd_bwd(sv["xbc"], sv["dtg"], sp["a_g"], sv["ent_b"], dy, dfin["ssd_b"], True, W, f"ssd_scan_b_bwd_{tag}", carry=carries[1])
    (dxf, ddt_f, db_f, dc_f, da_f, dh0_f), carried_f = res_f if carries[0] else (res_f, [])
    (dxb, ddt_b, db_b, dc_b, da_b, dh0_b), carried_b = res_b if carries[1] else (res_b, [])
    cw, cb = sp["ssd_conv_w"], sp["ssd_conv_b"]
    d_xs, dw1, db1 = conv_bwd(pg, lay.xbc_blk, cw[:, :W], cb[:, :W], [dxf, dxb] + dxs_extra, True, f"ssd_conv_x_bwd_{tag}")
    d_b, dw2, db2 = conv_bwd(pg, lay.xbc_blk + W // CONV_CB, cw[:, W:W + GN], cb[:, W:W + GN], [db_f, db_b], True, f"ssd_conv_b_bwd_{tag}")
    d_c, dw3, db3 = conv_bwd(pg, lay.xbc_blk + (W + GN) // CONV_CB, cw[:, W + GN:], cb[:, W + GN:], [dc_f, dc_b], True, f"ssd_conv_c_bwd_{tag}")
    d_dtraw, d_dtbias = rowwise_vjp(f_dt, [Cols(pg, gw, lay.dt_blk)], [sp["dt_bias_g"]], [[ddt_f, ddt_b]], [F32], f"ssd_dt_bwd_{tag}")
    du_b, dab, dl0_b = lru_scan(sv["a_b"], d_r, dfin["lru_b"], False, f"lru_scan_b_bwd_{tag}", adjoint_of=sv["hs_b"])
    du_f, daf, dl0_f = lru_scan(sv["a_f"], d_r, dfin["lru_f"], True, f"lru_scan_f_bwd_{tag}", adjoint_of=sv["hs_f"])
    res = rowwise_vjp(f_gates, [sv["xc"]], sp["gate_pars"], [daf, du_f, dab, du_b], [F32], f"lru_gates_bwd_{tag}",
                      tb=256, sub=256, ncol=D // LRU_HEAD)
    d_xc, gate_grads = res[0], res[1:]
    d_xr, g_lcw, g_lcb = conv_bwd(sv["xr"], 0, sp["lru_conv_w"], sp["lru_conv_b"], [d_xc], False, f"lru_conv_bwd_{tag}")
    if grid_rows:
        d_xr = _to_row_major(d_xr, grid_rows)
    grads = dict(ssd_conv_w=jnp.concatenate([dw1, dw2, dw3], axis=1), ssd_conv_b=jnp.concatenate([db1, db2, db3], axis=1),
                 dt_bias_g=d_dtbias, a_g=da_f + da_b, lru_conv_w=g_lcw, lru_conv_b=g_lcb, gate_pars=list(gate_grads))
    pieces = dict(xr=d_xr, xs=d_xs, b=d_b, c=d_c, dt=d_dtraw)
    return pieces, grads, dict(ssd_f=dh0_f, ssd_b=dh0_b, lru_f=dl0_f, lru_b=dl0_b), (carried_f, carried_b)


def _proj_cotangent(lay, px, pc, Lc, dz, dyr, dgs, dgr):
    zero = jnp.zeros((Lc, lay.D), BF16)
    cols = [(dz, zero), (px["xr"], pc["xr"]), (dyr, zero), (dgs, zero), (dgr, zero),
            (px["xs"], pc["xs"]), (px["b"], pc["b"]), (px["c"], pc["c"]), (px["dt"], pc["dt"])]
    return jnp.concatenate([jnp.concatenate([a.astype(BF16), b.astype(BF16)], axis=0) for a, b in cols], axis=1)


EARLY = ["w_in", "w_gate"]
LATE = ["w_out_ssd", "w_out_lru", "w_o", "ffn_w13", "ffn_w2"]


def _local_step(x, ctx, target, mod_x, mod_c, wb, late, ws):
    L, D = x.shape
    Lc = ctx.shape[0]
    nh = ws["ssd_d"].shape[0]
    hpg = nh // SSD_GROUPS
    W = nh * HEAD_DIM
    lay = _Layout(D, W, hpg)
    GN = lay.GN
    grid_rows = L // GRID_W
    nlh = D // LRU_HEAD

    w_in = wb["w_in"]
    o_dt, o_xr = 2 * W + 2 * GN, 2 * W + 2 * GN + 2 * nh
    w_all = jnp.concatenate([w_in[:, :W], w_in[:, o_xr:o_xr + D], w_in[:, o_xr + D:], wb["w_gate"], w_in[:, W:o_dt],
                             _heads_to_groups(w_in[:, o_dt:o_xr], hpg)], axis=1)
    a_neg = -jnp.exp(ws["ssd_a_log"])
    a_g = jnp.pad(_heads_to_groups(a_neg.reshape(-1), hpg).reshape(SSD_GROUPS, 1, LANES), [(0, 0), (0, SUBLANES - 1), (0, 0)])
    gate_pars = []
    for d in range(2):
        gate_pars += [ws["lru_w_a"][d], ws["lru_b_a"][d].reshape(nlh, 1, LRU_HEAD), ws["lru_w_x"][d],
                      ws["lru_b_x"][d].reshape(nlh, 1, LRU_HEAD), ws["lru_lambda"][d].reshape(nlh, 1, LRU_HEAD)]
    sp = dict(ssd_conv_w=ws["ssd_conv_w"], ssd_conv_b=_r1(ws["ssd_conv_b"]), dt_bias_g=_r1(_heads_to_groups(ws["ssd_dt_bias"].reshape(-1), hpg)),
              a_g=a_g, lru_conv_w=ws["lru_conv_w"], lru_conv_b=_r1(ws["lru_conv_b"]), gate_pars=gate_pars)
    dexp = _r1(jnp.repeat(ws["ssd_d"], HEAD_DIM))
    norm_mix, norm_ffn, ssd_norm, final_norm = _r1(ws["norm_mix"]), _r1(ws["norm_ffn"]), _r1(ws["ssd_norm"]), _r1(ws["final_norm"])
    bg_s, bg_r = _r1(ws["b_gate"][:D]), _r1(ws["b_gate"][D:])
    sh_m, sc_m, g_m, sh_f, sc_f, g_f = [mod_x[:, k * D:(k + 1) * D] for k in range(6)]
    csh_m, csc_m = mod_c[:, :D], mod_c[:, D:2 * D]

    zst = dict(ssd_f=jnp.zeros((SSD_GROUPS, hpg * HEAD_DIM, SSD_STATE), F32), ssd_b=jnp.zeros((SSD_GROUPS, hpg * HEAD_DIM, SSD_STATE), F32),
               lru_f=jnp.zeros((SUBLANES, D), F32), lru_b=jnp.zeros((SUBLANES, D), F32))
    hcb = rowwise(f_mod_in, [ctx], [norm_mix, csh_m, csc_m], [(D, BF16)], "norm_mix_ctx")[0]
    hb = rowwise(f_mod_in, [x], [norm_mix, sh_m, sc_m], [(D, BF16)], "norm_mix_x")[0]
    svc, fin_c, _ = _seq_forward(hcb, w_all, sp, lay, "ctx", None, zst)
    svx, _, stacks = _seq_forward(hb, w_all, sp, lay, "x", grid_rows, fin_c, carry=gather_exchange([late[n] for n in LATE]))
    wb = dict(wb)
    for n, st in zip(LATE, stacks):
        wb[n] = _whole_from_shards(fill_own_slot(st, late[n]), n in BIG_BY_COLUMNS)
    pg = svx["pg"]
    r_rm = _to_row_major(svx["r"], grid_rows)
    gn_rows = [svx["yf"], svx["yb"], Cols(svx["xbc"], W, 0), Cols(pg, D, 0)]
    yn = rowwise(f_gnorm, gn_rows, [dexp, ssd_norm], [(W, BF16)], "ssd_gnorm")[0]
    o_s = matmul(yn, wb["w_out_ssd"], "nn", F32, "out_ssd")
    o_in = rowwise(f_gelu_gate, [r_rm, Cols(pg, D, 2)], [], [(D, BF16)], "lru_gelu")[0]
    o_r = matmul(o_in, wb["w_out_lru"], "nn", F32, "out_lru")
    mg_rows = [Cols(pg, D, 3), Cols(pg, D, 4), o_s, o_r]
    mixed = rowwise(f_merge, mg_rows, [bg_s, bg_r], [(D, BF16)], "merge")[0]
    out = matmul(mixed, wb["w_o"], "nn", F32, "out_proj")
    x1, h2 = rowwise(f_res_mod, [x, out], [g_m, norm_ffn, sh_f, sc_f], [(D, F32), (D, BF16)], "res_norm_ffn")
    gu = matmul(h2, wb["ffn_w13"], "nn", F32, "ffn_in")
    act = swiglu(gu, None, "ffn_act")
    f = matmul(act, wb["ffn_w2"], "nn", F32, "ffn_out")

    ones = jnp.full((L, LANES), 1.0 / LANES, F32)
    dx1a, df, d_gf, d_fnorm, lsum = rowwise_vjp(f_final, [x1, f, target], [g_f, final_norm], [ones], [F32, BF16, None],
                                                "final_loss", out_sums=[(0, LANES)])
    loss = lsum[0, 0]
    d_act = matmul(df, wb["ffn_w2"], "nt", F32, "ffn_out_dx")
    g_w2 = matmul(act, df, "tn", F32, "ffn_out_dw")
    d_gu = swiglu(gu, d_act, "ffn_act_bwd")
    dh2 = matmul(d_gu, wb["ffn_w13"], "nt", F32, "ffn_in_dx")
    g_w13 = matmul(h2, d_gu, "tn", F32, "ffn_in_dw")
    dxa, d_out, d_gm, d_nffn, d_shf, d_scf = rowwise_vjp(f_res_mod, [x, out], [g_m, norm_ffn, sh_f, sc_f], [dx1a, dh2],
                                                         [F32, BF16], "res_norm_ffn_bwd")
    d_mixed = matmul(d_out, wb["w_o"], "nt", F32, "out_proj_dx")
    g_wo = matmul(mixed, d_out, "tn", F32, "out_proj_dw")
    dgs, dgr, do_s, do_r, d_bgs, d_bgr = rowwise_vjp(f_merge, mg_rows, [bg_s, bg_r], [d_mixed], [BF16] * 4, "merge_bwd")
    d_yn = matmul(do_s, wb["w_out_ssd"], "nt", F32, "out_ssd_dx")
    g_wos = matmul(yn, do_s, "tn", F32, "out_ssd_dw")
    d_oin = matmul(do_r, wb["w_out_lru"], "nt", F32, "out_lru_dx")
    g_wol = matmul(o_in, do_r, "tn", F32, "out_lru_dw")
    d_r_rm, d_yr = rowwise_vjp(f_gelu_gate, [r_rm, Cols(pg, D, 2)], [], [d_oin], [F32, BF16], "lru_gelu_bwd")
    dy, dxs_skip, dz, d_dexp, d_ssdn = rowwise_vjp(f_gnorm, gn_rows, [dexp, ssd_norm], [d_yn], [F32, None, F32, BF16], "ssd_gnorm_bwd")
    zfin = dict(zst)
    parts = {n: _shards_from_whole(g, n in BIG_BY_COLUMNS).astype(BF16)
             for n, g in (("ffn_w2", g_w2), ("ffn_w13", g_w13), ("w_o", g_wo), ("w_out_ssd", g_wos), ("w_out_lru", g_wol))}
    ride_f, ride_b = ["ffn_w2", "ffn_w13"], ["w_o", "w_out_ssd", "w_out_lru"]
    px, gx, dst, (land_f, land_b) = _seq_backward(
        svx, sp, lay, "x", grid_rows, dy, [dxs_skip], _to_col_major(d_r_rm, grid_rows), zfin,
        carries=(scatter_exchange([parts[n] for n in ride_f]), scatter_exchange([parts[n] for n in ride_b])))
    pc, gc, _, _ = _seq_backward(svc, sp, lay, "ctx", None, jnp.zeros((Lc, W), F32), [], jnp.zeros((Lc, D), F32), dst)
    dpg = _proj_cotangent(lay, px, pc, Lc, dz, d_yr, dgs, dgr)
    g_wall = matmul(jnp.concatenate([hb, hcb], axis=0), dpg, "tn", F32, "proj_dw")
    xb0 = lay.xbc0
    g_w_in = jnp.concatenate([g_wall[:, :W], g_wall[:, xb0:xb0 + W + 2 * GN], _groups_to_heads(g_wall[:, lay.dt0:], hpg),
                              g_wall[:, D:2 * D], g_wall[:, 2 * D:3 * D]], axis=1)
    ride_p = ["w_in", "w_gate"]
    parts.update({n: _shards_from_whole(g, True).astype(BF16) for n, g in (("w_in", g_w_in), ("w_gate", g_wall[:, 3 * D:5 * D]))})
    dh, land_p = matmul(dpg, w_all, "nt", F32, "proj_dx_x", a_rows=(0, L), carry=scatter_exchange([parts[n] for n in ride_p]))
    scattered = {n: (ld, own_piece(parts[n])) for n, ld in zip(ride_f + ride_b + ride_p, land_f + land_b + land_p)}
    dhc = matmul(dpg, w_all, "nt", F32, "proj_dx_ctx", a_rows=(L, Lc))
    grad_x, d_nmix_x, d_shm, d_scm = rowwise_vjp(f_mod_in, [x], [norm_mix, sh_m, sc_m], [dh, dxa], [F32], "norm_mix_x_bwd")
    d_nmix_c, d_cshm, d_cscm = rowwise_vjp(f_mod_in, [ctx], [norm_mix, csh_m, csc_m], [dhc, jnp.zeros((Lc, D), F32)], [None], "norm_mix_ctx_bwd")

    dmod_x = jnp.concatenate([d_shm, d_scm, d_gm, d_shf, d_scf, d_gf], axis=1)
    dmod_c = jnp.concatenate([d_cshm, d_cscm, jnp.zeros((1, 4 * D), F32)], axis=1)

    d_a = _groups_to_heads((gx["a_g"] + gc["a_g"])[:, 0, :].reshape(-1), hpg).reshape(2, nh)
    gg = [a + b for a, b in zip(gx["gate_pars"], gc["gate_pars"])]

    def gate(k, shape):
        return jnp.stack([gg[k].reshape(shape), gg[5 + k].reshape(shape)], axis=0)

    grads = dict(
        norm_mix=(d_nmix_x + d_nmix_c).reshape(-1), norm_ffn=d_nffn.reshape(-1),
        ssd_conv_w=gx["ssd_conv_w"] + gc["ssd_conv_w"], ssd_conv_b=(gx["ssd_conv_b"] + gc["ssd_conv_b"]).reshape(-1),
        ssd_dt_bias=_groups_to_heads((gx["dt_bias_g"] + gc["dt_bias_g"]).reshape(-1), hpg).reshape(2, nh),
        ssd_a_log=d_a * a_neg, ssd_d=d_dexp.reshape(nh, HEAD_DIM).sum(axis=1), ssd_norm=d_ssdn.reshape(-1),
        lru_conv_w=gx["lru_conv_w"] + gc["lru_conv_w"], lru_conv_b=(gx["lru_conv_b"] + gc["lru_conv_b"]).reshape(-1),
        lru_w_a=gate(0, (nlh, LRU_HEAD, LRU_HEAD)), lru_b_a=gate(1, (D,)), lru_w_x=gate(2, (nlh, LRU_HEAD, LRU_HEAD)),
        lru_b_x=gate(3, (D,)), lru_lambda=gate(4, (D,)),
        b_gate=jnp.concatenate([d_bgs, d_bgr], axis=1).reshape(-1), final_norm=d_fnorm.reshape(-1))
    return loss, grad_x, grads, scattered, dmod_x, dmod_c


WEIGHTS = ["c_ctx", "w_ada", "b_ada", "norm_mix", "norm_ffn", "w_in", "ssd_conv_w", "ssd_conv_b", "ssd_dt_bias", "ssd_a_log",
           "ssd_d", "ssd_norm", "w_out_ssd", "lru_conv_w", "lru_conv_b", "lru_w_a", "lru_b_a", "lru_w_x", "lru_b_x", "lru_lambda",
           "w_out_lru", "w_gate", "b_gate", "w_o", "ffn_w13", "ffn_w2", "final_norm"]
BIG = ["w_in", "w_out_ssd", "w_out_lru", "w_gate", "w_o", "ffn_w13", "ffn_w2"]
BIG_BY_COLUMNS = {"w_in", "w_gate", "ffn_w13"}
MOD_ROWS = 16
MOD_CTX_ROW = N_DEV
SMALL_SHARDED = ["ssd_conv_w", "lru_conv_w", "lru_b_a", "lru_b_x", "lru_lambda"]
ADAM_ROWS = 64


def _whole_from_shards(stack, by_columns):
    if by_columns:
        return stack.transpose(1, 0, 2).reshape(stack.shape[1], -1)
    return stack.reshape(-1, stack.shape[2])


def _shards_from_whole(g, by_columns):
    if by_columns:
        return g.reshape(g.shape[0], N_CHIP, -1).transpose(1, 0, 2)
    return g.reshape(N_CHIP, -1, g.shape[1])


def _adam(w, g, m, v, name):
    shape = w.shape
    cols = shape[-1]
    w2, g2, m2, v2 = [t.reshape(-1, cols) for t in (w, g, m, v)]
    d, nm, nv = rowwise(f_adam, [w2, g2, m2, v2], [], [(cols, F32)] * 3, name, tb=ADAM_ROWS, sub=8)
    return d.reshape(shape), nm.reshape(shape), nv.reshape(shape)


def kernel(x, c, ctx, c_ctx, w_ada, b_ada, norm_mix, norm_ffn, w_in, ssd_conv_w, ssd_conv_b, ssd_dt_bias, ssd_a_log, ssd_d, ssd_norm, w_out_ssd, lru_conv_w, lru_conv_b, lru_w_a, lru_b_a, lru_w_x, lru_b_x, lru_lambda, w_out_lru, w_gate, b_gate, w_o, ffn_w13, ffn_w2, final_norm, loss_target, m_c_ctx, m_w_ada, m_b_ada, m_norm_mix, m_norm_ffn, m_w_in, m_ssd_conv_w, m_ssd_conv_b, m_ssd_dt_bias, m_ssd_a_log, m_ssd_d, m_ssd_norm, m_w_out_ssd, m_lru_conv_w, m_lru_conv_b, m_lru_w_a, m_lru_b_a, m_lru_w_x, m_lru_b_x, m_lru_lambda, m_w_out_lru, m_w_gate, m_b_gate, m_w_o, m_ffn_w13, m_ffn_w2, m_final_norm, v_c_ctx, v_w_ada, v_b_ada, v_norm_mix, v_norm_ffn, v_w_in, v_ssd_conv_w, v_ssd_conv_b, v_ssd_dt_bias, v_ssd_a_log, v_ssd_d, v_ssd_norm, v_w_out_ssd, v_lru_conv_w, v_lru_conv_b, v_lru_w_a, v_lru_b_a, v_lru_w_x, v_lru_b_x, v_lru_lambda, v_w_out_lru, v_w_gate, v_b_gate, v_w_o, v_ffn_w13, v_ffn_w2, v_final_norm):
    given = dict(locals())
    layered = {n for n in WEIGHTS if n not in ("c_ctx", "final_norm")}
    w_blk = {n: (given[n][0] if n in layered else given[n]) for n in WEIGHTS}
    me = _me()
    chip, dev = _chip_index(me), _dev_index(me)
    D = c.shape[1]

    ada_w = w_blk["w_ada"].astype(BF16)
    ada_n = ada_w.shape[1]
    c_all = allgather8(jnp.concatenate([c, jnp.zeros((SUBLANES - 1, D), F32)], axis=0), "gather_c")[:, 0, :]
    cc = jnp.concatenate([c_all, c_ctx[None], jnp.zeros((MOD_ROWS - N_DEV - 1, D), F32)], axis=0)
    s_cc = rowwise(f_silu, [cc], [], [(D, BF16)], "mod_silu", tb=MOD_ROWS)[0]
    ada_b = lax.dynamic_slice_in_dim(w_blk["b_ada"], chip * ada_n, ada_n)[None]
    mod_blk = rowwise(f_add_bias, [matmul(s_cc, ada_w, "nn", F32, "mod_proj")], [ada_b], [(ada_n, F32)], "mod_bias", tb=MOD_ROWS)[0]
    mod = allgather8(mod_blk, "gather_mod")[::2].transpose(1, 0, 2).reshape(MOD_ROWS, N_CHIP * ada_n)
    mod_x = lax.dynamic_slice_in_dim(mod, dev, 1, axis=0)
    mod_c = mod[MOD_CTX_ROW:MOD_CTX_ROW + 1]

    shard16 = {n: w_blk[n].astype(BF16) for n in BIG}
    stacks = run_exchange(gather_exchange([shard16[n] for n in EARLY]), "gather_early")
    wb = {n: _whole_from_shards(fill_own_slot(st, shard16[n]), n in BIG_BY_COLUMNS) for n, st in zip(EARLY, stacks)}

    sm_shapes = [w_blk[n].shape for n in SMALL_SHARDED]
    sm_all = allgather8(_pack([w_blk[n] for n in SMALL_SHARDED], F32, ROW_TILE_16BIT), "gather_small")[::2]
    ws = {n: w_blk[n] for n in WEIGHTS if n not in BIG and n != "w_ada"}
    for n, s in zip(SMALL_SHARDED, _unpack_stacked(sm_all, sm_shapes)):
        ws[n] = jnp.concatenate([s[k] for k in range(N_CHIP)], axis=-1)

    loss, grad_x, grads, scattered, dmod_x, dmod_c = _local_step(x[0], ctx[0], loss_target[0], mod_x, mod_c, wb,
                                                                 {n: shard16[n] for n in LATE}, ws)
    loss = lax.psum(loss, ("x", "y", "c"))

    dm = allgather8(jnp.concatenate([dmod_x, dmod_c, jnp.zeros((SUBLANES - 2, 6 * D), F32)], axis=0), "gather_dmod")
    dm_ctx = sum_slots(dm, "sum_dmod")[1:2]
    dmod = jnp.concatenate([dm[:, 0, :], dm_ctx, jnp.zeros((MOD_ROWS - N_DEV - 1, 6 * D), F32)], axis=0)
    g_shard = {"b_ada": rowwise_vjp(f_add_bias, [dmod], [w_blk["b_ada"][None]], [dmod], [None], "mod_bias_bwd", tb=MOD_ROWS)[0].reshape(-1)}
    d_blk = lax.dynamic_slice_in_dim(dmod, chip * ada_n, ada_n, axis=1).astype(BF16)
    g_shard["w_ada"] = matmul(s_cc, d_blk, "tn", F32, "mod_proj_dw")
    d_silu_ctx = matmul(d_blk, ada_w, "nt", F32, "mod_proj_dx")[MOD_CTX_ROW]
    grads["c_ctx_silu"] = 0.5 * d_silu_ctx

    for n in BIG:
        landed, own = scattered[n]
        g_shard[n] = join_halves(sum_slots(landed, "sum_" + n, first=own), "join_" + n)

    small = [n for n in WEIGHTS if n not in BIG and n != "w_ada"]
    reduced = [n for n in small if n not in ("b_ada", "c_ctx")] + ["c_ctx_silu"]
    sm_g = sum_slots(allgather8(_pack([grads[n] for n in reduced], F32, ROW_TILE_16BIT), "gather_small_grads"), "sum_small_grads")
    for n, g in zip(reduced, _unpack(sm_g, [grads[n].shape for n in reduced])):
        if n in SMALL_SHARDED:
            per = g.shape[-1] // N_CHIP
            g = lax.dynamic_slice_in_dim(g, chip * per, per, axis=g.ndim - 1)
        g_shard[n] = g
    ctx_rows = jnp.concatenate([c_ctx[None], jnp.zeros((SUBLANES - 1, D), F32)], axis=0)
    ctx_cot = jnp.concatenate([g_shard.pop("c_ctx_silu")[None], jnp.zeros((SUBLANES - 1, D), F32)], axis=0)
    g_shard["c_ctx"] = rowwise_vjp(f_silu, [ctx_rows], [], [ctx_cot], [F32], "mod_silu_bwd", tb=SUBLANES, sub=SUBLANES)[0][0]

    out_g, out_d, out_m, out_v = {}, {}, {}, {}
    for n in BIG + ["w_ada"]:
        out_g[n] = g_shard[n]
        out_d[n], out_m[n], out_v[n] = _adam(w_blk[n], g_shard[n], given["m_" + n][0], given["v_" + n][0], "adamw_" + n)
    sm_blk_shapes = [w_blk[n].shape for n in small]
    packed = [_pack([t[n].reshape(w_blk[n].shape) for n in small], F32, ADAM_ROWS)
              for t in (w_blk, g_shard, {n: given["m_" + n] for n in small}, {n: given["v_" + n] for n in small})]
    res = _adam(*packed, "adamw_small")
    for tgt, buf in zip((out_d, out_m, out_v), res):
        tgt.update(zip(small, _unpack(buf, sm_blk_shapes)))
    for n in small:
        out_g[n] = g_shard[n].reshape(w_blk[n].shape)

    def full(n, t):
        return t.reshape(given[n].shape)

    return (loss, grad_x[None], *[full(n, out_g[n]) for n in WEIGHTS], *[full(n, out_d[n]) for n in WEIGHTS],
            *[full(n, out_m[n]) for n in WEIGHTS], *[full(n, out_v[n]) for n in WEIGHTS])


def _unpack_stacked(buf, shapes):
    k = buf.shape[0]
    flat = buf.reshape(k, -1)
    out, at = [], 0
    for s in shapes:
        n = 1
        for d in s:
            n *= d
        out.append(flat[:, at:at + n].reshape((k,) + tuple(s)))
        at += n
    return out
```

```python
import functools

import jax
import jax.numpy as jnp
from jax import lax
from jax.experimental import pallas as pl
from jax.experimental.pallas import tpu as pltpu

F32 = jnp.float32
BF16 = jnp.bfloat16
MESH = pl.DeviceIdType.MESH

V7X_VMEM_LIMIT_BYTES = 56 * 1024 * 1024
LANES = 128
SUBLANES = 8

HEAD_DIM = 64
SSD_STATE = 128
SSD_GROUPS = 4
SSD_CHUNK = 128
GRID_W = 64
LRU_HEAD = 128
LRU_C = 8.0
EPS = 1e-6
NEG = -1e30

ADAM_LR = 0.001
ADAM_B1 = 0.9
ADAM_B2 = 0.999
ADAM_EPS = 1e-08
ADAM_WD = 0.01
ADAM_STEP = 10


def _cparams(sem=None, **kw):
    if sem is not None:
        kw["dimension_semantics"] = sem
    return pltpu.CompilerParams(vmem_limit_bytes=V7X_VMEM_LIMIT_BYTES, **kw)


_DN = {"nn": (((1,), (0,)), ((), ())), "nt": (((1,), (1,)), ((), ())), "tn": (((0,), (0,)), ((), ()))}


def _raw_dot(a, b, form, precision=None):
    return lax.dot_general(a, b, _DN[form], precision=precision, preferred_element_type=F32)


def _dot_bwd_forms(form):
    return {"nn": (("g", "b", "nt"), ("a", "g", "tn")),
            "nt": (("g", "b", "nn"), ("g", "a", "tn")),
            "tn": (("b", "g", "nt"), ("a", "g", "nn"))}[form]


def _make_dot(rounding):
    @functools.partial(jax.custom_vjp, nondiff_argnums=(2,))
    def dot(a, b, form):
        if rounding is None:
            return _raw_dot(a, b, form, precision=lax.Precision.HIGHEST)
        return _raw_dot(a.astype(rounding), b.astype(rounding), form)

    def fwd(a, b, form):
        return dot(a, b, form), (a, b)

    def bwd(form, res, g):
        a, b = res
        ops = {"a": a, "b": b, "g": g}
        (l1, r1, f1), (l2, r2, f2) = _dot_bwd_forms(form)
        return dot(ops[l1], ops[r1], f1).astype(a.dtype), dot(ops[l2], ops[r2], f2).astype(b.dtype)

    dot.defvjp(fwd, bwd)
    return dot


bdot = _make_dot(BF16)
hdot = _make_dot(None)


def _sigmoid(v):
    return 0.5 * jnp.tanh(0.5 * v) + 0.5


def _silu(v):
    return v * _sigmoid(v)


def _softplus(v):
    return jnp.maximum(v, 0.0) + jnp.log1p(jnp.exp(-jnp.abs(v)))


def _gelu_tanh(v):
    return 0.5 * v * (1.0 + jnp.tanh(0.7978845608028654 * (v + 0.044715 * v * v * v)))


def _neg_expm1(v):
    poly = -v * (1.0 + v * (0.5 + v * (1.0 / 6.0 + v * (1.0 / 24.0 + v * (1.0 / 120.0)))))
    return jnp.where(v > -0.05, poly, 1.0 - jnp.exp(v))


def _rms(t, gain):
    return t * lax.rsqrt(jnp.mean(t * t, axis=-1, keepdims=True) + EPS) * gain


def _pick(dim, want, mult=1):
    if dim <= want:
        return dim
    for t in range(want - want % mult, 0, -mult):
        if dim % t == 0:
            return t
    t = min(dim, want)
    while dim % t:
        t //= 2
    return t


MM_TILE_M = 1024
MM_TILE_N = 1536
MM_VMEM_BYTES = 46 * 1024 * 1024


def matmul(a, b, form, out_dtype, name, tm=MM_TILE_M, tn=MM_TILE_N, tk=None, a_rows=None, carry=None):
    if form == "nn":
        (m, k), (k2, n) = a.shape, b.shape
    elif form == "nt":
        (m, k), (n, k2) = a.shape, b.shape
    else:
        (k, m), (k2, n) = a.shape, b.shape
    assert k == k2, (a.shape, b.shape, form)
    row0 = 0
    if a_rows is not None:
        assert form != "tn"
        row0, m = a_rows
    tm, tn = _pick(m, tm, LANES), _pick(n, tn, LANES)
    assert row0 % tm == 0
    blk0 = row0 // tm
    if tk is None:
        fixed = tm * tn * (2 * jnp.dtype(out_dtype).itemsize + 4)
        tk = (MM_VMEM_BYTES - fixed) // (2 * a.dtype.itemsize * (tm + tn))
    tk = _pick(k, tk, LANES)
    nk = k // tk

    grid = (m // tm, n // tn, nk)
    n_ci, n_co = (len(carry.ins), len(carry.outs)) if carry else (0, 0)

    def body(a_ref, b_ref, *rest):
        c_ins, o_ref, c_outs = rest[:n_ci], rest[n_ci], rest[n_ci + 1:n_ci + 1 + n_co]
        scratch = rest[n_ci + 1 + n_co:]
        kk = pl.program_id(2)
        if carry:
            at_step = [pl.program_id(d) for d in range(3)]
            sems = tuple(scratch[-2:])
            pl.when((at_step[0] == 0) & (at_step[1] == 0) & (at_step[2] == 0))(lambda: carry.start(c_ins, c_outs, sems))
        part = _raw_dot(a_ref[...], b_ref[...], form)
        if nk == 1:
            o_ref[...] = part.astype(o_ref.dtype)
        else:
            acc = scratch[0]

            @pl.when(kk == 0)
            def _():
                acc[...] = part

            @pl.when(kk > 0)
            def _():
                acc[...] += part

            @pl.when(kk == nk - 1)
            def _():
                o_ref[...] = acc[...].astype(o_ref.dtype)

        if carry:
            last = (at_step[0] == grid[0] - 1) & (at_step[1] == grid[1] - 1) & (at_step[2] == grid[2] - 1)
            pl.when(last)(lambda: carry.finish(c_ins, c_outs, sems))

    a_spec = pl.BlockSpec((tk, tm), lambda i, j, kk: (kk, i)) if form == "tn" else pl.BlockSpec((tm, tk), lambda i, j, kk: (blk0 + i, kk))
    b_spec = pl.BlockSpec((tn, tk), lambda i, j, kk: (j, kk)) if form == "nt" else pl.BlockSpec((tk, tn), lambda i, j, kk: (kk, j))
    res = pl.pallas_call(
        body, name=name, grid=grid, in_specs=[a_spec, b_spec] + [_HBM] * n_ci,
        out_specs=[pl.BlockSpec((tm, tn), lambda i, j, kk: (i, j))] + [_HBM] * n_co,
        out_shape=[jax.ShapeDtypeStruct((m, n), out_dtype)] + (carry.outs if carry else []),
        scratch_shapes=([pltpu.VMEM((tm, tn), F32)] if nk > 1 else []) + (carry.sem_shapes() if carry else []),
        compiler_params=_cparams(("arbitrary",) * 3 if carry else ("parallel", "parallel", "arbitrary")),
    )(a, b, *(carry.ins if carry else []))
    return (res[0], list(res[1:])) if carry else res[0]


class Cols:
    def __init__(self, arr, w, j):
        assert (j + 1) * w <= arr.shape[1]
        self.arr, self.w, self.j = arr, w, j


def _row_width(x, ncol):
    return x.w if isinstance(x, Cols) else x.shape[1] // ncol


def _row_spec(x, tb, ncol):
    if isinstance(x, Cols):
        j0 = x.j
        return x.arr, pl.BlockSpec((tb, x.w), lambda j, i: (i, j0 + j))
    return x, pl.BlockSpec((tb, x.shape[1] // ncol), lambda j, i: (i, j))


def _par_spec(shape, ncol):
    if ncol == 1:
        return pl.BlockSpec(shape, lambda j, i, nd=len(shape): (0,) * nd)
    assert len(shape) == 3 and shape[0] == ncol, shape
    return pl.BlockSpec((1,) + tuple(shape[1:]), lambda j, i: (j, 0, 0))


def _par_value(ref, ncol):
    return ref[...] if ncol == 1 else ref[0]


def rowwise(fn, rows, pars, outs, name, tb=256, sub=16, ncol=1):
    m = (rows[0].arr if isinstance(rows[0], Cols) else rows[0]).shape[0]
    tb = _pick(m, tb)
    sub = min(sub, tb)
    nr, npar = len(rows), len(pars)

    def body(*refs):
        row_refs, par_refs, out_refs = refs[:nr], refs[nr:nr + npar], refs[nr + npar:]
        pv = [_par_value(p, ncol) for p in par_refs]

        def step(i, carry):
            sl = pl.ds(pl.multiple_of(i * sub, sub), sub)
            res = fn(*[r[sl, :].astype(F32) for r in row_refs], *pv)
            for o, v in zip(out_refs, res):
                o[sl, :] = v.astype(o.dtype)
            return carry

        lax.fori_loop(0, tb // sub, step, 0)

    arrs, specs = zip(*[_row_spec(r, tb, ncol) for r in rows])
    return pl.pallas_call(
        body, name=name, grid=(ncol, m // tb),
        in_specs=list(specs) + [_par_spec(p.shape, ncol) for p in pars],
        out_specs=[pl.BlockSpec((tb, w), lambda j, i: (i, j)) for w, _ in outs],
        out_shape=[jax.ShapeDtypeStruct((m, w * ncol), dt) for w, dt in outs],
        compiler_params=_cparams(("parallel", "parallel")),
    )(*arrs, *pars)


def rowwise_vjp(fn, rows, pars, cots, drow_dtypes, name, tb=256, sub=16, out_sums=(), ncol=1):
    m = (rows[0].arr if isinstance(rows[0], Cols) else rows[0]).shape[0]
    tb = _pick(m, tb)
    sub = min(sub, tb)
    cots = [list(c) if isinstance(c, (list, tuple)) else [c] for c in cots]
    flat = [c for group in cots for c in group]
    nr, npar, nc = len(rows), len(pars), len(flat)
    want = [i for i, d in enumerate(drow_dtypes) if d is not None]
    assert ncol == 1 or not out_sums

    def body(*refs):
        row_refs, par_refs = refs[:nr], refs[nr:nr + npar]
        cot_refs = list(refs[nr + npar:nr + npar + nc])
        drow_refs = refs[nr + npar + nc:nr + npar + nc + len(want)]
        acc_refs = refs[nr + npar + nc + len(want):]
        pv = [_par_value(p, ncol) for p in par_refs]

        def step(i, acc):
            sl = pl.ds(pl.multiple_of(i * sub, sub), sub)
            rv = [r[sl, :].astype(F32) for r in row_refs]
            res, vjp = jax.vjp(lambda rr, pp: tuple(fn(*rr, *pp)), rv, pv)
            ct, at = [], 0
            for group in cots:
                ct.append(sum(c[sl, :].astype(F32) for c in cot_refs[at:at + len(group)]))
                at += len(group)
            d_rows, d_pars = vjp(tuple(ct))
            for o, idx in zip(drow_refs, want):
                o[sl, :] = d_rows[idx].astype(o.dtype)
            sums = [jnp.sum(res[k], axis=0, keepdims=True) for k, _ in out_sums]
            return tuple(a + d for a, d in zip(acc, list(d_pars) + sums))

        init = tuple(jnp.zeros(p.shape, F32) for p in pv) + tuple(jnp.zeros((1, w), F32) for _, w in out_sums)
        acc = lax.fori_loop(0, tb // sub, step, init)

        @pl.when(pl.program_id(1) == 0)
        def _():
            for o in acc_refs:
                o[...] = jnp.zeros_like(o)

        for o, a in zip(acc_refs, acc):
            if ncol == 1:
                o[...] += a
            else:
                o[0] += a

    arrs, specs = zip(*[_row_spec(r, tb, ncol) for r in rows])
    carrs, cspecs = zip(*[_row_spec(c, tb, ncol) for c in flat])
    widths = [_row_width(r, ncol) for r in rows]
    acc_shapes = [tuple(p.shape) for p in pars] + [(1, w) for _, w in out_sums]
    return pl.pallas_call(
        body, name=name, grid=(ncol, m // tb),
        in_specs=list(specs) + [_par_spec(p.shape, ncol) for p in pars] + list(cspecs),
        out_specs=[pl.BlockSpec((tb, widths[i]), lambda j, i_: (i_, j)) for i in want]
        + [_par_spec(s, ncol) for s in acc_shapes],
        out_shape=[jax.ShapeDtypeStruct((m, widths[i] * ncol), drow_dtypes[i]) for i in want]
        + [jax.ShapeDtypeStruct(s, F32) for s in acc_shapes],
        compiler_params=_cparams(("parallel", "arbitrary")),
    )(*arrs, *pars, *carrs)


def _col(v, c):
    lane = lax.broadcasted_iota(jnp.int32, v.shape, 1)
    return jnp.sum(jnp.where(lane == c, v, 0.0), axis=1, keepdims=True)


def _row(v, r):
    sub = lax.broadcasted_iota(jnp.int32, v.shape, 0)
    return jnp.sum(jnp.where(sub == r, v, 0.0), axis=0, keepdims=True)


def _ssd_chunk(xs, dt, bm, cm, hs, a_row, rev, col0):
    q = dt.shape[0]
    ii = lax.broadcasted_iota(jnp.int32, (q, q), 0)
    jj = lax.broadcasted_iota(jnp.int32, (q, q), 1)
    keep = (jj >= ii) if rev else (jj <= ii)
    tri = keep.astype(F32)
    dta = dt * a_row
    a_cum = hdot(tri, dta, "nn")
    tri_t = ((jj <= ii) if rev else (jj >= ii)).astype(F32)
    a_cum_t = hdot(dta, tri_t, "tn")
    cb = bdot(cm, bm, "nt")
    last = 0 if rev else q - 1
    ys, hn = [], []
    for e, (x, h) in enumerate(zip(xs, hs)):
        c = col0 + e
        ac = _col(a_cum, c)
        seg = ac - _row(a_cum_t, c)
        lm = jnp.exp(jnp.where(keep, seg, NEG))
        xdt = x * _col(dt, c)
        y_diag = bdot(cb * lm, xdt, "nn")
        y_off = bdot(cm * jnp.exp(ac), h, "nt")
        ys.append(y_diag + y_off)
        tot = _row(ac, last)
        states = bdot(xdt, bm * jnp.exp(tot - ac), "tn")
        hn.append(jnp.exp(tot) * h + states)
    return ys, hn


def _ssd_specs(nc, hpg, w_ssd, rev_order):
    q = SSD_CHUNK
    wx = hpg * HEAD_DIM
    boff = w_ssd // LANES

    def cidx(c):
        return nc - 1 - c if rev_order else c

    x_spec = pl.BlockSpec((q, wx), lambda g, c: (cidx(c), g))
    dt_spec = pl.BlockSpec((q, LANES), lambda g, c: (cidx(c), g))
    b_spec = pl.BlockSpec((q, LANES), lambda g, c: (cidx(c), boff + g))
    c_spec = pl.BlockSpec((q, LANES), lambda g, c: (cidx(c), boff + SSD_GROUPS + g))
    a_spec = pl.BlockSpec((1, SUBLANES, LANES), lambda g, c: (g, 0, 0))
    st_spec = pl.BlockSpec((1, wx, LANES), lambda g, c: (g, 0, 0))
    ent_spec = pl.BlockSpec((1, 1, wx, LANES), lambda g, c: (g, cidx(c), 0, 0))
    return x_spec, dt_spec, b_spec, c_spec, a_spec, st_spec, ent_spec


def ssd_fwd(xbc, dtg, a_g, h0, rev, w_ssd, name):
    L = xbc.shape[0]
    nc = L // SSD_CHUNK
    hpg = w_ssd // (SSD_GROUPS * HEAD_DIM)
    wx = hpg * HEAD_DIM
    col0 = hpg if rev else 0
    x_spec, dt_spec, b_spec, c_spec, a_spec, st_spec, ent_spec = _ssd_specs(nc, hpg, w_ssd, rev)

    def body(x_ref, dt_ref, b_ref, c_ref, a_ref, h0_ref, y_ref, ent_ref, fin_ref, hs):
        c = pl.program_id(1)

        @pl.when(c == 0)
        def _():
            hs[...] = h0_ref[0]

        ent_ref[0, 0] = hs[...]
        xs = [x_ref[:, e * HEAD_DIM:(e + 1) * HEAD_DIM] for e in range(hpg)]
        hin = [hs[e * HEAD_DIM:(e + 1) * HEAD_DIM, :] for e in range(hpg)]
        ys, hn = _ssd_chunk(xs, dt_ref[...], b_ref[...], c_ref[...], hin, a_ref[0, 0:1, :], rev, col0)
        for e in range(hpg):
            y_ref[:, e * HEAD_DIM:(e + 1) * HEAD_DIM] = ys[e]
            hs[e * HEAD_DIM:(e + 1) * HEAD_DIM, :] = hn[e]
        fin_ref[0] = hs[...]

    return pl.pallas_call(
        body, name=name, grid=(SSD_GROUPS, nc),
        in_specs=[x_spec, dt_spec, b_spec, c_spec, a_spec, st_spec],
        out_specs=[x_spec, ent_spec, st_spec],
        out_shape=[jax.ShapeDtypeStruct((L, w_ssd), F32),
                   jax.ShapeDtypeStruct((SSD_GROUPS, nc, wx, LANES), F32),
                   jax.ShapeDtypeStruct((SSD_GROUPS, wx, LANES), F32)],
        scratch_shapes=[pltpu.VMEM((wx, LANES), F32)],
        compiler_params=_cparams(("parallel", "arbitrary")),
    )(xbc, dtg, xbc, xbc, a_g, h0)


def ssd_bwd(xbc, dtg, a_g, ent, dy, dfin, rev, w_ssd, name, carry=None):
    L = xbc.shape[0]
    nc = L // SSD_CHUNK
    hpg = w_ssd // (SSD_GROUPS * HEAD_DIM)
    wx = hpg * HEAD_DIM
    col0 = hpg if rev else 0
    x_spec, dt_spec, b_spec, c_spec, a_spec, st_spec, ent_spec = _ssd_specs(nc, hpg, w_ssd, not rev)
    bc_out = pl.BlockSpec((SSD_CHUNK, LANES), lambda g, c: ((c if rev else nc - 1 - c), g))
    n_ci, n_co = (len(carry.ins), len(carry.outs)) if carry else (0, 0)

    def body(x_ref, dt_ref, b_ref, c_ref, a_ref, ent_ref, dy_ref, dfin_ref, *rest):
        c_ins = rest[:n_ci]
        dx_ref, ddt_ref, db_ref, dc_ref, da_ref, dh0_ref = rest[n_ci:n_ci + 6]
        c_outs = rest[n_ci + 6:n_ci + 6 + n_co]
        dhs = rest[n_ci + 6 + n_co]
        sems = tuple(rest[n_ci + 7 + n_co:])
        g, c = pl.program_id(0), pl.program_id(1)
        if carry:
            pl.when((g == 0) & (c == 0))(lambda: carry.start(c_ins, c_outs, sems))

        @pl.when(c == 0)
        def _():
            dhs[...] = dfin_ref[0]
            da_ref[...] = jnp.zeros_like(da_ref)

        xs = [x_ref[:, e * HEAD_DIM:(e + 1) * HEAD_DIM] for e in range(hpg)]
        hin = [ent_ref[0, 0, e * HEAD_DIM:(e + 1) * HEAD_DIM, :] for e in range(hpg)]
        _, vjp = jax.vjp(lambda *a: _ssd_chunk(*a, rev, col0), xs, dt_ref[...], b_ref[...], c_ref[...], hin, a_ref[0, 0:1, :])
        dys = [dy_ref[:, e * HEAD_DIM:(e + 1) * HEAD_DIM] for e in range(hpg)]
        dhn = [dhs[e * HEAD_DIM:(e + 1) * HEAD_DIM, :] for e in range(hpg)]
        dxs, ddt, db, dc, dh, da = vjp((dys, dhn))
        for e in range(hpg):
            dx_ref[:, e * HEAD_DIM:(e + 1) * HEAD_DIM] = dxs[e]
            dhs[e * HEAD_DIM:(e + 1) * HEAD_DIM, :] = dh[e]
        ddt_ref[...] = ddt
        db_ref[...] = db
        dc_ref[...] = dc
        da_ref[0, 0:1, :] += da
        dh0_ref[0] = dhs[...]
        if carry:
            pl.when((g == SSD_GROUPS - 1) & (c == nc - 1))(lambda: carry.finish(c_ins, c_outs, sems))

    res = pl.pallas_call(
        body, name=name, grid=(SSD_GROUPS, nc),
        in_specs=[x_spec, dt_spec, b_spec, c_spec, a_spec, ent_spec, x_spec, st_spec] + [_HBM] * n_ci,
        out_specs=[x_spec, dt_spec, bc_out, bc_out, a_spec, st_spec] + [_HBM] * n_co,
        out_shape=[jax.ShapeDtypeStruct((L, w_ssd), F32),
                   jax.ShapeDtypeStruct((L, SSD_GROUPS * LANES), F32),
                   jax.ShapeDtypeStruct((L, SSD_GROUPS * LANES), F32),
                   jax.ShapeDtypeStruct((L, SSD_GROUPS * LANES), F32),
                   jax.ShapeDtypeStruct((SSD_GROUPS, SUBLANES, LANES), F32),
                   jax.ShapeDtypeStruct((SSD_GROUPS, wx, LANES), F32)] + (carry.outs if carry else []),
        scratch_shapes=[pltpu.VMEM((wx, LANES), F32)] + (carry.sem_shapes() if carry else []),
        compiler_params=_cparams(("arbitrary", "arbitrary") if carry else ("parallel", "arbitrary")),
    )(xbc, dtg, xbc, xbc, a_g, ent, dy, dfin, *(carry.ins if carry else []))
    return (list(res[:6]), list(res[6:])) if carry else res


CONV_W = 4
CONV_LEFT = 2
CONV_CB = 128
CONV_ROWS = 256
HALO = 16


def _conv_window(ref, r, rows, nch, L):
    s = pl.multiple_of(r * rows, rows)
    cur = ref[pl.ds(s, rows), :].astype(F32)
    sp = pl.multiple_of(jnp.maximum(s - HALO, 0), HALO)
    sn = pl.multiple_of(jnp.minimum(s + rows, L - HALO), HALO)
    prev = jnp.where(r > 0, ref[pl.ds(sp, HALO), :].astype(F32), 0.0)
    nxt = jnp.where(r < nch - 1, ref[pl.ds(sn, HALO), :].astype(F32), 0.0)
    return jnp.concatenate([prev, cur, nxt], axis=0)


def _shifted(win, off, rows):
    n = win.shape[0]
    return pltpu.roll(win, (-off) % n, axis=0)[HALO:HALO + rows, :]


def _dsilu(p):
    s = _sigmoid(p)
    return s * (1.0 + p * (1.0 - s))


def conv_fwd(src, col_blk0, w, b, act, name):
    L, C = src.shape[0], w.shape[1]
    rows = min(CONV_ROWS, L)
    nch = L // rows

    def body(x_ref, w_ref, b_ref, o_ref):
        def chunk(r, carry):
            win = _conv_window(x_ref, r, rows, nch, L)
            pre = b_ref[...] + sum(w_ref[k:k + 1, :] * _shifted(win, k - CONV_LEFT, rows) for k in range(CONV_W))
            o_ref[pl.ds(pl.multiple_of(r * rows, rows), rows), :] = _silu(pre) if act else pre
            return carry

        lax.fori_loop(0, nch, chunk, 0)

    return pl.pallas_call(
        body, name=name, grid=(C // CONV_CB,),
        in_specs=[pl.BlockSpec((L, CONV_CB), lambda j: (0, col_blk0 + j)),
                  pl.BlockSpec((CONV_W, CONV_CB), lambda j: (0, j)),
                  pl.BlockSpec((1, CONV_CB), lambda j: (0, j))],
        out_specs=pl.BlockSpec((L, CONV_CB), lambda j: (0, j)),
        out_shape=jax.ShapeDtypeStruct((L, C), F32),
        compiler_params=_cparams(("parallel",)),
    )(src, w, b)


def conv_bwd(src, col_blk0, w, b, douts, act, name):
    L, C = src.shape[0], w.shape[1]
    rows = min(CONV_ROWS, L)
    nch = L // rows
    nd = len(douts)

    def body(*refs):
        x_ref, w_ref, b_ref = refs[:3]
        d_refs = refs[3:3 + nd]
        dx_ref, dw_ref, db_ref, dp = refs[3 + nd:]

        def pass1(r, acc):
            sl = pl.ds(pl.multiple_of(r * rows, rows), rows)
            win = _conv_window(x_ref, r, rows, nch, L)
            taps = [_shifted(win, k - CONV_LEFT, rows) for k in range(CONV_W)]
            dpre = sum(d[sl, :] for d in d_refs)
            if act:
                pre = b_ref[...] + sum(w_ref[k:k + 1, :] * taps[k] for k in range(CONV_W))
                dpre = dpre * _dsilu(pre)
            dp[sl, :] = dpre
            new = [acc[k] + jnp.sum(dpre * taps[k], axis=0, keepdims=True) for k in range(CONV_W)]
            return tuple(new) + (acc[CONV_W] + jnp.sum(dpre, axis=0, keepdims=True),)

        zero = jnp.zeros((1, CONV_CB), F32)
        acc = lax.fori_loop(0, nch, pass1, (zero,) * (CONV_W + 1))
        for k in range(CONV_W):
            dw_ref[k:k + 1, :] = acc[k]
        db_ref[...] = acc[CONV_W]

        def pass2(r, carry):
            win = _conv_window(dp, r, rows, nch, L)
            dx = sum(w_ref[k:k + 1, :] * _shifted(win, CONV_LEFT - k, rows) for k in range(CONV_W))
            dx_ref[pl.ds(pl.multiple_of(r * rows, rows), rows), :] = dx
            return carry

        lax.fori_loop(0, nch, pass2, 0)

    col = pl.BlockSpec((L, CONV_CB), lambda j: (0, j))
    return pl.pallas_call(
        body, name=name, grid=(C // CONV_CB,),
        in_specs=[pl.BlockSpec((L, CONV_CB), lambda j: (0, col_blk0 + j)),
                  pl.BlockSpec((CONV_W, CONV_CB), lambda j: (0, j)),
                  pl.BlockSpec((1, CONV_CB), lambda j: (0, j))] + [col] * nd,
        out_specs=[col, pl.BlockSpec((CONV_W, CONV_CB), lambda j: (0, j)), pl.BlockSpec((1, CONV_CB), lambda j: (0, j))],
        out_shape=[jax.ShapeDtypeStruct((L, C), F32), jax.ShapeDtypeStruct((CONV_W, C), F32), jax.ShapeDtypeStruct((1, C), F32)],
        scratch_shapes=[pltpu.VMEM((L, CONV_CB), F32)],
        compiler_params=_cparams(("parallel",)),
    )(src, w, b, *douts)


SCAN_ROWS = 512
SCAN_CB = 512


def _scan8(a, u, rev):
    sub = lax.broadcasted_iota(jnp.int32, a.shape, 0)
    for s in (1, 2, 4):
        if rev:
            ok = sub < SUBLANES - s
            a_sh, u_sh = pltpu.roll(a, SUBLANES - s, axis=0), pltpu.roll(u, SUBLANES - s, axis=0)
        else:
            ok = sub >= s
            a_sh, u_sh = pltpu.roll(a, s, axis=0), pltpu.roll(u, s, axis=0)
        u = a * jnp.where(ok, u_sh, 0.0) + u
        a = a * jnp.where(ok, a_sh, 1.0)
    return a, u


def _shift8(h, carry, rev):
    sub = lax.broadcasted_iota(jnp.int32, h.shape, 0)
    if rev:
        return jnp.where(sub == SUBLANES - 1, carry, pltpu.roll(h, SUBLANES - 1, axis=0))
    return jnp.where(sub == 0, carry, pltpu.roll(h, 1, axis=0))


def lru_scan(a, u, c0, rev, name, adjoint_of=None, add=None):
    L, C = a.shape
    tc, cb = _pick(L, SCAN_ROWS), _pick(C, SCAN_CB)
    nt = L // tc
    ng = tc // SUBLANES
    adj = adjoint_of is not None
    n_in = 3 + (1 if adj else 0) + (1 if add is not None else 0)

    def body(*refs):
        a_ref, u_ref, c0_ref = refs[:3]
        x_ref = refs[3] if (adj or add is not None) else None
        outs = refs[n_in:-1]
        st = refs[-1]

        @pl.when(pl.program_id(1) == 0)
        def _():
            st[...] = c0_ref[...]

        def group(i, carry):
            g = (ng - 1 - i) if rev else i
            sl = pl.ds(pl.multiple_of(g * SUBLANES, SUBLANES), SUBLANES)
            av, uv = a_ref[sl, :], u_ref[sl, :]
            pa, h = _scan8(av, av * uv if adj else uv, rev)
            h = pa * carry + h
            hs = _shift8(h, carry, rev)
            if adj:
                lam = uv + hs
                outs[0][sl, :] = lam
                outs[1][sl, :] = lam * x_ref[sl, :]
            else:
                outs[0][sl, :] = h
                outs[1][sl, :] = hs
                if add is not None:
                    outs[3][sl, :] = h + x_ref[sl, :]
            last = 0 if rev else SUBLANES - 1
            return jnp.broadcast_to(h[last:last + 1, :], h.shape)

        st[...] = lax.fori_loop(0, ng, group, st[...])
        outs[2][...] = st[...]

    def tmap(j, t):
        return ((nt - 1 - t) if rev else t, j)

    blk = pl.BlockSpec((tc, cb), tmap)
    vec = pl.BlockSpec((SUBLANES, cb), lambda j, t: (0, j))
    args = [a, u, c0] + ([adjoint_of] if adj else []) + ([add] if add is not None else [])
    n_big = 2 if (adj or add is None) else 3
    out_specs = [blk, blk, vec] + ([blk] if n_big == 3 else [])
    out_shape = [jax.ShapeDtypeStruct((L, C), F32), jax.ShapeDtypeStruct((L, C), F32), jax.ShapeDtypeStruct((SUBLANES, C), F32)]
    out_shape += [jax.ShapeDtypeStruct((L, C), F32)] if n_big == 3 else []
    return pl.pallas_call(
        body, name=name, grid=(C // cb, nt),
        in_specs=[blk, blk, vec] + [blk] * (n_in - 3),
        out_specs=out_specs, out_shape=out_shape,
        scratch_shapes=[pltpu.VMEM((SUBLANES, cb), F32)],
        compiler_params=_cparams(("parallel", "arbitrary")),
    )(*args)


def f_silu(v):
    return (_silu(v),)


def f_add_bias(v, b):
    return (v + b,)


def f_mod_in(x, gain, shift, scale):
    return _rms(x, gain) * (1.0 + scale) + shift, x


def f_dt(raw, bias):
    return (_softplus(raw + bias),)


def f_gnorm(yf, yb, xs, z, dexp, gain):
    return (_rms((yf + yb + dexp * xs) * _silu(z), gain),)


def f_gelu_gate(r, yr):
    return (r * _gelu_tanh(yr),)


def f_merge(gs, gr, o_s, o_r, bs, br):
    return (_sigmoid(gs + bs) * o_s + _sigmoid(gr + br) * o_r,)


def f_res_mod(x, out, gm, gain, shift, scale):
    x1 = x + gm * out
    return x1, _rms(x1, gain) * (1.0 + scale) + shift


def f_final(x1, f, t, gf, gain):
    e = _rms(x1 + gf * f, gain) - t
    return (jnp.broadcast_to(0.5 * jnp.mean(e * e, axis=-1, keepdims=True), (e.shape[0], LANES)),)


def _lru_coeffs(xh, wa, ba, wx, bx, lam):
    r = _sigmoid(bdot(xh, wa, "nn") + ba)
    i = _sigmoid(bdot(xh, wx, "nn") + bx)
    log_a = -LRU_C * r * _softplus(-lam)
    return jnp.exp(log_a), jnp.sqrt(_neg_expm1(2.0 * log_a)) * (i * xh)


def f_gates(xh, wa_f, ba_f, wx_f, bx_f, lam_f, wa_b, ba_b, wx_b, bx_b, lam_b):
    return _lru_coeffs(xh, wa_f, ba_f, wx_f, bx_f, lam_f) + _lru_coeffs(xh, wa_b, ba_b, wx_b, bx_b, lam_b)


def f_adam(w, g, m, v):
    m2 = ADAM_B1 * m + (1.0 - ADAM_B1) * g
    v2 = ADAM_B2 * v + (1.0 - ADAM_B2) * (g * g)
    m_hat = m2 / (1.0 - ADAM_B1 ** ADAM_STEP)
    v_hat = v2 / (1.0 - ADAM_B2 ** ADAM_STEP)
    return -ADAM_LR * (m_hat / (jnp.sqrt(v_hat) + ADAM_EPS) + ADAM_WD * w), m2, v2


def swiglu(gu, dact, name, tb=128, sub=16):
    L, f2 = gu.shape
    f = f2 // 2
    tb = _pick(L, tb)

    def body(*refs):
        gu_ref, o_ref = refs[0], refs[-1]

        def step(i, carry):
            sl = pl.ds(pl.multiple_of(i * sub, sub), sub)
            g, u = gu_ref[sl, :f].astype(F32), gu_ref[sl, f:].astype(F32)
            if dact is None:
                o_ref[sl, :] = (_silu(g) * u).astype(o_ref.dtype)
            else:
                d = refs[1][sl, :].astype(F32)
                o_ref[sl, :f] = (d * u * _dsilu(g)).astype(o_ref.dtype)
                o_ref[sl, f:] = (d * _silu(g)).astype(o_ref.dtype)
            return carry

        lax.fori_loop(0, tb // sub, step, 0)

    wout = f if dact is None else f2
    ins = [gu] + ([] if dact is None else [dact])
    return pl.pallas_call(
        body, name=name, grid=(L // tb,),
        in_specs=[pl.BlockSpec((tb, a.shape[1]), lambda i: (i, 0)) for a in ins],
        out_specs=pl.BlockSpec((tb, wout), lambda i: (i, 0)),
        out_shape=jax.ShapeDtypeStruct((L, wout), BF16),
        compiler_params=_cparams(("parallel",)),
    )(*ins)


N_DEV = 8
N_CHIP = 4
PACK_W = 1024
_HBM = pl.BlockSpec(memory_space=pltpu.HBM)


def _me():
    return lax.axis_index("x"), lax.axis_index("y"), lax.axis_index("c")


def _flip(pos, k):
    x, y, c = pos
    return (1 - x if k & 4 else x, 1 - y if k & 2 else y, 1 - c if k & 1 else c)


def _dev_index(pos):
    return 4 * pos[0] + 2 * pos[1] + pos[2]


def _chip_index(pos):
    return 2 * pos[0] + pos[1]


def _rcopy(src, dst, sems, k, to):
    send_sems, recv_sems = sems
    return pltpu.make_async_remote_copy(src_ref=src, dst_ref=dst, send_sem=send_sems.at[k], recv_sem=recv_sems.at[k],
                                        device_id=to, device_id_type=MESH)


def fill_own_device_slot(stack, mine):
    return lax.dynamic_update_slice(stack, mine[None], (_dev_index(_me()),) + (0,) * mine.ndim)


COPY_PIECES = 4
ROW_TILE_16BIT = 16


def _pieces(rows, n=COPY_PIECES, align=ROW_TILE_16BIT):
    while n > 1 and rows % (n * align):
        n -= 1
    return [(q * (rows // n), rows // n) for q in range(n)]


class Exchange:
    def __init__(self, ins, outs, n_sems, start, finish):
        self.ins, self.outs, self.n_sems, self.start, self.finish = list(ins), list(outs), n_sems, start, finish

    def sem_shapes(self):
        return [pltpu.SemaphoreType.DMA((self.n_sems,)), pltpu.SemaphoreType.DMA((self.n_sems,))]


def run_exchange(ex, name):
    n_in, n_out = len(ex.ins), len(ex.outs)

    def body(*refs):
        ins, outs, sems = refs[:n_in], refs[n_in:n_in + n_out], tuple(refs[n_in + n_out:])
        ex.start(ins, outs, sems)
        ex.finish(ins, outs, sems)

    return pl.pallas_call(body, name=name, in_specs=[_HBM] * n_in, out_specs=[_HBM] * n_out, out_shape=ex.outs,
                          scratch_shapes=ex.sem_shapes())(*ex.ins)


def allgather_exchange(v):
    def plan(ins, outs):
        me = _me()
        for k in range(1, N_DEV):
            peer = _flip(me, k)
            yield outs[0].at[_dev_index(me)], outs[0].at[_dev_index(peer)], k - 1, peer

    def start(ins, outs, sems):
        for mine, _, k, peer in plan(ins, outs):
            _rcopy(ins[0], mine, sems, k, peer).start()

    def finish(ins, outs, sems):
        for _, theirs, k, peer in plan(ins, outs):
            _rcopy(theirs, theirs, sems, k, peer).wait_recv()
        for mine, _, k, peer in plan(ins, outs):
            _rcopy(ins[0], mine, sems, k, peer).wait_send()

    return Exchange([v], [jax.ShapeDtypeStruct((N_DEV,) + v.shape, v.dtype)], N_DEV - 1, start, finish)


def allgather8(v, name):
    return fill_own_device_slot(run_exchange(allgather_exchange(v), name)[0], v)


def gather_exchange(shards):
    cuts = [_pieces(s.shape[0] // 2) for s in shards]
    base = [0]
    for cu in cuts:
        base.append(base[-1] + 6 * len(cu))

    def plan(ins, outs):
        me = _me()
        chips = [_flip(me, 4), _flip(me, 2), _flip(me, 6)]
        for w, cu in enumerate(cuts):
            half, nq = shards[w].shape[0] // 2, len(cu)
            for q, (s, n) in enumerate(cu):
                mine = pl.ds(pl.multiple_of(me[2] * half + s, ROW_TILE_16BIT), n)
                sibs = pl.ds(pl.multiple_of((1 - me[2]) * half + s, ROW_TILE_16BIT), n)
                for j, p in enumerate(chips):
                    yield w, p, mine, sibs, base[w] + j * nq + q, base[w] + (3 + j) * nq + q

    def start(ins, outs, sems):
        me = _me()
        for w, p, mine, _, k_ici, _ in plan(ins, outs):
            _rcopy(ins[w].at[mine], outs[w].at[_chip_index(me), mine], sems, k_ici, p).start()

    def finish(ins, outs, sems):
        me = _me()
        sibling = _flip(me, 1)
        for w, p, mine, _, k_ici, k_fwd in plan(ins, outs):
            got = outs[w].at[_chip_index(p), mine]
            _rcopy(got, got, sems, k_ici, p).wait_recv()
            _rcopy(got, got, sems, k_fwd, sibling).start()
        for w, p, _, sibs, _, k_fwd in plan(ins, outs):
            got = outs[w].at[_chip_index(p), sibs]
            _rcopy(got, got, sems, k_fwd, sibling).wait_recv()
        for w, p, mine, _, k_ici, k_fwd in plan(ins, outs):
            got = outs[w].at[_chip_index(p), mine]
            _rcopy(ins[w].at[mine], got, sems, k_ici, p).wait_send()
            _rcopy(got, got, sems, k_fwd, sibling).wait_send()

    outs = [jax.ShapeDtypeStruct((N_CHIP,) + s.shape, s.dtype) for s in shards]
    return Exchange(shards, outs, base[-1], start, finish)


def fill_own_slot(stack, mine):
    return lax.dynamic_update_slice(stack, mine[None], (_chip_index(_me()), 0, 0))


def scatter_exchange(parts):
    def plan(ins, outs):
        me = _me()
        for w in range(len(parts)):
            half = parts[w].shape[1] // 2
            for k in range(1, N_DEV):
                to = _flip(me, k)
                src = ins[w].at[_chip_index(to), pl.ds(pl.multiple_of(to[2] * half, ROW_TILE_16BIT), half)]
                yield src, outs[w].at[k - 1], (N_DEV - 1) * w + k - 1, to

    def start(ins, outs, sems):
        for src, dst, k, to in plan(ins, outs):
            _rcopy(src, dst, sems, k, to).start()

    def finish(ins, outs, sems):
        for src, dst, k, to in plan(ins, outs):
            _rcopy(dst, dst, sems, k, to).wait_recv()
        for src, dst, k, to in plan(ins, outs):
            _rcopy(src, dst, sems, k, to).wait_send()

    outs = [jax.ShapeDtypeStruct((N_DEV - 1, p.shape[1] // 2, p.shape[2]), p.dtype) for p in parts]
    return Exchange(parts, outs, (N_DEV - 1) * len(parts), start, finish)


def own_piece(parts):
    me = _me()
    half = parts.shape[1] // 2
    return lax.dynamic_slice(parts, (_chip_index(me), me[2] * half, 0), (1, half, parts.shape[2]))[0]


def join_halves(mine, name):
    half, C = mine.shape
    cuts = _pieces(half, 2 * COPY_PIECES, SUBLANES)
    nq = len(cuts)

    def body(m_ref, out_ref, send_sems, recv_sems):
        me = _me()
        sems = (send_sems, recv_sems)
        sibling = _flip(me, 1)
        sends = [_rcopy(m_ref.at[pl.ds(s, n)], out_ref.at[pl.ds(s, n)], sems, q, sibling) for q, (s, n) in enumerate(cuts)]
        for cp in sends:
            cp.start()
        for q, (s, n) in enumerate(cuts):
            got = out_ref.at[pl.ds(s, n)]
            _rcopy(got, got, sems, q, sibling).wait_recv()
        for cp in sends:
            cp.wait_send()

    other = pl.pallas_call(
        body, name=name, in_specs=[_HBM], out_specs=_HBM,
        out_shape=jax.ShapeDtypeStruct((half, C), mine.dtype),
        scratch_shapes=[pltpu.SemaphoreType.DMA((nq,)), pltpu.SemaphoreType.DMA((nq,))],
    )(mine)
    south = _me()[2] == 0
    return jnp.concatenate([jnp.where(south, mine, other), jnp.where(south, other, mine)], axis=0)


SUM_BLOCK_ELEMS = 256 * 1024


def sum_slots(a, name, first=None):
    n, R, C = a.shape
    tb = _pick(R, max(ROW_TILE_16BIT, SUM_BLOCK_ELEMS // C), ROW_TILE_16BIT)

    def body(a_ref, *rest):
        o_ref = rest[-1]
        acc = a_ref[0].astype(F32) if first is None else rest[0][...].astype(F32) + a_ref[0].astype(F32)
        for j in range(1, n):
            acc = acc + a_ref[j].astype(F32)
        o_ref[...] = acc

    row = pl.BlockSpec((tb, C), lambda i: (i, 0))
    return pl.pallas_call(
        body, name=name, grid=(R // tb,),
        in_specs=[pl.BlockSpec((n, tb, C), lambda i: (0, i, 0))] + ([] if first is None else [row]),
        out_specs=row,
        out_shape=jax.ShapeDtypeStruct((R, C), F32),
        compiler_params=_cparams(("parallel",)),
    )(*([a] if first is None else [a, first]))


def _pack(arrays, dtype, row_align):
    flat = jnp.concatenate([a.astype(dtype).reshape(-1) for a in arrays])
    per = PACK_W * row_align
    n = -(-flat.shape[0] // per) * per
    return jnp.pad(flat, (0, n - flat.shape[0])).reshape(-1, PACK_W)


def _unpack(buf, shapes):
    flat = buf.reshape(-1)
    out, at = [], 0
    for s in shapes:
        n = 1
        for d in s:
            n *= d
        out.append(flat[at:at + n].reshape(s))
        at += n
    return out


def _to_col_major(t, rows):
    L, C = t.shape
    return t.reshape(rows, GRID_W, C).transpose(1, 0, 2).reshape(L, C)


def _to_row_major(t, rows):
    L, C = t.shape
    return t.reshape(GRID_W, rows, C).transpose(1, 0, 2).reshape(L, C)


def _heads_to_groups(v, hpg):
    lead = v.shape[:-1]
    t = jnp.moveaxis(v.reshape(lead + (2, SSD_GROUPS, hpg)), -3, -2).reshape(lead + (SSD_GROUPS, 2 * hpg))
    t = jnp.pad(t, [(0, 0)] * (len(lead) + 1) + [(0, LANES - 2 * hpg)])
    return t.reshape(lead + (SSD_GROUPS * LANES,))


def _groups_to_heads(t, hpg):
    lead = t.shape[:-1]
    t = t.reshape(lead + (SSD_GROUPS, LANES))[..., :2 * hpg].reshape(lead + (SSD_GROUPS, 2, hpg))
    return jnp.moveaxis(t, -2, -3).reshape(lead + (2 * SSD_GROUPS * hpg,))


def _r1(v):
    return v.reshape(1, -1)


class _Layout:
    def __init__(self, D, W, hpg):
        assert W == D and D % (SSD_GROUPS * LANES) == 0
        self.D, self.W, self.hpg = D, W, hpg
        self.GN = SSD_GROUPS * SSD_STATE
        self.xbc0 = 5 * D
        self.dt0 = 5 * D + W + 2 * self.GN
        self.width = self.dt0 + SSD_GROUPS * LANES
        self.xbc_blk = self.xbc0 // CONV_CB
        self.dt_blk = self.dt0 // (SSD_GROUPS * LANES)
        assert self.dt0 % (SSD_GROUPS * LANES) == 0


def _seq_forward(hb, w_all, sp, lay, tag, grid_rows, init, carry=None):
    D, W = lay.D, lay.W
    gw = SSD_GROUPS * LANES
    pg = matmul(hb, w_all, "nn", BF16, f"proj_{tag}", carry=carry)
    pg, carried = pg if carry else (pg, [])
    xbc = conv_fwd(pg, lay.xbc_blk, sp["ssd_conv_w"], sp["ssd_conv_b"], True, f"ssd_conv_{tag}")
    dtg = rowwise(f_dt, [Cols(pg, gw, lay.dt_blk)], [sp["dt_bias_g"]], [(gw, F32)], f"ssd_dt_{tag}")[0]
    yf, ent_f, fin_f = ssd_fwd(xbc, dtg, sp["a_g"], init["ssd_f"], False, W, f"ssd_scan_f_{tag}")
    yb, ent_b, fin_b = ssd_fwd(xbc, dtg, sp["a_g"], init["ssd_b"], True, W, f"ssd_scan_b_{tag}")
    xr = pg[:, D:2 * D]
    if grid_rows:
        xr = _to_col_major(xr, grid_rows)
    xc = conv_fwd(xr, 0, sp["lru_conv_w"], sp["lru_conv_b"], False, f"lru_conv_{tag}")
    a_f, u_f, a_b, u_b = rowwise(f_gates, [xc], sp["gate_pars"], [(LRU_HEAD, F32)] * 4, f"lru_gates_{tag}",
                                 tb=256, sub=256, ncol=D // LRU_HEAD)
    h_f, hs_f, fl_f = lru_scan(a_f, u_f, init["lru_f"], False, f"lru_scan_f_{tag}")
    _, hs_b, fl_b, r = lru_scan(a_b, u_b, init["lru_b"], True, f"lru_scan_b_{tag}", add=h_f)
    saved = dict(pg=pg, xbc=xbc, dtg=dtg, yf=yf, yb=yb, ent_f=ent_f, ent_b=ent_b, xr=xr, xc=xc,
                 a_f=a_f, a_b=a_b, hs_f=hs_f, hs_b=hs_b, r=r)
    return saved, dict(ssd_f=fin_f, ssd_b=fin_b, lru_f=fl_f, lru_b=fl_b), carried


def _seq_backward(sv, sp, lay, tag, grid_rows, dy, dxs_extra, d_r, dfin, carries=(None, None)):
    D, W, GN = lay.D, lay.W, lay.GN
    gw = SSD_GROUPS * LANES
    pg = sv["pg"]
    res_f = ssd_bwd(sv["xbc"], sv["dtg"], sp["a_g"], sv["ent_f"], dy, dfin["ssd_f"], False, W, f"ssd_scan_f_bwd_{tag}", carry=carries[0])
    res_b = ssd_bwd(sv["xbc"], sv["dtg"], sp["a_g"], sv["ent_b"], dy, dfin["ssd_b"], True, W, f"ssd_scan_b_bwd_{tag}", carry=carries[1])
    (dxf, ddt_f, db_f, dc_f, da_f, dh0_f), carried_f = res_f if carries[0] else (res_f, [])
    (dxb, ddt_b, db_b, dc_b, da_b, dh0_b), carried_b = res_b if carries[1] else (res_b, [])
    cw, cb = sp["ssd_conv_w"], sp["ssd_conv_b"]
    d_xs, dw1, db1 = conv_bwd(pg, lay.xbc_blk, cw[:, :W], cb[:, :W], [dxf, dxb] + dxs_extra, True, f"ssd_conv_x_bwd_{tag}")
    d_b, dw2, db2 = conv_bwd(pg, lay.xbc_blk + W // CONV_CB, cw[:, W:W + GN], cb[:, W:W + GN], [db_f, db_b], True, f"ssd_conv_b_bwd_{tag}")
    d_c, dw3, db3 = conv_bwd(pg, lay.xbc_blk + (W + GN) // CONV_CB, cw[:, W + GN:], cb[:, W + GN:], [dc_f, dc_b], True, f"ssd_conv_c_bwd_{tag}")
    d_dtraw, d_dtbias = rowwise_vjp(f_dt, [Cols(pg, gw, lay.dt_blk)], [sp["dt_bias_g"]], [[ddt_f, ddt_b]], [F32], f"ssd_dt_bwd_{tag}")
    du_b, dab, dl0_b = lru_scan(sv["a_b"], d_r, dfin["lru_b"], False, f"lru_scan_b_bwd_{tag}", adjoint_of=sv["hs_b"])
    du_f, daf, dl0_f = lru_scan(sv["a_f"], d_r, dfin["lru_f"], True, f"lru_scan_f_bwd_{tag}", adjoint_of=sv["hs_f"])
    res = rowwise_vjp(f_gates, [sv["xc"]], sp["gate_pars"], [daf, du_f, dab, du_b], [F32], f"lru_gates_bwd_{tag}",
                      tb=256, sub=256, ncol=D // LRU_HEAD)
    d_xc, gate_grads = res[0], res[1:]
    d_xr, g_lcw, g_lcb = conv_bwd(sv["xr"], 0, sp["lru_conv_w"], sp["lru_conv_b"], [d_xc], False, f"lru_conv_bwd_{tag}")
    if grid_rows:
        d_xr = _to_row_major(d_xr, grid_rows)
    grads = dict(ssd_conv_w=jnp.concatenate([dw1, dw2, dw3], axis=1), ssd_conv_b=jnp.concatenate([db1, db2, db3], axis=1),
                 dt_bias_g=d_dtbias, a_g=da_f + da_b, lru_conv_w=g_lcw, lru_conv_b=g_lcb, gate_pars=list(gate_grads))
    pieces = dict(xr=d_xr, xs=d_xs, b=d_b, c=d_c, dt=d_dtraw)
    return pieces, grads, dict(ssd_f=dh0_f, ssd_b=dh0_b, lru_f=dl0_f, lru_b=dl0_b), (carried_f, carried_b)


def _proj_cotangent(lay, px, pc, Lc, dz, dyr, dgs, dgr):
    zero = jnp.zeros((Lc, lay.D), BF16)
    cols = [(dz, zero), (px["xr"], pc["xr"]), (dyr, zero), (dgs, zero), (dgr, zero),
            (px["xs"], pc["xs"]), (px["b"], pc["b"]), (px["c"], pc["c"]), (px["dt"], pc["dt"])]
    return jnp.concatenate([jnp.concatenate([a.astype(BF16), b.astype(BF16)], axis=0) for a, b in cols], axis=1)


EARLY = ["w_in", "w_gate"]
LATE = ["w_out_ssd", "w_out_lru", "w_o", "ffn_w13", "ffn_w2"]


def _local_step(x, ctx, target, mod_x, mod_c, wb, late, ws):
    L, D = x.shape
    Lc = ctx.shape[0]
    nh = ws["ssd_d"].shape[0]
    hpg = nh // SSD_GROUPS
    W = nh * HEAD_DIM
    lay = _Layout(D, W, hpg)
    GN = lay.GN
    grid_rows = L // GRID_W
    nlh = D // LRU_HEAD

    w_in = wb["w_in"]
    o_dt, o_xr = 2 * W + 2 * GN, 2 * W + 2 * GN + 2 * nh
    w_all = jnp.concatenate([w_in[:, :W], w_in[:, o_xr:o_xr + D], w_in[:, o_xr + D:], wb["w_gate"], w_in[:, W:o_dt],
                             _heads_to_groups(w_in[:, o_dt:o_xr], hpg)], axis=1)
    a_neg = -jnp.exp(ws["ssd_a_log"])
    a_g = jnp.pad(_heads_to_groups(a_neg.reshape(-1), hpg).reshape(SSD_GROUPS, 1, LANES), [(0, 0), (0, SUBLANES - 1), (0, 0)])
    gate_pars = []
    for d in range(2):
        gate_pars += [ws["lru_w_a"][d], ws["lru_b_a"][d].reshape(nlh, 1, LRU_HEAD), ws["lru_w_x"][d],
                      ws["lru_b_x"][d].reshape(nlh, 1, LRU_HEAD), ws["lru_lambda"][d].reshape(nlh, 1, LRU_HEAD)]
    sp = dict(ssd_conv_w=ws["ssd_conv_w"], ssd_conv_b=_r1(ws["ssd_conv_b"]), dt_bias_g=_r1(_heads_to_groups(ws["ssd_dt_bias"].reshape(-1), hpg)),
              a_g=a_g, lru_conv_w=ws["lru_conv_w"], lru_conv_b=_r1(ws["lru_conv_b"]), gate_pars=gate_pars)
    dexp = _r1(jnp.repeat(ws["ssd_d"], HEAD_DIM))
    norm_mix, norm_ffn, ssd_norm, final_norm = _r1(ws["norm_mix"]), _r1(ws["norm_ffn"]), _r1(ws["ssd_norm"]), _r1(ws["final_norm"])
    bg_s, bg_r = _r1(ws["b_gate"][:D]), _r1(ws["b_gate"][D:])
    sh_m, sc_m, g_m, sh_f, sc_f, g_f = [mod_x[:, k * D:(k + 1) * D] for k in range(6)]
    csh_m, csc_m = mod_c[:, :D], mod_c[:, D:2 * D]

    zst = dict(ssd_f=jnp.zeros((SSD_GROUPS, hpg * HEAD_DIM, SSD_STATE), F32), ssd_b=jnp.zeros((SSD_GROUPS, hpg * HEAD_DIM, SSD_STATE), F32),
               lru_f=jnp.zeros((SUBLANES, D), F32), lru_b=jnp.zeros((SUBLANES, D), F32))
    hcb = rowwise(f_mod_in, [ctx], [norm_mix, csh_m, csc_m], [(D, BF16)], "norm_mix_ctx")[0]
    hb = rowwise(f_mod_in, [x], [norm_mix, sh_m, sc_m], [(D, BF16)], "norm_mix_x")[0]
    svc, fin_c, _ = _seq_forward(hcb, w_all, sp, lay, "ctx", None, zst)
    svx, _, stacks = _seq_forward(hb, w_all, sp, lay, "x", grid_rows, fin_c, carry=gather_exchange([late[n] for n in LATE]))
    wb = dict(wb)
    for n, st in zip(LATE, stacks):
        wb[n] = _whole_from_shards(fill_own_slot(st, late[n]), n in BIG_BY_COLUMNS)
    pg = svx["pg"]
    r_rm = _to_row_major(svx["r"], grid_rows)
    gn_rows = [svx["yf"], svx["yb"], Cols(svx["xbc"], W, 0), Cols(pg, D, 0)]
    yn = rowwise(f_gnorm, gn_rows, [dexp, ssd_norm], [(W, BF16)], "ssd_gnorm")[0]
    o_s = matmul(yn, wb["w_out_ssd"], "nn", BF16, "out_ssd")
    o_in = rowwise(f_gelu_gate, [r_rm, Cols(pg, D, 2)], [], [(D, BF16)], "lru_gelu")[0]
    o_r = matmul(o_in, wb["w_out_lru"], "nn", BF16, "out_lru")
    mg_rows = [Cols(pg, D, 3), Cols(pg, D, 4), o_s, o_r]
    mixed = rowwise(f_merge, mg_rows, [bg_s, bg_r], [(D, BF16)], "merge")[0]
    out = matmul(mixed, wb["w_o"], "nn", F32, "out_proj")
    x1, h2 = rowwise(f_res_mod, [x, out], [g_m, norm_ffn, sh_f, sc_f], [(D, F32), (D, BF16)], "res_norm_ffn")
    gu = matmul(h2, wb["ffn_w13"], "nn", BF16, "ffn_in")
    act = swiglu(gu, None, "ffn_act")
    f = matmul(act, wb["ffn_w2"], "nn", F32, "ffn_out")

    ones = jnp.full((L, LANES), 1.0 / LANES, F32)
    dx1a, df, d_gf, d_fnorm, lsum = rowwise_vjp(f_final, [x1, f, target], [g_f, final_norm], [ones], [F32, BF16, None],
                                                "final_loss", out_sums=[(0, LANES)])
    loss = lsum[0, 0]
    d_act = matmul(df, wb["ffn_w2"], "nt", BF16, "ffn_out_dx")
    g_w2 = matmul(act, df, "tn", BF16, "ffn_out_dw")
    d_gu = swiglu(gu, d_act, "ffn_act_bwd")
    dh2 = matmul(d_gu, wb["ffn_w13"], "nt", F32, "ffn_in_dx")
    g_w13 = matmul(h2, d_gu, "tn", BF16, "ffn_in_dw")
    dxa, d_out, d_gm, d_nffn, d_shf, d_scf = rowwise_vjp(f_res_mod, [x, out], [g_m, norm_ffn, sh_f, sc_f], [dx1a, dh2],
                                                         [F32, BF16], "res_norm_ffn_bwd")
    d_mixed = matmul(d_out, wb["w_o"], "nt", F32, "out_proj_dx")
    g_wo = matmul(mixed, d_out, "tn", BF16, "out_proj_dw")
    dgs, dgr, do_s, do_r, d_bgs, d_bgr = rowwise_vjp(f_merge, mg_rows, [bg_s, bg_r], [d_mixed], [BF16] * 4, "merge_bwd")
    d_yn = matmul(do_s, wb["w_out_ssd"], "nt", F32, "out_ssd_dx")
    g_wos = matmul(yn, do_s, "tn", BF16, "out_ssd_dw")
    d_oin = matmul(do_r, wb["w_out_lru"], "nt", F32, "out_lru_dx")
    g_wol = matmul(o_in, do_r, "tn", BF16, "out_lru_dw")
    d_r_rm, d_yr = rowwise_vjp(f_gelu_gate, [r_rm, Cols(pg, D, 2)], [], [d_oin], [F32, BF16], "lru_gelu_bwd")
    dy, dxs_skip, dz, d_dexp, d_ssdn = rowwise_vjp(f_gnorm, gn_rows, [dexp, ssd_norm], [d_yn], [F32, None, F32, BF16], "ssd_gnorm_bwd")
    zfin = dict(zst)
    parts = {n: _shards_from_whole(g, n in BIG_BY_COLUMNS).astype(BF16)
             for n, g in (("ffn_w2", g_w2), ("ffn_w13", g_w13), ("w_o", g_wo), ("w_out_ssd", g_wos), ("w_out_lru", g_wol))}
    ride_f, ride_b = ["ffn_w2", "ffn_w13"], ["w_o", "w_out_ssd", "w_out_lru"]
    px, gx, dst, (land_f, land_b) = _seq_backward(
        svx, sp, lay, "x", grid_rows, dy, [dxs_skip], _to_col_major(d_r_rm, grid_rows), zfin,
        carries=(scatter_exchange([parts[n] for n in ride_f]), scatter_exchange([parts[n] for n in ride_b])))
    pc, gc, _, _ = _seq_backward(svc, sp, lay, "ctx", None, jnp.zeros((Lc, W), F32), [], jnp.zeros((Lc, D), F32), dst)
    dpg = _proj_cotangent(lay, px, pc, Lc, dz, d_yr, dgs, dgr)
    gg = [a + b for a, b in zip(gx["gate_pars"], gc["gate_pars"])]
    lru_w = jnp.concatenate([gg[0], gg[5], gg[2], gg[7]], axis=0).reshape(-1, PACK_W)
    g_wall, (lru_w_all,) = matmul(jnp.concatenate([hb, hcb], axis=0), dpg, "tn", BF16, "proj_dw", carry=allgather_exchange(lru_w))
    lru_w_all = fill_own_device_slot(lru_w_all, lru_w)
    xb0 = lay.xbc0
    g_w_in = jnp.concatenate([g_wall[:, :W], g_wall[:, xb0:xb0 + W + 2 * GN], _groups_to_heads(g_wall[:, lay.dt0:], hpg),
                              g_wall[:, D:2 * D], g_wall[:, 2 * D:3 * D]], axis=1)
    ride_p = ["w_in", "w_gate"]
    parts.update({n: _shards_from_whole(g, True).astype(BF16) for n, g in (("w_in", g_w_in), ("w_gate", g_wall[:, 3 * D:5 * D]))})
    dh, land_p = matmul(dpg, w_all, "nt", F32, "proj_dx_x", a_rows=(0, L), carry=scatter_exchange([parts[n] for n in ride_p]))
    scattered = {n: (ld, own_piece(parts[n])) for n, ld in zip(ride_f + ride_b + ride_p, land_f + land_b + land_p)}
    dhc = matmul(dpg, w_all, "nt", F32, "proj_dx_ctx", a_rows=(L, Lc))
    grad_x, d_nmix_x, d_shm, d_scm = rowwise_vjp(f_mod_in, [x], [norm_mix, sh_m, sc_m], [dh, dxa], [F32], "norm_mix_x_bwd")
    d_nmix_c, d_cshm, d_cscm = rowwise_vjp(f_mod_in, [ctx], [norm_mix, csh_m, csc_m], [dhc, jnp.zeros((Lc, D), F32)], [None], "norm_mix_ctx_bwd")

    dmod_x = jnp.concatenate([d_shm, d_scm, d_gm, d_shf, d_scf, d_gf], axis=1)
    dmod_c = jnp.concatenate([d_cshm, d_cscm, jnp.zeros((1, 4 * D), F32)], axis=1)

    d_a = _groups_to_heads((gx["a_g"] + gc["a_g"])[:, 0, :].reshape(-1), hpg).reshape(2, nh)

    def gate(k, shape):
        return jnp.stack([gg[k].reshape(shape), gg[5 + k].reshape(shape)], axis=0)

    grads = dict(
        norm_mix=(d_nmix_x + d_nmix_c).reshape(-1), norm_ffn=d_nffn.reshape(-1),
        ssd_conv_w=gx["ssd_conv_w"] + gc["ssd_conv_w"], ssd_conv_b=(gx["ssd_conv_b"] + gc["ssd_conv_b"]).reshape(-1),
        ssd_dt_bias=_groups_to_heads((gx["dt_bias_g"] + gc["dt_bias_g"]).reshape(-1), hpg).reshape(2, nh),
        ssd_a_log=d_a * a_neg, ssd_d=d_dexp.reshape(nh, HEAD_DIM).sum(axis=1), ssd_norm=d_ssdn.reshape(-1),
        lru_conv_w=gx["lru_conv_w"] + gc["lru_conv_w"], lru_conv_b=(gx["lru_conv_b"] + gc["lru_conv_b"]).reshape(-1),
        lru_b_a=gate(1, (D,)), lru_b_x=gate(3, (D,)), lru_lambda=gate(4, (D,)),
        b_gate=jnp.concatenate([d_bgs, d_bgr], axis=1).reshape(-1), final_norm=d_fnorm.reshape(-1))
    return loss, grad_x, grads, scattered, lru_w_all, dmod_x, dmod_c


WEIGHTS = ["c_ctx", "w_ada", "b_ada", "norm_mix", "norm_ffn", "w_in", "ssd_conv_w", "ssd_conv_b", "ssd_dt_bias", "ssd_a_log",
           "ssd_d", "ssd_norm", "w_out_ssd", "lru_conv_w", "lru_conv_b", "lru_w_a", "lru_b_a", "lru_w_x", "lru_b_x", "lru_lambda",
           "w_out_lru", "w_gate", "b_gate", "w_o", "ffn_w13", "ffn_w2", "final_norm"]
BIG = ["w_in", "w_out_ssd", "w_out_lru", "w_gate", "w_o", "ffn_w13", "ffn_w2"]
BIG_BY_COLUMNS = {"w_in", "w_gate", "ffn_w13"}
MOD_ROWS = 16
MOD_CTX_ROW = N_DEV
SMALL_SHARDED = ["ssd_conv_w", "lru_conv_w", "lru_b_a", "lru_b_x", "lru_lambda"]
ADAM_ROWS = 64


def _whole_from_shards(stack, by_columns):
    if by_columns:
        return stack.transpose(1, 0, 2).reshape(stack.shape[1], -1)
    return stack.reshape(-1, stack.shape[2])


def _shards_from_whole(g, by_columns):
    if by_columns:
        return g.reshape(g.shape[0], N_CHIP, -1).transpose(1, 0, 2)
    return g.reshape(N_CHIP, -1, g.shape[1])


def _adam(w, g, m, v, name):
    shape = w.shape
    cols = shape[-1]
    w2, g2, m2, v2 = [t.reshape(-1, cols) for t in (w, g, m, v)]
    d, nm, nv = rowwise(f_adam, [w2, g2, m2, v2], [], [(cols, F32)] * 3, name, tb=ADAM_ROWS, sub=8)
    return d.reshape(shape), nm.reshape(shape), nv.reshape(shape)


def kernel(x, c, ctx, c_ctx, w_ada, b_ada, norm_mix, norm_ffn, w_in, ssd_conv_w, ssd_conv_b, ssd_dt_bias, ssd_a_log, ssd_d, ssd_norm, w_out_ssd, lru_conv_w, lru_conv_b, lru_w_a, lru_b_a, lru_w_x, lru_b_x, lru_lambda, w_out_lru, w_gate, b_gate, w_o, ffn_w13, ffn_w2, final_norm, loss_target, m_c_ctx, m_w_ada, m_b_ada, m_norm_mix, m_norm_ffn, m_w_in, m_ssd_conv_w, m_ssd_conv_b, m_ssd_dt_bias, m_ssd_a_log, m_ssd_d, m_ssd_norm, m_w_out_ssd, m_lru_conv_w, m_lru_conv_b, m_lru_w_a, m_lru_b_a, m_lru_w_x, m_lru_b_x, m_lru_lambda, m_w_out_lru, m_w_gate, m_b_gate, m_w_o, m_ffn_w13, m_ffn_w2, m_final_norm, v_c_ctx, v_w_ada, v_b_ada, v_norm_mix, v_norm_ffn, v_w_in, v_ssd_conv_w, v_ssd_conv_b, v_ssd_dt_bias, v_ssd_a_log, v_ssd_d, v_ssd_norm, v_w_out_ssd, v_lru_conv_w, v_lru_conv_b, v_lru_w_a, v_lru_b_a, v_lru_w_x, v_lru_b_x, v_lru_lambda, v_w_out_lru, v_w_gate, v_b_gate, v_w_o, v_ffn_w13, v_ffn_w2, v_final_norm):
    given = dict(locals())
    layered = {n for n in WEIGHTS if n not in ("c_ctx", "final_norm")}
    w_blk = {n: (given[n][0] if n in layered else given[n]) for n in WEIGHTS}
    me = _me()
    chip, dev = _chip_index(me), _dev_index(me)
    D = c.shape[1]

    ada_w = w_blk["w_ada"].astype(BF16)
    ada_n = ada_w.shape[1]
    c_all = allgather8(jnp.concatenate([c, jnp.zeros((SUBLANES - 1, D), F32)], axis=0), "gather_c")[:, 0, :]
    cc = jnp.concatenate([c_all, c_ctx[None], jnp.zeros((MOD_ROWS - N_DEV - 1, D), F32)], axis=0)
    s_cc = rowwise(f_silu, [cc], [], [(D, BF16)], "mod_silu", tb=MOD_ROWS)[0]
    ada_b = lax.dynamic_slice_in_dim(w_blk["b_ada"], chip * ada_n, ada_n)[None]
    mod_blk = rowwise(f_add_bias, [matmul(s_cc, ada_w, "nn", F32, "mod_proj")], [ada_b], [(ada_n, F32)], "mod_bias", tb=MOD_ROWS)[0]
    mod = allgather8(mod_blk, "gather_mod")[::2].transpose(1, 0, 2).reshape(MOD_ROWS, N_CHIP * ada_n)
    mod_x = lax.dynamic_slice_in_dim(mod, dev, 1, axis=0)
    mod_c = mod[MOD_CTX_ROW:MOD_CTX_ROW + 1]

    shard16 = {n: w_blk[n].astype(BF16) for n in BIG}
    stacks = run_exchange(gather_exchange([shard16[n] for n in EARLY]), "gather_early")
    wb = {n: _whole_from_shards(fill_own_slot(st, shard16[n]), n in BIG_BY_COLUMNS) for n, st in zip(EARLY, stacks)}

    sm_shapes = [w_blk[n].shape for n in SMALL_SHARDED]
    sm_all = allgather8(_pack([w_blk[n] for n in SMALL_SHARDED], F32, ROW_TILE_16BIT), "gather_small")[::2]
    ws = {n: w_blk[n] for n in WEIGHTS if n not in BIG and n != "w_ada"}
    for n, s in zip(SMALL_SHARDED, _unpack_stacked(sm_all, sm_shapes)):
        ws[n] = jnp.concatenate([s[k] for k in range(N_CHIP)], axis=-1)

    loss, grad_x, grads, scattered, lru_w_all, dmod_x, dmod_c = _local_step(x[0], ctx[0], loss_target[0], mod_x, mod_c, wb,
                                                                            {n: shard16[n] for n in LATE}, ws)
    loss = lax.psum(loss, ("x", "y", "c"))

    dm = allgather8(jnp.concatenate([dmod_x, dmod_c, jnp.zeros((SUBLANES - 2, 6 * D), F32)], axis=0), "gather_dmod")
    dm_ctx = sum_slots(dm, "sum_dmod")[1:2]
    dmod = jnp.concatenate([dm[:, 0, :], dm_ctx, jnp.zeros((MOD_ROWS - N_DEV - 1, 6 * D), F32)], axis=0)
    g_shard = {"b_ada": rowwise_vjp(f_add_bias, [dmod], [w_blk["b_ada"][None]], [dmod], [None], "mod_bias_bwd", tb=MOD_ROWS)[0].reshape(-1)}
    d_blk = lax.dynamic_slice_in_dim(dmod, chip * ada_n, ada_n, axis=1).astype(BF16)
    g_shard["w_ada"] = matmul(s_cc, d_blk, "tn", F32, "mod_proj_dw")
    d_silu_ctx = matmul(d_blk, ada_w, "nt", F32, "mod_proj_dx")[MOD_CTX_ROW]
    grads["c_ctx_silu"] = 0.5 * d_silu_ctx

    for n in BIG:
        landed, own = scattered[n]
        g_shard[n] = join_halves(sum_slots(landed, "sum_" + n, first=own), "join_" + n)

    small = [n for n in WEIGHTS if n not in BIG and n != "w_ada"]
    lru_w = sum_slots(lru_w_all, "sum_lru_w").reshape((4,) + w_blk["lru_w_a"].shape[1:])
    g_shard["lru_w_a"], g_shard["lru_w_x"] = lru_w[:2], lru_w[2:]
    reduced = [n for n in small if n not in ("b_ada", "c_ctx", "lru_w_a", "lru_w_x")] + ["c_ctx_silu"]
    sm_g = sum_slots(allgather8(_pack([grads[n] for n in reduced], F32, ROW_TILE_16BIT), "gather_small_grads"), "sum_small_grads")
    for n, g in zip(reduced, _unpack(sm_g, [grads[n].shape for n in reduced])):
        if n in SMALL_SHARDED:
            per = g.shape[-1] // N_CHIP
            g = lax.dynamic_slice_in_dim(g, chip * per, per, axis=g.ndim - 1)
        g_shard[n] = g
    ctx_rows = jnp.concatenate([c_ctx[None], jnp.zeros((SUBLANES - 1, D), F32)], axis=0)
    ctx_cot = jnp.concatenate([g_shard.pop("c_ctx_silu")[None], jnp.zeros((SUBLANES - 1, D), F32)], axis=0)
    g_shard["c_ctx"] = rowwise_vjp(f_silu, [ctx_rows], [], [ctx_cot], [F32], "mod_silu_bwd", tb=SUBLANES, sub=SUBLANES)[0][0]

    out_g, out_d, out_m, out_v = {}, {}, {}, {}
    for n in BIG + ["w_ada"]:
        out_g[n] = g_shard[n]
        out_d[n], out_m[n], out_v[n] = _adam(w_blk[n], g_shard[n], given["m_" + n][0], given["v_" + n][0], "adamw_" + n)
    sm_blk_shapes = [w_blk[n].shape for n in small]
    packed = [_pack([t[n].reshape(w_blk[n].shape) for n in small], F32, ADAM_ROWS)
              for t in (w_blk, g_shard, {n: given["m_" + n] for n in small}, {n: given["v_" + n] for n in small})]
    res = _adam(*packed, "adamw_small")
    for tgt, buf in zip((out_d, out_m, out_v), res):
        tgt.update(zip(small, _unpack(buf, sm_blk_shapes)))
    for n in small:
        out_g[n] = g_shard[n].reshape(w_blk[n].shape)

    def full(n, t):
        return t.reshape(given[n].shape)

    return (loss, grad_x[None], *[full(n, out_g[n]) for n in WEIGHTS], *[full(n, out_d[n]) for n in WEIGHTS],
            *[full(n, out_m[n]) for n in WEIGHTS], *[full(n, out_v[n]) for n in WEIGHTS])


def _unpack_stacked(buf, shapes):
    k = buf.shape[0]
    flat = buf.reshape(k, -1)
    out, at = [], 0
    for s in shapes:
        n = 1
        for d in s:
            n *= d
        out.append(flat[:, at:at + n].reshape((k,) + tuple(s)))
        at += n
    return out
```

```python
import functools

import jax
import jax.numpy as jnp
from jax import lax
from jax.experimental import pallas as pl
from jax.experimental.pallas import tpu as pltpu

F32 = jnp.float32
BF16 = jnp.bfloat16
MESH = pl.DeviceIdType.MESH

V7X_VMEM_LIMIT_BYTES = 56 * 1024 * 1024
LANES = 128
SUBLANES = 8

HEAD_DIM = 64
SSD_STATE = 128
SSD_GROUPS = 4
SSD_CHUNK = 128
GRID_W = 64
LRU_HEAD = 128
LRU_C = 8.0
EPS = 1e-6
NEG = -1e30

ADAM_LR = 0.001
ADAM_B1 = 0.9
ADAM_B2 = 0.999
ADAM_EPS = 1e-08
ADAM_WD = 0.01
ADAM_STEP = 10


def _cparams(sem=None, **kw):
    if sem is not None:
        kw["dimension_semantics"] = sem
    return pltpu.CompilerParams(vmem_limit_bytes=V7X_VMEM_LIMIT_BYTES, **kw)


_DN = {"nn": (((1,), (0,)), ((), ())), "nt": (((1,), (1,)), ((), ())), "tn": (((0,), (0,)), ((), ()))}


def _raw_dot(a, b, form, precision=None):
    return lax.dot_general(a, b, _DN[form], precision=precision, preferred_element_type=F32)


def _dot_bwd_forms(form):
    return {"nn": (("g", "b", "nt"), ("a", "g", "tn")),
            "nt": (("g", "b", "nn"), ("g", "a", "tn")),
            "tn": (("b", "g", "nt"), ("a", "g", "nn"))}[form]


def _make_dot(rounding):
    @functools.partial(jax.custom_vjp, nondiff_argnums=(2,))
    def dot(a, b, form):
        if rounding is None:
            return _raw_dot(a, b, form, precision=lax.Precision.HIGHEST)
        return _raw_dot(a.astype(rounding), b.astype(rounding), form)

    def fwd(a, b, form):
        return dot(a, b, form), (a, b)

    def bwd(form, res, g):
        a, b = res
        ops = {"a": a, "b": b, "g": g}
        (l1, r1, f1), (l2, r2, f2) = _dot_bwd_forms(form)
        return dot(ops[l1], ops[r1], f1).astype(a.dtype), dot(ops[l2], ops[r2], f2).astype(b.dtype)

    dot.defvjp(fwd, bwd)
    return dot


bdot = _make_dot(BF16)
hdot = _make_dot(None)


def _sigmoid(v):
    return 0.5 * jnp.tanh(0.5 * v) + 0.5


def _silu(v):
    return v * _sigmoid(v)


def _softplus(v):
    return jnp.maximum(v, 0.0) + jnp.log1p(jnp.exp(-jnp.abs(v)))


def _gelu_tanh(v):
    return 0.5 * v * (1.0 + jnp.tanh(0.7978845608028654 * (v + 0.044715 * v * v * v)))


def _neg_expm1(v):
    poly = -v * (1.0 + v * (0.5 + v * (1.0 / 6.0 + v * (1.0 / 24.0 + v * (1.0 / 120.0)))))
    return jnp.where(v > -0.05, poly, 1.0 - jnp.exp(v))


def _rms(t, gain):
    return t * lax.rsqrt(jnp.mean(t * t, axis=-1, keepdims=True) + EPS) * gain


def _pick(dim, want, mult=1):
    if dim <= want:
        return dim
    for t in range(want - want % mult, 0, -mult):
        if dim % t == 0:
            return t
    t = min(dim, want)
    while dim % t:
        t //= 2
    return t


MM_TILE_M = 1024
MM_TILE_N = 1536
MM_VMEM_BYTES = 46 * 1024 * 1024


def matmul(a, b, form, out_dtype, name, tm=MM_TILE_M, tn=MM_TILE_N, tk=None, a_rows=None, carry=None):
    if form == "nn":
        (m, k), (k2, n) = a.shape, b.shape
    elif form == "nt":
        (m, k), (n, k2) = a.shape, b.shape
    else:
        (k, m), (k2, n) = a.shape, b.shape
    assert k == k2, (a.shape, b.shape, form)
    row0 = 0
    if a_rows is not None:
        assert form != "tn"
        row0, m = a_rows
    tm, tn = _pick(m, tm, LANES), _pick(n, tn, LANES)
    assert row0 % tm == 0
    blk0 = row0 // tm
    if tk is None:
        fixed = tm * tn * (2 * jnp.dtype(out_dtype).itemsize + 4)
        tk = (MM_VMEM_BYTES - fixed) // (2 * a.dtype.itemsize * (tm + tn))
    tk = _pick(k, tk, LANES)
    nk = k // tk

    grid = (m // tm, n // tn, nk)
    n_ci, n_co = (len(carry.ins), len(carry.outs)) if carry else (0, 0)

    def body(a_ref, b_ref, *rest):
        c_ins, o_ref, c_outs = rest[:n_ci], rest[n_ci], rest[n_ci + 1:n_ci + 1 + n_co]
        scratch = rest[n_ci + 1 + n_co:]
        kk = pl.program_id(2)
        if carry:
            at_step = [pl.program_id(d) for d in range(3)]
            sems = tuple(scratch[-2:])
            pl.when((at_step[0] == 0) & (at_step[1] == 0) & (at_step[2] == 0))(lambda: carry.start(c_ins, c_outs, sems))
        part = _raw_dot(a_ref[...], b_ref[...], form)
        if nk == 1:
            o_ref[...] = part.astype(o_ref.dtype)
        else:
            acc = scratch[0]

            @pl.when(kk == 0)
            def _():
                acc[...] = part

            @pl.when(kk > 0)
            def _():
                acc[...] += part

            @pl.when(kk == nk - 1)
            def _():
                o_ref[...] = acc[...].astype(o_ref.dtype)

        if carry:
            last = (at_step[0] == grid[0] - 1) & (at_step[1] == grid[1] - 1) & (at_step[2] == grid[2] - 1)
            pl.when(last)(lambda: carry.finish(c_ins, c_outs, sems))

    a_spec = pl.BlockSpec((tk, tm), lambda i, j, kk: (kk, i)) if form == "tn" else pl.BlockSpec((tm, tk), lambda i, j, kk: (blk0 + i, kk))
    b_spec = pl.BlockSpec((tn, tk), lambda i, j, kk: (j, kk)) if form == "nt" else pl.BlockSpec((tk, tn), lambda i, j, kk: (kk, j))
    res = pl.pallas_call(
        body, name=name, grid=grid, in_specs=[a_spec, b_spec] + [_HBM] * n_ci,
        out_specs=[pl.BlockSpec((tm, tn), lambda i, j, kk: (i, j))] + [_HBM] * n_co,
        out_shape=[jax.ShapeDtypeStruct((m, n), out_dtype)] + (carry.outs if carry else []),
        scratch_shapes=([pltpu.VMEM((tm, tn), F32)] if nk > 1 else []) + (carry.sem_shapes() if carry else []),
        compiler_params=_cparams(("arbitrary",) * 3 if carry else ("parallel", "parallel", "arbitrary")),
    )(a, b, *(carry.ins if carry else []))
    return (res[0], list(res[1:])) if carry else res[0]


class Cols:
    def __init__(self, arr, w, j):
        assert (j + 1) * w <= arr.shape[1]
        self.arr, self.w, self.j = arr, w, j


def _row_width(x, ncol):
    return x.w if isinstance(x, Cols) else x.shape[1] // ncol


def _row_spec(x, tb, ncol):
    if isinstance(x, Cols):
        j0 = x.j
        return x.arr, pl.BlockSpec((tb, x.w), lambda j, i: (i, j0 + j))
    return x, pl.BlockSpec((tb, x.shape[1] // ncol), lambda j, i: (i, j))


def _par_spec(shape, ncol):
    if ncol == 1:
        return pl.BlockSpec(shape, lambda j, i, nd=len(shape): (0,) * nd)
    assert len(shape) == 3 and shape[0] == ncol, shape
    return pl.BlockSpec((1,) + tuple(shape[1:]), lambda j, i: (j, 0, 0))


def _par_value(ref, ncol):
    return ref[...] if ncol == 1 else ref[0]


def rowwise(fn, rows, pars, outs, name, tb=256, sub=16, ncol=1):
    m = (rows[0].arr if isinstance(rows[0], Cols) else rows[0]).shape[0]
    tb = _pick(m, tb)
    sub = min(sub, tb)
    nr, npar = len(rows), len(pars)

    def body(*refs):
        row_refs, par_refs, out_refs = refs[:nr], refs[nr:nr + npar], refs[nr + npar:]
        pv = [_par_value(p, ncol) for p in par_refs]

        def step(i, carry):
            sl = pl.ds(pl.multiple_of(i * sub, sub), sub)
            res = fn(*[r[sl, :].astype(F32) for r in row_refs], *pv)
            for o, v in zip(out_refs, res):
                o[sl, :] = v.astype(o.dtype)
            return carry

        lax.fori_loop(0, tb // sub, step, 0)

    arrs, specs = zip(*[_row_spec(r, tb, ncol) for r in rows])
    return pl.pallas_call(
        body, name=name, grid=(ncol, m // tb),
        in_specs=list(specs) + [_par_spec(p.shape, ncol) for p in pars],
        out_specs=[pl.BlockSpec((tb, w), lambda j, i: (i, j)) for w, _ in outs],
        out_shape=[jax.ShapeDtypeStruct((m, w * ncol), dt) for w, dt in outs],
        compiler_params=_cparams(("parallel", "parallel")),
    )(*arrs, *pars)


def rowwise_vjp(fn, rows, pars, cots, drow_dtypes, name, tb=256, sub=16, out_sums=(), ncol=1):
    m = (rows[0].arr if isinstance(rows[0], Cols) else rows[0]).shape[0]
    tb = _pick(m, tb)
    sub = min(sub, tb)
    cots = [list(c) if isinstance(c, (list, tuple)) else [c] for c in cots]
    flat = [c for group in cots for c in group]
    nr, npar, nc = len(rows), len(pars), len(flat)
    want = [i for i, d in enumerate(drow_dtypes) if d is not None]
    assert ncol == 1 or not out_sums

    def body(*refs):
        row_refs, par_refs = refs[:nr], refs[nr:nr + npar]
        cot_refs = list(refs[nr + npar:nr + npar + nc])
        drow_refs = refs[nr + npar + nc:nr + npar + nc + len(want)]
        acc_refs = refs[nr + npar + nc + len(want):]
        pv = [_par_value(p, ncol) for p in par_refs]

        def step(i, acc):
            sl = pl.ds(pl.multiple_of(i * sub, sub), sub)
            rv = [r[sl, :].astype(F32) for r in row_refs]
            res, vjp = jax.vjp(lambda rr, pp: tuple(fn(*rr, *pp)), rv, pv)
            ct, at = [], 0
            for group in cots:
                ct.append(sum(c[sl, :].astype(F32) for c in cot_refs[at:at + len(group)]))
                at += len(group)
            d_rows, d_pars = vjp(tuple(ct))
            for o, idx in zip(drow_refs, want):
                o[sl, :] = d_rows[idx].astype(o.dtype)
            sums = [jnp.sum(res[k], axis=0, keepdims=True) for k, _ in out_sums]
            return tuple(a + d for a, d in zip(acc, list(d_pars) + sums))

        init = tuple(jnp.zeros(p.shape, F32) for p in pv) + tuple(jnp.zeros((1, w), F32) for _, w in out_sums)
        acc = lax.fori_loop(0, tb // sub, step, init)

        @pl.when(pl.program_id(1) == 0)
        def _():
            for o in acc_refs:
                o[...] = jnp.zeros_like(o)

        for o, a in zip(acc_refs, acc):
            if ncol == 1:
                o[...] += a
            else:
                o[0] += a

    arrs, specs = zip(*[_row_spec(r, tb, ncol) for r in rows])
    carrs, cspecs = zip(*[_row_spec(c, tb, ncol) for c in flat])
    widths = [_row_width(r, ncol) for r in rows]
    acc_shapes = [tuple(p.shape) for p in pars] + [(1, w) for _, w in out_sums]
    return pl.pallas_call(
        body, name=name, grid=(ncol, m // tb),
        in_specs=list(specs) + [_par_spec(p.shape, ncol) for p in pars] + list(cspecs),
        out_specs=[pl.BlockSpec((tb, widths[i]), lambda j, i_: (i_, j)) for i in want]
        + [_par_spec(s, ncol) for s in acc_shapes],
        out_shape=[jax.ShapeDtypeStruct((m, widths[i] * ncol), drow_dtypes[i]) for i in want]
        + [jax.ShapeDtypeStruct(s, F32) for s in acc_shapes],
        compiler_params=_cparams(("parallel", "arbitrary")),
    )(*arrs, *pars, *carrs)


def _col(v, c):
    lane = lax.broadcasted_iota(jnp.int32, v.shape, 1)
    return jnp.sum(jnp.where(lane == c, v, 0.0), axis=1, keepdims=True)


def _row(v, r):
    sub = lax.broadcasted_iota(jnp.int32, v.shape, 0)
    return jnp.sum(jnp.where(sub == r, v, 0.0), axis=0, keepdims=True)


def _ssd_chunk(xs, dt, bm, cm, hs, a_row, rev, col0):
    q = dt.shape[0]
    ii = lax.broadcasted_iota(jnp.int32, (q, q), 0)
    jj = lax.broadcasted_iota(jnp.int32, (q, q), 1)
    keep = (jj >= ii) if rev else (jj <= ii)
    tri = keep.astype(F32)
    dta = dt * a_row
    a_cum = hdot(tri, dta, "nn")
    tri_t = ((jj <= ii) if rev else (jj >= ii)).astype(F32)
    a_cum_t = hdot(dta, tri_t, "tn")
    cb = bdot(cm, bm, "nt")
    last = 0 if rev else q - 1
    ys, hn = [], []
    for e, (x, h) in enumerate(zip(xs, hs)):
        c = col0 + e
        ac = _col(a_cum, c)
        seg = ac - _row(a_cum_t, c)
        lm = jnp.exp(jnp.where(keep, seg, NEG))
        xdt = x * _col(dt, c)
        y_diag = bdot(cb * lm, xdt, "nn")
        y_off = bdot(cm * jnp.exp(ac), h, "nt")
        ys.append(y_diag + y_off)
        tot = _row(ac, last)
        states = bdot(xdt, bm * jnp.exp(tot - ac), "tn")
        hn.append(jnp.exp(tot) * h + states)
    return ys, hn


def _ssd_specs(nc, hpg, w_ssd, rev_order):
    q = SSD_CHUNK
    wx = hpg * HEAD_DIM
    boff = w_ssd // LANES

    def cidx(c):
        return nc - 1 - c if rev_order else c

    x_spec = pl.BlockSpec((q, wx), lambda g, c: (cidx(c), g))
    dt_spec = pl.BlockSpec((q, LANES), lambda g, c: (cidx(c), g))
    b_spec = pl.BlockSpec((q, LANES), lambda g, c: (cidx(c), boff + g))
    c_spec = pl.BlockSpec((q, LANES), lambda g, c: (cidx(c), boff + SSD_GROUPS + g))
    a_spec = pl.BlockSpec((1, SUBLANES, LANES), lambda g, c: (g, 0, 0))
    st_spec = pl.BlockSpec((1, wx, LANES), lambda g, c: (g, 0, 0))
    ent_spec = pl.BlockSpec((1, 1, wx, LANES), lambda g, c: (g, cidx(c), 0, 0))
    return x_spec, dt_spec, b_spec, c_spec, a_spec, st_spec, ent_spec


def ssd_fwd(xbc, dtg, a_g, h0, rev, w_ssd, name):
    L = xbc.shape[0]
    nc = L // SSD_CHUNK
    hpg = w_ssd // (SSD_GROUPS * HEAD_DIM)
    wx = hpg * HEAD_DIM
    col0 = hpg if rev else 0
    x_spec, dt_spec, b_spec, c_spec, a_spec, st_spec, ent_spec = _ssd_specs(nc, hpg, w_ssd, rev)

    def body(x_ref, dt_ref, b_ref, c_ref, a_ref, h0_ref, y_ref, ent_ref, fin_ref, hs):
        c = pl.program_id(1)

        @pl.when(c == 0)
        def _():
            hs[...] = h0_ref[0]

        ent_ref[0, 0] = hs[...]
        xs = [x_ref[:, e * HEAD_DIM:(e + 1) * HEAD_DIM].astype(F32) for e in range(hpg)]
        hin = [hs[e * HEAD_DIM:(e + 1) * HEAD_DIM, :] for e in range(hpg)]
        ys, hn = _ssd_chunk(xs, dt_ref[...], b_ref[...].astype(F32), c_ref[...].astype(F32), hin, a_ref[0, 0:1, :], rev, col0)
        for e in range(hpg):
            y_ref[:, e * HEAD_DIM:(e + 1) * HEAD_DIM] = ys[e].astype(y_ref.dtype)
            hs[e * HEAD_DIM:(e + 1) * HEAD_DIM, :] = hn[e]
        fin_ref[0] = hs[...]

    return pl.pallas_call(
        body, name=name, grid=(SSD_GROUPS, nc),
        in_specs=[x_spec, dt_spec, b_spec, c_spec, a_spec, st_spec],
        out_specs=[x_spec, ent_spec, st_spec],
        out_shape=[jax.ShapeDtypeStruct((L, w_ssd), BF16),
                   jax.ShapeDtypeStruct((SSD_GROUPS, nc, wx, LANES), F32),
                   jax.ShapeDtypeStruct((SSD_GROUPS, wx, LANES), F32)],
        scratch_shapes=[pltpu.VMEM((wx, LANES), F32)],
        compiler_params=_cparams(("parallel", "arbitrary")),
    )(xbc, dtg, xbc, xbc, a_g, h0)


def ssd_bwd(xbc, dtg, a_g, ent, dy, dfin, rev, w_ssd, name, carry=None):
    L = xbc.shape[0]
    nc = L // SSD_CHUNK
    hpg = w_ssd // (SSD_GROUPS * HEAD_DIM)
    wx = hpg * HEAD_DIM
    col0 = hpg if rev else 0
    x_spec, dt_spec, b_spec, c_spec, a_spec, st_spec, ent_spec = _ssd_specs(nc, hpg, w_ssd, not rev)
    bc_out = pl.BlockSpec((SSD_CHUNK, LANES), lambda g, c: ((c if rev else nc - 1 - c), g))
    n_ci, n_co = (len(carry.ins), len(carry.outs)) if carry else (0, 0)

    def body(x_ref, dt_ref, b_ref, c_ref, a_ref, ent_ref, dy_ref, dfin_ref, *rest):
        c_ins = rest[:n_ci]
        dx_ref, ddt_ref, db_ref, dc_ref, da_ref, dh0_ref = rest[n_ci:n_ci + 6]
        c_outs = rest[n_ci + 6:n_ci + 6 + n_co]
        dhs = rest[n_ci + 6 + n_co]
        sems = tuple(rest[n_ci + 7 + n_co:])
        g, c = pl.program_id(0), pl.program_id(1)
        if carry:
            pl.when((g == 0) & (c == 0))(lambda: carry.start(c_ins, c_outs, sems))

        @pl.when(c == 0)
        def _():
            dhs[...] = dfin_ref[0]
            da_ref[...] = jnp.zeros_like(da_ref)

        xs = [x_ref[:, e * HEAD_DIM:(e + 1) * HEAD_DIM].astype(F32) for e in range(hpg)]
        hin = [ent_ref[0, 0, e * HEAD_DIM:(e + 1) * HEAD_DIM, :] for e in range(hpg)]
        _, vjp = jax.vjp(lambda *a: _ssd_chunk(*a, rev, col0), xs, dt_ref[...], b_ref[...].astype(F32), c_ref[...].astype(F32), hin,
                         a_ref[0, 0:1, :])
        dys = [dy_ref[:, e * HEAD_DIM:(e + 1) * HEAD_DIM].astype(F32) for e in range(hpg)]
        dhn = [dhs[e * HEAD_DIM:(e + 1) * HEAD_DIM, :] for e in range(hpg)]
        dxs, ddt, db, dc, dh, da = vjp((dys, dhn))
        for e in range(hpg):
            dx_ref[:, e * HEAD_DIM:(e + 1) * HEAD_DIM] = dxs[e].astype(dx_ref.dtype)
            dhs[e * HEAD_DIM:(e + 1) * HEAD_DIM, :] = dh[e]
        ddt_ref[...] = ddt
        db_ref[...] = db
        dc_ref[...] = dc
        da_ref[0, 0:1, :] += da
        dh0_ref[0] = dhs[...]
        if carry:
            pl.when((g == SSD_GROUPS - 1) & (c == nc - 1))(lambda: carry.finish(c_ins, c_outs, sems))

    res = pl.pallas_call(
        body, name=name, grid=(SSD_GROUPS, nc),
        in_specs=[x_spec, dt_spec, b_spec, c_spec, a_spec, ent_spec, x_spec, st_spec] + [_HBM] * n_ci,
        out_specs=[x_spec, dt_spec, bc_out, bc_out, a_spec, st_spec] + [_HBM] * n_co,
        out_shape=[jax.ShapeDtypeStruct((L, w_ssd), BF16),
                   jax.ShapeDtypeStruct((L, SSD_GROUPS * LANES), F32),
                   jax.ShapeDtypeStruct((L, SSD_GROUPS * LANES), F32),
                   jax.ShapeDtypeStruct((L, SSD_GROUPS * LANES), F32),
                   jax.ShapeDtypeStruct((SSD_GROUPS, SUBLANES, LANES), F32),
                   jax.ShapeDtypeStruct((SSD_GROUPS, wx, LANES), F32)] + (carry.outs if carry else []),
        scratch_shapes=[pltpu.VMEM((wx, LANES), F32)] + (carry.sem_shapes() if carry else []),
        compiler_params=_cparams(("arbitrary", "arbitrary") if carry else ("parallel", "arbitrary")),
    )(xbc, dtg, xbc, xbc, a_g, ent, dy, dfin, *(carry.ins if carry else []))
    return (list(res[:6]), list(res[6:])) if carry else res


CONV_W = 4
CONV_LEFT = 2
CONV_CB = 128
CONV_ROWS = 256
HALO = 16


def _conv_window(ref, r, rows, nch, L):
    s = pl.multiple_of(r * rows, rows)
    cur = ref[pl.ds(s, rows), :].astype(F32)
    sp = pl.multiple_of(jnp.maximum(s - HALO, 0), HALO)
    sn = pl.multiple_of(jnp.minimum(s + rows, L - HALO), HALO)
    prev = jnp.where(r > 0, ref[pl.ds(sp, HALO), :].astype(F32), 0.0)
    nxt = jnp.where(r < nch - 1, ref[pl.ds(sn, HALO), :].astype(F32), 0.0)
    return jnp.concatenate([prev, cur, nxt], axis=0)


def _shifted(win, off, rows):
    n = win.shape[0]
    return pltpu.roll(win, (-off) % n, axis=0)[HALO:HALO + rows, :]


def _dsilu(p):
    s = _sigmoid(p)
    return s * (1.0 + p * (1.0 - s))


def conv_fwd(src, col_blk0, w, b, act, name, out_dtype=F32):
    L, C = src.shape[0], w.shape[1]
    rows = min(CONV_ROWS, L)
    nch = L // rows

    def body(x_ref, w_ref, b_ref, o_ref):
        def chunk(r, carry):
            win = _conv_window(x_ref, r, rows, nch, L)
            pre = b_ref[...] + sum(w_ref[k:k + 1, :] * _shifted(win, k - CONV_LEFT, rows) for k in range(CONV_W))
            o_ref[pl.ds(pl.multiple_of(r * rows, rows), rows), :] = (_silu(pre) if act else pre).astype(o_ref.dtype)
            return carry

        lax.fori_loop(0, nch, chunk, 0)

    return pl.pallas_call(
        body, name=name, grid=(C // CONV_CB,),
        in_specs=[pl.BlockSpec((L, CONV_CB), lambda j: (0, col_blk0 + j)),
                  pl.BlockSpec((CONV_W, CONV_CB), lambda j: (0, j)),
                  pl.BlockSpec((1, CONV_CB), lambda j: (0, j))],
        out_specs=pl.BlockSpec((L, CONV_CB), lambda j: (0, j)),
        out_shape=jax.ShapeDtypeStruct((L, C), out_dtype),
        compiler_params=_cparams(("parallel",)),
    )(src, w, b)


def conv_bwd(src, col_blk0, w, b, douts, act, name):
    L, C = src.shape[0], w.shape[1]
    rows = min(CONV_ROWS, L)
    nch = L // rows
    nd = len(douts)

    def body(*refs):
        x_ref, w_ref, b_ref = refs[:3]
        d_refs = refs[3:3 + nd]
        dx_ref, dw_ref, db_ref, dp = refs[3 + nd:]

        def pass1(r, acc):
            sl = pl.ds(pl.multiple_of(r * rows, rows), rows)
            win = _conv_window(x_ref, r, rows, nch, L)
            taps = [_shifted(win, k - CONV_LEFT, rows) for k in range(CONV_W)]
            dpre = sum(d[sl, :].astype(F32) for d in d_refs)
            if act:
                pre = b_ref[...] + sum(w_ref[k:k + 1, :] * taps[k] for k in range(CONV_W))
                dpre = dpre * _dsilu(pre)
            dp[sl, :] = dpre
            new = [acc[k] + jnp.sum(dpre * taps[k], axis=0, keepdims=True) for k in range(CONV_W)]
            return tuple(new) + (acc[CONV_W] + jnp.sum(dpre, axis=0, keepdims=True),)

        zero = jnp.zeros((1, CONV_CB), F32)
        acc = lax.fori_loop(0, nch, pass1, (zero,) * (CONV_W + 1))
        for k in range(CONV_W):
            dw_ref[k:k + 1, :] = acc[k]
        db_ref[...] = acc[CONV_W]

        def pass2(r, carry):
            win = _conv_window(dp, r, rows, nch, L)
            dx = sum(w_ref[k:k + 1, :] * _shifted(win, CONV_LEFT - k, rows) for k in range(CONV_W))
            dx_ref[pl.ds(pl.multiple_of(r * rows, rows), rows), :] = dx.astype(dx_ref.dtype)
            return carry

        lax.fori_loop(0, nch, pass2, 0)

    col = pl.BlockSpec((L, CONV_CB), lambda j: (0, j))
    return pl.pallas_call(
        body, name=name, grid=(C // CONV_CB,),
        in_specs=[pl.BlockSpec((L, CONV_CB), lambda j: (0, col_blk0 + j)),
                  pl.BlockSpec((CONV_W, CONV_CB), lambda j: (0, j)),
                  pl.BlockSpec((1, CONV_CB), lambda j: (0, j))] + [col] * nd,
        out_specs=[col, pl.BlockSpec((CONV_W, CONV_CB), lambda j: (0, j)), pl.BlockSpec((1, CONV_CB), lambda j: (0, j))],
        out_shape=[jax.ShapeDtypeStruct((L, C), BF16), jax.ShapeDtypeStruct((CONV_W, C), F32), jax.ShapeDtypeStruct((1, C), F32)],
        scratch_shapes=[pltpu.VMEM((L, CONV_CB), F32)],
        compiler_params=_cparams(("parallel",)),
    )(src, w, b, *douts)


SCAN_ROWS = 512
SCAN_CB = 512


def _scan8(a, u, rev):
    sub = lax.broadcasted_iota(jnp.int32, a.shape, 0)
    for s in (1, 2, 4):
        if rev:
            ok = sub < SUBLANES - s
            a_sh, u_sh = pltpu.roll(a, SUBLANES - s, axis=0), pltpu.roll(u, SUBLANES - s, axis=0)
        else:
            ok = sub >= s
            a_sh, u_sh = pltpu.roll(a, s, axis=0), pltpu.roll(u, s, axis=0)
        u = a * jnp.where(ok, u_sh, 0.0) + u
        a = a * jnp.where(ok, a_sh, 1.0)
    return a, u


def _shift8(h, carry, rev):
    sub = lax.broadcasted_iota(jnp.int32, h.shape, 0)
    if rev:
        return jnp.where(sub == SUBLANES - 1, carry, pltpu.roll(h, SUBLANES - 1, axis=0))
    return jnp.where(sub == 0, carry, pltpu.roll(h, 1, axis=0))


def lru_scan(a, u, c0, rev, name, adjoint_of=None, add=None):
    L, C = a.shape
    tc, cb = _pick(L, SCAN_ROWS), _pick(C, SCAN_CB)
    nt = L // tc
    ng = tc // SUBLANES
    adj = adjoint_of is not None
    n_in = 3 + (1 if adj else 0) + (1 if add is not None else 0)

    def body(*refs):
        a_ref, u_ref, c0_ref = refs[:3]
        x_ref = refs[3] if (adj or add is not None) else None
        outs = refs[n_in:-1]
        st = refs[-1]

        @pl.when(pl.program_id(1) == 0)
        def _():
            st[...] = c0_ref[...]

        def group(i, carry):
            g = (ng - 1 - i) if rev else i
            sl = pl.ds(pl.multiple_of(g * SUBLANES, SUBLANES), SUBLANES)
            av, uv = a_ref[sl, :], u_ref[sl, :]
            pa, h = _scan8(av, av * uv if adj else uv, rev)
            h = pa * carry + h
            hs = _shift8(h, carry, rev)
            if adj:
                lam = uv + hs
                outs[0][sl, :] = lam
                outs[1][sl, :] = lam * x_ref[sl, :]
            else:
                outs[0][sl, :] = h
                outs[1][sl, :] = hs
                if add is not None:
                    outs[3][sl, :] = h + x_ref[sl, :]
            last = 0 if rev else SUBLANES - 1
            return jnp.broadcast_to(h[last:last + 1, :], h.shape)

        st[...] = lax.fori_loop(0, ng, group, st[...])
        outs[2][...] = st[...]

    def tmap(j, t):
        return ((nt - 1 - t) if rev else t, j)

    blk = pl.BlockSpec((tc, cb), tmap)
    vec = pl.BlockSpec((SUBLANES, cb), lambda j, t: (0, j))
    args = [a, u, c0] + ([adjoint_of] if adj else []) + ([add] if add is not None else [])
    n_big = 2 if (adj or add is None) else 3
    out_specs = [blk, blk, vec] + ([blk] if n_big == 3 else [])
    out_shape = [jax.ShapeDtypeStruct((L, C), F32), jax.ShapeDtypeStruct((L, C), F32), jax.ShapeDtypeStruct((SUBLANES, C), F32)]
    out_shape += [jax.ShapeDtypeStruct((L, C), F32)] if n_big == 3 else []
    return pl.pallas_call(
        body, name=name, grid=(C // cb, nt),
        in_specs=[blk, blk, vec] + [blk] * (n_in - 3),
        out_specs=out_specs, out_shape=out_shape,
        scratch_shapes=[pltpu.VMEM((SUBLANES, cb), F32)],
        compiler_params=_cparams(("parallel", "arbitrary")),
    )(*args)


def f_silu(v):
    return (_silu(v),)


def f_add_bias(v, b):
    return (v + b,)


def f_mod_in(x, gain, shift, scale):
    return _rms(x, gain) * (1.0 + scale) + shift, x


def f_dt(raw, bias):
    return (_softplus(raw + bias),)


def f_gnorm(yf, yb, xs, z, dexp, gain):
    return (_rms((yf + yb + dexp * xs) * _silu(z), gain),)


def f_gelu_gate(r, yr):
    return (r * _gelu_tanh(yr),)


def f_merge(gs, gr, o_s, o_r, bs, br):
    return (_sigmoid(gs + bs) * o_s + _sigmoid(gr + br) * o_r,)


def f_res_mod(x, out, gm, gain, shift, scale):
    x1 = x + gm * out
    return x1, _rms(x1, gain) * (1.0 + scale) + shift


def f_final(x1, f, t, gf, gain):
    e = _rms(x1 + gf * f, gain) - t
    return (jnp.broadcast_to(0.5 * jnp.mean(e * e, axis=-1, keepdims=True), (e.shape[0], LANES)),)


def _lru_coeffs(xh, wa, ba, wx, bx, lam):
    r = _sigmoid(bdot(xh, wa, "nn") + ba)
    i = _sigmoid(bdot(xh, wx, "nn") + bx)
    log_a = -LRU_C * r * _softplus(-lam)
    return jnp.exp(log_a), jnp.sqrt(_neg_expm1(2.0 * log_a)) * (i * xh)


def f_gates(xh, wa_f, ba_f, wx_f, bx_f, lam_f, wa_b, ba_b, wx_b, bx_b, lam_b):
    return _lru_coeffs(xh, wa_f, ba_f, wx_f, bx_f, lam_f) + _lru_coeffs(xh, wa_b, ba_b, wx_b, bx_b, lam_b)


def f_adam(w, g, m, v):
    m2 = ADAM_B1 * m + (1.0 - ADAM_B1) * g
    v2 = ADAM_B2 * v + (1.0 - ADAM_B2) * (g * g)
    m_hat = m2 / (1.0 - ADAM_B1 ** ADAM_STEP)
    v_hat = v2 / (1.0 - ADAM_B2 ** ADAM_STEP)
    return -ADAM_LR * (m_hat / (jnp.sqrt(v_hat) + ADAM_EPS) + ADAM_WD * w), m2, v2


def swiglu(gu, dact, name, tb=128, sub=16):
    L, f2 = gu.shape
    f = f2 // 2
    tb = _pick(L, tb)

    def body(*refs):
        gu_ref, o_ref = refs[0], refs[-1]

        def step(i, carry):
            sl = pl.ds(pl.multiple_of(i * sub, sub), sub)
            g, u = gu_ref[sl, :f].astype(F32), gu_ref[sl, f:].astype(F32)
            if dact is None:
                o_ref[sl, :] = (_silu(g) * u).astype(o_ref.dtype)
            else:
                d = refs[1][sl, :].astype(F32)
                o_ref[sl, :f] = (d * u * _dsilu(g)).astype(o_ref.dtype)
                o_ref[sl, f:] = (d * _silu(g)).astype(o_ref.dtype)
            return carry

        lax.fori_loop(0, tb // sub, step, 0)

    wout = f if dact is None else f2
    ins = [gu] + ([] if dact is None else [dact])
    return pl.pallas_call(
        body, name=name, grid=(L // tb,),
        in_specs=[pl.BlockSpec((tb, a.shape[1]), lambda i: (i, 0)) for a in ins],
        out_specs=pl.BlockSpec((tb, wout), lambda i: (i, 0)),
        out_shape=jax.ShapeDtypeStruct((L, wout), BF16),
        compiler_params=_cparams(("parallel",)),
    )(*ins)


N_DEV = 8
N_CHIP = 4
PACK_W = 1024
_HBM = pl.BlockSpec(memory_space=pltpu.HBM)


def _me():
    return lax.axis_index("x"), lax.axis_index("y"), lax.axis_index("c")


def _flip(pos, k):
    x, y, c = pos
    return (1 - x if k & 4 else x, 1 - y if k & 2 else y, 1 - c if k & 1 else c)


def _dev_index(pos):
    return 4 * pos[0] + 2 * pos[1] + pos[2]


def _chip_index(pos):
    return 2 * pos[0] + pos[1]


def _rcopy(src, dst, sems, k, to):
    send_sems, recv_sems = sems
    return pltpu.make_async_remote_copy(src_ref=src, dst_ref=dst, send_sem=send_sems.at[k], recv_sem=recv_sems.at[k],
                                        device_id=to, device_id_type=MESH)


def fill_own_device_slot(stack, mine):
    return lax.dynamic_update_slice(stack, mine[None], (_dev_index(_me()),) + (0,) * mine.ndim)


COPY_PIECES = 4
ROW_TILE_16BIT = 16


def _pieces(rows, n=COPY_PIECES, align=ROW_TILE_16BIT):
    while n > 1 and rows % (n * align):
        n -= 1
    return [(q * (rows // n), rows // n) for q in range(n)]


class Exchange:
    def __init__(self, ins, outs, n_sems, start, finish):
        self.ins, self.outs, self.n_sems, self.start, self.finish = list(ins), list(outs), n_sems, start, finish

    def sem_shapes(self):
        return [pltpu.SemaphoreType.DMA((self.n_sems,)), pltpu.SemaphoreType.DMA((self.n_sems,))]


def run_exchange(ex, name):
    n_in, n_out = len(ex.ins), len(ex.outs)

    def body(*refs):
        ins, outs, sems = refs[:n_in], refs[n_in:n_in + n_out], tuple(refs[n_in + n_out:])
        ex.start(ins, outs, sems)
        ex.finish(ins, outs, sems)

    return pl.pallas_call(body, name=name, in_specs=[_HBM] * n_in, out_specs=[_HBM] * n_out, out_shape=ex.outs,
                          scratch_shapes=ex.sem_shapes())(*ex.ins)


def allgather_exchange(v):
    def plan(ins, outs):
        me = _me()
        for k in range(1, N_DEV):
            peer = _flip(me, k)
            yield outs[0].at[_dev_index(me)], outs[0].at[_dev_index(peer)], k - 1, peer

    def start(ins, outs, sems):
        for mine, _, k, peer in plan(ins, outs):
            _rcopy(ins[0], mine, sems, k, peer).start()

    def finish(ins, outs, sems):
        for _, theirs, k, peer in plan(ins, outs):
            _rcopy(theirs, theirs, sems, k, peer).wait_recv()
        for mine, _, k, peer in plan(ins, outs):
            _rcopy(ins[0], mine, sems, k, peer).wait_send()

    return Exchange([v], [jax.ShapeDtypeStruct((N_DEV,) + v.shape, v.dtype)], N_DEV - 1, start, finish)


def allgather8(v, name):
    return fill_own_device_slot(run_exchange(allgather_exchange(v), name)[0], v)


def gather_exchange(shards):
    cuts = [_pieces(s.shape[0] // 2) for s in shards]
    base = [0]
    for cu in cuts:
        base.append(base[-1] + 6 * len(cu))

    def plan(ins, outs):
        me = _me()
        chips = [_flip(me, 4), _flip(me, 2), _flip(me, 6)]
        for w, cu in enumerate(cuts):
            half, nq = shards[w].shape[0] // 2, len(cu)
            for q, (s, n) in enumerate(cu):
                mine = pl.ds(pl.multiple_of(me[2] * half + s, ROW_TILE_16BIT), n)
                sibs = pl.ds(pl.multiple_of((1 - me[2]) * half + s, ROW_TILE_16BIT), n)
                for j, p in enumerate(chips):
                    yield w, p, mine, sibs, base[w] + j * nq + q, base[w] + (3 + j) * nq + q

    def start(ins, outs, sems):
        me = _me()
        for w, p, mine, _, k_ici, _ in plan(ins, outs):
            _rcopy(ins[w].at[mine], outs[w].at[_chip_index(me), mine], sems, k_ici, p).start()

    def finish(ins, outs, sems):
        me = _me()
        sibling = _flip(me, 1)
        for w, p, mine, _, k_ici, k_fwd in plan(ins, outs):
            got = outs[w].at[_chip_index(p), mine]
            _rcopy(got, got, sems, k_ici, p).wait_recv()
            _rcopy(got, got, sems, k_fwd, sibling).start()
        for w, p, _, sibs, _, k_fwd in plan(ins, outs):
            got = outs[w].at[_chip_index(p), sibs]
            _rcopy(got, got, sems, k_fwd, sibling).wait_recv()
        for w, p, mine, _, k_ici, k_fwd in plan(ins, outs):
            got = outs[w].at[_chip_index(p), mine]
            _rcopy(ins[w].at[mine], got, sems, k_ici, p).wait_send()
            _rcopy(got, got, sems, k_fwd, sibling).wait_send()

    outs = [jax.ShapeDtypeStruct((N_CHIP,) + s.shape, s.dtype) for s in shards]
    return Exchange(shards, outs, base[-1], start, finish)


def fill_own_slot(stack, mine):
    return lax.dynamic_update_slice(stack, mine[None], (_chip_index(_me()), 0, 0))


def scatter_exchange(parts):
    def plan(ins, outs):
        me = _me()
        for w in range(len(parts)):
            half = parts[w].shape[1] // 2
            for k in range(1, N_DEV):
                to = _flip(me, k)
                src = ins[w].at[_chip_index(to), pl.ds(pl.multiple_of(to[2] * half, ROW_TILE_16BIT), half)]
                yield src, outs[w].at[k - 1], (N_DEV - 1) * w + k - 1, to

    def start(ins, outs, sems):
        for src, dst, k, to in plan(ins, outs):
            _rcopy(src, dst, sems, k, to).start()

    def finish(ins, outs, sems):
        for src, dst, k, to in plan(ins, outs):
            _rcopy(dst, dst, sems, k, to).wait_recv()
        for src, dst, k, to in plan(ins, outs):
            _rcopy(src, dst, sems, k, to).wait_send()

    outs = [jax.ShapeDtypeStruct((N_DEV - 1, p.shape[1] // 2, p.shape[2]), p.dtype) for p in parts]
    return Exchange(parts, outs, (N_DEV - 1) * len(parts), start, finish)


def own_piece(parts):
    me = _me()
    half = parts.shape[1] // 2
    return lax.dynamic_slice(parts, (_chip_index(me), me[2] * half, 0), (1, half, parts.shape[2]))[0]


def join_halves(mine, name):
    half, C = mine.shape
    cuts = _pieces(half, 2 * COPY_PIECES, SUBLANES)
    nq = len(cuts)

    def body(m_ref, out_ref, send_sems, recv_sems):
        me = _me()
        sems = (send_sems, recv_sems)
        sibling = _flip(me, 1)
        sends = [_rcopy(m_ref.at[pl.ds(s, n)], out_ref.at[pl.ds(s, n)], sems, q, sibling) for q, (s, n) in enumerate(cuts)]
        for cp in sends:
            cp.start()
        for q, (s, n) in enumerate(cuts):
            got = out_ref.at[pl.ds(s, n)]
            _rcopy(got, got, sems, q, sibling).wait_recv()
        for cp in sends:
            cp.wait_send()

    other = pl.pallas_call(
        body, name=name, in_specs=[_HBM], out_specs=_HBM,
        out_shape=jax.ShapeDtypeStruct((half, C), mine.dtype),
        scratch_shapes=[pltpu.SemaphoreType.DMA((nq,)), pltpu.SemaphoreType.DMA((nq,))],
    )(mine)
    south = _me()[2] == 0
    return jnp.concatenate([jnp.where(south, mine, other), jnp.where(south, other, mine)], axis=0)


SUM_BLOCK_ELEMS = 256 * 1024


def sum_slots(a, name, first=None):
    n, R, C = a.shape
    tb = _pick(R, max(ROW_TILE_16BIT, SUM_BLOCK_ELEMS // C), ROW_TILE_16BIT)

    def body(a_ref, *rest):
        o_ref = rest[-1]
        acc = a_ref[0].astype(F32) if first is None else rest[0][...].astype(F32) + a_ref[0].astype(F32)
        for j in range(1, n):
            acc = acc + a_ref[j].astype(F32)
        o_ref[...] = acc

    row = pl.BlockSpec((tb, C), lambda i: (i, 0))
    return pl.pallas_call(
        body, name=name, grid=(R // tb,),
        in_specs=[pl.BlockSpec((n, tb, C), lambda i: (0, i, 0))] + ([] if first is None else [row]),
        out_specs=row,
        out_shape=jax.ShapeDtypeStruct((R, C), F32),
        compiler_params=_cparams(("parallel",)),
    )(*([a] if first is None else [a, first]))


def _pack(arrays, dtype, row_align):
    flat = jnp.concatenate([a.astype(dtype).reshape(-1) for a in arrays])
    per = PACK_W * row_align
    n = -(-flat.shape[0] // per) * per
    return jnp.pad(flat, (0, n - flat.shape[0])).reshape(-1, PACK_W)


def _unpack(buf, shapes):
    flat = buf.reshape(-1)
    out, at = [], 0
    for s in shapes:
        n = 1
        for d in s:
            n *= d
        out.append(flat[at:at + n].reshape(s))
        at += n
    return out


def _to_col_major(t, rows):
    L, C = t.shape
    return t.reshape(rows, GRID_W, C).transpose(1, 0, 2).reshape(L, C)


def _to_row_major(t, rows):
    L, C = t.shape
    return t.reshape(GRID_W, rows, C).transpose(1, 0, 2).reshape(L, C)


def _heads_to_groups(v, hpg):
    lead = v.shape[:-1]
    t = jnp.moveaxis(v.reshape(lead + (2, SSD_GROUPS, hpg)), -3, -2).reshape(lead + (SSD_GROUPS, 2 * hpg))
    t = jnp.pad(t, [(0, 0)] * (len(lead) + 1) + [(0, LANES - 2 * hpg)])
    return t.reshape(lead + (SSD_GROUPS * LANES,))


def _groups_to_heads(t, hpg):
    lead = t.shape[:-1]
    t = t.reshape(lead + (SSD_GROUPS, LANES))[..., :2 * hpg].reshape(lead + (SSD_GROUPS, 2, hpg))
    return jnp.moveaxis(t, -2, -3).reshape(lead + (2 * SSD_GROUPS * hpg,))


def _r1(v):
    return v.reshape(1, -1)


class _Layout:
    def __init__(self, D, W, hpg):
        assert W == D and D % (SSD_GROUPS * LANES) == 0
        self.D, self.W, self.hpg = D, W, hpg
        self.GN = SSD_GROUPS * SSD_STATE
        self.xbc0 = 5 * D
        self.dt0 = 5 * D + W + 2 * self.GN
        self.width = self.dt0 + SSD_GROUPS * LANES
        self.xbc_blk = self.xbc0 // CONV_CB
        self.dt_blk = self.dt0 // (SSD_GROUPS * LANES)
        assert self.dt0 % (SSD_GROUPS * LANES) == 0


def _seq_forward(hb, w_all, sp, lay, tag, grid_rows, init, carry=None):
    D, W = lay.D, lay.W
    gw = SSD_GROUPS * LANES
    pg = matmul(hb, w_all, "nn", BF16, f"proj_{tag}", carry=carry)
    pg, carried = pg if carry else (pg, [])
    xbc = conv_fwd(pg, lay.xbc_blk, sp["ssd_conv_w"], sp["ssd_conv_b"], True, f"ssd_conv_{tag}", out_dtype=BF16)
    dtg = rowwise(f_dt, [Cols(pg, gw, lay.dt_blk)], [sp["dt_bias_g"]], [(gw, F32)], f"ssd_dt_{tag}")[0]
    yf, ent_f, fin_f = ssd_fwd(xbc, dtg, sp["a_g"], init["ssd_f"], False, W, f"ssd_scan_f_{tag}")
    yb, ent_b, fin_b = ssd_fwd(xbc, dtg, sp["a_g"], init["ssd_b"], True, W, f"ssd_scan_b_{tag}")
    xr = pg[:, D:2 * D]
    if grid_rows:
        xr = _to_col_major(xr, grid_rows)
    xc = conv_fwd(xr, 0, sp["lru_conv_w"], sp["lru_conv_b"], False, f"lru_conv_{tag}")
    a_f, u_f, a_b, u_b = rowwise(f_gates, [xc], sp["gate_pars"], [(LRU_HEAD, F32)] * 4, f"lru_gates_{tag}",
                                 tb=256, sub=256, ncol=D // LRU_HEAD)
    h_f, hs_f, fl_f = lru_scan(a_f, u_f, init["lru_f"], False, f"lru_scan_f_{tag}")
    _, hs_b, fl_b, r = lru_scan(a_b, u_b, init["lru_b"], True, f"lru_scan_b_{tag}", add=h_f)
    saved = dict(pg=pg, xbc=xbc, dtg=dtg, yf=yf, yb=yb, ent_f=ent_f, ent_b=ent_b, xr=xr, xc=xc,
                 a_f=a_f, a_b=a_b, hs_f=hs_f, hs_b=hs_b, r=r)
    return saved, dict(ssd_f=fin_f, ssd_b=fin_b, lru_f=fl_f, lru_b=fl_b), carried


def _seq_backward(sv, sp, lay, tag, grid_rows, dy, dxs_extra, d_r, dfin, carries=(None, None)):
    D, W, GN = lay.D, lay.W, lay.GN
    gw = SSD_GROUPS * LANES
    pg = sv["pg"]
    res_f = ssd_bwd(sv["xbc"], sv["dtg"], sp["a_g"], sv["ent_f"], dy, dfin["ssd_f"], False, W, f"ssd_scan_f_bwd_{tag}", carry=carries[0])
    res_b = ssd_bwd(sv["xbc"], sv["dtg"], sp["a_g"], sv["ent_b"], dy, dfin["ssd_b"], True, W, f"ssd_scan_b_bwd_{tag}", carry=carries[1])
    (dxf, ddt_f, db_f, dc_f, da_f, dh0_f), carried_f = res_f if carries[0] else (res_f, [])
    (dxb, ddt_b, db_b, dc_b, da_b, dh0_b), carried_b = res_b if carries[1] else (res_b, [])
    cw, cb = sp["ssd_conv_w"], sp["ssd_conv_b"]
    d_xs, dw1, db1 = conv_bwd(pg, lay.xbc_blk, cw[:, :W], cb[:, :W], [dxf, dxb] + dxs_extra, True, f"ssd_conv_x_bwd_{tag}")
    d_b, dw2, db2 = conv_bwd(pg, lay.xbc_blk + W // CONV_CB, cw[:, W:W + GN], cb[:, W:W + GN], [db_f, db_b], True, f"ssd_conv_b_bwd_{tag}")
    d_c, dw3, db3 = conv_bwd(pg, lay.xbc_blk + (W + GN) // CONV_CB, cw[:, W + GN:], cb[:, W + GN:], [dc_f, dc_b], True, f"ssd_conv_c_bwd_{tag}")
    d_dtraw, d_dtbias = rowwise_vjp(f_dt, [Cols(pg, gw, lay.dt_blk)], [sp["dt_bias_g"]], [[ddt_f, ddt_b]], [BF16], f"ssd_dt_bwd_{tag}")
    du_b, dab, dl0_b = lru_scan(sv["a_b"], d_r, dfin["lru_b"], False, f"lru_scan_b_bwd_{tag}", adjoint_of=sv["hs_b"])
    du_f, daf, dl0_f = lru_scan(sv["a_f"], d_r, dfin["lru_f"], True, f"lru_scan_f_bwd_{tag}", adjoint_of=sv["hs_f"])
    res = rowwise_vjp(f_gates, [sv["xc"]], sp["gate_pars"], [daf, du_f, dab, du_b], [F32], f"lru_gates_bwd_{tag}",
                      tb=256, sub=256, ncol=D // LRU_HEAD)
    d_xc, gate_grads = res[0], res[1:]
    d_xr, g_lcw, g_lcb = conv_bwd(sv["xr"], 0, sp["lru_conv_w"], sp["lru_conv_b"], [d_xc], False, f"lru_conv_bwd_{tag}")
    if grid_rows:
        d_xr = _to_row_major(d_xr, grid_rows)
    grads = dict(ssd_conv_w=jnp.concatenate([dw1, dw2, dw3], axis=1), ssd_conv_b=jnp.concatenate([db1, db2, db3], axis=1),
                 dt_bias_g=d_dtbias, a_g=da_f + da_b, lru_conv_w=g_lcw, lru_conv_b=g_lcb, gate_pars=list(gate_grads))
    pieces = dict(xr=d_xr, xs=d_xs, b=d_b, c=d_c, dt=d_dtraw)
    return pieces, grads, dict(ssd_f=dh0_f, ssd_b=dh0_b, lru_f=dl0_f, lru_b=dl0_b), (carried_f, carried_b)


def _proj_cotangent(lay, px, pc, Lc, dz, dyr, dgs, dgr):
    zero = jnp.zeros((Lc, lay.D), BF16)
    cols = [(dz, zero), (px["xr"], pc["xr"]), (dyr, zero), (dgs, zero), (dgr, zero),
            (px["xs"], pc["xs"]), (px["b"], pc["b"]), (px["c"], pc["c"]), (px["dt"], pc["dt"])]
    return jnp.concatenate([jnp.concatenate([a.astype(BF16), b.astype(BF16)], axis=0) for a, b in cols], axis=1)


EARLY = ["w_in", "w_gate"]
LATE = ["w_out_ssd", "w_out_lru", "w_o", "ffn_w13", "ffn_w2"]


def _local_step(x, ctx, target, mod_x, mod_c, wb, late, ws):
    L, D = x.shape
    Lc = ctx.shape[0]
    nh = ws["ssd_d"].shape[0]
    hpg = nh // SSD_GROUPS
    W = nh * HEAD_DIM
    lay = _Layout(D, W, hpg)
    GN = lay.GN
    grid_rows = L // GRID_W
    nlh = D // LRU_HEAD

    w_in = wb["w_in"]
    o_dt, o_xr = 2 * W + 2 * GN, 2 * W + 2 * GN + 2 * nh
    w_all = jnp.concatenate([w_in[:, :W], w_in[:, o_xr:o_xr + D], w_in[:, o_xr + D:], wb["w_gate"], w_in[:, W:o_dt],
                             _heads_to_groups(w_in[:, o_dt:o_xr], hpg)], axis=1)
    a_neg = -jnp.exp(ws["ssd_a_log"])
    a_g = jnp.pad(_heads_to_groups(a_neg.reshape(-1), hpg).reshape(SSD_GROUPS, 1, LANES), [(0, 0), (0, SUBLANES - 1), (0, 0)])
    gate_pars = []
    for d in range(2):
        gate_pars += [ws["lru_w_a"][d], ws["lru_b_a"][d].reshape(nlh, 1, LRU_HEAD), ws["lru_w_x"][d],
                      ws["lru_b_x"][d].reshape(nlh, 1, LRU_HEAD), ws["lru_lambda"][d].reshape(nlh, 1, LRU_HEAD)]
    sp = dict(ssd_conv_w=ws["ssd_conv_w"], ssd_conv_b=_r1(ws["ssd_conv_b"]), dt_bias_g=_r1(_heads_to_groups(ws["ssd_dt_bias"].reshape(-1), hpg)),
              a_g=a_g, lru_conv_w=ws["lru_conv_w"], lru_conv_b=_r1(ws["lru_conv_b"]), gate_pars=gate_pars)
    dexp = _r1(jnp.repeat(ws["ssd_d"], HEAD_DIM))
    norm_mix, norm_ffn, ssd_norm, final_norm = _r1(ws["norm_mix"]), _r1(ws["norm_ffn"]), _r1(ws["ssd_norm"]), _r1(ws["final_norm"])
    bg_s, bg_r = _r1(ws["b_gate"][:D]), _r1(ws["b_gate"][D:])
    sh_m, sc_m, g_m, sh_f, sc_f, g_f = [mod_x[:, k * D:(k + 1) * D] for k in range(6)]
    csh_m, csc_m = mod_c[:, :D], mod_c[:, D:2 * D]

    zst = dict(ssd_f=jnp.zeros((SSD_GROUPS, hpg * HEAD_DIM, SSD_STATE), F32), ssd_b=jnp.zeros((SSD_GROUPS, hpg * HEAD_DIM, SSD_STATE), F32),
               lru_f=jnp.zeros((SUBLANES, D), F32), lru_b=jnp.zeros((SUBLANES, D), F32))
    hcb = rowwise(f_mod_in, [ctx], [norm_mix, csh_m, csc_m], [(D, BF16)], "norm_mix_ctx")[0]
    hb = rowwise(f_mod_in, [x], [norm_mix, sh_m, sc_m], [(D, BF16)], "norm_mix_x")[0]
    svc, fin_c, _ = _seq_forward(hcb, w_all, sp, lay, "ctx", None, zst)
    svx, _, stacks = _seq_forward(hb, w_all, sp, lay, "x", grid_rows, fin_c, carry=gather_exchange([late[n] for n in LATE]))
    wb = dict(wb)
    for n, st in zip(LATE, stacks):
        wb[n] = _whole_from_shards(fill_own_slot(st, late[n]), n in BIG_BY_COLUMNS)
    pg = svx["pg"]
    r_rm = _to_row_major(svx["r"], grid_rows)
    gn_rows = [svx["yf"], svx["yb"], Cols(svx["xbc"], W, 0), Cols(pg, D, 0)]
    yn = rowwise(f_gnorm, gn_rows, [dexp, ssd_norm], [(W, BF16)], "ssd_gnorm")[0]
    o_s = matmul(yn, wb["w_out_ssd"], "nn", BF16, "out_ssd")
    o_in = rowwise(f_gelu_gate, [r_rm, Cols(pg, D, 2)], [], [(D, BF16)], "lru_gelu")[0]
    o_r = matmul(o_in, wb["w_out_lru"], "nn", BF16, "out_lru")
    mg_rows = [Cols(pg, D, 3), Cols(pg, D, 4), o_s, o_r]
    mixed = rowwise(f_merge, mg_rows, [bg_s, bg_r], [(D, BF16)], "merge")[0]
    out = matmul(mixed, wb["w_o"], "nn", BF16, "out_proj")
    x1, h2 = rowwise(f_res_mod, [x, out], [g_m, norm_ffn, sh_f, sc_f], [(D, F32), (D, BF16)], "res_norm_ffn")
    gu = matmul(h2, wb["ffn_w13"], "nn", BF16, "ffn_in")
    act = swiglu(gu, None, "ffn_act")
    f = matmul(act, wb["ffn_w2"], "nn", BF16, "ffn_out")

    ones = jnp.full((L, LANES), 1.0 / LANES, F32)
    dx1a, df, d_gf, d_fnorm, lsum = rowwise_vjp(f_final, [x1, f, target], [g_f, final_norm], [ones], [F32, BF16, None],
                                                "final_loss", out_sums=[(0, LANES)])
    loss = lsum[0, 0]
    d_act = matmul(df, wb["ffn_w2"], "nt", BF16, "ffn_out_dx")
    g_w2 = matmul(act, df, "tn", BF16, "ffn_out_dw")
    d_gu = swiglu(gu, d_act, "ffn_act_bwd")
    dh2 = matmul(d_gu, wb["ffn_w13"], "nt", BF16, "ffn_in_dx")
    g_w13 = matmul(h2, d_gu, "tn", BF16, "ffn_in_dw")
    dxa, d_out, d_gm, d_nffn, d_shf, d_scf = rowwise_vjp(f_res_mod, [x, out], [g_m, norm_ffn, sh_f, sc_f], [dx1a, dh2],
                                                         [F32, BF16], "res_norm_ffn_bwd")
    d_mixed = matmul(d_out, wb["w_o"], "nt", BF16, "out_proj_dx")
    g_wo = matmul(mixed, d_out, "tn", BF16, "out_proj_dw")
    dgs, dgr, do_s, do_r, d_bgs, d_bgr = rowwise_vjp(f_merge, mg_rows, [bg_s, bg_r], [d_mixed], [BF16] * 4, "merge_bwd")
    d_yn = matmul(do_s, wb["w_out_ssd"], "nt", BF16, "out_ssd_dx")
    g_wos = matmul(yn, do_s, "tn", BF16, "out_ssd_dw")
    d_oin = matmul(do_r, wb["w_out_lru"], "nt", BF16, "out_lru_dx")
    g_wol = matmul(o_in, do_r, "tn", BF16, "out_lru_dw")
    d_r_rm, d_yr = rowwise_vjp(f_gelu_gate, [r_rm, Cols(pg, D, 2)], [], [d_oin], [F32, BF16], "lru_gelu_bwd")
    dy, dxs_skip, dz, d_dexp, d_ssdn = rowwise_vjp(f_gnorm, gn_rows, [dexp, ssd_norm], [d_yn], [BF16, None, BF16, BF16], "ssd_gnorm_bwd")
    zfin = dict(zst)
    parts = {n: _shards_from_whole(g, n in BIG_BY_COLUMNS).astype(BF16)
             for n, g in (("ffn_w2", g_w2), ("ffn_w13", g_w13), ("w_o", g_wo), ("w_out_ssd", g_wos), ("w_out_lru", g_wol))}
    ride_f, ride_b = ["ffn_w2", "ffn_w13"], ["w_o", "w_out_ssd", "w_out_lru"]
    px, gx, dst, (land_f, land_b) = _seq_backward(
        svx, sp, lay, "x", grid_rows, dy, [dxs_skip], _to_col_major(d_r_rm, grid_rows), zfin,
        carries=(scatter_exchange([parts[n] for n in ride_f]), scatter_exchange([parts[n] for n in ride_b])))
    pc, gc, _, _ = _seq_backward(svc, sp, lay, "ctx", None, jnp.zeros((Lc, W), F32), [], jnp.zeros((Lc, D), F32), dst)
    dpg = _proj_cotangent(lay, px, pc, Lc, dz, d_yr, dgs, dgr)
    gg = [a + b for a, b in zip(gx["gate_pars"], gc["gate_pars"])]
    lru_w = jnp.concatenate([gg[0], gg[5], gg[2], gg[7]], axis=0).reshape(-1, PACK_W)
    g_wall, (lru_w_all,) = matmul(jnp.concatenate([hb, hcb], axis=0), dpg, "tn", BF16, "proj_dw", carry=allgather_exchange(lru_w))
    lru_w_all = fill_own_device_slot(lru_w_all, lru_w)
    xb0 = lay.xbc0
    g_w_in = jnp.concatenate([g_wall[:, :W], g_wall[:, xb0:xb0 + W + 2 * GN], _groups_to_heads(g_wall[:, lay.dt0:], hpg),
                              g_wall[:, D:2 * D], g_wall[:, 2 * D:3 * D]], axis=1)
    ride_p = ["w_in", "w_gate"]
    parts.update({n: _shards_from_whole(g, True).astype(BF16) for n, g in (("w_in", g_w_in), ("w_gate", g_wall[:, 3 * D:5 * D]))})
    dh, land_p = matmul(dpg, w_all, "nt", BF16, "proj_dx_x", a_rows=(0, L), carry=scatter_exchange([parts[n] for n in ride_p]))
    scattered = {n: (ld, own_piece(parts[n])) for n, ld in zip(ride_f + ride_b + ride_p, land_f + land_b + land_p)}
    dhc = matmul(dpg, w_all, "nt", BF16, "proj_dx_ctx", a_rows=(L, Lc))
    grad_x, d_nmix_x, d_shm, d_scm = rowwise_vjp(f_mod_in, [x], [norm_mix, sh_m, sc_m], [dh, dxa], [F32], "norm_mix_x_bwd")
    d_nmix_c, d_cshm, d_cscm = rowwise_vjp(f_mod_in, [ctx], [norm_mix, csh_m, csc_m], [dhc, jnp.zeros((Lc, D), F32)], [None], "norm_mix_ctx_bwd")

    dmod_x = jnp.concatenate([d_shm, d_scm, d_gm, d_shf, d_scf, d_gf], axis=1)
    dmod_c = jnp.concatenate([d_cshm, d_cscm, jnp.zeros((1, 4 * D), F32)], axis=1)

    d_a = _groups_to_heads((gx["a_g"] + gc["a_g"])[:, 0, :].reshape(-1), hpg).reshape(2, nh)

    def gate(k, shape):
        return jnp.stack([gg[k].reshape(shape), gg[5 + k].reshape(shape)], axis=0)

    grads = dict(
        norm_mix=(d_nmix_x + d_nmix_c).reshape(-1), norm_ffn=d_nffn.reshape(-1),
        ssd_conv_w=gx["ssd_conv_w"] + gc["ssd_conv_w"], ssd_conv_b=(gx["ssd_conv_b"] + gc["ssd_conv_b"]).reshape(-1),
        ssd_dt_bias=_groups_to_heads((gx["dt_bias_g"] + gc["dt_bias_g"]).reshape(-1), hpg).reshape(2, nh),
        ssd_a_log=d_a * a_neg, ssd_d=d_dexp.reshape(nh, HEAD_DIM).sum(axis=1), ssd_norm=d_ssdn.reshape(-1),
        lru_conv_w=gx["lru_conv_w"] + gc["lru_conv_w"], lru_conv_b=(gx["lru_conv_b"] + gc["lru_conv_b"]).reshape(-1),
        lru_b_a=gate(1, (D,)), lru_b_x=gate(3, (D,)), lru_lambda=gate(4, (D,)),
        b_gate=jnp.concatenate([d_bgs, d_bgr], axis=1).reshape(-1), final_norm=d_fnorm.reshape(-1))
    return loss, grad_x, grads, scattered, lru_w_all, dmod_x, dmod_c


WEIGHTS = ["c_ctx", "w_ada", "b_ada", "norm_mix", "norm_ffn", "w_in", "ssd_conv_w", "ssd_conv_b", "ssd_dt_bias", "ssd_a_log",
           "ssd_d", "ssd_norm", "w_out_ssd", "lru_conv_w", "lru_conv_b", "lru_w_a", "lru_b_a", "lru_w_x", "lru_b_x", "lru_lambda",
           "w_out_lru", "w_gate", "b_gate", "w_o", "ffn_w13", "ffn_w2", "final_norm"]
BIG = ["w_in", "w_out_ssd", "w_out_lru", "w_gate", "w_o", "ffn_w13", "ffn_w2"]
BIG_BY_COLUMNS = {"w_in", "w_gate", "ffn_w13"}
MOD_ROWS = 16
MOD_CTX_ROW = N_DEV
SMALL_SHARDED = ["ssd_conv_w", "lru_conv_w", "lru_b_a", "lru_b_x", "lru_lambda"]
ADAM_ROWS = 64


def _whole_from_shards(stack, by_columns):
    if by_columns:
        return stack.transpose(1, 0, 2).reshape(stack.shape[1], -1)
    return stack.reshape(-1, stack.shape[2])


def _shards_from_whole(g, by_columns):
    if by_columns:
        return g.reshape(g.shape[0], N_CHIP, -1).transpose(1, 0, 2)
    return g.reshape(N_CHIP, -1, g.shape[1])


def _adam(w, g, m, v, name):
    shape = w.shape
    cols = shape[-1]
    w2, g2, m2, v2 = [t.reshape(-1, cols) for t in (w, g, m, v)]
    d, nm, nv = rowwise(f_adam, [w2, g2, m2, v2], [], [(cols, F32)] * 3, name, tb=ADAM_ROWS, sub=8)
    return d.reshape(shape), nm.reshape(shape), nv.reshape(shape)


def kernel(x, c, ctx, c_ctx, w_ada, b_ada, norm_mix, norm_ffn, w_in, ssd_conv_w, ssd_conv_b, ssd_dt_bias, ssd_a_log, ssd_d, ssd_norm, w_out_ssd, lru_conv_w, lru_conv_b, lru_w_a, lru_b_a, lru_w_x, lru_b_x, lru_lambda, w_out_lru, w_gate, b_gate, w_o, ffn_w13, ffn_w2, final_norm, loss_target, m_c_ctx, m_w_ada, m_b_ada, m_norm_mix, m_norm_ffn, m_w_in, m_ssd_conv_w, m_ssd_conv_b, m_ssd_dt_bias, m_ssd_a_log, m_ssd_d, m_ssd_norm, m_w_out_ssd, m_lru_conv_w, m_lru_conv_b, m_lru_w_a, m_lru_b_a, m_lru_w_x, m_lru_b_x, m_lru_lambda, m_w_out_lru, m_w_gate, m_b_gate, m_w_o, m_ffn_w13, m_ffn_w2, m_final_norm, v_c_ctx, v_w_ada, v_b_ada, v_norm_mix, v_norm_ffn, v_w_in, v_ssd_conv_w, v_ssd_conv_b, v_ssd_dt_bias, v_ssd_a_log, v_ssd_d, v_ssd_norm, v_w_out_ssd, v_lru_conv_w, v_lru_conv_b, v_lru_w_a, v_lru_b_a, v_lru_w_x, v_lru_b_x, v_lru_lambda, v_w_out_lru, v_w_gate, v_b_gate, v_w_o, v_ffn_w13, v_ffn_w2, v_final_norm):
    given = dict(locals())
    layered = {n for n in WEIGHTS if n not in ("c_ctx", "final_norm")}
    w_blk = {n: (given[n][0] if n in layered else given[n]) for n in WEIGHTS}
    me = _me()
    chip, dev = _chip_index(me), _dev_index(me)
    D = c.shape[1]

    ada_w = w_blk["w_ada"].astype(BF16)
    ada_n = ada_w.shape[1]
    c_all = allgather8(jnp.concatenate([c, jnp.zeros((SUBLANES - 1, D), F32)], axis=0), "gather_c")[:, 0, :]
    cc = jnp.concatenate([c_all, c_ctx[None], jnp.zeros((MOD_ROWS - N_DEV - 1, D), F32)], axis=0)
    s_cc = rowwise(f_silu, [cc], [], [(D, BF16)], "mod_silu", tb=MOD_ROWS)[0]
    ada_b = lax.dynamic_slice_in_dim(w_blk["b_ada"], chip * ada_n, ada_n)[None]
    mod_blk = rowwise(f_add_bias, [matmul(s_cc, ada_w, "nn", F32, "mod_proj")], [ada_b], [(ada_n, F32)], "mod_bias", tb=MOD_ROWS)[0]
    mod = allgather8(mod_blk, "gather_mod")[::2].transpose(1, 0, 2).reshape(MOD_ROWS, N_CHIP * ada_n)
    mod_x = lax.dynamic_slice_in_dim(mod, dev, 1, axis=0)
    mod_c = mod[MOD_CTX_ROW:MOD_CTX_ROW + 1]

    shard16 = {n: w_blk[n].astype(BF16) for n in BIG}
    stacks = run_exchange(gather_exchange([shard16[n] for n in EARLY]), "gather_early")
    wb = {n: _whole_from_shards(fill_own_slot(st, shard16[n]), n in BIG_BY_COLUMNS) for n, st in zip(EARLY, stacks)}

    sm_shapes = [w_blk[n].shape for n in SMALL_SHARDED]
    sm_all = allgather8(_pack([w_blk[n] for n in SMALL_SHARDED], F32, ROW_TILE_16BIT), "gather_small")[::2]
    ws = {n: w_blk[n] for n in WEIGHTS if n not in BIG and n != "w_ada"}
    for n, s in zip(SMALL_SHARDED, _unpack_stacked(sm_all, sm_shapes)):
        ws[n] = jnp.concatenate([s[k] for k in range(N_CHIP)], axis=-1)

    loss, grad_x, grads, scattered, lru_w_all, dmod_x, dmod_c = _local_step(x[0], ctx[0], loss_target[0], mod_x, mod_c, wb,
                                                                            {n: shard16[n] for n in LATE}, ws)
    loss = lax.psum(loss, ("x", "y", "c"))

    dm = allgather8(jnp.concatenate([dmod_x, dmod_c, jnp.zeros((SUBLANES - 2, 6 * D), F32)], axis=0), "gather_dmod")
    dm_ctx = sum_slots(dm, "sum_dmod")[1:2]
    dmod = jnp.concatenate([dm[:, 0, :], dm_ctx, jnp.zeros((MOD_ROWS - N_DEV - 1, 6 * D), F32)], axis=0)
    g_shard = {"b_ada": rowwise_vjp(f_add_bias, [dmod], [w_blk["b_ada"][None]], [dmod], [None], "mod_bias_bwd", tb=MOD_ROWS)[0].reshape(-1)}
    d_blk = lax.dynamic_slice_in_dim(dmod, chip * ada_n, ada_n, axis=1).astype(BF16)
    g_shard["w_ada"] = matmul(s_cc, d_blk, "tn", F32, "mod_proj_dw")
    d_silu_ctx = matmul(d_blk, ada_w, "nt", F32, "mod_proj_dx")[MOD_CTX_ROW]
    grads["c_ctx_silu"] = 0.5 * d_silu_ctx

    for n in BIG:
        landed, own = scattered[n]
        g_shard[n] = join_halves(sum_slots(landed, "sum_" + n, first=own), "join_" + n)

    small = [n for n in WEIGHTS if n not in BIG and n != "w_ada"]
    lru_w = sum_slots(lru_w_all, "sum_lru_w").reshape((4,) + w_blk["lru_w_a"].shape[1:])
    g_shard["lru_w_a"], g_shard["lru_w_x"] = lru_w[:2], lru_w[2:]
    reduced = [n for n in small if n not in ("b_ada", "c_ctx", "lru_w_a", "lru_w_x")] + ["c_ctx_silu"]
    sm_g = sum_slots(allgather8(_pack([grads[n] for n in reduced], F32, ROW_TILE_16BIT), "gather_small_grads"), "sum_small_grads")
    for n, g in zip(reduced, _unpack(sm_g, [grads[n].shape for n in reduced])):
        if n in SMALL_SHARDED:
            per = g.shape[-1] // N_CHIP
            g = lax.dynamic_slice_in_dim(g, chip * per, per, axis=g.ndim - 1)
        g_shard[n] = g
    ctx_rows = jnp.concatenate([c_ctx[None], jnp.zeros((SUBLANES - 1, D), F32)], axis=0)
    ctx_cot = jnp.concatenate([g_shard.pop("c_ctx_silu")[None], jnp.zeros((SUBLANES - 1, D), F32)], axis=0)
    g_shard["c_ctx"] = rowwise_vjp(f_silu, [ctx_rows], [], [ctx_cot], [F32], "mod_silu_bwd", tb=SUBLANES, sub=SUBLANES)[0][0]

    out_g, out_d, out_m, out_v = {}, {}, {}, {}
    for n in BIG + ["w_ada"]:
        out_g[n] = g_shard[n]
        out_d[n], out_m[n], out_v[n] = _adam(w_blk[n], g_shard[n], given["m_" + n][0], given["v_" + n][0], "adamw_" + n)
    sm_blk_shapes = [w_blk[n].shape for n in small]
    packed = [_pack([t[n].reshape(w_blk[n].shape) for n in small], F32, ADAM_ROWS)
              for t in (w_blk, g_shard, {n: given["m_" + n] for n in small}, {n: given["v_" + n] for n in small})]
    res = _adam(*packed, "adamw_small")
    for tgt, buf in zip((out_d, out_m, out_v), res):
        tgt.update(zip(small, _unpack(buf, sm_blk_shapes)))
    for n in small:
        out_g[n] = g_shard[n].reshape(w_blk[n].shape)

    def full(n, t):
        return t.reshape(given[n].shape)

    return (loss, grad_x[None], *[full(n, out_g[n]) for n in WEIGHTS], *[full(n, out_d[n]) for n in WEIGHTS],
            *[full(n, out_m[n]) for n in WEIGHTS], *[full(n, out_v[n]) for n in WEIGHTS])


def _unpack_stacked(buf, shapes):
    k = buf.shape[0]
    flat = buf.reshape(k, -1)
    out, at = [], 0
    for s in shapes:
        n = 1
        for d in s:
            n *= d
        out.append(flat[:, at:at + n].reshape((k,) + tuple(s)))
        at += n
    return out
```

```python
import functools

import jax
import jax.numpy as jnp
from jax import lax
from jax.experimental import pallas as pl
from jax.experimental.pallas import tpu as pltpu

F32 = jnp.float32
BF16 = jnp.bfloat16
MESH = pl.DeviceIdType.MESH

V7X_VMEM_LIMIT_BYTES = 56 * 1024 * 1024
LANES = 128
SUBLANES = 8

HEAD_DIM = 64
SSD_STATE = 128
SSD_GROUPS = 4
SSD_CHUNK = 128
GRID_W = 64
LRU_HEAD = 128
LRU_C = 8.0
EPS = 1e-6
NEG = -1e30

ADAM_LR = 0.001
ADAM_B1 = 0.9
ADAM_B2 = 0.999
ADAM_EPS = 1e-08
ADAM_WD = 0.01
ADAM_STEP = 10


def _cparams(sem=None, **kw):
    if sem is not None:
        kw["dimension_semantics"] = sem
    return pltpu.CompilerParams(vmem_limit_bytes=V7X_VMEM_LIMIT_BYTES, **kw)


_DN = {"nn": (((1,), (0,)), ((), ())), "nt": (((1,), (1,)), ((), ())), "tn": (((0,), (0,)), ((), ()))}


def _raw_dot(a, b, form, precision=None):
    return lax.dot_general(a, b, _DN[form], precision=precision, preferred_element_type=F32)


def _dot_bwd_forms(form):
    return {"nn": (("g", "b", "nt"), ("a", "g", "tn")),
            "nt": (("g", "b", "nn"), ("g", "a", "tn")),
            "tn": (("b", "g", "nt"), ("a", "g", "nn"))}[form]


def _make_dot(rounding):
    @functools.partial(jax.custom_vjp, nondiff_argnums=(2,))
    def dot(a, b, form):
        if rounding is None:
            return _raw_dot(a, b, form, precision=lax.Precision.HIGHEST)
        return _raw_dot(a.astype(rounding), b.astype(rounding), form)

    def fwd(a, b, form):
        return dot(a, b, form), (a, b)

    def bwd(form, res, g):
        a, b = res
        ops = {"a": a, "b": b, "g": g}
        (l1, r1, f1), (l2, r2, f2) = _dot_bwd_forms(form)
        return dot(ops[l1], ops[r1], f1).astype(a.dtype), dot(ops[l2], ops[r2], f2).astype(b.dtype)

    dot.defvjp(fwd, bwd)
    return dot


bdot = _make_dot(BF16)
hdot = _make_dot(None)


def _sigmoid(v):
    return 0.5 * jnp.tanh(0.5 * v) + 0.5


def _silu(v):
    return v * _sigmoid(v)


def _softplus(v):
    return jnp.maximum(v, 0.0) + jnp.log1p(jnp.exp(-jnp.abs(v)))


def _gelu_tanh(v):
    return 0.5 * v * (1.0 + jnp.tanh(0.7978845608028654 * (v + 0.044715 * v * v * v)))


def _rms(t, gain):
    return t * lax.rsqrt(jnp.mean(t * t, axis=-1, keepdims=True) + EPS) * gain


def _pick(dim, want, mult=1):
    if dim <= want:
        return dim
    for t in range(want - want % mult, 0, -mult):
        if dim % t == 0:
            return t
    t = min(dim, want)
    while dim % t:
        t //= 2
    return t


MM_TILE_M = 1024
MM_TILE_N = 1536
MM_VMEM_BYTES = 46 * 1024 * 1024


def matmul(a, b, form, out_dtype, name, tm=MM_TILE_M, tn=MM_TILE_N, tk=None, a_rows=None, carry=None):
    if form == "nn":
        (m, k), (k2, n) = a.shape, b.shape
    elif form == "nt":
        (m, k), (n, k2) = a.shape, b.shape
    else:
        (k, m), (k2, n) = a.shape, b.shape
    assert k == k2, (a.shape, b.shape, form)
    row0 = 0
    if a_rows is not None:
        assert form != "tn"
        row0, m = a_rows
    tm, tn = _pick(m, tm, LANES), _pick(n, tn, LANES)
    assert row0 % tm == 0
    blk0 = row0 // tm
    if tk is None:
        fixed = tm * tn * (2 * jnp.dtype(out_dtype).itemsize + 4)
        tk = (MM_VMEM_BYTES - fixed) // (2 * a.dtype.itemsize * (tm + tn))
    tk = _pick(k, tk, LANES)
    nk = k // tk

    grid = (m // tm, n // tn, nk)
    n_ci, n_co = (len(carry.ins), len(carry.outs)) if carry else (0, 0)

    def body(a_ref, b_ref, *rest):
        c_ins, o_ref, c_outs = rest[:n_ci], rest[n_ci], rest[n_ci + 1:n_ci + 1 + n_co]
        scratch = rest[n_ci + 1 + n_co:]
        kk = pl.program_id(2)
        if carry:
            at_step = [pl.program_id(d) for d in range(3)]
            sems = tuple(scratch[-2:])
            pl.when((at_step[0] == 0) & (at_step[1] == 0) & (at_step[2] == 0))(lambda: carry.start(c_ins, c_outs, sems))
        part = _raw_dot(a_ref[...], b_ref[...], form)
        if nk == 1:
            o_ref[...] = part.astype(o_ref.dtype)
        else:
            acc = scratch[0]

            @pl.when(kk == 0)
            def _():
                acc[...] = part

            @pl.when(kk > 0)
            def _():
                acc[...] += part

            @pl.when(kk == nk - 1)
            def _():
                o_ref[...] = acc[...].astype(o_ref.dtype)

        if carry:
            last = (at_step[0] == grid[0] - 1) & (at_step[1] == grid[1] - 1) & (at_step[2] == grid[2] - 1)
            pl.when(last)(lambda: carry.finish(c_ins, c_outs, sems))

    a_spec = pl.BlockSpec((tk, tm), lambda i, j, kk: (kk, i)) if form == "tn" else pl.BlockSpec((tm, tk), lambda i, j, kk: (blk0 + i, kk))
    b_spec = pl.BlockSpec((tn, tk), lambda i, j, kk: (j, kk)) if form == "nt" else pl.BlockSpec((tk, tn), lambda i, j, kk: (kk, j))
    res = pl.pallas_call(
        body, name=name, grid=grid, in_specs=[a_spec, b_spec] + [_HBM] * n_ci,
        out_specs=[pl.BlockSpec((tm, tn), lambda i, j, kk: (i, j))] + [_HBM] * n_co,
        out_shape=[jax.ShapeDtypeStruct((m, n), out_dtype)] + (carry.outs if carry else []),
        scratch_shapes=([pltpu.VMEM((tm, tn), F32)] if nk > 1 else []) + (carry.sem_shapes() if carry else []),
        compiler_params=_cparams(("arbitrary",) * 3 if carry else ("parallel", "parallel", "arbitrary")),
    )(a, b, *(carry.ins if carry else []))
    return (res[0], list(res[1:])) if carry else res[0]


class Cols:
    def __init__(self, arr, w, j):
        assert (j + 1) * w <= arr.shape[1]
        self.arr, self.w, self.j = arr, w, j


def _row_width(x, ncol):
    return x.w if isinstance(x, Cols) else x.shape[1] // ncol


def _row_spec(x, tb, ncol):
    if isinstance(x, Cols):
        j0 = x.j
        return x.arr, pl.BlockSpec((tb, x.w), lambda j, i: (i, j0 + j))
    return x, pl.BlockSpec((tb, x.shape[1] // ncol), lambda j, i: (i, j))


def _par_spec(shape, ncol):
    if ncol == 1:
        return pl.BlockSpec(shape, lambda j, i, nd=len(shape): (0,) * nd)
    assert len(shape) == 3 and shape[0] == ncol, shape
    return pl.BlockSpec((1,) + tuple(shape[1:]), lambda j, i: (j, 0, 0))


def _par_value(ref, ncol):
    return ref[...] if ncol == 1 else ref[0]


def rowwise(fn, rows, pars, outs, name, tb=256, sub=16, ncol=1):
    m = (rows[0].arr if isinstance(rows[0], Cols) else rows[0]).shape[0]
    tb = _pick(m, tb)
    sub = min(sub, tb)
    nr, npar = len(rows), len(pars)

    def body(*refs):
        row_refs, par_refs, out_refs = refs[:nr], refs[nr:nr + npar], refs[nr + npar:]
        pv = [_par_value(p, ncol) for p in par_refs]

        def step(i, carry):
            sl = pl.ds(pl.multiple_of(i * sub, sub), sub)
            res = fn(*[r[sl, :].astype(F32) for r in row_refs], *pv)
            for o, v in zip(out_refs, res):
                o[sl, :] = v.astype(o.dtype)
            return carry

        lax.fori_loop(0, tb // sub, step, 0)

    arrs, specs = zip(*[_row_spec(r, tb, ncol) for r in rows])
    return pl.pallas_call(
        body, name=name, grid=(ncol, m // tb),
        in_specs=list(specs) + [_par_spec(p.shape, ncol) for p in pars],
        out_specs=[pl.BlockSpec((tb, w), lambda j, i: (i, j)) for w, _ in outs],
        out_shape=[jax.ShapeDtypeStruct((m, w * ncol), dt) for w, dt in outs],
        compiler_params=_cparams(("parallel", "parallel")),
    )(*arrs, *pars)


def rowwise_vjp(fn, rows, pars, cots, drow_dtypes, name, tb=256, sub=16, out_sums=(), ncol=1):
    m = (rows[0].arr if isinstance(rows[0], Cols) else rows[0]).shape[0]
    tb = _pick(m, tb)
    sub = min(sub, tb)
    cots = [list(c) if isinstance(c, (list, tuple)) else [c] for c in cots]
    flat = [c for group in cots for c in group]
    nr, npar, nc = len(rows), len(pars), len(flat)
    want = [i for i, d in enumerate(drow_dtypes) if d is not None]
    assert ncol == 1 or not out_sums

    def body(*refs):
        row_refs, par_refs = refs[:nr], refs[nr:nr + npar]
        cot_refs = list(refs[nr + npar:nr + npar + nc])
        drow_refs = refs[nr + npar + nc:nr + npar + nc + len(want)]
        acc_refs = refs[nr + npar + nc + len(want):]
        pv = [_par_value(p, ncol) for p in par_refs]

        def step(i, acc):
            sl = pl.ds(pl.multiple_of(i * sub, sub), sub)
            rv = [r[sl, :].astype(F32) for r in row_refs]
            res, vjp = jax.vjp(lambda rr, pp: tuple(fn(*rr, *pp)), rv, pv)
            ct, at = [], 0
            for group in cots:
                ct.append(sum(c[sl, :].astype(F32) for c in cot_refs[at:at + len(group)]))
                at += len(group)
            d_rows, d_pars = vjp(tuple(ct))
            for o, idx in zip(drow_refs, want):
                o[sl, :] = d_rows[idx].astype(o.dtype)
            sums = [jnp.sum(res[k], axis=0, keepdims=True) for k, _ in out_sums]
            return tuple(a + d for a, d in zip(acc, list(d_pars) + sums))

        init = tuple(jnp.zeros(p.shape, F32) for p in pv) + tuple(jnp.zeros((1, w), F32) for _, w in out_sums)
        acc = lax.fori_loop(0, tb // sub, step, init)

        @pl.when(pl.program_id(1) == 0)
        def _():
            for o in acc_refs:
                o[...] = jnp.zeros_like(o)

        for o, a in zip(acc_refs, acc):
            if ncol == 1:
                o[...] += a
            else:
                o[0] += a

    arrs, specs = zip(*[_row_spec(r, tb, ncol) for r in rows])
    carrs, cspecs = zip(*[_row_spec(c, tb, ncol) for c in flat])
    widths = [_row_width(r, ncol) for r in rows]
    acc_shapes = [tuple(p.shape) for p in pars] + [(1, w) for _, w in out_sums]
    return pl.pallas_call(
        body, name=name, grid=(ncol, m // tb),
        in_specs=list(specs) + [_par_spec(p.shape, ncol) for p in pars] + list(cspecs),
        out_specs=[pl.BlockSpec((tb, widths[i]), lambda j, i_: (i_, j)) for i in want]
        + [_par_spec(s, ncol) for s in acc_shapes],
        out_shape=[jax.ShapeDtypeStruct((m, widths[i] * ncol), drow_dtypes[i]) for i in want]
        + [jax.ShapeDtypeStruct(s, F32) for s in acc_shapes],
        compiler_params=_cparams(("parallel", "arbitrary")),
    )(*arrs, *pars, *carrs)


def _col(v, c):
    lane = lax.broadcasted_iota(jnp.int32, v.shape, 1)
    return jnp.sum(jnp.where(lane == c, v, 0.0), axis=1, keepdims=True)


def _row(v, r):
    sub = lax.broadcasted_iota(jnp.int32, v.shape, 0)
    return jnp.sum(jnp.where(sub == r, v, 0.0), axis=0, keepdims=True)


def _ssd_chunk(xs, dt, bm, cm, hs, a_row, rev, col0):
    q = dt.shape[0]
    ii = lax.broadcasted_iota(jnp.int32, (q, q), 0)
    jj = lax.broadcasted_iota(jnp.int32, (q, q), 1)
    keep = (jj >= ii) if rev else (jj <= ii)
    tri = keep.astype(F32)
    dta = dt * a_row
    a_cum = hdot(tri, dta, "nn")
    tri_t = ((jj <= ii) if rev else (jj >= ii)).astype(F32)
    a_cum_t = hdot(dta, tri_t, "tn")
    cb = bdot(cm, bm, "nt")
    last = 0 if rev else q - 1
    ys, hn = [], []
    for e, (x, h) in enumerate(zip(xs, hs)):
        c = col0 + e
        ac = _col(a_cum, c)
        seg = ac - _row(a_cum_t, c)
        lm = jnp.exp(jnp.where(keep, seg, NEG))
        xdt = x * _col(dt, c)
        y_diag = bdot(cb * lm, xdt, "nn")
        y_off = bdot(cm * jnp.exp(ac), h, "nt")
        ys.append(y_diag + y_off)
        tot = _row(ac, last)
        states = bdot(xdt, bm * jnp.exp(tot - ac), "tn")
        hn.append(jnp.exp(tot) * h + states)
    return ys, hn


def _ssd_specs(nc, hpg, w_ssd, rev_order):
    q = SSD_CHUNK
    wx = hpg * HEAD_DIM
    boff = w_ssd // LANES

    def cidx(c):
        return nc - 1 - c if rev_order else c

    x_spec = pl.BlockSpec((q, wx), lambda g, c: (cidx(c), g))
    dt_spec = pl.BlockSpec((q, LANES), lambda g, c: (cidx(c), g))
    b_spec = pl.BlockSpec((q, LANES), lambda g, c: (cidx(c), boff + g))
    c_spec = pl.BlockSpec((q, LANES), lambda g, c: (cidx(c), boff + SSD_GROUPS + g))
    a_spec = pl.BlockSpec((1, SUBLANES, LANES), lambda g, c: (g, 0, 0))
    st_spec = pl.BlockSpec((1, wx, LANES), lambda g, c: (g, 0, 0))
    ent_spec = pl.BlockSpec((1, 1, wx, LANES), lambda g, c: (g, cidx(c), 0, 0))
    return x_spec, dt_spec, b_spec, c_spec, a_spec, st_spec, ent_spec


def ssd_fwd(xbc, dtg, a_g, h0, rev, w_ssd, name):
    L = xbc.shape[0]
    nc = L // SSD_CHUNK
    hpg = w_ssd // (SSD_GROUPS * HEAD_DIM)
    wx = hpg * HEAD_DIM
    col0 = hpg if rev else 0
    x_spec, dt_spec, b_spec, c_spec, a_spec, st_spec, ent_spec = _ssd_specs(nc, hpg, w_ssd, rev)

    def body(x_ref, dt_ref, b_ref, c_ref, a_ref, h0_ref, y_ref, ent_ref, fin_ref, hs):
        c = pl.program_id(1)

        @pl.when(c == 0)
        def _():
            hs[...] = h0_ref[0]

        ent_ref[0, 0] = hs[...]
        xs = [x_ref[:, e * HEAD_DIM:(e + 1) * HEAD_DIM].astype(F32) for e in range(hpg)]
        hin = [hs[e * HEAD_DIM:(e + 1) * HEAD_DIM, :] for e in range(hpg)]
        ys, hn = _ssd_chunk(xs, dt_ref[...], b_ref[...].astype(F32), c_ref[...].astype(F32), hin, a_ref[0, 0:1, :], rev, col0)
        for e in range(hpg):
            y_ref[:, e * HEAD_DIM:(e + 1) * HEAD_DIM] = ys[e].astype(y_ref.dtype)
            hs[e * HEAD_DIM:(e + 1) * HEAD_DIM, :] = hn[e]
        fin_ref[0] = hs[...]

    return pl.pallas_call(
        body, name=name, grid=(SSD_GROUPS, nc),
        in_specs=[x_spec, dt_spec, b_spec, c_spec, a_spec, st_spec],
        out_specs=[x_spec, ent_spec, st_spec],
        out_shape=[jax.ShapeDtypeStruct((L, w_ssd), BF16),
                   jax.ShapeDtypeStruct((SSD_GROUPS, nc, wx, LANES), F32),
                   jax.ShapeDtypeStruct((SSD_GROUPS, wx, LANES), F32)],
        scratch_shapes=[pltpu.VMEM((wx, LANES), F32)],
        compiler_params=_cparams(("parallel", "arbitrary")),
    )(xbc, dtg, xbc, xbc, a_g, h0)


def ssd_bwd(xbc, dtg, a_g, ent, dy, dfin, rev, w_ssd, name, carry=None):
    L = xbc.shape[0]
    nc = L // SSD_CHUNK
    hpg = w_ssd // (SSD_GROUPS * HEAD_DIM)
    wx = hpg * HEAD_DIM
    col0 = hpg if rev else 0
    x_spec, dt_spec, b_spec, c_spec, a_spec, st_spec, ent_spec = _ssd_specs(nc, hpg, w_ssd, not rev)
    bc_out = pl.BlockSpec((SSD_CHUNK, LANES), lambda g, c: ((c if rev else nc - 1 - c), g))
    n_ci, n_co = (len(carry.ins), len(carry.outs)) if carry else (0, 0)

    def body(x_ref, dt_ref, b_ref, c_ref, a_ref, ent_ref, dy_ref, dfin_ref, *rest):
        c_ins = rest[:n_ci]
        dx_ref, ddt_ref, db_ref, dc_ref, da_ref, dh0_ref = rest[n_ci:n_ci + 6]
        c_outs = rest[n_ci + 6:n_ci + 6 + n_co]
        dhs = rest[n_ci + 6 + n_co]
        sems = tuple(rest[n_ci + 7 + n_co:])
        g, c = pl.program_id(0), pl.program_id(1)
        if carry:
            pl.when((g == 0) & (c == 0))(lambda: carry.start(c_ins, c_outs, sems))

        @pl.when(c == 0)
        def _():
            dhs[...] = dfin_ref[0]
            da_ref[...] = jnp.zeros_like(da_ref)

        xs = [x_ref[:, e * HEAD_DIM:(e + 1) * HEAD_DIM].astype(F32) for e in range(hpg)]
        hin = [ent_ref[0, 0, e * HEAD_DIM:(e + 1) * HEAD_DIM, :] for e in range(hpg)]
        _, vjp = jax.vjp(lambda *a: _ssd_chunk(*a, rev, col0), xs, dt_ref[...], b_ref[...].astype(F32), c_ref[...].astype(F32), hin,
                         a_ref[0, 0:1, :])
        dys = [dy_ref[:, e * HEAD_DIM:(e + 1) * HEAD_DIM].astype(F32) for e in range(hpg)]
        dhn = [dhs[e * HEAD_DIM:(e + 1) * HEAD_DIM, :] for e in range(hpg)]
        dxs, ddt, db, dc, dh, da = vjp((dys, dhn))
        for e in range(hpg):
            dx_ref[:, e * HEAD_DIM:(e + 1) * HEAD_DIM] = dxs[e].astype(dx_ref.dtype)
            dhs[e * HEAD_DIM:(e + 1) * HEAD_DIM, :] = dh[e]
        ddt_ref[...] = ddt
        db_ref[...] = db
        dc_ref[...] = dc
        da_ref[0, 0:1, :] += da
        dh0_ref[0] = dhs[...]
        if carry:
            pl.when((g == SSD_GROUPS - 1) & (c == nc - 1))(lambda: carry.finish(c_ins, c_outs, sems))

    res = pl.pallas_call(
        body, name=name, grid=(SSD_GROUPS, nc),
        in_specs=[x_spec, dt_spec, b_spec, c_spec, a_spec, ent_spec, x_spec, st_spec] + [_HBM] * n_ci,
        out_specs=[x_spec, dt_spec, bc_out, bc_out, a_spec, st_spec] + [_HBM] * n_co,
        out_shape=[jax.ShapeDtypeStruct((L, w_ssd), BF16),
                   jax.ShapeDtypeStruct((L, SSD_GROUPS * LANES), F32),
                   jax.ShapeDtypeStruct((L, SSD_GROUPS * LANES), F32),
                   jax.ShapeDtypeStruct((L, SSD_GROUPS * LANES), F32),
                   jax.ShapeDtypeStruct((SSD_GROUPS, SUBLANES, LANES), F32),
                   jax.ShapeDtypeStruct((SSD_GROUPS, wx, LANES), F32)] + (carry.outs if carry else []),
        scratch_shapes=[pltpu.VMEM((wx, LANES), F32)] + (carry.sem_shapes() if carry else []),
        compiler_params=_cparams(("arbitrary", "arbitrary") if carry else ("parallel", "arbitrary")),
    )(xbc, dtg, xbc, xbc, a_g, ent, dy, dfin, *(carry.ins if carry else []))
    return (list(res[:6]), list(res[6:])) if carry else res


CONV_W = 4
CONV_LEFT = 2
CONV_CB = 128
CONV_ROWS = 256
HALO = 16


def _conv_window(ref, r, rows, nch, L):
    s = pl.multiple_of(r * rows, rows)
    cur = ref[pl.ds(s, rows), :].astype(F32)
    sp = pl.multiple_of(jnp.maximum(s - HALO, 0), HALO)
    sn = pl.multiple_of(jnp.minimum(s + rows, L - HALO), HALO)
    prev = jnp.where(r > 0, ref[pl.ds(sp, HALO), :].astype(F32), 0.0)
    nxt = jnp.where(r < nch - 1, ref[pl.ds(sn, HALO), :].astype(F32), 0.0)
    return jnp.concatenate([prev, cur, nxt], axis=0)


def _shifted(win, off, rows):
    n = win.shape[0]
    return pltpu.roll(win, (-off) % n, axis=0)[HALO:HALO + rows, :]


def _dsilu(p):
    s = _sigmoid(p)
    return s * (1.0 + p * (1.0 - s))


def conv_fwd(src, col_blk0, w, b, act, name, out_dtype=F32):
    L, C = src.shape[0], w.shape[1]
    rows = min(CONV_ROWS, L)
    nch = L // rows

    def body(x_ref, w_ref, b_ref, o_ref):
        def chunk(r, carry):
            win = _conv_window(x_ref, r, rows, nch, L)
            pre = b_ref[...] + sum(w_ref[k:k + 1, :] * _shifted(win, k - CONV_LEFT, rows) for k in range(CONV_W))
            o_ref[pl.ds(pl.multiple_of(r * rows, rows), rows), :] = (_silu(pre) if act else pre).astype(o_ref.dtype)
            return carry

        lax.fori_loop(0, nch, chunk, 0)

    return pl.pallas_call(
        body, name=name, grid=(C // CONV_CB,),
        in_specs=[pl.BlockSpec((L, CONV_CB), lambda j: (0, col_blk0 + j)),
                  pl.BlockSpec((CONV_W, CONV_CB), lambda j: (0, j)),
                  pl.BlockSpec((1, CONV_CB), lambda j: (0, j))],
        out_specs=pl.BlockSpec((L, CONV_CB), lambda j: (0, j)),
        out_shape=jax.ShapeDtypeStruct((L, C), out_dtype),
        compiler_params=_cparams(("parallel",)),
    )(src, w, b)


def conv_bwd(src, col_blk0, w, b, douts, act, name):
    L, C = src.shape[0], w.shape[1]
    rows = min(CONV_ROWS, L)
    nch = L // rows
    nd = len(douts)

    def body(*refs):
        x_ref, w_ref, b_ref = refs[:3]
        d_refs = refs[3:3 + nd]
        dx_ref, dw_ref, db_ref, dp = refs[3 + nd:]

        def pass1(r, acc):
            sl = pl.ds(pl.multiple_of(r * rows, rows), rows)
            win = _conv_window(x_ref, r, rows, nch, L)
            taps = [_shifted(win, k - CONV_LEFT, rows) for k in range(CONV_W)]
            dpre = sum(d[sl, :].astype(F32) for d in d_refs)
            if act:
                pre = b_ref[...] + sum(w_ref[k:k + 1, :] * taps[k] for k in range(CONV_W))
                dpre = dpre * _dsilu(pre)
            dp[sl, :] = dpre
            new = [acc[k] + jnp.sum(dpre * taps[k], axis=0, keepdims=True) for k in range(CONV_W)]
            return tuple(new) + (acc[CONV_W] + jnp.sum(dpre, axis=0, keepdims=True),)

        zero = jnp.zeros((1, CONV_CB), F32)
        acc = lax.fori_loop(0, nch, pass1, (zero,) * (CONV_W + 1))
        for k in range(CONV_W):
            dw_ref[k:k + 1, :] = acc[k]
        db_ref[...] = acc[CONV_W]

        def pass2(r, carry):
            win = _conv_window(dp, r, rows, nch, L)
            dx = sum(w_ref[k:k + 1, :] * _shifted(win, CONV_LEFT - k, rows) for k in range(CONV_W))
            dx_ref[pl.ds(pl.multiple_of(r * rows, rows), rows), :] = dx.astype(dx_ref.dtype)
            return carry

        lax.fori_loop(0, nch, pass2, 0)

    col = pl.BlockSpec((L, CONV_CB), lambda j: (0, j))
    return pl.pallas_call(
        body, name=name, grid=(C // CONV_CB,),
        in_specs=[pl.BlockSpec((L, CONV_CB), lambda j: (0, col_blk0 + j)),
                  pl.BlockSpec((CONV_W, CONV_CB), lambda j: (0, j)),
                  pl.BlockSpec((1, CONV_CB), lambda j: (0, j))] + [col] * nd,
        out_specs=[col, pl.BlockSpec((CONV_W, CONV_CB), lambda j: (0, j)), pl.BlockSpec((1, CONV_CB), lambda j: (0, j))],
        out_shape=[jax.ShapeDtypeStruct((L, C), BF16), jax.ShapeDtypeStruct((CONV_W, C), F32), jax.ShapeDtypeStruct((1, C), F32)],
        scratch_shapes=[pltpu.VMEM((L, CONV_CB), F32)],
        compiler_params=_cparams(("parallel",)),
    )(src, w, b, *douts)


SCAN_ROWS = 512
SCAN_CB = 512


def _scan8(a, u, rev):
    sub = lax.broadcasted_iota(jnp.int32, a.shape, 0)
    for s in (1, 2, 4):
        if rev:
            ok = sub < SUBLANES - s
            a_sh, u_sh = pltpu.roll(a, SUBLANES - s, axis=0), pltpu.roll(u, SUBLANES - s, axis=0)
        else:
            ok = sub >= s
            a_sh, u_sh = pltpu.roll(a, s, axis=0), pltpu.roll(u, s, axis=0)
        u = a * jnp.where(ok, u_sh, 0.0) + u
        a = a * jnp.where(ok, a_sh, 1.0)
    return a, u


def _shift8(h, carry, rev):
    sub = lax.broadcasted_iota(jnp.int32, h.shape, 0)
    if rev:
        return jnp.where(sub == SUBLANES - 1, carry, pltpu.roll(h, SUBLANES - 1, axis=0))
    return jnp.where(sub == 0, carry, pltpu.roll(h, 1, axis=0))


def lru_scan(a, u, c0, rev, name, adjoint_of=None, add=None):
    L, C = a.shape
    tc, cb = _pick(L, SCAN_ROWS), _pick(C, SCAN_CB)
    nt = L // tc
    ng = tc // SUBLANES
    adj = adjoint_of is not None
    n_in = 3 + (1 if adj else 0) + (1 if add is not None else 0)

    def body(*refs):
        a_ref, u_ref, c0_ref = refs[:3]
        x_ref = refs[3] if (adj or add is not None) else None
        outs = refs[n_in:-1]
        st = refs[-1]

        @pl.when(pl.program_id(1) == 0)
        def _():
            st[...] = c0_ref[...]

        def group(i, carry):
            g = (ng - 1 - i) if rev else i
            sl = pl.ds(pl.multiple_of(g * SUBLANES, SUBLANES), SUBLANES)
            av, uv = a_ref[sl, :], u_ref[sl, :]
            pa, h = _scan8(av, av * uv if adj else uv, rev)
            h = pa * carry + h
            hs = _shift8(h, carry, rev)
            if adj:
                lam = uv + hs
                outs[0][sl, :] = lam
                outs[1][sl, :] = lam * x_ref[sl, :]
            else:
                outs[0][sl, :] = h
                outs[1][sl, :] = hs
                if add is not None:
                    outs[3][sl, :] = h + x_ref[sl, :]
            last = 0 if rev else SUBLANES - 1
            return jnp.broadcast_to(h[last:last + 1, :], h.shape)

        st[...] = lax.fori_loop(0, ng, group, st[...])
        outs[2][...] = st[...]

    def tmap(j, t):
        return ((nt - 1 - t) if rev else t, j)

    blk = pl.BlockSpec((tc, cb), tmap)
    vec = pl.BlockSpec((SUBLANES, cb), lambda j, t: (0, j))
    args = [a, u, c0] + ([adjoint_of] if adj else []) + ([add] if add is not None else [])
    n_big = 2 if (adj or add is None) else 3
    out_specs = [blk, blk, vec] + ([blk] if n_big == 3 else [])
    out_shape = [jax.ShapeDtypeStruct((L, C), F32), jax.ShapeDtypeStruct((L, C), F32), jax.ShapeDtypeStruct((SUBLANES, C), F32)]
    out_shape += [jax.ShapeDtypeStruct((L, C), F32)] if n_big == 3 else []
    return pl.pallas_call(
        body, name=name, grid=(C // cb, nt),
        in_specs=[blk, blk, vec] + [blk] * (n_in - 3),
        out_specs=out_specs, out_shape=out_shape,
        scratch_shapes=[pltpu.VMEM((SUBLANES, cb), F32)],
        compiler_params=_cparams(("parallel", "arbitrary")),
    )(*args)


def f_silu(v):
    return (_silu(v),)


def f_add_bias(v, b):
    return (v + b,)


def f_mod_in(x, gain, shift, scale):
    return _rms(x, gain) * (1.0 + scale) + shift, x


def f_dt(raw, bias):
    return (_softplus(raw + bias),)


def f_gnorm(yf, yb, xs, z, dexp, gain):
    return (_rms((yf + yb + dexp * xs) * _silu(z), gain),)


def f_gelu_gate(r, yr):
    return (r * _gelu_tanh(yr),)


def f_merge(gs, gr, o_s, o_r, bs, br):
    return (_sigmoid(gs + bs) * o_s + _sigmoid(gr + br) * o_r,)


def f_res_mod(x, out, gm, gain, shift, scale):
    x1 = x + gm * out
    return x1, _rms(x1, gain) * (1.0 + scale) + shift


def f_final(x1, f, t, gf, gain):
    e = _rms(x1 + gf * f, gain) - t
    return (jnp.broadcast_to(0.5 * jnp.mean(e * e, axis=-1, keepdims=True), (e.shape[0], LANES)),)


def _lru_coeffs(xh, wa, ba, wx, bx, lam):
    r = _sigmoid(bdot(xh, wa, "nn") + ba)
    i = _sigmoid(bdot(xh, wx, "nn") + bx)
    log_a = -LRU_C * r * _softplus(-lam)
    a = jnp.exp(log_a)
    one_minus_a2 = -jnp.tanh(log_a) * (a * a + 1.0)
    return a, jnp.sqrt(one_minus_a2) * (i * xh)


def f_gates(xh, wa_f, ba_f, wx_f, bx_f, lam_f, wa_b, ba_b, wx_b, bx_b, lam_b):
    return _lru_coeffs(xh, wa_f, ba_f, wx_f, bx_f, lam_f) + _lru_coeffs(xh, wa_b, ba_b, wx_b, bx_b, lam_b)


def f_adam(w, g, m, v):
    m2 = ADAM_B1 * m + (1.0 - ADAM_B1) * g
    v2 = ADAM_B2 * v + (1.0 - ADAM_B2) * (g * g)
    m_hat = m2 / (1.0 - ADAM_B1 ** ADAM_STEP)
    v_hat = v2 / (1.0 - ADAM_B2 ** ADAM_STEP)
    return -ADAM_LR * (m_hat / (jnp.sqrt(v_hat) + ADAM_EPS) + ADAM_WD * w), m2, v2


def swiglu(gu, dact, name, tb=128, sub=16):
    L, f2 = gu.shape
    f = f2 // 2
    tb = _pick(L, tb)

    def body(*refs):
        gu_ref, o_ref = refs[0], refs[-1]

        def step(i, carry):
            sl = pl.ds(pl.multiple_of(i * sub, sub), sub)
            g, u = gu_ref[sl, :f].astype(F32), gu_ref[sl, f:].astype(F32)
            if dact is None:
                o_ref[sl, :] = (_silu(g) * u).astype(o_ref.dtype)
            else:
                d = refs[1][sl, :].astype(F32)
                o_ref[sl, :f] = (d * u * _dsilu(g)).astype(o_ref.dtype)
                o_ref[sl, f:] = (d * _silu(g)).astype(o_ref.dtype)
            return carry

        lax.fori_loop(0, tb // sub, step, 0)

    wout = f if dact is None else f2
    ins = [gu] + ([] if dact is None else [dact])
    return pl.pallas_call(
        body, name=name, grid=(L // tb,),
        in_specs=[pl.BlockSpec((tb, a.shape[1]), lambda i: (i, 0)) for a in ins],
        out_specs=pl.BlockSpec((tb, wout), lambda i: (i, 0)),
        out_shape=jax.ShapeDtypeStruct((L, wout), BF16),
        compiler_params=_cparams(("parallel",)),
    )(*ins)


N_DEV = 8
N_CHIP = 4
PACK_W = 1024
_HBM = pl.BlockSpec(memory_space=pltpu.HBM)


def _me():
    return lax.axis_index("x"), lax.axis_index("y"), lax.axis_index("c")


def _flip(pos, k):
    x, y, c = pos
    return (1 - x if k & 4 else x, 1 - y if k & 2 else y, 1 - c if k & 1 else c)


def _dev_index(pos):
    return 4 * pos[0] + 2 * pos[1] + pos[2]


def _chip_index(pos):
    return 2 * pos[0] + pos[1]


def _rcopy(src, dst, sems, k, to):
    send_sems, recv_sems = sems
    return pltpu.make_async_remote_copy(src_ref=src, dst_ref=dst, send_sem=send_sems.at[k], recv_sem=recv_sems.at[k],
                                        device_id=to, device_id_type=MESH)


def fill_own_device_slot(stack, mine):
    return lax.dynamic_update_slice(stack, mine[None], (_dev_index(_me()),) + (0,) * mine.ndim)


COPY_PIECES = 4
ROW_TILE_16BIT = 16


def _pieces(rows, n=COPY_PIECES, align=ROW_TILE_16BIT):
    while n > 1 and rows % (n * align):
        n -= 1
    return [(q * (rows // n), rows // n) for q in range(n)]


class Exchange:
    def __init__(self, ins, outs, n_sems, start, finish):
        self.ins, self.outs, self.n_sems, self.start, self.finish = list(ins), list(outs), n_sems, start, finish

    def sem_shapes(self):
        return [pltpu.SemaphoreType.DMA((self.n_sems,)), pltpu.SemaphoreType.DMA((self.n_sems,))]


def run_exchange(ex, name):
    n_in, n_out = len(ex.ins), len(ex.outs)

    def body(*refs):
        ins, outs, sems = refs[:n_in], refs[n_in:n_in + n_out], tuple(refs[n_in + n_out:])
        ex.start(ins, outs, sems)
        ex.finish(ins, outs, sems)

    return pl.pallas_call(body, name=name, in_specs=[_HBM] * n_in, out_specs=[_HBM] * n_out, out_shape=ex.outs,
                          scratch_shapes=ex.sem_shapes())(*ex.ins)


def allgather_exchange(v):
    def plan(ins, outs):
        me = _me()
        for k in range(1, N_DEV):
            peer = _flip(me, k)
            yield outs[0].at[_dev_index(me)], outs[0].at[_dev_index(peer)], k - 1, peer

    def start(ins, outs, sems):
        for mine, _, k, peer in plan(ins, outs):
            _rcopy(ins[0], mine, sems, k, peer).start()

    def finish(ins, outs, sems):
        for _, theirs, k, peer in plan(ins, outs):
            _rcopy(theirs, theirs, sems, k, peer).wait_recv()
        for mine, _, k, peer in plan(ins, outs):
            _rcopy(ins[0], mine, sems, k, peer).wait_send()

    return Exchange([v], [jax.ShapeDtypeStruct((N_DEV,) + v.shape, v.dtype)], N_DEV - 1, start, finish)


def allgather8(v, name):
    return fill_own_device_slot(run_exchange(allgather_exchange(v), name)[0], v)


def gather_exchange(shards):
    cuts = [_pieces(s.shape[0] // 2) for s in shards]
    base = [0]
    for cu in cuts:
        base.append(base[-1] + 6 * len(cu))

    def plan(ins, outs):
        me = _me()
        chips = [_flip(me, 4), _flip(me, 2), _flip(me, 6)]
        for w, cu in enumerate(cuts):
            half, nq = shards[w].shape[0] // 2, len(cu)
            for q, (s, n) in enumerate(cu):
                mine = pl.ds(pl.multiple_of(me[2] * half + s, ROW_TILE_16BIT), n)
                sibs = pl.ds(pl.multiple_of((1 - me[2]) * half + s, ROW_TILE_16BIT), n)
                for j, p in enumerate(chips):
                    yield w, p, mine, sibs, base[w] + j * nq + q, base[w] + (3 + j) * nq + q

    def start(ins, outs, sems):
        me = _me()
        for w, p, mine, _, k_ici, _ in plan(ins, outs):
            _rcopy(ins[w].at[mine], outs[w].at[_chip_index(me), mine], sems, k_ici, p).start()

    def finish(ins, outs, sems):
        me = _me()
        sibling = _flip(me, 1)
        for w, p, mine, _, k_ici, k_fwd in plan(ins, outs):
            got = outs[w].at[_chip_index(p), mine]
            _rcopy(got, got, sems, k_ici, p).wait_recv()
            _rcopy(got, got, sems, k_fwd, sibling).start()
        for w, p, _, sibs, _, k_fwd in plan(ins, outs):
            got = outs[w].at[_chip_index(p), sibs]
            _rcopy(got, got, sems, k_fwd, sibling).wait_recv()
        for w, p, mine, _, k_ici, k_fwd in plan(ins, outs):
            got = outs[w].at[_chip_index(p), mine]
            _rcopy(ins[w].at[mine], got, sems, k_ici, p).wait_send()
            _rcopy(got, got, sems, k_fwd, sibling).wait_send()

    outs = [jax.ShapeDtypeStruct((N_CHIP,) + s.shape, s.dtype) for s in shards]
    return Exchange(shards, outs, base[-1], start, finish)


def fill_own_slot(stack, mine):
    return lax.dynamic_update_slice(stack, mine[None], (_chip_index(_me()), 0, 0))


def scatter_exchange(parts):
    def plan(ins, outs):
        me = _me()
        for w in range(len(parts)):
            half = parts[w].shape[1] // 2
            for k in range(1, N_DEV):
                to = _flip(me, k)
                src = ins[w].at[_chip_index(to), pl.ds(pl.multiple_of(to[2] * half, ROW_TILE_16BIT), half)]
                yield src, outs[w].at[k - 1], (N_DEV - 1) * w + k - 1, to

    def start(ins, outs, sems):
        for src, dst, k, to in plan(ins, outs):
            _rcopy(src, dst, sems, k, to).start()

    def finish(ins, outs, sems):
        for src, dst, k, to in plan(ins, outs):
            _rcopy(dst, dst, sems, k, to).wait_recv()
        for src, dst, k, to in plan(ins, outs):
            _rcopy(src, dst, sems, k, to).wait_send()

    outs = [jax.ShapeDtypeStruct((N_DEV - 1, p.shape[1] // 2, p.shape[2]), p.dtype) for p in parts]
    return Exchange(parts, outs, (N_DEV - 1) * len(parts), start, finish)


def own_piece(parts):
    me = _me()
    half = parts.shape[1] // 2
    return lax.dynamic_slice(parts, (_chip_index(me), me[2] * half, 0), (1, half, parts.shape[2]))[0]


def join_halves(mine, name):
    half, C = mine.shape
    cuts = _pieces(half, 2 * COPY_PIECES, SUBLANES)
    nq = len(cuts)

    def body(m_ref, out_ref, send_sems, recv_sems):
        me = _me()
        sems = (send_sems, recv_sems)
        sibling = _flip(me, 1)
        sends = [_rcopy(m_ref.at[pl.ds(s, n)], out_ref.at[pl.ds(s, n)], sems, q, sibling) for q, (s, n) in enumerate(cuts)]
        for cp in sends:
            cp.start()
        for q, (s, n) in enumerate(cuts):
            got = out_ref.at[pl.ds(s, n)]
            _rcopy(got, got, sems, q, sibling).wait_recv()
        for cp in sends:
            cp.wait_send()

    other = pl.pallas_call(
        body, name=name, in_specs=[_HBM], out_specs=_HBM,
        out_shape=jax.ShapeDtypeStruct((half, C), mine.dtype),
        scratch_shapes=[pltpu.SemaphoreType.DMA((nq,)), pltpu.SemaphoreType.DMA((nq,))],
    )(mine)
    south = _me()[2] == 0
    return jnp.concatenate([jnp.where(south, mine, other), jnp.where(south, other, mine)], axis=0)


SUM_BLOCK_ELEMS = 256 * 1024


def sum_slots(a, name, first=None):
    n, R, C = a.shape
    tb = _pick(R, max(ROW_TILE_16BIT, SUM_BLOCK_ELEMS // C), ROW_TILE_16BIT)

    def body(a_ref, *rest):
        o_ref = rest[-1]
        acc = a_ref[0].astype(F32) if first is None else rest[0][...].astype(F32) + a_ref[0].astype(F32)
        for j in range(1, n):
            acc = acc + a_ref[j].astype(F32)
        o_ref[...] = acc

    row = pl.BlockSpec((tb, C), lambda i: (i, 0))
    return pl.pallas_call(
        body, name=name, grid=(R // tb,),
        in_specs=[pl.BlockSpec((n, tb, C), lambda i: (0, i, 0))] + ([] if first is None else [row]),
        out_specs=row,
        out_shape=jax.ShapeDtypeStruct((R, C), F32),
        compiler_params=_cparams(("parallel",)),
    )(*([a] if first is None else [a, first]))


def _pack(arrays, dtype, row_align):
    flat = jnp.concatenate([a.astype(dtype).reshape(-1) for a in arrays])
    per = PACK_W * row_align
    n = -(-flat.shape[0] // per) * per
    return jnp.pad(flat, (0, n - flat.shape[0])).reshape(-1, PACK_W)


def _unpack(buf, shapes):
    flat = buf.reshape(-1)
    out, at = [], 0
    for s in shapes:
        n = 1
        for d in s:
            n *= d
        out.append(flat[at:at + n].reshape(s))
        at += n
    return out


def _to_col_major(t, rows):
    L, C = t.shape
    return t.reshape(rows, GRID_W, C).transpose(1, 0, 2).reshape(L, C)


def _to_row_major(t, rows):
    L, C = t.shape
    return t.reshape(GRID_W, rows, C).transpose(1, 0, 2).reshape(L, C)


def _heads_to_groups(v, hpg):
    lead = v.shape[:-1]
    t = jnp.moveaxis(v.reshape(lead + (2, SSD_GROUPS, hpg)), -3, -2).reshape(lead + (SSD_GROUPS, 2 * hpg))
    t = jnp.pad(t, [(0, 0)] * (len(lead) + 1) + [(0, LANES - 2 * hpg)])
    return t.reshape(lead + (SSD_GROUPS * LANES,))


def _groups_to_heads(t, hpg):
    lead = t.shape[:-1]
    t = t.reshape(lead + (SSD_GROUPS, LANES))[..., :2 * hpg].reshape(lead + (SSD_GROUPS, 2, hpg))
    return jnp.moveaxis(t, -2, -3).reshape(lead + (2 * SSD_GROUPS * hpg,))


def _r1(v):
    return v.reshape(1, -1)


class _Layout:
    def __init__(self, D, W, hpg):
        assert W == D and D % (SSD_GROUPS * LANES) == 0
        self.D, self.W, self.hpg = D, W, hpg
        self.GN = SSD_GROUPS * SSD_STATE
        self.xbc0 = 5 * D
        self.dt0 = 5 * D + W + 2 * self.GN
        self.width = self.dt0 + SSD_GROUPS * LANES
        self.xbc_blk = self.xbc0 // CONV_CB
        self.dt_blk = self.dt0 // (SSD_GROUPS * LANES)
        assert self.dt0 % (SSD_GROUPS * LANES) == 0


def _seq_forward(hb, w_all, sp, lay, tag, grid_rows, init, carry=None):
    D, W = lay.D, lay.W
    gw = SSD_GROUPS * LANES
    pg = matmul(hb, w_all, "nn", BF16, f"proj_{tag}", carry=carry)
    pg, carried = pg if carry else (pg, [])
    xbc = conv_fwd(pg, lay.xbc_blk, sp["ssd_conv_w"], sp["ssd_conv_b"], True, f"ssd_conv_{tag}", out_dtype=BF16)
    dtg = rowwise(f_dt, [Cols(pg, gw, lay.dt_blk)], [sp["dt_bias_g"]], [(gw, F32)], f"ssd_dt_{tag}")[0]
    yf, ent_f, fin_f = ssd_fwd(xbc, dtg, sp["a_g"], init["ssd_f"], False, W, f"ssd_scan_f_{tag}")
    yb, ent_b, fin_b = ssd_fwd(xbc, dtg, sp["a_g"], init["ssd_b"], True, W, f"ssd_scan_b_{tag}")
    xr = pg[:, D:2 * D]
    if grid_rows:
        xr = _to_col_major(xr, grid_rows)
    xc = conv_fwd(xr, 0, sp["lru_conv_w"], sp["lru_conv_b"], False, f"lru_conv_{tag}")
    a_f, u_f, a_b, u_b = rowwise(f_gates, [xc], sp["gate_pars"], [(LRU_HEAD, F32)] * 4, f"lru_gates_{tag}",
                                 tb=256, sub=256, ncol=D // LRU_HEAD)
    h_f, hs_f, fl_f = lru_scan(a_f, u_f, init["lru_f"], False, f"lru_scan_f_{tag}")
    _, hs_b, fl_b, r = lru_scan(a_b, u_b, init["lru_b"], True, f"lru_scan_b_{tag}", add=h_f)
    saved = dict(pg=pg, xbc=xbc, dtg=dtg, yf=yf, yb=yb, ent_f=ent_f, ent_b=ent_b, xr=xr, xc=xc,
                 a_f=a_f, a_b=a_b, hs_f=hs_f, hs_b=hs_b, r=r)
    return saved, dict(ssd_f=fin_f, ssd_b=fin_b, lru_f=fl_f, lru_b=fl_b), carried


def _seq_backward(sv, sp, lay, tag, grid_rows, dy, dxs_extra, d_r, dfin, carries=(None, None)):
    D, W, GN = lay.D, lay.W, lay.GN
    gw = SSD_GROUPS * LANES
    pg = sv["pg"]
    res_f = ssd_bwd(sv["xbc"], sv["dtg"], sp["a_g"], sv["ent_f"], dy, dfin["ssd_f"], False, W, f"ssd_scan_f_bwd_{tag}", carry=carries[0])
    res_b = ssd_bwd(sv["xbc"], sv["dtg"], sp["a_g"], sv["ent_b"], dy, dfin["ssd_b"], True, W, f"ssd_scan_b_bwd_{tag}", carry=carries[1])
    (dxf, ddt_f, db_f, dc_f, da_f, dh0_f), carried_f = res_f if carries[0] else (res_f, [])
    (dxb, ddt_b, db_b, dc_b, da_b, dh0_b), carried_b = res_b if carries[1] else (res_b, [])
    cw, cb = sp["ssd_conv_w"], sp["ssd_conv_b"]
    d_xs, dw1, db1 = conv_bwd(pg, lay.xbc_blk, cw[:, :W], cb[:, :W], [dxf, dxb] + dxs_extra, True, f"ssd_conv_x_bwd_{tag}")
    d_b, dw2, db2 = conv_bwd(pg, lay.xbc_blk + W // CONV_CB, cw[:, W:W + GN], cb[:, W:W + GN], [db_f, db_b], True, f"ssd_conv_b_bwd_{tag}")
    d_c, dw3, db3 = conv_bwd(pg, lay.xbc_blk + (W + GN) // CONV_CB, cw[:, W + GN:], cb[:, W + GN:], [dc_f, dc_b], True, f"ssd_conv_c_bwd_{tag}")
    d_dtraw, d_dtbias = rowwise_vjp(f_dt, [Cols(pg, gw, lay.dt_blk)], [sp["dt_bias_g"]], [[ddt_f, ddt_b]], [BF16], f"ssd_dt_bwd_{tag}")
    du_b, dab, dl0_b = lru_scan(sv["a_b"], d_r, dfin["lru_b"], False, f"lru_scan_b_bwd_{tag}", adjoint_of=sv["hs_b"])
    du_f, daf, dl0_f = lru_scan(sv["a_f"], d_r, dfin["lru_f"], True, f"lru_scan_f_bwd_{tag}", adjoint_of=sv["hs_f"])
    res = rowwise_vjp(f_gates, [sv["xc"]], sp["gate_pars"], [daf, du_f, dab, du_b], [F32], f"lru_gates_bwd_{tag}",
                      tb=256, sub=256, ncol=D // LRU_HEAD)
    d_xc, gate_grads = res[0], res[1:]
    d_xr, g_lcw, g_lcb = conv_bwd(sv["xr"], 0, sp["lru_conv_w"], sp["lru_conv_b"], [d_xc], False, f"lru_conv_bwd_{tag}")
    if grid_rows:
        d_xr = _to_row_major(d_xr, grid_rows)
    grads = dict(ssd_conv_w=jnp.concatenate([dw1, dw2, dw3], axis=1), ssd_conv_b=jnp.concatenate([db1, db2, db3], axis=1),
                 dt_bias_g=d_dtbias, a_g=da_f + da_b, lru_conv_w=g_lcw, lru_conv_b=g_lcb, gate_pars=list(gate_grads))
    pieces = dict(xr=d_xr, xs=d_xs, b=d_b, c=d_c, dt=d_dtraw)
    return pieces, grads, dict(ssd_f=dh0_f, ssd_b=dh0_b, lru_f=dl0_f, lru_b=dl0_b), (carried_f, carried_b)


def _proj_cotangent(lay, px, pc, Lc, dz, dyr, dgs, dgr):
    zero = jnp.zeros((Lc, lay.D), BF16)
    cols = [(dz, zero), (px["xr"], pc["xr"]), (dyr, zero), (dgs, zero), (dgr, zero),
            (px["xs"], pc["xs"]), (px["b"], pc["b"]), (px["c"], pc["c"]), (px["dt"], pc["dt"])]
    return jnp.concatenate([jnp.concatenate([a.astype(BF16), b.astype(BF16)], axis=0) for a, b in cols], axis=1)


EARLY = ["w_in", "w_gate"]
LATE = ["w_out_ssd", "w_out_lru", "w_o", "ffn_w13", "ffn_w2"]


def _local_step(x, ctx, target, mod_x, mod_c, wb, late, ws):
    L, D = x.shape
    Lc = ctx.shape[0]
    nh = ws["ssd_d"].shape[0]
    hpg = nh // SSD_GROUPS
    W = nh * HEAD_DIM
    lay = _Layout(D, W, hpg)
    GN = lay.GN
    grid_rows = L // GRID_W
    nlh = D // LRU_HEAD

    w_in = wb["w_in"]
    o_dt, o_xr = 2 * W + 2 * GN, 2 * W + 2 * GN + 2 * nh
    w_all = jnp.concatenate([w_in[:, :W], w_in[:, o_xr:o_xr + D], w_in[:, o_xr + D:], wb["w_gate"], w_in[:, W:o_dt],
                             _heads_to_groups(w_in[:, o_dt:o_xr], hpg)], axis=1)
    a_neg = -jnp.exp(ws["ssd_a_log"])
    a_g = jnp.pad(_heads_to_groups(a_neg.reshape(-1), hpg).reshape(SSD_GROUPS, 1, LANES), [(0, 0), (0, SUBLANES - 1), (0, 0)])
    gate_pars = []
    for d in range(2):
        gate_pars += [ws["lru_w_a"][d], ws["lru_b_a"][d].reshape(nlh, 1, LRU_HEAD), ws["lru_w_x"][d],
                      ws["lru_b_x"][d].reshape(nlh, 1, LRU_HEAD), ws["lru_lambda"][d].reshape(nlh, 1, LRU_HEAD)]
    sp = dict(ssd_conv_w=ws["ssd_conv_w"], ssd_conv_b=_r1(ws["ssd_conv_b"]), dt_bias_g=_r1(_heads_to_groups(ws["ssd_dt_bias"].reshape(-1), hpg)),
              a_g=a_g, lru_conv_w=ws["lru_conv_w"], lru_conv_b=_r1(ws["lru_conv_b"]), gate_pars=gate_pars)
    dexp = _r1(jnp.repeat(ws["ssd_d"], HEAD_DIM))
    norm_mix, norm_ffn, ssd_norm, final_norm = _r1(ws["norm_mix"]), _r1(ws["norm_ffn"]), _r1(ws["ssd_norm"]), _r1(ws["final_norm"])
    bg_s, bg_r = _r1(ws["b_gate"][:D]), _r1(ws["b_gate"][D:])
    sh_m, sc_m, g_m, sh_f, sc_f, g_f = [mod_x[:, k * D:(k + 1) * D] for k in range(6)]
    csh_m, csc_m = mod_c[:, :D], mod_c[:, D:2 * D]

    zst = dict(ssd_f=jnp.zeros((SSD_GROUPS, hpg * HEAD_DIM, SSD_STATE), F32), ssd_b=jnp.zeros((SSD_GROUPS, hpg * HEAD_DIM, SSD_STATE), F32),
               lru_f=jnp.zeros((SUBLANES, D), F32), lru_b=jnp.zeros((SUBLANES, D), F32))
    hcb = rowwise(f_mod_in, [ctx], [norm_mix, csh_m, csc_m], [(D, BF16)], "norm_mix_ctx")[0]
    hb = rowwise(f_mod_in, [x], [norm_mix, sh_m, sc_m], [(D, BF16)], "norm_mix_x")[0]
    svc, fin_c, _ = _seq_forward(hcb, w_all, sp, lay, "ctx", None, zst)
    svx, _, stacks = _seq_forward(hb, w_all, sp, lay, "x", grid_rows, fin_c, carry=gather_exchange([late[n] for n in LATE]))
    wb = dict(wb)
    for n, st in zip(LATE, stacks):
        wb[n] = _whole_from_shards(fill_own_slot(st, late[n]), n in BIG_BY_COLUMNS)
    pg = svx["pg"]
    r_rm = _to_row_major(svx["r"], grid_rows)
    gn_rows = [svx["yf"], svx["yb"], Cols(svx["xbc"], W, 0), Cols(pg, D, 0)]
    yn = rowwise(f_gnorm, gn_rows, [dexp, ssd_norm], [(W, BF16)], "ssd_gnorm")[0]
    o_s = matmul(yn, wb["w_out_ssd"], "nn", BF16, "out_ssd")
    o_in = rowwise(f_gelu_gate, [r_rm, Cols(pg, D, 2)], [], [(D, BF16)], "lru_gelu")[0]
    o_r = matmul(o_in, wb["w_out_lru"], "nn", BF16, "out_lru")
    mg_rows = [Cols(pg, D, 3), Cols(pg, D, 4), o_s, o_r]
    mixed = rowwise(f_merge, mg_rows, [bg_s, bg_r], [(D, BF16)], "merge")[0]
    out = matmul(mixed, wb["w_o"], "nn", BF16, "out_proj")
    x1, h2 = rowwise(f_res_mod, [x, out], [g_m, norm_ffn, sh_f, sc_f], [(D, F32), (D, BF16)], "res_norm_ffn")
    gu = matmul(h2, wb["ffn_w13"], "nn", BF16, "ffn_in")
    act = swiglu(gu, None, "ffn_act")
    f = matmul(act, wb["ffn_w2"], "nn", BF16, "ffn_out")

    ones = jnp.full((L, LANES), 1.0 / LANES, F32)
    dx1a, df, d_gf, d_fnorm, lsum = rowwise_vjp(f_final, [x1, f, target], [g_f, final_norm], [ones], [F32, BF16, None],
                                                "final_loss", out_sums=[(0, LANES)])
    loss = lsum[0, 0]
    d_act = matmul(df, wb["ffn_w2"], "nt", BF16, "ffn_out_dx")
    g_w2 = matmul(act, df, "tn", BF16, "ffn_out_dw")
    d_gu = swiglu(gu, d_act, "ffn_act_bwd")
    dh2 = matmul(d_gu, wb["ffn_w13"], "nt", BF16, "ffn_in_dx")
    g_w13 = matmul(h2, d_gu, "tn", BF16, "ffn_in_dw")
    dxa, d_out, d_gm, d_nffn, d_shf, d_scf = rowwise_vjp(f_res_mod, [x, out], [g_m, norm_ffn, sh_f, sc_f], [dx1a, dh2],
                                                         [F32, BF16], "res_norm_ffn_bwd")
    d_mixed = matmul(d_out, wb["w_o"], "nt", BF16, "out_proj_dx")
    g_wo = matmul(mixed, d_out, "tn", BF16, "out_proj_dw")
    dgs, dgr, do_s, do_r, d_bgs, d_bgr = rowwise_vjp(f_merge, mg_rows, [bg_s, bg_r], [d_mixed], [BF16] * 4, "merge_bwd")
    d_yn = matmul(do_s, wb["w_out_ssd"], "nt", BF16, "out_ssd_dx")
    g_wos = matmul(yn, do_s, "tn", BF16, "out_ssd_dw")
    d_oin = matmul(do_r, wb["w_out_lru"], "nt", BF16, "out_lru_dx")
    g_wol = matmul(o_in, do_r, "tn", BF16, "out_lru_dw")
    d_r_rm, d_yr = rowwise_vjp(f_gelu_gate, [r_rm, Cols(pg, D, 2)], [], [d_oin], [F32, BF16], "lru_gelu_bwd")
    dy, dxs_skip, dz, d_dexp, d_ssdn = rowwise_vjp(f_gnorm, gn_rows, [dexp, ssd_norm], [d_yn], [BF16, None, BF16, BF16], "ssd_gnorm_bwd")
    zfin = dict(zst)
    parts = {n: _shards_from_whole(g, n in BIG_BY_COLUMNS).astype(BF16)
             for n, g in (("ffn_w2", g_w2), ("ffn_w13", g_w13), ("w_o", g_wo), ("w_out_ssd", g_wos), ("w_out_lru", g_wol))}
    ride_f, ride_b = ["ffn_w2", "ffn_w13"], ["w_o", "w_out_ssd", "w_out_lru"]
    px, gx, dst, (land_f, land_b) = _seq_backward(
        svx, sp, lay, "x", grid_rows, dy, [dxs_skip], _to_col_major(d_r_rm, grid_rows), zfin,
        carries=(scatter_exchange([parts[n] for n in ride_f]), scatter_exchange([parts[n] for n in ride_b])))
    pc, gc, _, _ = _seq_backward(svc, sp, lay, "ctx", None, jnp.zeros((Lc, W), F32), [], jnp.zeros((Lc, D), F32), dst)
    dpg = _proj_cotangent(lay, px, pc, Lc, dz, d_yr, dgs, dgr)
    gg = [a + b for a, b in zip(gx["gate_pars"], gc["gate_pars"])]
    lru_w = jnp.concatenate([gg[0], gg[5], gg[2], gg[7]], axis=0).reshape(-1, PACK_W)
    g_wall, (lru_w_all,) = matmul(jnp.concatenate([hb, hcb], axis=0), dpg, "tn", BF16, "proj_dw", carry=allgather_exchange(lru_w))
    lru_w_all = fill_own_device_slot(lru_w_all, lru_w)
    xb0 = lay.xbc0
    g_w_in = jnp.concatenate([g_wall[:, :W], g_wall[:, xb0:xb0 + W + 2 * GN], _groups_to_heads(g_wall[:, lay.dt0:], hpg),
                              g_wall[:, D:2 * D], g_wall[:, 2 * D:3 * D]], axis=1)
    ride_p = ["w_in", "w_gate"]
    parts.update({n: _shards_from_whole(g, True).astype(BF16) for n, g in (("w_in", g_w_in), ("w_gate", g_wall[:, 3 * D:5 * D]))})
    dh, land_p = matmul(dpg, w_all, "nt", BF16, "proj_dx_x", a_rows=(0, L), carry=scatter_exchange([parts[n] for n in ride_p]))
    scattered = {n: (ld, own_piece(parts[n])) for n, ld in zip(ride_f + ride_b + ride_p, land_f + land_b + land_p)}
    dhc = matmul(dpg, w_all, "nt", BF16, "proj_dx_ctx", a_rows=(L, Lc))
    grad_x, d_nmix_x, d_shm, d_scm = rowwise_vjp(f_mod_in, [x], [norm_mix, sh_m, sc_m], [dh, dxa], [F32], "norm_mix_x_bwd")
    d_nmix_c, d_cshm, d_cscm = rowwise_vjp(f_mod_in, [ctx], [norm_mix, csh_m, csc_m], [dhc, jnp.zeros((Lc, D), F32)], [None], "norm_mix_ctx_bwd")

    dmod_x = jnp.concatenate([d_shm, d_scm, d_gm, d_shf, d_scf, d_gf], axis=1)
    dmod_c = jnp.concatenate([d_cshm, d_cscm, jnp.zeros((1, 4 * D), F32)], axis=1)

    d_a = _groups_to_heads((gx["a_g"] + gc["a_g"])[:, 0, :].reshape(-1), hpg).reshape(2, nh)

    def gate(k, shape):
        return jnp.stack([gg[k].reshape(shape), gg[5 + k].reshape(shape)], axis=0)

    grads = dict(
        norm_mix=(d_nmix_x + d_nmix_c).reshape(-1), norm_ffn=d_nffn.reshape(-1),
        ssd_conv_w=gx["ssd_conv_w"] + gc["ssd_conv_w"], ssd_conv_b=(gx["ssd_conv_b"] + gc["ssd_conv_b"]).reshape(-1),
        ssd_dt_bias=_groups_to_heads((gx["dt_bias_g"] + gc["dt_bias_g"]).reshape(-1), hpg).reshape(2, nh),
        ssd_a_log=d_a * a_neg, ssd_d=d_dexp.reshape(nh, HEAD_DIM).sum(axis=1), ssd_norm=d_ssdn.reshape(-1),
        lru_conv_w=gx["lru_conv_w"] + gc["lru_conv_w"], lru_conv_b=(gx["lru_conv_b"] + gc["lru_conv_b"]).reshape(-1),
        lru_b_a=gate(1, (D,)), lru_b_x=gate(3, (D,)), lru_lambda=gate(4, (D,)),
        b_gate=jnp.concatenate([d_bgs, d_bgr], axis=1).reshape(-1), final_norm=d_fnorm.reshape(-1))
    return loss, grad_x, grads, scattered, lru_w_all, dmod_x, dmod_c


WEIGHTS = ["c_ctx", "w_ada", "b_ada", "norm_mix", "norm_ffn", "w_in", "ssd_conv_w", "ssd_conv_b", "ssd_dt_bias", "ssd_a_log",
           "ssd_d", "ssd_norm", "w_out_ssd", "lru_conv_w", "lru_conv_b", "lru_w_a", "lru_b_a", "lru_w_x", "lru_b_x", "lru_lambda",
           "w_out_lru", "w_gate", "b_gate", "w_o", "ffn_w13", "ffn_w2", "final_norm"]
BIG = ["w_in", "w_out_ssd", "w_out_lru", "w_gate", "w_o", "ffn_w13", "ffn_w2"]
BIG_BY_COLUMNS = {"w_in", "w_gate", "ffn_w13"}
MOD_ROWS = 16
MOD_CTX_ROW = N_DEV
SMALL_SHARDED = ["ssd_conv_w", "lru_conv_w", "lru_b_a", "lru_b_x", "lru_lambda"]
ADAM_ROWS = 64


def _whole_from_shards(stack, by_columns):
    if by_columns:
        return stack.transpose(1, 0, 2).reshape(stack.shape[1], -1)
    return stack.reshape(-1, stack.shape[2])


def _shards_from_whole(g, by_columns):
    if by_columns:
        return g.reshape(g.shape[0], N_CHIP, -1).transpose(1, 0, 2)
    return g.reshape(N_CHIP, -1, g.shape[1])


def _adam(w, g, m, v, name):
    shape = w.shape
    cols = shape[-1]
    w2, g2, m2, v2 = [t.reshape(-1, cols) for t in (w, g, m, v)]
    d, nm, nv = rowwise(f_adam, [w2, g2, m2, v2], [], [(cols, F32)] * 3, name, tb=ADAM_ROWS, sub=8)
    return d.reshape(shape), nm.reshape(shape), nv.reshape(shape)


def kernel(x, c, ctx, c_ctx, w_ada, b_ada, norm_mix, norm_ffn, w_in, ssd_conv_w, ssd_conv_b, ssd_dt_bias, ssd_a_log, ssd_d, ssd_norm, w_out_ssd, lru_conv_w, lru_conv_b, lru_w_a, lru_b_a, lru_w_x, lru_b_x, lru_lambda, w_out_lru, w_gate, b_gate, w_o, ffn_w13, ffn_w2, final_norm, loss_target, m_c_ctx, m_w_ada, m_b_ada, m_norm_mix, m_norm_ffn, m_w_in, m_ssd_conv_w, m_ssd_conv_b, m_ssd_dt_bias, m_ssd_a_log, m_ssd_d, m_ssd_norm, m_w_out_ssd, m_lru_conv_w, m_lru_conv_b, m_lru_w_a, m_lru_b_a, m_lru_w_x, m_lru_b_x, m_lru_lambda, m_w_out_lru, m_w_gate, m_b_gate, m_w_o, m_ffn_w13, m_ffn_w2, m_final_norm, v_c_ctx, v_w_ada, v_b_ada, v_norm_mix, v_norm_ffn, v_w_in, v_ssd_conv_w, v_ssd_conv_b, v_ssd_dt_bias, v_ssd_a_log, v_ssd_d, v_ssd_norm, v_w_out_ssd, v_lru_conv_w, v_lru_conv_b, v_lru_w_a, v_lru_b_a, v_lru_w_x, v_lru_b_x, v_lru_lambda, v_w_out_lru, v_w_gate, v_b_gate, v_w_o, v_ffn_w13, v_ffn_w2, v_final_norm):
    given = dict(locals())
    layered = {n for n in WEIGHTS if n not in ("c_ctx", "final_norm")}
    w_blk = {n: (given[n][0] if n in layered else given[n]) for n in WEIGHTS}
    me = _me()
    chip, dev = _chip_index(me), _dev_index(me)
    D = c.shape[1]

    ada_w = w_blk["w_ada"].astype(BF16)
    ada_n = ada_w.shape[1]
    c_all = allgather8(jnp.concatenate([c, jnp.zeros((SUBLANES - 1, D), F32)], axis=0), "gather_c")[:, 0, :]
    cc = jnp.concatenate([c_all, c_ctx[None], jnp.zeros((MOD_ROWS - N_DEV - 1, D), F32)], axis=0)
    s_cc = rowwise(f_silu, [cc], [], [(D, BF16)], "mod_silu", tb=MOD_ROWS)[0]
    ada_b = lax.dynamic_slice_in_dim(w_blk["b_ada"], chip * ada_n, ada_n)[None]
    mod_blk = rowwise(f_add_bias, [matmul(s_cc, ada_w, "nn", F32, "mod_proj")], [ada_b], [(ada_n, F32)], "mod_bias", tb=MOD_ROWS)[0]
    mod = allgather8(mod_blk, "gather_mod")[::2].transpose(1, 0, 2).reshape(MOD_ROWS, N_CHIP * ada_n)
    mod_x = lax.dynamic_slice_in_dim(mod, dev, 1, axis=0)
    mod_c = mod[MOD_CTX_ROW:MOD_CTX_ROW + 1]

    shard16 = {n: w_blk[n].astype(BF16) for n in BIG}
    stacks = run_exchange(gather_exchange([shard16[n] for n in EARLY]), "gather_early")
    wb = {n: _whole_from_shards(fill_own_slot(st, shard16[n]), n in BIG_BY_COLUMNS) for n, st in zip(EARLY, stacks)}

    sm_shapes = [w_blk[n].shape for n in SMALL_SHARDED]
    sm_all = allgather8(_pack([w_blk[n] for n in SMALL_SHARDED], F32, ROW_TILE_16BIT), "gather_small")[::2]
    ws = {n: w_blk[n] for n in WEIGHTS if n not in BIG and n != "w_ada"}
    for n, s in zip(SMALL_SHARDED, _unpack_stacked(sm_all, sm_shapes)):
        ws[n] = jnp.concatenate([s[k] for k in range(N_CHIP)], axis=-1)

    loss, grad_x, grads, scattered, lru_w_all, dmod_x, dmod_c = _local_step(x[0], ctx[0], loss_target[0], mod_x, mod_c, wb,
                                                                            {n: shard16[n] for n in LATE}, ws)
    loss = lax.psum(loss, ("x", "y", "c"))

    dm = allgather8(jnp.concatenate([dmod_x, dmod_c, jnp.zeros((SUBLANES - 2, 6 * D), F32)], axis=0), "gather_dmod")
    dm_ctx = sum_slots(dm, "sum_dmod")[1:2]
    dmod = jnp.concatenate([dm[:, 0, :], dm_ctx, jnp.zeros((MOD_ROWS - N_DEV - 1, 6 * D), F32)], axis=0)
    g_shard = {"b_ada": rowwise_vjp(f_add_bias, [dmod], [w_blk["b_ada"][None]], [dmod], [None], "mod_bias_bwd", tb=MOD_ROWS)[0].reshape(-1)}
    d_blk = lax.dynamic_slice_in_dim(dmod, chip * ada_n, ada_n, axis=1).astype(BF16)
    g_shard["w_ada"] = matmul(s_cc, d_blk, "tn", F32, "mod_proj_dw")
    d_silu_ctx = matmul(d_blk, ada_w, "nt", F32, "mod_proj_dx")[MOD_CTX_ROW]
    grads["c_ctx_silu"] = 0.5 * d_silu_ctx

    for n in BIG:
        landed, own = scattered[n]
        g_shard[n] = join_halves(sum_slots(landed, "sum_" + n, first=own), "join_" + n)

    small = [n for n in WEIGHTS if n not in BIG and n != "w_ada"]
    lru_w = sum_slots(lru_w_all, "sum_lru_w").reshape((4,) + w_blk["lru_w_a"].shape[1:])
    g_shard["lru_w_a"], g_shard["lru_w_x"] = lru_w[:2], lru_w[2:]
    reduced = [n for n in small if n not in ("b_ada", "c_ctx", "lru_w_a", "lru_w_x")] + ["c_ctx_silu"]
    sm_g = sum_slots(allgather8(_pack([grads[n] for n in reduced], F32, ROW_TILE_16BIT), "gather_small_grads"), "sum_small_grads")
    for n, g in zip(reduced, _unpack(sm_g, [grads[n].shape for n in reduced])):
        if n in SMALL_SHARDED:
            per = g.shape[-1] // N_CHIP
            g = lax.dynamic_slice_in_dim(g, chip * per, per, axis=g.ndim - 1)
        g_shard[n] = g
    ctx_rows = jnp.concatenate([c_ctx[None], jnp.zeros((SUBLANES - 1, D), F32)], axis=0)
    ctx_cot = jnp.concatenate([g_shard.pop("c_ctx_silu")[None], jnp.zeros((SUBLANES - 1, D), F32)], axis=0)
    g_shard["c_ctx"] = rowwise_vjp(f_silu, [ctx_rows], [], [ctx_cot], [F32], "mod_silu_bwd", tb=SUBLANES, sub=SUBLANES)[0][0]

    out_g, out_d, out_m, out_v = {}, {}, {}, {}
    for n in BIG + ["w_ada"]:
        out_g[n] = g_shard[n]
        out_d[n], out_m[n], out_v[n] = _adam(w_blk[n], g_shard[n], given["m_" + n][0], given["v_" + n][0], "adamw_" + n)
    sm_blk_shapes = [w_blk[n].shape for n in small]
    packed = [_pack([t[n].reshape(w_blk[n].shape) for n in small], F32, ADAM_ROWS)
              for t in (w_blk, g_shard, {n: given["m_" + n] for n in small}, {n: given["v_" + n] for n in small})]
    res = _adam(*packed, "adamw_small")
    for tgt, buf in zip((out_d, out_m, out_v), res):
        tgt.update(zip(small, _unpack(buf, sm_blk_shapes)))
    for n in small:
        out_g[n] = g_shard[n].reshape(w_blk[n].shape)

    def full(n, t):
        return t.reshape(given[n].shape)

    return (loss, grad_x[None], *[full(n, out_g[n]) for n in WEIGHTS], *[full(n, out_d[n]) for n in WEIGHTS],
            *[full(n, out_m[n]) for n in WEIGHTS], *[full(n, out_v[n]) for n in WEIGHTS])


def _unpack_stacked(buf, shapes):
    k = buf.shape[0]
    flat = buf.reshape(k, -1)
    out, at = [], 0
    for s in shapes:
        n = 1
        for d in s:
            n *= d
        out.append(flat[:, at:at + n].reshape((k,) + tuple(s)))
        at += n
    return out
```

```python
import functools

import jax
import jax.numpy as jnp
from jax import lax
from jax.experimental import pallas as pl
from jax.experimental.pallas import tpu as pltpu

F32 = jnp.float32
BF16 = jnp.bfloat16
MESH = pl.DeviceIdType.MESH

V7X_VMEM_LIMIT_BYTES = 56 * 1024 * 1024
LANES = 128
SUBLANES = 8

HEAD_DIM = 64
SSD_STATE = 128
SSD_GROUPS = 4
SSD_CHUNK = 128
GRID_W = 64
LRU_HEAD = 128
LRU_C = 8.0
EPS = 1e-6
NEG = -1e30

ADAM_LR = 0.001
ADAM_B1 = 0.9
ADAM_B2 = 0.999
ADAM_EPS = 1e-08
ADAM_WD = 0.01
ADAM_STEP = 10


def _cparams(sem=None, **kw):
    if sem is not None:
        kw["dimension_semantics"] = sem
    return pltpu.CompilerParams(vmem_limit_bytes=V7X_VMEM_LIMIT_BYTES, **kw)


_DN = {"nn": (((1,), (0,)), ((), ())), "nt": (((1,), (1,)), ((), ())), "tn": (((0,), (0,)), ((), ()))}


def _raw_dot(a, b, form, precision=None):
    return lax.dot_general(a, b, _DN[form], precision=precision, preferred_element_type=F32)


def _dot_bwd_forms(form):
    return {"nn": (("g", "b", "nt"), ("a", "g", "tn")),
            "nt": (("g", "b", "nn"), ("g", "a", "tn")),
            "tn": (("b", "g", "nt"), ("a", "g", "nn"))}[form]


def _make_dot(rounding):
    @functools.partial(jax.custom_vjp, nondiff_argnums=(2,))
    def dot(a, b, form):
        if rounding is None:
            return _raw_dot(a, b, form, precision=lax.Precision.HIGHEST)
        return _raw_dot(a.astype(rounding), b.astype(rounding), form)

    def fwd(a, b, form):
        return dot(a, b, form), (a, b)

    def bwd(form, res, g):
        a, b = res
        ops = {"a": a, "b": b, "g": g}
        (l1, r1, f1), (l2, r2, f2) = _dot_bwd_forms(form)
        return dot(ops[l1], ops[r1], f1).astype(a.dtype), dot(ops[l2], ops[r2], f2).astype(b.dtype)

    dot.defvjp(fwd, bwd)
    return dot


bdot = _make_dot(BF16)
hdot = _make_dot(None)


def _sigmoid(v):
    return 0.5 * jnp.tanh(0.5 * v) + 0.5


def _silu(v):
    return v * _sigmoid(v)


def _softplus(v):
    return jnp.maximum(v, 0.0) + jnp.log1p(jnp.exp(-jnp.abs(v)))


def _gelu_tanh(v):
    return 0.5 * v * (1.0 + jnp.tanh(0.7978845608028654 * (v + 0.044715 * v * v * v)))


def _rms(t, gain):
    return t * lax.rsqrt(jnp.mean(t * t, axis=-1, keepdims=True) + EPS) * gain


def _pick(dim, want, mult=1):
    if dim <= want:
        return dim
    for t in range(want - want % mult, 0, -mult):
        if dim % t == 0:
            return t
    t = min(dim, want)
    while dim % t:
        t //= 2
    return t


MM_TILE_M = 1024
MM_TILE_N = 1536
MM_VMEM_BYTES = 46 * 1024 * 1024


def matmul(a, b, form, out_dtype, name, tm=MM_TILE_M, tn=MM_TILE_N, tk=None, a_rows=None, carry=None):
    if form == "nn":
        (m, k), (k2, n) = a.shape, b.shape
    elif form == "nt":
        (m, k), (n, k2) = a.shape, b.shape
    else:
        (k, m), (k2, n) = a.shape, b.shape
    assert k == k2, (a.shape, b.shape, form)
    row0 = 0
    if a_rows is not None:
        assert form != "tn"
        row0, m = a_rows
    tm, tn = _pick(m, tm, LANES), _pick(n, tn, LANES)
    assert row0 % tm == 0
    blk0 = row0 // tm
    if tk is None:
        fixed = tm * tn * (2 * jnp.dtype(out_dtype).itemsize + 4)
        tk = (MM_VMEM_BYTES - fixed) // (2 * a.dtype.itemsize * (tm + tn))
    tk = _pick(k, tk, LANES)
    nk = k // tk

    grid = (m // tm, n // tn, nk)
    n_ci, n_co = (len(carry.ins), len(carry.outs)) if carry else (0, 0)

    def body(a_ref, b_ref, *rest):
        c_ins, o_ref, c_outs = rest[:n_ci], rest[n_ci], rest[n_ci + 1:n_ci + 1 + n_co]
        scratch = rest[n_ci + 1 + n_co:]
        kk = pl.program_id(2)
        if carry:
            at_step = [pl.program_id(d) for d in range(3)]
            sems = tuple(scratch[-2:])
            pl.when((at_step[0] == 0) & (at_step[1] == 0) & (at_step[2] == 0))(lambda: carry.start(c_ins, c_outs, sems))
        part = _raw_dot(a_ref[...], b_ref[...], form)
        if nk == 1:
            o_ref[...] = part.astype(o_ref.dtype)
        else:
            acc = scratch[0]

            @pl.when(kk == 0)
            def _():
                acc[...] = part

            @pl.when(kk > 0)
            def _():
                acc[...] += part

            @pl.when(kk == nk - 1)
            def _():
                o_ref[...] = acc[...].astype(o_ref.dtype)

        if carry:
            last = (at_step[0] == grid[0] - 1) & (at_step[1] == grid[1] - 1) & (at_step[2] == grid[2] - 1)
            pl.when(last)(lambda: carry.finish(c_ins, c_outs, sems))

    a_spec = pl.BlockSpec((tk, tm), lambda i, j, kk: (kk, i)) if form == "tn" else pl.BlockSpec((tm, tk), lambda i, j, kk: (blk0 + i, kk))
    b_spec = pl.BlockSpec((tn, tk), lambda i, j, kk: (j, kk)) if form == "nt" else pl.BlockSpec((tk, tn), lambda i, j, kk: (kk, j))
    res = pl.pallas_call(
        body, name=name, grid=grid, in_specs=[a_spec, b_spec] + [_HBM] * n_ci,
        out_specs=[pl.BlockSpec((tm, tn), lambda i, j, kk: (i, j))] + [_HBM] * n_co,
        out_shape=[jax.ShapeDtypeStruct((m, n), out_dtype)] + (carry.outs if carry else []),
        scratch_shapes=([pltpu.VMEM((tm, tn), F32)] if nk > 1 else []) + (carry.sem_shapes() if carry else []),
        compiler_params=_cparams(("arbitrary",) * 3 if carry else ("parallel", "parallel", "arbitrary")),
    )(a, b, *(carry.ins if carry else []))
    return (res[0], list(res[1:])) if carry else res[0]


class Cols:
    def __init__(self, arr, w, j):
        assert (j + 1) * w <= arr.shape[1]
        self.arr, self.w, self.j = arr, w, j


def _row_width(x, ncol):
    return x.w if isinstance(x, Cols) else x.shape[1] // ncol


def _row_spec(x, tb, ncol):
    if isinstance(x, Cols):
        j0 = x.j
        return x.arr, pl.BlockSpec((tb, x.w), lambda j, i: (i, j0 + j))
    return x, pl.BlockSpec((tb, x.shape[1] // ncol), lambda j, i: (i, j))


def _par_spec(shape, ncol):
    if ncol == 1:
        return pl.BlockSpec(shape, lambda j, i, nd=len(shape): (0,) * nd)
    assert len(shape) == 3 and shape[0] == ncol, shape
    return pl.BlockSpec((1,) + tuple(shape[1:]), lambda j, i: (j, 0, 0))


def _par_value(ref, ncol):
    return ref[...] if ncol == 1 else ref[0]


def rowwise(fn, rows, pars, outs, name, tb=256, sub=16, ncol=1):
    m = (rows[0].arr if isinstance(rows[0], Cols) else rows[0]).shape[0]
    tb = _pick(m, tb)
    sub = min(sub, tb)
    nr, npar = len(rows), len(pars)

    def body(*refs):
        row_refs, par_refs, out_refs = refs[:nr], refs[nr:nr + npar], refs[nr + npar:]
        pv = [_par_value(p, ncol) for p in par_refs]

        def step(i, carry):
            sl = pl.ds(pl.multiple_of(i * sub, sub), sub)
            res = fn(*[r[sl, :].astype(F32) for r in row_refs], *pv)
            for o, v in zip(out_refs, res):
                o[sl, :] = v.astype(o.dtype)
            return carry

        lax.fori_loop(0, tb // sub, step, 0)

    arrs, specs = zip(*[_row_spec(r, tb, ncol) for r in rows])
    return pl.pallas_call(
        body, name=name, grid=(ncol, m // tb),
        in_specs=list(specs) + [_par_spec(p.shape, ncol) for p in pars],
        out_specs=[pl.BlockSpec((tb, w), lambda j, i: (i, j)) for w, _ in outs],
        out_shape=[jax.ShapeDtypeStruct((m, w * ncol), dt) for w, dt in outs],
        compiler_params=_cparams(("parallel", "parallel")),
    )(*arrs, *pars)


def rowwise_vjp(fn, rows, pars, cots, drow_dtypes, name, tb=256, sub=16, out_sums=(), ncol=1):
    m = (rows[0].arr if isinstance(rows[0], Cols) else rows[0]).shape[0]
    tb = _pick(m, tb)
    sub = min(sub, tb)
    cots = [list(c) if isinstance(c, (list, tuple)) else [c] for c in cots]
    flat = [c for group in cots for c in group]
    nr, npar, nc = len(rows), len(pars), len(flat)
    want = [i for i, d in enumerate(drow_dtypes) if d is not None]
    assert ncol == 1 or not out_sums

    def body(*refs):
        row_refs, par_refs = refs[:nr], refs[nr:nr + npar]
        cot_refs = list(refs[nr + npar:nr + npar + nc])
        drow_refs = refs[nr + npar + nc:nr + npar + nc + len(want)]
        acc_refs = refs[nr + npar + nc + len(want):]
        pv = [_par_value(p, ncol) for p in par_refs]

        def step(i, acc):
            sl = pl.ds(pl.multiple_of(i * sub, sub), sub)
            rv = [r[sl, :].astype(F32) for r in row_refs]
            res, vjp = jax.vjp(lambda rr, pp: tuple(fn(*rr, *pp)), rv, pv)
            ct, at = [], 0
            for group in cots:
                ct.append(sum(c[sl, :].astype(F32) for c in cot_refs[at:at + len(group)]))
                at += len(group)
            d_rows, d_pars = vjp(tuple(ct))
            for o, idx in zip(drow_refs, want):
                o[sl, :] = d_rows[idx].astype(o.dtype)
            sums = [jnp.sum(res[k], axis=0, keepdims=True) for k, _ in out_sums]
            return tuple(a + d for a, d in zip(acc, list(d_pars) + sums))

        init = tuple(jnp.zeros(p.shape, F32) for p in pv) + tuple(jnp.zeros((1, w), F32) for _, w in out_sums)
        acc = lax.fori_loop(0, tb // sub, step, init)

        @pl.when(pl.program_id(1) == 0)
        def _():
            for o in acc_refs:
                o[...] = jnp.zeros_like(o)

        for o, a in zip(acc_refs, acc):
            if ncol == 1:
                o[...] += a
            else:
                o[0] += a

    arrs, specs = zip(*[_row_spec(r, tb, ncol) for r in rows])
    carrs, cspecs = zip(*[_row_spec(c, tb, ncol) for c in flat])
    widths = [_row_width(r, ncol) for r in rows]
    acc_shapes = [tuple(p.shape) for p in pars] + [(1, w) for _, w in out_sums]
    return pl.pallas_call(
        body, name=name, grid=(ncol, m // tb),
        in_specs=list(specs) + [_par_spec(p.shape, ncol) for p in pars] + list(cspecs),
        out_specs=[pl.BlockSpec((tb, widths[i]), lambda j, i_: (i_, j)) for i in want]
        + [_par_spec(s, ncol) for s in acc_shapes],
        out_shape=[jax.ShapeDtypeStruct((m, widths[i] * ncol), drow_dtypes[i]) for i in want]
        + [jax.ShapeDtypeStruct(s, F32) for s in acc_shapes],
        compiler_params=_cparams(("parallel", "arbitrary")),
    )(*arrs, *pars, *carrs)


def _col(v, c):
    lane = lax.broadcasted_iota(jnp.int32, v.shape, 1)
    return jnp.sum(jnp.where(lane == c, v, 0.0), axis=1, keepdims=True)


def _row(v, r):
    sub = lax.broadcasted_iota(jnp.int32, v.shape, 0)
    return jnp.sum(jnp.where(sub == r, v, 0.0), axis=0, keepdims=True)


def _ssd_chunk(xs, dt, bm, cm, hs, a_row, rev, col0):
    q = dt.shape[0]
    ii = lax.broadcasted_iota(jnp.int32, (q, q), 0)
    jj = lax.broadcasted_iota(jnp.int32, (q, q), 1)
    keep = (jj >= ii) if rev else (jj <= ii)
    tri = keep.astype(F32)
    dta = dt * a_row
    a_cum = hdot(tri, dta, "nn")
    tri_t = ((jj <= ii) if rev else (jj >= ii)).astype(F32)
    a_cum_t = hdot(dta, tri_t, "tn")
    cb = bdot(cm, bm, "nt")
    last = 0 if rev else q - 1
    ys, hn = [], []
    for e, (x, h) in enumerate(zip(xs, hs)):
        c = col0 + e
        ac = _col(a_cum, c)
        seg = ac - _row(a_cum_t, c)
        lm = jnp.exp(jnp.where(keep, seg, NEG))
        xdt = x * _col(dt, c)
        y_diag = bdot(cb * lm, xdt, "nn")
        y_off = bdot(cm * jnp.exp(ac), h, "nt")
        ys.append(y_diag + y_off)
        tot = _row(ac, last)
        states = bdot(xdt, bm * jnp.exp(tot - ac), "tn")
        hn.append(jnp.exp(tot) * h + states)
    return ys, hn


def _ssd_specs(nc, hpg, w_ssd, rev_order):
    q = SSD_CHUNK
    wx = hpg * HEAD_DIM
    boff = w_ssd // LANES

    def cidx(c):
        return nc - 1 - c if rev_order else c

    x_spec = pl.BlockSpec((q, wx), lambda g, c: (cidx(c), g))
    dt_spec = pl.BlockSpec((q, LANES), lambda g, c: (cidx(c), g))
    b_spec = pl.BlockSpec((q, LANES), lambda g, c: (cidx(c), boff + g))
    c_spec = pl.BlockSpec((q, LANES), lambda g, c: (cidx(c), boff + SSD_GROUPS + g))
    a_spec = pl.BlockSpec((1, SUBLANES, LANES), lambda g, c: (g, 0, 0))
    st_spec = pl.BlockSpec((1, wx, LANES), lambda g, c: (g, 0, 0))
    ent_spec = pl.BlockSpec((1, 1, wx, LANES), lambda g, c: (g, cidx(c), 0, 0))
    return x_spec, dt_spec, b_spec, c_spec, a_spec, st_spec, ent_spec


def ssd_fwd(xbc, dtg, a_g, h0, rev, w_ssd, name):
    L = xbc.shape[0]
    nc = L // SSD_CHUNK
    hpg = w_ssd // (SSD_GROUPS * HEAD_DIM)
    wx = hpg * HEAD_DIM
    col0 = hpg if rev else 0
    x_spec, dt_spec, b_spec, c_spec, a_spec, st_spec, ent_spec = _ssd_specs(nc, hpg, w_ssd, rev)

    def body(x_ref, dt_ref, b_ref, c_ref, a_ref, h0_ref, y_ref, ent_ref, fin_ref, hs):
        c = pl.program_id(1)

        @pl.when(c == 0)
        def _():
            hs[...] = h0_ref[0]

        ent_ref[0, 0] = hs[...]
        xs = [x_ref[:, e * HEAD_DIM:(e + 1) * HEAD_DIM].astype(F32) for e in range(hpg)]
        hin = [hs[e * HEAD_DIM:(e + 1) * HEAD_DIM, :] for e in range(hpg)]
        ys, hn = _ssd_chunk(xs, dt_ref[...], b_ref[...].astype(F32), c_ref[...].astype(F32), hin, a_ref[0, 0:1, :], rev, col0)
        for e in range(hpg):
            y_ref[:, e * HEAD_DIM:(e + 1) * HEAD_DIM] = ys[e].astype(y_ref.dtype)
            hs[e * HEAD_DIM:(e + 1) * HEAD_DIM, :] = hn[e]
        fin_ref[0] = hs[...]

    return pl.pallas_call(
        body, name=name, grid=(SSD_GROUPS, nc),
        in_specs=[x_spec, dt_spec, b_spec, c_spec, a_spec, st_spec],
        out_specs=[x_spec, ent_spec, st_spec],
        out_shape=[jax.ShapeDtypeStruct((L, w_ssd), BF16),
                   jax.ShapeDtypeStruct((SSD_GROUPS, nc, wx, LANES), F32),
                   jax.ShapeDtypeStruct((SSD_GROUPS, wx, LANES), F32)],
        scratch_shapes=[pltpu.VMEM((wx, LANES), F32)],
        compiler_params=_cparams(("parallel", "arbitrary")),
    )(xbc, dtg, xbc, xbc, a_g, h0)


def ssd_bwd(xbc, dtg, a_g, ent, dy, dfin, rev, w_ssd, name, carry=None):
    L = xbc.shape[0]
    nc = L // SSD_CHUNK
    hpg = w_ssd // (SSD_GROUPS * HEAD_DIM)
    wx = hpg * HEAD_DIM
    col0 = hpg if rev else 0
    x_spec, dt_spec, b_spec, c_spec, a_spec, st_spec, ent_spec = _ssd_specs(nc, hpg, w_ssd, not rev)
    bc_out = pl.BlockSpec((SSD_CHUNK, LANES), lambda g, c: ((c if rev else nc - 1 - c), g))
    n_ci, n_co = (len(carry.ins), len(carry.outs)) if carry else (0, 0)

    def body(x_ref, dt_ref, b_ref, c_ref, a_ref, ent_ref, dy_ref, dfin_ref, *rest):
        c_ins = rest[:n_ci]
        dx_ref, ddt_ref, db_ref, dc_ref, da_ref, dh0_ref = rest[n_ci:n_ci + 6]
        c_outs = rest[n_ci + 6:n_ci + 6 + n_co]
        dhs = rest[n_ci + 6 + n_co]
        sems = tuple(rest[n_ci + 7 + n_co:])
        g, c = pl.program_id(0), pl.program_id(1)
        if carry:
            pl.when((g == 0) & (c == 0))(lambda: carry.start(c_ins, c_outs, sems))

        @pl.when(c == 0)
        def _():
            dhs[...] = dfin_ref[0]
            da_ref[...] = jnp.zeros_like(da_ref)

        xs = [x_ref[:, e * HEAD_DIM:(e + 1) * HEAD_DIM].astype(F32) for e in range(hpg)]
        hin = [ent_ref[0, 0, e * HEAD_DIM:(e + 1) * HEAD_DIM, :] for e in range(hpg)]
        _, vjp = jax.vjp(lambda *a: _ssd_chunk(*a, rev, col0), xs, dt_ref[...], b_ref[...].astype(F32), c_ref[...].astype(F32), hin,
                         a_ref[0, 0:1, :])
        dys = [dy_ref[:, e * HEAD_DIM:(e + 1) * HEAD_DIM].astype(F32) for e in range(hpg)]
        dhn = [dhs[e * HEAD_DIM:(e + 1) * HEAD_DIM, :] for e in range(hpg)]
        dxs, ddt, db, dc, dh, da = vjp((dys, dhn))
        for e in range(hpg):
            dx_ref[:, e * HEAD_DIM:(e + 1) * HEAD_DIM] = dxs[e].astype(dx_ref.dtype)
            dhs[e * HEAD_DIM:(e + 1) * HEAD_DIM, :] = dh[e]
        ddt_ref[...] = ddt
        db_ref[...] = db
        dc_ref[...] = dc
        da_ref[0, 0:1, :] += da
        dh0_ref[0] = dhs[...]
        if carry:
            pl.when((g == SSD_GROUPS - 1) & (c == nc - 1))(lambda: carry.finish(c_ins, c_outs, sems))

    res = pl.pallas_call(
        body, name=name, grid=(SSD_GROUPS, nc),
        in_specs=[x_spec, dt_spec, b_spec, c_spec, a_spec, ent_spec, x_spec, st_spec] + [_HBM] * n_ci,
        out_specs=[x_spec, dt_spec, bc_out, bc_out, a_spec, st_spec] + [_HBM] * n_co,
        out_shape=[jax.ShapeDtypeStruct((L, w_ssd), BF16),
                   jax.ShapeDtypeStruct((L, SSD_GROUPS * LANES), F32),
                   jax.ShapeDtypeStruct((L, SSD_GROUPS * LANES), F32),
                   jax.ShapeDtypeStruct((L, SSD_GROUPS * LANES), F32),
                   jax.ShapeDtypeStruct((SSD_GROUPS, SUBLANES, LANES), F32),
                   jax.ShapeDtypeStruct((SSD_GROUPS, wx, LANES), F32)] + (carry.outs if carry else []),
        scratch_shapes=[pltpu.VMEM((wx, LANES), F32)] + (carry.sem_shapes() if carry else []),
        compiler_params=_cparams(("arbitrary", "arbitrary") if carry else ("parallel", "arbitrary")),
    )(xbc, dtg, xbc, xbc, a_g, ent, dy, dfin, *(carry.ins if carry else []))
    return (list(res[:6]), list(res[6:])) if carry else res


CONV_W = 4
CONV_LEFT = 2
CONV_CB = 128
CONV_ROWS = 256
HALO = 16


def _conv_window(ref, r, rows, nch, L):
    s = pl.multiple_of(r * rows, rows)
    cur = ref[pl.ds(s, rows), :].astype(F32)
    sp = pl.multiple_of(jnp.maximum(s - HALO, 0), HALO)
    sn = pl.multiple_of(jnp.minimum(s + rows, L - HALO), HALO)
    prev = jnp.where(r > 0, ref[pl.ds(sp, HALO), :].astype(F32), 0.0)
    nxt = jnp.where(r < nch - 1, ref[pl.ds(sn, HALO), :].astype(F32), 0.0)
    return jnp.concatenate([prev, cur, nxt], axis=0)


def _shifted(win, off, rows):
    n = win.shape[0]
    return pltpu.roll(win, (-off) % n, axis=0)[HALO:HALO + rows, :]


def _dsilu(p):
    s = _sigmoid(p)
    return s * (1.0 + p * (1.0 - s))


def conv_fwd(src, col_blk0, w, b, act, name, out_dtype=F32):
    L, C = src.shape[0], w.shape[1]
    rows = min(CONV_ROWS, L)
    nch = L // rows

    def body(x_ref, w_ref, b_ref, o_ref):
        def chunk(r, carry):
            win = _conv_window(x_ref, r, rows, nch, L)
            pre = b_ref[...] + sum(w_ref[k:k + 1, :] * _shifted(win, k - CONV_LEFT, rows) for k in range(CONV_W))
            o_ref[pl.ds(pl.multiple_of(r * rows, rows), rows), :] = (_silu(pre) if act else pre).astype(o_ref.dtype)
            return carry

        lax.fori_loop(0, nch, chunk, 0)

    return pl.pallas_call(
        body, name=name, grid=(C // CONV_CB,),
        in_specs=[pl.BlockSpec((L, CONV_CB), lambda j: (0, col_blk0 + j)),
                  pl.BlockSpec((CONV_W, CONV_CB), lambda j: (0, j)),
                  pl.BlockSpec((1, CONV_CB), lambda j: (0, j))],
        out_specs=pl.BlockSpec((L, CONV_CB), lambda j: (0, j)),
        out_shape=jax.ShapeDtypeStruct((L, C), out_dtype),
        compiler_params=_cparams(("parallel",)),
    )(src, w, b)


def conv_bwd(src, col_blk0, w, b, douts, act, name):
    L, C = src.shape[0], w.shape[1]
    rows = min(CONV_ROWS, L)
    nch = L // rows
    nd = len(douts)

    def body(*refs):
        x_ref, w_ref, b_ref = refs[:3]
        d_refs = refs[3:3 + nd]
        dx_ref, dw_ref, db_ref, dp = refs[3 + nd:]

        def pass1(r, acc):
            sl = pl.ds(pl.multiple_of(r * rows, rows), rows)
            win = _conv_window(x_ref, r, rows, nch, L)
            taps = [_shifted(win, k - CONV_LEFT, rows) for k in range(CONV_W)]
            dpre = sum(d[sl, :].astype(F32) for d in d_refs)
            if act:
                pre = b_ref[...] + sum(w_ref[k:k + 1, :] * taps[k] for k in range(CONV_W))
                dpre = dpre * _dsilu(pre)
            dp[sl, :] = dpre
            new = [acc[k] + jnp.sum(dpre * taps[k], axis=0, keepdims=True) for k in range(CONV_W)]
            return tuple(new) + (acc[CONV_W] + jnp.sum(dpre, axis=0, keepdims=True),)

        zero = jnp.zeros((1, CONV_CB), F32)
        acc = lax.fori_loop(0, nch, pass1, (zero,) * (CONV_W + 1))
        for k in range(CONV_W):
            dw_ref[k:k + 1, :] = acc[k]
        db_ref[...] = acc[CONV_W]

        def pass2(r, carry):
            win = _conv_window(dp, r, rows, nch, L)
            dx = sum(w_ref[k:k + 1, :] * _shifted(win, CONV_LEFT - k, rows) for k in range(CONV_W))
            dx_ref[pl.ds(pl.multiple_of(r * rows, rows), rows), :] = dx.astype(dx_ref.dtype)
            return carry

        lax.fori_loop(0, nch, pass2, 0)

    col = pl.BlockSpec((L, CONV_CB), lambda j: (0, j))
    return pl.pallas_call(
        body, name=name, grid=(C // CONV_CB,),
        in_specs=[pl.BlockSpec((L, CONV_CB), lambda j: (0, col_blk0 + j)),
                  pl.BlockSpec((CONV_W, CONV_CB), lambda j: (0, j)),
                  pl.BlockSpec((1, CONV_CB), lambda j: (0, j))] + [col] * nd,
        out_specs=[col, pl.BlockSpec((CONV_W, CONV_CB), lambda j: (0, j)), pl.BlockSpec((1, CONV_CB), lambda j: (0, j))],
        out_shape=[jax.ShapeDtypeStruct((L, C), BF16), jax.ShapeDtypeStruct((CONV_W, C), F32), jax.ShapeDtypeStruct((1, C), F32)],
        scratch_shapes=[pltpu.VMEM((L, CONV_CB), F32)],
        compiler_params=_cparams(("parallel",)),
    )(src, w, b, *douts)


SCAN_ROWS = 512
SCAN_CB = 512


def _scan8(a, u, rev):
    sub = lax.broadcasted_iota(jnp.int32, a.shape, 0)
    for s in (1, 2, 4):
        if rev:
            ok = sub < SUBLANES - s
            a_sh, u_sh = pltpu.roll(a, SUBLANES - s, axis=0), pltpu.roll(u, SUBLANES - s, axis=0)
        else:
            ok = sub >= s
            a_sh, u_sh = pltpu.roll(a, s, axis=0), pltpu.roll(u, s, axis=0)
        u = a * jnp.where(ok, u_sh, 0.0) + u
        a = a * jnp.where(ok, a_sh, 1.0)
    return a, u


def _shift8(h, carry, rev):
    sub = lax.broadcasted_iota(jnp.int32, h.shape, 0)
    if rev:
        return jnp.where(sub == SUBLANES - 1, carry, pltpu.roll(h, SUBLANES - 1, axis=0))
    return jnp.where(sub == 0, carry, pltpu.roll(h, 1, axis=0))


def lru_scan(a, u, c0, rev, name, adjoint_of=None, add=None):
    L, C = a.shape
    tc, cb = _pick(L, SCAN_ROWS), _pick(C, SCAN_CB)
    nt = L // tc
    ng = tc // SUBLANES
    adj = adjoint_of is not None
    n_in = 3 + (1 if adj else 0) + (1 if add is not None else 0)

    def body(*refs):
        a_ref, u_ref, c0_ref = refs[:3]
        x_ref = refs[3] if (adj or add is not None) else None
        outs = refs[n_in:-1]
        st = refs[-1]

        @pl.when(pl.program_id(1) == 0)
        def _():
            st[...] = c0_ref[...]

        def group(i, carry):
            g = (ng - 1 - i) if rev else i
            sl = pl.ds(pl.multiple_of(g * SUBLANES, SUBLANES), SUBLANES)
            av, uv = a_ref[sl, :], u_ref[sl, :]
            pa, h = _scan8(av, av * uv if adj else uv, rev)
            h = pa * carry + h
            hs = _shift8(h, carry, rev)
            if adj:
                lam = uv + hs
                outs[0][sl, :] = lam
                outs[1][sl, :] = lam * x_ref[sl, :]
            else:
                outs[0][sl, :] = h
                outs[1][sl, :] = hs
                if add is not None:
                    outs[3][sl, :] = h + x_ref[sl, :]
            last = 0 if rev else SUBLANES - 1
            return jnp.broadcast_to(h[last:last + 1, :], h.shape)

        st[...] = lax.fori_loop(0, ng, group, st[...])
        outs[2][...] = st[...]

    def tmap(j, t):
        return ((nt - 1 - t) if rev else t, j)

    blk = pl.BlockSpec((tc, cb), tmap)
    vec = pl.BlockSpec((SUBLANES, cb), lambda j, t: (0, j))
    args = [a, u, c0] + ([adjoint_of] if adj else []) + ([add] if add is not None else [])
    n_big = 2 if (adj or add is None) else 3
    out_specs = [blk, blk, vec] + ([blk] if n_big == 3 else [])
    out_shape = [jax.ShapeDtypeStruct((L, C), F32), jax.ShapeDtypeStruct((L, C), F32), jax.ShapeDtypeStruct((SUBLANES, C), F32)]
    out_shape += [jax.ShapeDtypeStruct((L, C), F32)] if n_big == 3 else []
    return pl.pallas_call(
        body, name=name, grid=(C // cb, nt),
        in_specs=[blk, blk, vec] + [blk] * (n_in - 3),
        out_specs=out_specs, out_shape=out_shape,
        scratch_shapes=[pltpu.VMEM((SUBLANES, cb), F32)],
        compiler_params=_cparams(("parallel", "arbitrary")),
    )(*args)


def f_silu(v):
    return (_silu(v),)


def f_add_bias(v, b):
    return (v + b,)


def f_mod_in(x, gain, shift, scale):
    return _rms(x, gain) * (1.0 + scale) + shift, x


def f_dt(raw, bias):
    return (_softplus(raw + bias),)


def f_gnorm(yf, yb, xs, z, dexp, gain):
    return (_rms((yf + yb + dexp * xs) * _silu(z), gain),)


def f_gelu_gate(r, yr):
    return (r * _gelu_tanh(yr),)


def f_merge(gs, gr, o_s, o_r, bs, br):
    return (_sigmoid(gs + bs) * o_s + _sigmoid(gr + br) * o_r,)


def f_res_mod(x, out, gm, gain, shift, scale):
    x1 = x + gm * out
    return x1, _rms(x1, gain) * (1.0 + scale) + shift


def f_final(x1, f, t, gf, gain):
    e = _rms(x1 + gf * f, gain) - t
    return (jnp.broadcast_to(0.5 * jnp.mean(e * e, axis=-1, keepdims=True), (e.shape[0], LANES)),)


GATE_ROWS = 1024
GATE_SUB_ROWS = 256


def _lru_coeffs(xh, wa, ba, wx, bx, lam):
    r = _sigmoid(bdot(xh, wa, "nn") + ba)
    i = _sigmoid(bdot(xh, wx, "nn") + bx)
    log_a = -LRU_C * r * _softplus(-lam)
    a = jnp.exp(log_a)
    one_minus_a2 = -jnp.tanh(log_a) * (a * a + 1.0)
    return a, jnp.sqrt(one_minus_a2) * (i * xh)


def f_gates(xh, wa_f, ba_f, wx_f, bx_f, lam_f, wa_b, ba_b, wx_b, bx_b, lam_b):
    return _lru_coeffs(xh, wa_f, ba_f, wx_f, bx_f, lam_f) + _lru_coeffs(xh, wa_b, ba_b, wx_b, bx_b, lam_b)


def f_adam(w, g, m, v):
    m2 = ADAM_B1 * m + (1.0 - ADAM_B1) * g
    v2 = ADAM_B2 * v + (1.0 - ADAM_B2) * (g * g)
    m_hat = m2 / (1.0 - ADAM_B1 ** ADAM_STEP)
    v_hat = v2 / (1.0 - ADAM_B2 ** ADAM_STEP)
    return -ADAM_LR * (m_hat / (jnp.sqrt(v_hat) + ADAM_EPS) + ADAM_WD * w), m2, v2


def swiglu(gu, dact, name, tb=128, sub=16):
    L, f2 = gu.shape
    f = f2 // 2
    tb = _pick(L, tb)

    def body(*refs):
        gu_ref, o_ref = refs[0], refs[-1]

        def step(i, carry):
            sl = pl.ds(pl.multiple_of(i * sub, sub), sub)
            g, u = gu_ref[sl, :f].astype(F32), gu_ref[sl, f:].astype(F32)
            if dact is None:
                o_ref[sl, :] = (_silu(g) * u).astype(o_ref.dtype)
            else:
                d = refs[1][sl, :].astype(F32)
                o_ref[sl, :f] = (d * u * _dsilu(g)).astype(o_ref.dtype)
                o_ref[sl, f:] = (d * _silu(g)).astype(o_ref.dtype)
            return carry

        lax.fori_loop(0, tb // sub, step, 0)

    wout = f if dact is None else f2
    ins = [gu] + ([] if dact is None else [dact])
    return pl.pallas_call(
        body, name=name, grid=(L // tb,),
        in_specs=[pl.BlockSpec((tb, a.shape[1]), lambda i: (i, 0)) for a in ins],
        out_specs=pl.BlockSpec((tb, wout), lambda i: (i, 0)),
        out_shape=jax.ShapeDtypeStruct((L, wout), BF16),
        compiler_params=_cparams(("parallel",)),
    )(*ins)


N_DEV = 8
N_CHIP = 4
PACK_W = 1024
_HBM = pl.BlockSpec(memory_space=pltpu.HBM)


def _me():
    return lax.axis_index("x"), lax.axis_index("y"), lax.axis_index("c")


def _flip(pos, k):
    x, y, c = pos
    return (1 - x if k & 4 else x, 1 - y if k & 2 else y, 1 - c if k & 1 else c)


def _dev_index(pos):
    return 4 * pos[0] + 2 * pos[1] + pos[2]


def _chip_index(pos):
    return 2 * pos[0] + pos[1]


def _rcopy(src, dst, sems, k, to):
    send_sems, recv_sems = sems
    return pltpu.make_async_remote_copy(src_ref=src, dst_ref=dst, send_sem=send_sems.at[k], recv_sem=recv_sems.at[k],
                                        device_id=to, device_id_type=MESH)


def fill_own_device_slot(stack, mine):
    return lax.dynamic_update_slice(stack, mine[None], (_dev_index(_me()),) + (0,) * mine.ndim)


COPY_PIECES = 4
ROW_TILE_16BIT = 16


def _pieces(rows, n=COPY_PIECES, align=ROW_TILE_16BIT):
    while n > 1 and rows % (n * align):
        n -= 1
    return [(q * (rows // n), rows // n) for q in range(n)]


class Exchange:
    def __init__(self, ins, outs, n_sems, start, finish):
        self.ins, self.outs, self.n_sems, self.start, self.finish = list(ins), list(outs), n_sems, start, finish

    def sem_shapes(self):
        return [pltpu.SemaphoreType.DMA((self.n_sems,)), pltpu.SemaphoreType.DMA((self.n_sems,))]


def run_exchange(ex, name):
    n_in, n_out = len(ex.ins), len(ex.outs)

    def body(*refs):
        ins, outs, sems = refs[:n_in], refs[n_in:n_in + n_out], tuple(refs[n_in + n_out:])
        ex.start(ins, outs, sems)
        ex.finish(ins, outs, sems)

    return pl.pallas_call(body, name=name, in_specs=[_HBM] * n_in, out_specs=[_HBM] * n_out, out_shape=ex.outs,
                          scratch_shapes=ex.sem_shapes())(*ex.ins)


def allgather_exchange(v):
    def plan(ins, outs):
        me = _me()
        for k in range(1, N_DEV):
            peer = _flip(me, k)
            yield outs[0].at[_dev_index(me)], outs[0].at[_dev_index(peer)], k - 1, peer

    def start(ins, outs, sems):
        for mine, _, k, peer in plan(ins, outs):
            _rcopy(ins[0], mine, sems, k, peer).start()

    def finish(ins, outs, sems):
        for _, theirs, k, peer in plan(ins, outs):
            _rcopy(theirs, theirs, sems, k, peer).wait_recv()
        for mine, _, k, peer in plan(ins, outs):
            _rcopy(ins[0], mine, sems, k, peer).wait_send()

    return Exchange([v], [jax.ShapeDtypeStruct((N_DEV,) + v.shape, v.dtype)], N_DEV - 1, start, finish)


def allgather8(v, name):
    return fill_own_device_slot(run_exchange(allgather_exchange(v), name)[0], v)


def gather_exchange(shards):
    cuts = [_pieces(s.shape[0] // 2) for s in shards]
    base = [0]
    for cu in cuts:
        base.append(base[-1] + 6 * len(cu))

    def plan(ins, outs):
        me = _me()
        chips = [_flip(me, 4), _flip(me, 2), _flip(me, 6)]
        for w, cu in enumerate(cuts):
            half, nq = shards[w].shape[0] // 2, len(cu)
            for q, (s, n) in enumerate(cu):
                mine = pl.ds(pl.multiple_of(me[2] * half + s, ROW_TILE_16BIT), n)
                sibs = pl.ds(pl.multiple_of((1 - me[2]) * half + s, ROW_TILE_16BIT), n)
                for j, p in enumerate(chips):
                    yield w, p, mine, sibs, base[w] + j * nq + q, base[w] + (3 + j) * nq + q

    def start(ins, outs, sems):
        me = _me()
        for w, p, mine, _, k_ici, _ in plan(ins, outs):
            _rcopy(ins[w].at[mine], outs[w].at[_chip_index(me), mine], sems, k_ici, p).start()

    def finish(ins, outs, sems):
        me = _me()
        sibling = _flip(me, 1)
        for w, p, mine, _, k_ici, k_fwd in plan(ins, outs):
            got = outs[w].at[_chip_index(p), mine]
            _rcopy(got, got, sems, k_ici, p).wait_recv()
            _rcopy(got, got, sems, k_fwd, sibling).start()
        for w, p, _, sibs, _, k_fwd in plan(ins, outs):
            got = outs[w].at[_chip_index(p), sibs]
            _rcopy(got, got, sems, k_fwd, sibling).wait_recv()
        for w, p, mine, _, k_ici, k_fwd in plan(ins, outs):
            got = outs[w].at[_chip_index(p), mine]
            _rcopy(ins[w].at[mine], got, sems, k_ici, p).wait_send()
            _rcopy(got, got, sems, k_fwd, sibling).wait_send()

    outs = [jax.ShapeDtypeStruct((N_CHIP,) + s.shape, s.dtype) for s in shards]
    return Exchange(shards, outs, base[-1], start, finish)


def fill_own_slot(stack, mine):
    return lax.dynamic_update_slice(stack, mine[None], (_chip_index(_me()), 0, 0))


def scatter_exchange(parts):
    def plan(ins, outs):
        me = _me()
        for w in range(len(parts)):
            half = parts[w].shape[1] // 2
            for k in range(1, N_DEV):
                to = _flip(me, k)
                src = ins[w].at[_chip_index(to), pl.ds(pl.multiple_of(to[2] * half, ROW_TILE_16BIT), half)]
                yield src, outs[w].at[k - 1], (N_DEV - 1) * w + k - 1, to

    def start(ins, outs, sems):
        for src, dst, k, to in plan(ins, outs):
            _rcopy(src, dst, sems, k, to).start()

    def finish(ins, outs, sems):
        for src, dst, k, to in plan(ins, outs):
            _rcopy(dst, dst, sems, k, to).wait_recv()
        for src, dst, k, to in plan(ins, outs):
            _rcopy(src, dst, sems, k, to).wait_send()

    outs = [jax.ShapeDtypeStruct((N_DEV - 1, p.shape[1] // 2, p.shape[2]), p.dtype) for p in parts]
    return Exchange(parts, outs, (N_DEV - 1) * len(parts), start, finish)


def own_piece(parts):
    me = _me()
    half = parts.shape[1] // 2
    return lax.dynamic_slice(parts, (_chip_index(me), me[2] * half, 0), (1, half, parts.shape[2]))[0]


def join_halves(mine, name):
    half, C = mine.shape
    cuts = _pieces(half, 2 * COPY_PIECES, SUBLANES)
    nq = len(cuts)

    def body(m_ref, out_ref, send_sems, recv_sems):
        me = _me()
        sems = (send_sems, recv_sems)
        sibling = _flip(me, 1)
        sends = [_rcopy(m_ref.at[pl.ds(s, n)], out_ref.at[pl.ds(s, n)], sems, q, sibling) for q, (s, n) in enumerate(cuts)]
        for cp in sends:
            cp.start()
        for q, (s, n) in enumerate(cuts):
            got = out_ref.at[pl.ds(s, n)]
            _rcopy(got, got, sems, q, sibling).wait_recv()
        for cp in sends:
            cp.wait_send()

    other = pl.pallas_call(
        body, name=name, in_specs=[_HBM], out_specs=_HBM,
        out_shape=jax.ShapeDtypeStruct((half, C), mine.dtype),
        scratch_shapes=[pltpu.SemaphoreType.DMA((nq,)), pltpu.SemaphoreType.DMA((nq,))],
    )(mine)
    south = _me()[2] == 0
    return jnp.concatenate([jnp.where(south, mine, other), jnp.where(south, other, mine)], axis=0)


SUM_BLOCK_ELEMS = 256 * 1024


def sum_slots(a, name, first=None):
    n, R, C = a.shape
    tb = _pick(R, max(ROW_TILE_16BIT, SUM_BLOCK_ELEMS // C), ROW_TILE_16BIT)

    def body(a_ref, *rest):
        o_ref = rest[-1]
        acc = a_ref[0].astype(F32) if first is None else rest[0][...].astype(F32) + a_ref[0].astype(F32)
        for j in range(1, n):
            acc = acc + a_ref[j].astype(F32)
        o_ref[...] = acc

    row = pl.BlockSpec((tb, C), lambda i: (i, 0))
    return pl.pallas_call(
        body, name=name, grid=(R // tb,),
        in_specs=[pl.BlockSpec((n, tb, C), lambda i: (0, i, 0))] + ([] if first is None else [row]),
        out_specs=row,
        out_shape=jax.ShapeDtypeStruct((R, C), F32),
        compiler_params=_cparams(("parallel",)),
    )(*([a] if first is None else [a, first]))


def _pack(arrays, dtype, row_align):
    flat = jnp.concatenate([a.astype(dtype).reshape(-1) for a in arrays])
    per = PACK_W * row_align
    n = -(-flat.shape[0] // per) * per
    return jnp.pad(flat, (0, n - flat.shape[0])).reshape(-1, PACK_W)


def _unpack(buf, shapes):
    flat = buf.reshape(-1)
    out, at = [], 0
    for s in shapes:
        n = 1
        for d in s:
            n *= d
        out.append(flat[at:at + n].reshape(s))
        at += n
    return out


def _to_col_major(t, rows):
    L, C = t.shape
    return t.reshape(rows, GRID_W, C).transpose(1, 0, 2).reshape(L, C)


def _to_row_major(t, rows):
    L, C = t.shape
    return t.reshape(GRID_W, rows, C).transpose(1, 0, 2).reshape(L, C)


def _heads_to_groups(v, hpg):
    lead = v.shape[:-1]
    t = jnp.moveaxis(v.reshape(lead + (2, SSD_GROUPS, hpg)), -3, -2).reshape(lead + (SSD_GROUPS, 2 * hpg))
    t = jnp.pad(t, [(0, 0)] * (len(lead) + 1) + [(0, LANES - 2 * hpg)])
    return t.reshape(lead + (SSD_GROUPS * LANES,))


def _groups_to_heads(t, hpg):
    lead = t.shape[:-1]
    t = t.reshape(lead + (SSD_GROUPS, LANES))[..., :2 * hpg].reshape(lead + (SSD_GROUPS, 2, hpg))
    return jnp.moveaxis(t, -2, -3).reshape(lead + (2 * SSD_GROUPS * hpg,))


def _r1(v):
    return v.reshape(1, -1)


class _Layout:
    def __init__(self, D, W, hpg):
        assert W == D and D % (SSD_GROUPS * LANES) == 0
        self.D, self.W, self.hpg = D, W, hpg
        self.GN = SSD_GROUPS * SSD_STATE
        self.xbc0 = 5 * D
        self.dt0 = 5 * D + W + 2 * self.GN
        self.width = self.dt0 + SSD_GROUPS * LANES
        self.xbc_blk = self.xbc0 // CONV_CB
        self.dt_blk = self.dt0 // (SSD_GROUPS * LANES)
        assert self.dt0 % (SSD_GROUPS * LANES) == 0


def _seq_forward(hb, w_all, sp, lay, tag, grid_rows, init, carry=None):
    D, W = lay.D, lay.W
    gw = SSD_GROUPS * LANES
    pg = matmul(hb, w_all, "nn", BF16, f"proj_{tag}", carry=carry)
    pg, carried = pg if carry else (pg, [])
    xbc = conv_fwd(pg, lay.xbc_blk, sp["ssd_conv_w"], sp["ssd_conv_b"], True, f"ssd_conv_{tag}", out_dtype=BF16)
    dtg = rowwise(f_dt, [Cols(pg, gw, lay.dt_blk)], [sp["dt_bias_g"]], [(gw, F32)], f"ssd_dt_{tag}")[0]
    yf, ent_f, fin_f = ssd_fwd(xbc, dtg, sp["a_g"], init["ssd_f"], False, W, f"ssd_scan_f_{tag}")
    yb, ent_b, fin_b = ssd_fwd(xbc, dtg, sp["a_g"], init["ssd_b"], True, W, f"ssd_scan_b_{tag}")
    xr = pg[:, D:2 * D]
    if grid_rows:
        xr = _to_col_major(xr, grid_rows)
    xc = conv_fwd(xr, 0, sp["lru_conv_w"], sp["lru_conv_b"], False, f"lru_conv_{tag}")
    a_f, u_f, a_b, u_b = rowwise(f_gates, [xc], sp["gate_pars"], [(LRU_HEAD, F32)] * 4, f"lru_gates_{tag}",
                                 tb=GATE_ROWS, sub=GATE_SUB_ROWS, ncol=D // LRU_HEAD)
    h_f, hs_f, fl_f = lru_scan(a_f, u_f, init["lru_f"], False, f"lru_scan_f_{tag}")
    _, hs_b, fl_b, r = lru_scan(a_b, u_b, init["lru_b"], True, f"lru_scan_b_{tag}", add=h_f)
    saved = dict(pg=pg, xbc=xbc, dtg=dtg, yf=yf, yb=yb, ent_f=ent_f, ent_b=ent_b, xr=xr, xc=xc,
                 a_f=a_f, a_b=a_b, hs_f=hs_f, hs_b=hs_b, r=r)
    return saved, dict(ssd_f=fin_f, ssd_b=fin_b, lru_f=fl_f, lru_b=fl_b), carried


def _seq_backward(sv, sp, lay, tag, grid_rows, dy, dxs_extra, d_r, dfin, carries=(None, None)):
    D, W, GN = lay.D, lay.W, lay.GN
    gw = SSD_GROUPS * LANES
    pg = sv["pg"]
    res_f = ssd_bwd(sv["xbc"], sv["dtg"], sp["a_g"], sv["ent_f"], dy, dfin["ssd_f"], False, W, f"ssd_scan_f_bwd_{tag}", carry=carries[0])
    res_b = ssd_bwd(sv["xbc"], sv["dtg"], sp["a_g"], sv["ent_b"], dy, dfin["ssd_b"], True, W, f"ssd_scan_b_bwd_{tag}", carry=carries[1])
    (dxf, ddt_f, db_f, dc_f, da_f, dh0_f), carried_f = res_f if carries[0] else (res_f, [])
    (dxb, ddt_b, db_b, dc_b, da_b, dh0_b), carried_b = res_b if carries[1] else (res_b, [])
    cw, cb = sp["ssd_conv_w"], sp["ssd_conv_b"]
    d_xs, dw1, db1 = conv_bwd(pg, lay.xbc_blk, cw[:, :W], cb[:, :W], [dxf, dxb] + dxs_extra, True, f"ssd_conv_x_bwd_{tag}")
    d_b, dw2, db2 = conv_bwd(pg, lay.xbc_blk + W // CONV_CB, cw[:, W:W + GN], cb[:, W:W + GN], [db_f, db_b], True, f"ssd_conv_b_bwd_{tag}")
    d_c, dw3, db3 = conv_bwd(pg, lay.xbc_blk + (W + GN) // CONV_CB, cw[:, W + GN:], cb[:, W + GN:], [dc_f, dc_b], True, f"ssd_conv_c_bwd_{tag}")
    d_dtraw, d_dtbias = rowwise_vjp(f_dt, [Cols(pg, gw, lay.dt_blk)], [sp["dt_bias_g"]], [[ddt_f, ddt_b]], [BF16], f"ssd_dt_bwd_{tag}")
    du_b, dab, dl0_b = lru_scan(sv["a_b"], d_r, dfin["lru_b"], False, f"lru_scan_b_bwd_{tag}", adjoint_of=sv["hs_b"])
    du_f, daf, dl0_f = lru_scan(sv["a_f"], d_r, dfin["lru_f"], True, f"lru_scan_f_bwd_{tag}", adjoint_of=sv["hs_f"])
    res = rowwise_vjp(f_gates, [sv["xc"]], sp["gate_pars"], [daf, du_f, dab, du_b], [F32], f"lru_gates_bwd_{tag}",
                      tb=GATE_ROWS, sub=GATE_SUB_ROWS, ncol=D // LRU_HEAD)
    d_xc, gate_grads = res[0], res[1:]
    d_xr, g_lcw, g_lcb = conv_bwd(sv["xr"], 0, sp["lru_conv_w"], sp["lru_conv_b"], [d_xc], False, f"lru_conv_bwd_{tag}")
    if grid_rows:
        d_xr = _to_row_major(d_xr, grid_rows)
    grads = dict(ssd_conv_w=jnp.concatenate([dw1, dw2, dw3], axis=1), ssd_conv_b=jnp.concatenate([db1, db2, db3], axis=1),
                 dt_bias_g=d_dtbias, a_g=da_f + da_b, lru_conv_w=g_lcw, lru_conv_b=g_lcb, gate_pars=list(gate_grads))
    pieces = dict(xr=d_xr, xs=d_xs, b=d_b, c=d_c, dt=d_dtraw)
    return pieces, grads, dict(ssd_f=dh0_f, ssd_b=dh0_b, lru_f=dl0_f, lru_b=dl0_b), (carried_f, carried_b)


def _proj_cotangent(lay, px, pc, Lc, dz, dyr, dgs, dgr):
    zero = jnp.zeros((Lc, lay.D), BF16)
    cols = [(dz, zero), (px["xr"], pc["xr"]), (dyr, zero), (dgs, zero), (dgr, zero),
            (px["xs"], pc["xs"]), (px["b"], pc["b"]), (px["c"], pc["c"]), (px["dt"], pc["dt"])]
    return jnp.concatenate([jnp.concatenate([a.astype(BF16), b.astype(BF16)], axis=0) for a, b in cols], axis=1)


EARLY = ["w_in", "w_gate"]
LATE = ["w_out_ssd", "w_out_lru", "w_o", "ffn_w13", "ffn_w2"]


def _local_step(x, ctx, target, mod_x, mod_c, wb, late, ws):
    L, D = x.shape
    Lc = ctx.shape[0]
    nh = ws["ssd_d"].shape[0]
    hpg = nh // SSD_GROUPS
    W = nh * HEAD_DIM
    lay = _Layout(D, W, hpg)
    GN = lay.GN
    grid_rows = L // GRID_W
    nlh = D // LRU_HEAD

    w_in = wb["w_in"]
    o_dt, o_xr = 2 * W + 2 * GN, 2 * W + 2 * GN + 2 * nh
    w_all = jnp.concatenate([w_in[:, :W], w_in[:, o_xr:o_xr + D], w_in[:, o_xr + D:], wb["w_gate"], w_in[:, W:o_dt],
                             _heads_to_groups(w_in[:, o_dt:o_xr], hpg)], axis=1)
    a_neg = -jnp.exp(ws["ssd_a_log"])
    a_g = jnp.pad(_heads_to_groups(a_neg.reshape(-1), hpg).reshape(SSD_GROUPS, 1, LANES), [(0, 0), (0, SUBLANES - 1), (0, 0)])
    gate_pars = []
    for d in range(2):
        gate_pars += [ws["lru_w_a"][d], ws["lru_b_a"][d].reshape(nlh, 1, LRU_HEAD), ws["lru_w_x"][d],
                      ws["lru_b_x"][d].reshape(nlh, 1, LRU_HEAD), ws["lru_lambda"][d].reshape(nlh, 1, LRU_HEAD)]
    sp = dict(ssd_conv_w=ws["ssd_conv_w"], ssd_conv_b=_r1(ws["ssd_conv_b"]), dt_bias_g=_r1(_heads_to_groups(ws["ssd_dt_bias"].reshape(-1), hpg)),
              a_g=a_g, lru_conv_w=ws["lru_conv_w"], lru_conv_b=_r1(ws["lru_conv_b"]), gate_pars=gate_pars)
    dexp = _r1(jnp.repeat(ws["ssd_d"], HEAD_DIM))
    norm_mix, norm_ffn, ssd_norm, final_norm = _r1(ws["norm_mix"]), _r1(ws["norm_ffn"]), _r1(ws["ssd_norm"]), _r1(ws["final_norm"])
    bg_s, bg_r = _r1(ws["b_gate"][:D]), _r1(ws["b_gate"][D:])
    sh_m, sc_m, g_m, sh_f, sc_f, g_f = [mod_x[:, k * D:(k + 1) * D] for k in range(6)]
    csh_m, csc_m = mod_c[:, :D], mod_c[:, D:2 * D]

    zst = dict(ssd_f=jnp.zeros((SSD_GROUPS, hpg * HEAD_DIM, SSD_STATE), F32), ssd_b=jnp.zeros((SSD_GROUPS, hpg * HEAD_DIM, SSD_STATE), F32),
               lru_f=jnp.zeros((SUBLANES, D), F32), lru_b=jnp.zeros((SUBLANES, D), F32))
    hcb = rowwise(f_mod_in, [ctx], [norm_mix, csh_m, csc_m], [(D, BF16)], "norm_mix_ctx")[0]
    hb = rowwise(f_mod_in, [x], [norm_mix, sh_m, sc_m], [(D, BF16)], "norm_mix_x")[0]
    svc, fin_c, _ = _seq_forward(hcb, w_all, sp, lay, "ctx", None, zst)
    svx, _, stacks = _seq_forward(hb, w_all, sp, lay, "x", grid_rows, fin_c, carry=gather_exchange([late[n] for n in LATE]))
    wb = dict(wb)
    for n, st in zip(LATE, stacks):
        wb[n] = _whole_from_shards(fill_own_slot(st, late[n]), n in BIG_BY_COLUMNS)
    pg = svx["pg"]
    r_rm = _to_row_major(svx["r"], grid_rows)
    gn_rows = [svx["yf"], svx["yb"], Cols(svx["xbc"], W, 0), Cols(pg, D, 0)]
    yn = rowwise(f_gnorm, gn_rows, [dexp, ssd_norm], [(W, BF16)], "ssd_gnorm")[0]
    o_s = matmul(yn, wb["w_out_ssd"], "nn", BF16, "out_ssd")
    o_in = rowwise(f_gelu_gate, [r_rm, Cols(pg, D, 2)], [], [(D, BF16)], "lru_gelu")[0]
    o_r = matmul(o_in, wb["w_out_lru"], "nn", BF16, "out_lru")
    mg_rows = [Cols(pg, D, 3), Cols(pg, D, 4), o_s, o_r]
    mixed = rowwise(f_merge, mg_rows, [bg_s, bg_r], [(D, BF16)], "merge")[0]
    out = matmul(mixed, wb["w_o"], "nn", BF16, "out_proj")
    x1, h2 = rowwise(f_res_mod, [x, out], [g_m, norm_ffn, sh_f, sc_f], [(D, F32), (D, BF16)], "res_norm_ffn")
    gu = matmul(h2, wb["ffn_w13"], "nn", BF16, "ffn_in")
    act = swiglu(gu, None, "ffn_act")
    f = matmul(act, wb["ffn_w2"], "nn", BF16, "ffn_out")

    ones = jnp.full((L, LANES), 1.0 / LANES, F32)
    dx1a, df, d_gf, d_fnorm, lsum = rowwise_vjp(f_final, [x1, f, target], [g_f, final_norm], [ones], [F32, BF16, None],
                                                "final_loss", out_sums=[(0, LANES)])
    loss = lsum[0, 0]
    d_act = matmul(df, wb["ffn_w2"], "nt", BF16, "ffn_out_dx")
    g_w2 = matmul(act, df, "tn", BF16, "ffn_out_dw")
    d_gu = swiglu(gu, d_act, "ffn_act_bwd")
    dh2 = matmul(d_gu, wb["ffn_w13"], "nt", BF16, "ffn_in_dx")
    g_w13 = matmul(h2, d_gu, "tn", BF16, "ffn_in_dw")
    dxa, d_out, d_gm, d_nffn, d_shf, d_scf = rowwise_vjp(f_res_mod, [x, out], [g_m, norm_ffn, sh_f, sc_f], [dx1a, dh2],
                                                         [F32, BF16], "res_norm_ffn_bwd")
    d_mixed = matmul(d_out, wb["w_o"], "nt", BF16, "out_proj_dx")
    g_wo = matmul(mixed, d_out, "tn", BF16, "out_proj_dw")
    dgs, dgr, do_s, do_r, d_bgs, d_bgr = rowwise_vjp(f_merge, mg_rows, [bg_s, bg_r], [d_mixed], [BF16] * 4, "merge_bwd")
    d_yn = matmul(do_s, wb["w_out_ssd"], "nt", BF16, "out_ssd_dx")
    g_wos = matmul(yn, do_s, "tn", BF16, "out_ssd_dw")
    d_oin = matmul(do_r, wb["w_out_lru"], "nt", BF16, "out_lru_dx")
    g_wol = matmul(o_in, do_r, "tn", BF16, "out_lru_dw")
    d_r_rm, d_yr = rowwise_vjp(f_gelu_gate, [r_rm, Cols(pg, D, 2)], [], [d_oin], [F32, BF16], "lru_gelu_bwd")
    dy, dxs_skip, dz, d_dexp, d_ssdn = rowwise_vjp(f_gnorm, gn_rows, [dexp, ssd_norm], [d_yn], [BF16, None, BF16, BF16], "ssd_gnorm_bwd")
    zfin = dict(zst)
    parts = {n: _shards_from_whole(g, n in BIG_BY_COLUMNS).astype(BF16)
             for n, g in (("ffn_w2", g_w2), ("ffn_w13", g_w13), ("w_o", g_wo), ("w_out_ssd", g_wos), ("w_out_lru", g_wol))}
    ride_f, ride_b = ["ffn_w2", "ffn_w13"], ["w_o", "w_out_ssd", "w_out_lru"]
    px, gx, dst, (land_f, land_b) = _seq_backward(
        svx, sp, lay, "x", grid_rows, dy, [dxs_skip], _to_col_major(d_r_rm, grid_rows), zfin,
        carries=(scatter_exchange([parts[n] for n in ride_f]), scatter_exchange([parts[n] for n in ride_b])))
    pc, gc, _, _ = _seq_backward(svc, sp, lay, "ctx", None, jnp.zeros((Lc, W), F32), [], jnp.zeros((Lc, D), F32), dst)
    dpg = _proj_cotangent(lay, px, pc, Lc, dz, d_yr, dgs, dgr)
    gg = [a + b for a, b in zip(gx["gate_pars"], gc["gate_pars"])]
    lru_w = jnp.concatenate([gg[0], gg[5], gg[2], gg[7]], axis=0).reshape(-1, PACK_W)
    g_wall, (lru_w_all,) = matmul(jnp.concatenate([hb, hcb], axis=0), dpg, "tn", BF16, "proj_dw", carry=allgather_exchange(lru_w))
    lru_w_all = fill_own_device_slot(lru_w_all, lru_w)
    xb0 = lay.xbc0
    g_w_in = jnp.concatenate([g_wall[:, :W], g_wall[:, xb0:xb0 + W + 2 * GN], _groups_to_heads(g_wall[:, lay.dt0:], hpg),
                              g_wall[:, D:2 * D], g_wall[:, 2 * D:3 * D]], axis=1)
    ride_p = ["w_in", "w_gate"]
    parts.update({n: _shards_from_whole(g, True).astype(BF16) for n, g in (("w_in", g_w_in), ("w_gate", g_wall[:, 3 * D:5 * D]))})
    dh, land_p = matmul(dpg, w_all, "nt", BF16, "proj_dx_x", a_rows=(0, L), carry=scatter_exchange([parts[n] for n in ride_p]))
    scattered = {n: (ld, own_piece(parts[n])) for n, ld in zip(ride_f + ride_b + ride_p, land_f + land_b + land_p)}
    dhc = matmul(dpg, w_all, "nt", BF16, "proj_dx_ctx", a_rows=(L, Lc))
    grad_x, d_nmix_x, d_shm, d_scm = rowwise_vjp(f_mod_in, [x], [norm_mix, sh_m, sc_m], [dh, dxa], [F32], "norm_mix_x_bwd")
    d_nmix_c, d_cshm, d_cscm = rowwise_vjp(f_mod_in, [ctx], [norm_mix, csh_m, csc_m], [dhc, jnp.zeros((Lc, D), F32)], [None], "norm_mix_ctx_bwd")

    dmod_x = jnp.concatenate([d_shm, d_scm, d_gm, d_shf, d_scf, d_gf], axis=1)
    dmod_c = jnp.concatenate([d_cshm, d_cscm, jnp.zeros((1, 4 * D), F32)], axis=1)

    d_a = _groups_to_heads((gx["a_g"] + gc["a_g"])[:, 0, :].reshape(-1), hpg).reshape(2, nh)

    def gate(k, shape):
        return jnp.stack([gg[k].reshape(shape), gg[5 + k].reshape(shape)], axis=0)

    grads = dict(
        norm_mix=(d_nmix_x + d_nmix_c).reshape(-1), norm_ffn=d_nffn.reshape(-1),
        ssd_conv_w=gx["ssd_conv_w"] + gc["ssd_conv_w"], ssd_conv_b=(gx["ssd_conv_b"] + gc["ssd_conv_b"]).reshape(-1),
        ssd_dt_bias=_groups_to_heads((gx["dt_bias_g"] + gc["dt_bias_g"]).reshape(-1), hpg).reshape(2, nh),
        ssd_a_log=d_a * a_neg, ssd_d=d_dexp.reshape(nh, HEAD_DIM).sum(axis=1), ssd_norm=d_ssdn.reshape(-1),
        lru_conv_w=gx["lru_conv_w"] + gc["lru_conv_w"], lru_conv_b=(gx["lru_conv_b"] + gc["lru_conv_b"]).reshape(-1),
        lru_b_a=gate(1, (D,)), lru_b_x=gate(3, (D,)), lru_lambda=gate(4, (D,)),
        b_gate=jnp.concatenate([d_bgs, d_bgr], axis=1).reshape(-1), final_norm=d_fnorm.reshape(-1))
    return loss, grad_x, grads, scattered, lru_w_all, dmod_x, dmod_c


WEIGHTS = ["c_ctx", "w_ada", "b_ada", "norm_mix", "norm_ffn", "w_in", "ssd_conv_w", "ssd_conv_b", "ssd_dt_bias", "ssd_a_log",
           "ssd_d", "ssd_norm", "w_out_ssd", "lru_conv_w", "lru_conv_b", "lru_w_a", "lru_b_a", "lru_w_x", "lru_b_x", "lru_lambda",
           "w_out_lru", "w_gate", "b_gate", "w_o", "ffn_w13", "ffn_w2", "final_norm"]
BIG = ["w_in", "w_out_ssd", "w_out_lru", "w_gate", "w_o", "ffn_w13", "ffn_w2"]
BIG_BY_COLUMNS = {"w_in", "w_gate", "ffn_w13"}
MOD_ROWS = 16
MOD_CTX_ROW = N_DEV
SMALL_SHARDED = ["ssd_conv_w", "lru_conv_w", "lru_b_a", "lru_b_x", "lru_lambda"]
ADAM_ROWS = 64


def _whole_from_shards(stack, by_columns):
    if by_columns:
        return stack.transpose(1, 0, 2).reshape(stack.shape[1], -1)
    return stack.reshape(-1, stack.shape[2])


def _shards_from_whole(g, by_columns):
    if by_columns:
        return g.reshape(g.shape[0], N_CHIP, -1).transpose(1, 0, 2)
    return g.reshape(N_CHIP, -1, g.shape[1])


def _adam(w, g, m, v, name):
    shape = w.shape
    cols = shape[-1]
    w2, g2, m2, v2 = [t.reshape(-1, cols) for t in (w, g, m, v)]
    d, nm, nv = rowwise(f_adam, [w2, g2, m2, v2], [], [(cols, F32)] * 3, name, tb=ADAM_ROWS, sub=8)
    return d.reshape(shape), nm.reshape(shape), nv.reshape(shape)


def kernel(x, c, ctx, c_ctx, w_ada, b_ada, norm_mix, norm_ffn, w_in, ssd_conv_w, ssd_conv_b, ssd_dt_bias, ssd_a_log, ssd_d, ssd_norm, w_out_ssd, lru_conv_w, lru_conv_b, lru_w_a, lru_b_a, lru_w_x, lru_b_x, lru_lambda, w_out_lru, w_gate, b_gate, w_o, ffn_w13, ffn_w2, final_norm, loss_target, m_c_ctx, m_w_ada, m_b_ada, m_norm_mix, m_norm_ffn, m_w_in, m_ssd_conv_w, m_ssd_conv_b, m_ssd_dt_bias, m_ssd_a_log, m_ssd_d, m_ssd_norm, m_w_out_ssd, m_lru_conv_w, m_lru_conv_b, m_lru_w_a, m_lru_b_a, m_lru_w_x, m_lru_b_x, m_lru_lambda, m_w_out_lru, m_w_gate, m_b_gate, m_w_o, m_ffn_w13, m_ffn_w2, m_final_norm, v_c_ctx, v_w_ada, v_b_ada, v_norm_mix, v_norm_ffn, v_w_in, v_ssd_conv_w, v_ssd_conv_b, v_ssd_dt_bias, v_ssd_a_log, v_ssd_d, v_ssd_norm, v_w_out_ssd, v_lru_conv_w, v_lru_conv_b, v_lru_w_a, v_lru_b_a, v_lru_w_x, v_lru_b_x, v_lru_lambda, v_w_out_lru, v_w_gate, v_b_gate, v_w_o, v_ffn_w13, v_ffn_w2, v_final_norm):
    given = dict(locals())
    layered = {n for n in WEIGHTS if n not in ("c_ctx", "final_norm")}
    w_blk = {n: (given[n][0] if n in layered else given[n]) for n in WEIGHTS}
    me = _me()
    chip, dev = _chip_index(me), _dev_index(me)
    D = c.shape[1]

    ada_w = w_blk["w_ada"].astype(BF16)
    ada_n = ada_w.shape[1]
    c_all = allgather8(jnp.concatenate([c, jnp.zeros((SUBLANES - 1, D), F32)], axis=0), "gather_c")[:, 0, :]
    cc = jnp.concatenate([c_all, c_ctx[None], jnp.zeros((MOD_ROWS - N_DEV - 1, D), F32)], axis=0)
    s_cc = rowwise(f_silu, [cc], [], [(D, BF16)], "mod_silu", tb=MOD_ROWS)[0]
    ada_b = lax.dynamic_slice_in_dim(w_blk["b_ada"], chip * ada_n, ada_n)[None]
    mod_blk = rowwise(f_add_bias, [matmul(s_cc, ada_w, "nn", F32, "mod_proj")], [ada_b], [(ada_n, F32)], "mod_bias", tb=MOD_ROWS)[0]
    mod = allgather8(mod_blk, "gather_mod")[::2].transpose(1, 0, 2).reshape(MOD_ROWS, N_CHIP * ada_n)
    mod_x = lax.dynamic_slice_in_dim(mod, dev, 1, axis=0)
    mod_c = mod[MOD_CTX_ROW:MOD_CTX_ROW + 1]

    shard16 = {n: w_blk[n].astype(BF16) for n in BIG}
    stacks = run_exchange(gather_exchange([shard16[n] for n in EARLY]), "gather_early")
    wb = {n: _whole_from_shards(fill_own_slot(st, shard16[n]), n in BIG_BY_COLUMNS) for n, st in zip(EARLY, stacks)}

    sm_shapes = [w_blk[n].shape for n in SMALL_SHARDED]
    sm_all = allgather8(_pack([w_blk[n] for n in SMALL_SHARDED], F32, ROW_TILE_16BIT), "gather_small")[::2]
    ws = {n: w_blk[n] for n in WEIGHTS if n not in BIG and n != "w_ada"}
    for n, s in zip(SMALL_SHARDED, _unpack_stacked(sm_all, sm_shapes)):
        ws[n] = jnp.concatenate([s[k] for k in range(N_CHIP)], axis=-1)

    loss, grad_x, grads, scattered, lru_w_all, dmod_x, dmod_c = _local_step(x[0], ctx[0], loss_target[0], mod_x, mod_c, wb,
                                                                            {n: shard16[n] for n in LATE}, ws)
    loss = lax.psum(loss, ("x", "y", "c"))

    dm = allgather8(jnp.concatenate([dmod_x, dmod_c, jnp.zeros((SUBLANES - 2, 6 * D), F32)], axis=0), "gather_dmod")
    dm_ctx = sum_slots(dm, "sum_dmod")[1:2]
    dmod = jnp.concatenate([dm[:, 0, :], dm_ctx, jnp.zeros((MOD_ROWS - N_DEV - 1, 6 * D), F32)], axis=0)
    g_shard = {"b_ada": rowwise_vjp(f_add_bias, [dmod], [w_blk["b_ada"][None]], [dmod], [None], "mod_bias_bwd", tb=MOD_ROWS)[0].reshape(-1)}
    d_blk = lax.dynamic_slice_in_dim(dmod, chip * ada_n, ada_n, axis=1).astype(BF16)
    g_shard["w_ada"] = matmul(s_cc, d_blk, "tn", F32, "mod_proj_dw")
    d_silu_ctx = matmul(d_blk, ada_w, "nt", F32, "mod_proj_dx")[MOD_CTX_ROW]
    grads["c_ctx_silu"] = 0.5 * d_silu_ctx

    for n in BIG:
        landed, own = scattered[n]
        g_shard[n] = join_halves(sum_slots(landed, "sum_" + n, first=own), "join_" + n)

    small = [n for n in WEIGHTS if n not in BIG and n != "w_ada"]
    lru_w = sum_slots(lru_w_all, "sum_lru_w").reshape((4,) + w_blk["lru_w_a"].shape[1:])
    g_shard["lru_w_a"], g_shard["lru_w_x"] = lru_w[:2], lru_w[2:]
    reduced = [n for n in small if n not in ("b_ada", "c_ctx", "lru_w_a", "lru_w_x")] + ["c_ctx_silu"]
    sm_g = sum_slots(allgather8(_pack([grads[n] for n in reduced], F32, ROW_TILE_16BIT), "gather_small_grads"), "sum_small_grads")
    for n, g in zip(reduced, _unpack(sm_g, [grads[n].shape for n in reduced])):
        if n in SMALL_SHARDED:
            per = g.shape[-1] // N_CHIP
            g = lax.dynamic_slice_in_dim(g, chip * per, per, axis=g.ndim - 1)
        g_shard[n] = g
    ctx_rows = jnp.concatenate([c_ctx[None], jnp.zeros((SUBLANES - 1, D), F32)], axis=0)
    ctx_cot = jnp.concatenate([g_shard.pop("c_ctx_silu")[None], jnp.zeros((SUBLANES - 1, D), F32)], axis=0)
    g_shard["c_ctx"] = rowwise_vjp(f_silu, [ctx_rows], [], [ctx_cot], [F32], "mod_silu_bwd", tb=SUBLANES, sub=SUBLANES)[0][0]

    out_g, out_d, out_m, out_v = {}, {}, {}, {}
    for n in BIG + ["w_ada"]:
        out_g[n] = g_shard[n]
        out_d[n], out_m[n], out_v[n] = _adam(w_blk[n], g_shard[n], given["m_" + n][0], given["v_" + n][0], "adamw_" + n)
    sm_blk_shapes = [w_blk[n].shape for n in small]
    packed = [_pack([t[n].reshape(w_blk[n].shape) for n in small], F32, ADAM_ROWS)
              for t in (w_blk, g_shard, {n: given["m_" + n] for n in small}, {n: given["v_" + n] for n in small})]
    res = _adam(*packed, "adamw_small")
    for tgt, buf in zip((out_d, out_m, out_v), res):
        tgt.update(zip(small, _unpack(buf, sm_blk_shapes)))
    for n in small:
        out_g[n] = g_shard[n].reshape(w_blk[n].shape)

    def full(n, t):
        return t.reshape(given[n].shape)

    return (loss, grad_x[None], *[full(n, out_g[n]) for n in WEIGHTS], *[full(n, out_d[n]) for n in WEIGHTS],
            *[full(n, out_m[n]) for n in WEIGHTS], *[full(n, out_v[n]) for n in WEIGHTS])


def _unpack_stacked(buf, shapes):
    k = buf.shape[0]
    flat = buf.reshape(k, -1)
    out, at = [], 0
    for s in shapes:
        n = 1
        for d in s:
            n *= d
        out.append(flat[:, at:at + n].reshape((k,) + tuple(s)))
        at += n
    return out
```

```python
import functools

import jax
import jax.numpy as jnp
from jax import lax
from jax.experimental import pallas as pl
from jax.experimental.pallas import tpu as pltpu

F32 = jnp.float32
BF16 = jnp.bfloat16
MESH = pl.DeviceIdType.MESH

V7X_VMEM_LIMIT_BYTES = 56 * 1024 * 1024
LANES = 128
SUBLANES = 8

HEAD_DIM = 64
SSD_STATE = 128
SSD_GROUPS = 4
SSD_CHUNK = 128
GRID_W = 64
LRU_HEAD = 128
LRU_C = 8.0
EPS = 1e-6
NEG = -1e30

ADAM_LR = 0.001
ADAM_B1 = 0.9
ADAM_B2 = 0.999
ADAM_EPS = 1e-08
ADAM_WD = 0.01
ADAM_STEP = 10


def _cparams(sem=None, **kw):
    if sem is not None:
        kw["dimension_semantics"] = sem
    return pltpu.CompilerParams(vmem_limit_bytes=V7X_VMEM_LIMIT_BYTES, **kw)


_DN = {"nn": (((1,), (0,)), ((), ())), "nt": (((1,), (1,)), ((), ())), "tn": (((0,), (0,)), ((), ()))}


def _raw_dot(a, b, form, precision=None):
    return lax.dot_general(a, b, _DN[form], precision=precision, preferred_element_type=F32)


def _dot_bwd_forms(form):
    return {"nn": (("g", "b", "nt"), ("a", "g", "tn")),
            "nt": (("g", "b", "nn"), ("g", "a", "tn")),
            "tn": (("b", "g", "nt"), ("a", "g", "nn"))}[form]


def _make_dot(rounding):
    @functools.partial(jax.custom_vjp, nondiff_argnums=(2,))
    def dot(a, b, form):
        if rounding is None:
            return _raw_dot(a, b, form, precision=lax.Precision.HIGHEST)
        return _raw_dot(a.astype(rounding), b.astype(rounding), form)

    def fwd(a, b, form):
        return dot(a, b, form), (a, b)

    def bwd(form, res, g):
        a, b = res
        ops = {"a": a, "b": b, "g": g}
        (l1, r1, f1), (l2, r2, f2) = _dot_bwd_forms(form)
        return dot(ops[l1], ops[r1], f1).astype(a.dtype), dot(ops[l2], ops[r2], f2).astype(b.dtype)

    dot.defvjp(fwd, bwd)
    return dot


bdot = _make_dot(BF16)
hdot = _make_dot(None)


def _sigmoid(v):
    return 0.5 * jnp.tanh(0.5 * v) + 0.5


def _silu(v):
    return v * _sigmoid(v)


def _softplus(v):
    return jnp.maximum(v, 0.0) + jnp.log1p(jnp.exp(-jnp.abs(v)))


def _gelu_tanh(v):
    return 0.5 * v * (1.0 + jnp.tanh(0.7978845608028654 * (v + 0.044715 * v * v * v)))


def _rms(t, gain):
    return t * lax.rsqrt(jnp.mean(t * t, axis=-1, keepdims=True) + EPS) * gain


def _pick(dim, want, mult=1):
    if dim <= want:
        return dim
    for t in range(want - want % mult, 0, -mult):
        if dim % t == 0:
            return t
    t = min(dim, want)
    while dim % t:
        t //= 2
    return t


MM_TILE_M = 1024
MM_TILE_N = 1536
MM_VMEM_BYTES = 46 * 1024 * 1024


def matmul(a, b, form, out_dtype, name, tm=MM_TILE_M, tn=MM_TILE_N, tk=None, a_rows=None, carry=None):
    if form == "nn":
        (m, k), (k2, n) = a.shape, b.shape
    elif form == "nt":
        (m, k), (n, k2) = a.shape, b.shape
    else:
        (k, m), (k2, n) = a.shape, b.shape
    assert k == k2, (a.shape, b.shape, form)
    row0 = 0
    if a_rows is not None:
        assert form != "tn"
        row0, m = a_rows
    tm, tn = _pick(m, tm, LANES), _pick(n, tn, LANES)
    assert row0 % tm == 0
    blk0 = row0 // tm
    if tk is None:
        fixed = tm * tn * (2 * jnp.dtype(out_dtype).itemsize + 4)
        tk = (MM_VMEM_BYTES - fixed) // (2 * a.dtype.itemsize * (tm + tn))
    tk = _pick(k, tk, LANES)
    nk = k // tk

    grid = (m // tm, n // tn, nk)
    n_ci, n_co = (len(carry.ins), len(carry.outs)) if carry else (0, 0)

    def body(a_ref, b_ref, *rest):
        c_ins, o_ref, c_outs = rest[:n_ci], rest[n_ci], rest[n_ci + 1:n_ci + 1 + n_co]
        scratch = rest[n_ci + 1 + n_co:]
        kk = pl.program_id(2)
        if carry:
            at_step = [pl.program_id(d) for d in range(3)]
            sems = tuple(scratch[-2:])
            pl.when((at_step[0] == 0) & (at_step[1] == 0) & (at_step[2] == 0))(lambda: carry.start(c_ins, c_outs, sems))
        part = _raw_dot(a_ref[...], b_ref[...], form)
        if nk == 1:
            o_ref[...] = part.astype(o_ref.dtype)
        else:
            acc = scratch[0]

            @pl.when(kk == 0)
            def _():
                acc[...] = part

            @pl.when(kk > 0)
            def _():
                acc[...] += part

            @pl.when(kk == nk - 1)
            def _():
                o_ref[...] = acc[...].astype(o_ref.dtype)

        if carry:
            last = (at_step[0] == grid[0] - 1) & (at_step[1] == grid[1] - 1) & (at_step[2] == grid[2] - 1)
            pl.when(last)(lambda: carry.finish(c_ins, c_outs, sems))

    a_spec = pl.BlockSpec((tk, tm), lambda i, j, kk: (kk, i)) if form == "tn" else pl.BlockSpec((tm, tk), lambda i, j, kk: (blk0 + i, kk))
    b_spec = pl.BlockSpec((tn, tk), lambda i, j, kk: (j, kk)) if form == "nt" else pl.BlockSpec((tk, tn), lambda i, j, kk: (kk, j))
    res = pl.pallas_call(
        body, name=name, grid=grid, in_specs=[a_spec, b_spec] + [_HBM] * n_ci,
        out_specs=[pl.BlockSpec((tm, tn), lambda i, j, kk: (i, j))] + [_HBM] * n_co,
        out_shape=[jax.ShapeDtypeStruct((m, n), out_dtype)] + (carry.outs if carry else []),
        scratch_shapes=([pltpu.VMEM((tm, tn), F32)] if nk > 1 else []) + (carry.sem_shapes() if carry else []),
        compiler_params=_cparams(("arbitrary",) * 3 if carry else ("parallel", "parallel", "arbitrary")),
    )(a, b, *(carry.ins if carry else []))
    return (res[0], list(res[1:])) if carry else res[0]


class Cols:
    def __init__(self, arr, w, j):
        assert (j + 1) * w <= arr.shape[1]
        self.arr, self.w, self.j = arr, w, j


def _row_width(x, ncol):
    return x.w if isinstance(x, Cols) else x.shape[1] // ncol


def _row_spec(x, tb, ncol):
    if isinstance(x, Cols):
        j0 = x.j
        return x.arr, pl.BlockSpec((tb, x.w), lambda j, i: (i, j0 + j))
    return x, pl.BlockSpec((tb, x.shape[1] // ncol), lambda j, i: (i, j))


def _par_spec(shape, ncol):
    if ncol == 1:
        return pl.BlockSpec(shape, lambda j, i, nd=len(shape): (0,) * nd)
    assert len(shape) == 3 and shape[0] == ncol, shape
    return pl.BlockSpec((1,) + tuple(shape[1:]), lambda j, i: (j, 0, 0))


def _par_value(ref, ncol):
    return ref[...] if ncol == 1 else ref[0]


def rowwise(fn, rows, pars, outs, name, tb=256, sub=16, ncol=1):
    m = (rows[0].arr if isinstance(rows[0], Cols) else rows[0]).shape[0]
    tb = _pick(m, tb)
    sub = min(sub, tb)
    nr, npar = len(rows), len(pars)

    def body(*refs):
        row_refs, par_refs, out_refs = refs[:nr], refs[nr:nr + npar], refs[nr + npar:]
        pv = [_par_value(p, ncol) for p in par_refs]

        def step(i, carry):
            sl = pl.ds(pl.multiple_of(i * sub, sub), sub)
            res = fn(*[r[sl, :].astype(F32) for r in row_refs], *pv)
            for o, v in zip(out_refs, res):
                o[sl, :] = v.astype(o.dtype)
            return carry

        lax.fori_loop(0, tb // sub, step, 0)

    arrs, specs = zip(*[_row_spec(r, tb, ncol) for r in rows])
    return pl.pallas_call(
        body, name=name, grid=(ncol, m // tb),
        in_specs=list(specs) + [_par_spec(p.shape, ncol) for p in pars],
        out_specs=[pl.BlockSpec((tb, w), lambda j, i: (i, j)) for w, _ in outs],
        out_shape=[jax.ShapeDtypeStruct((m, w * ncol), dt) for w, dt in outs],
        compiler_params=_cparams(("parallel", "parallel")),
    )(*arrs, *pars)


def rowwise_vjp(fn, rows, pars, cots, drow_dtypes, name, tb=256, sub=16, out_sums=(), ncol=1):
    m = (rows[0].arr if isinstance(rows[0], Cols) else rows[0]).shape[0]
    tb = _pick(m, tb)
    sub = min(sub, tb)
    cots = [list(c) if isinstance(c, (list, tuple)) else [c] for c in cots]
    flat = [c for group in cots for c in group]
    nr, npar, nc = len(rows), len(pars), len(flat)
    want = [i for i, d in enumerate(drow_dtypes) if d is not None]
    assert ncol == 1 or not out_sums

    def body(*refs):
        row_refs, par_refs = refs[:nr], refs[nr:nr + npar]
        cot_refs = list(refs[nr + npar:nr + npar + nc])
        drow_refs = refs[nr + npar + nc:nr + npar + nc + len(want)]
        acc_refs = refs[nr + npar + nc + len(want):]
        pv = [_par_value(p, ncol) for p in par_refs]

        def step(i, acc):
            sl = pl.ds(pl.multiple_of(i * sub, sub), sub)
            rv = [r[sl, :].astype(F32) for r in row_refs]
            res, vjp = jax.vjp(lambda rr, pp: tuple(fn(*rr, *pp)), rv, pv)
            ct, at = [], 0
            for group in cots:
                ct.append(sum(c[sl, :].astype(F32) for c in cot_refs[at:at + len(group)]))
                at += len(group)
            d_rows, d_pars = vjp(tuple(ct))
            for o, idx in zip(drow_refs, want):
                o[sl, :] = d_rows[idx].astype(o.dtype)
            sums = [jnp.sum(res[k], axis=0, keepdims=True) for k, _ in out_sums]
            return tuple(a + d for a, d in zip(acc, list(d_pars) + sums))

        init = tuple(jnp.zeros(p.shape, F32) for p in pv) + tuple(jnp.zeros((1, w), F32) for _, w in out_sums)
        acc = lax.fori_loop(0, tb // sub, step, init)

        @pl.when(pl.program_id(1) == 0)
        def _():
            for o in acc_refs:
                o[...] = jnp.zeros_like(o)

        for o, a in zip(acc_refs, acc):
            if ncol == 1:
                o[...] += a
            else:
                o[0] += a

    arrs, specs = zip(*[_row_spec(r, tb, ncol) for r in rows])
    carrs, cspecs = zip(*[_row_spec(c, tb, ncol) for c in flat])
    widths = [_row_width(r, ncol) for r in rows]
    acc_shapes = [tuple(p.shape) for p in pars] + [(1, w) for _, w in out_sums]
    return pl.pallas_call(
        body, name=name, grid=(ncol, m // tb),
        in_specs=list(specs) + [_par_spec(p.shape, ncol) for p in pars] + list(cspecs),
        out_specs=[pl.BlockSpec((tb, widths[i]), lambda j, i_: (i_, j)) for i in want]
        + [_par_spec(s, ncol) for s in acc_shapes],
        out_shape=[jax.ShapeDtypeStruct((m, widths[i] * ncol), drow_dtypes[i]) for i in want]
        + [jax.ShapeDtypeStruct(s, F32) for s in acc_shapes],
        compiler_params=_cparams(("parallel", "arbitrary")),
    )(*arrs, *pars, *carrs)


def _col(v, c):
    lane = lax.broadcasted_iota(jnp.int32, v.shape, 1)
    return jnp.sum(jnp.where(lane == c, v, 0.0), axis=1, keepdims=True)


def _row(v, r):
    sub = lax.broadcasted_iota(jnp.int32, v.shape, 0)
    return jnp.sum(jnp.where(sub == r, v, 0.0), axis=0, keepdims=True)


def _ssd_chunk(xs, dt, bm, cm, hs, a_row, rev, col0):
    q = dt.shape[0]
    ii = lax.broadcasted_iota(jnp.int32, (q, q), 0)
    jj = lax.broadcasted_iota(jnp.int32, (q, q), 1)
    keep = (jj >= ii) if rev else (jj <= ii)
    tri = keep.astype(F32)
    dta = dt * a_row
    a_cum = hdot(tri, dta, "nn")
    tri_t = ((jj <= ii) if rev else (jj >= ii)).astype(F32)
    a_cum_t = hdot(dta, tri_t, "tn")
    cb = bdot(cm, bm, "nt")
    last = 0 if rev else q - 1
    ys, hn = [], []
    for e, (x, h) in enumerate(zip(xs, hs)):
        c = col0 + e
        ac = _col(a_cum, c)
        seg = ac - _row(a_cum_t, c)
        lm = jnp.exp(jnp.where(keep, seg, NEG))
        xdt = x * _col(dt, c)
        y_diag = bdot(cb * lm, xdt, "nn")
        y_off = bdot(cm * jnp.exp(ac), h, "nt")
        ys.append(y_diag + y_off)
        tot = _row(ac, last)
        states = bdot(xdt, bm * jnp.exp(tot - ac), "tn")
        hn.append(jnp.exp(tot) * h + states)
    return ys, hn


SSD_STEP_CHUNKS = 2


def _ssd_steps(nc):
    cps = SSD_STEP_CHUNKS if nc % SSD_STEP_CHUNKS == 0 else 1
    return cps, nc // cps


def _ssd_specs(nc, hpg, w_ssd, rev_order):
    cps, ns = _ssd_steps(nc)
    q = cps * SSD_CHUNK
    wx = hpg * HEAD_DIM
    boff = w_ssd // LANES

    def cidx(c):
        return ns - 1 - c if rev_order else c

    x_spec = pl.BlockSpec((q, wx), lambda g, c: (cidx(c), g))
    dt_spec = pl.BlockSpec((q, LANES), lambda g, c: (cidx(c), g))
    b_spec = pl.BlockSpec((q, LANES), lambda g, c: (cidx(c), boff + g))
    c_spec = pl.BlockSpec((q, LANES), lambda g, c: (cidx(c), boff + SSD_GROUPS + g))
    a_spec = pl.BlockSpec((1, SUBLANES, LANES), lambda g, c: (g, 0, 0))
    st_spec = pl.BlockSpec((1, wx, LANES), lambda g, c: (g, 0, 0))
    ent_spec = pl.BlockSpec((1, cps, wx, LANES), lambda g, c: (g, cidx(c), 0, 0))
    return x_spec, dt_spec, b_spec, c_spec, a_spec, st_spec, ent_spec


def ssd_fwd(xbc, dtg, a_g, h0, rev, w_ssd, name):
    L = xbc.shape[0]
    nc = L // SSD_CHUNK
    hpg = w_ssd // (SSD_GROUPS * HEAD_DIM)
    wx = hpg * HEAD_DIM
    col0 = hpg if rev else 0
    cps, ns = _ssd_steps(nc)
    x_spec, dt_spec, b_spec, c_spec, a_spec, st_spec, ent_spec = _ssd_specs(nc, hpg, w_ssd, rev)

    def body(x_ref, dt_ref, b_ref, c_ref, a_ref, h0_ref, y_ref, ent_ref, fin_ref, hs):
        c = pl.program_id(1)

        @pl.when(c == 0)
        def _():
            hs[...] = h0_ref[0]

        for s in (reversed(range(cps)) if rev else range(cps)):
            rows = slice(s * SSD_CHUNK, (s + 1) * SSD_CHUNK)
            ent_ref[0, s] = hs[...]
            xs = [x_ref[rows, e * HEAD_DIM:(e + 1) * HEAD_DIM].astype(F32) for e in range(hpg)]
            hin = [hs[e * HEAD_DIM:(e + 1) * HEAD_DIM, :] for e in range(hpg)]
            ys, hn = _ssd_chunk(xs, dt_ref[rows, :], b_ref[rows, :].astype(F32), c_ref[rows, :].astype(F32), hin, a_ref[0, 0:1, :], rev, col0)
            for e in range(hpg):
                y_ref[rows, e * HEAD_DIM:(e + 1) * HEAD_DIM] = ys[e].astype(y_ref.dtype)
                hs[e * HEAD_DIM:(e + 1) * HEAD_DIM, :] = hn[e]
        fin_ref[0] = hs[...]

    return pl.pallas_call(
        body, name=name, grid=(SSD_GROUPS, ns),
        in_specs=[x_spec, dt_spec, b_spec, c_spec, a_spec, st_spec],
        out_specs=[x_spec, ent_spec, st_spec],
        out_shape=[jax.ShapeDtypeStruct((L, w_ssd), BF16),
                   jax.ShapeDtypeStruct((SSD_GROUPS, nc, wx, LANES), F32),
                   jax.ShapeDtypeStruct((SSD_GROUPS, wx, LANES), F32)],
        scratch_shapes=[pltpu.VMEM((wx, LANES), F32)],
        compiler_params=_cparams(("parallel", "arbitrary")),
    )(xbc, dtg, xbc, xbc, a_g, h0)


def ssd_bwd(xbc, dtg, a_g, ent, dy, dfin, rev, w_ssd, name, carry=None):
    L = xbc.shape[0]
    nc = L // SSD_CHUNK
    hpg = w_ssd // (SSD_GROUPS * HEAD_DIM)
    wx = hpg * HEAD_DIM
    col0 = hpg if rev else 0
    cps, ns = _ssd_steps(nc)
    x_spec, dt_spec, b_spec, c_spec, a_spec, st_spec, ent_spec = _ssd_specs(nc, hpg, w_ssd, not rev)
    bc_out = pl.BlockSpec((cps * SSD_CHUNK, LANES), lambda g, c: ((c if rev else ns - 1 - c), g))
    n_ci, n_co = (len(carry.ins), len(carry.outs)) if carry else (0, 0)

    def body(x_ref, dt_ref, b_ref, c_ref, a_ref, ent_ref, dy_ref, dfin_ref, *rest):
        c_ins = rest[:n_ci]
        dx_ref, ddt_ref, db_ref, dc_ref, da_ref, dh0_ref = rest[n_ci:n_ci + 6]
        c_outs = rest[n_ci + 6:n_ci + 6 + n_co]
        dhs = rest[n_ci + 6 + n_co]
        sems = tuple(rest[n_ci + 7 + n_co:])
        g, c = pl.program_id(0), pl.program_id(1)
        if carry:
            pl.when((g == 0) & (c == 0))(lambda: carry.start(c_ins, c_outs, sems))

        @pl.when(c == 0)
        def _():
            dhs[...] = dfin_ref[0]
            da_ref[...] = jnp.zeros_like(da_ref)

        for s in (range(cps) if rev else reversed(range(cps))):
            rows = slice(s * SSD_CHUNK, (s + 1) * SSD_CHUNK)
            xs = [x_ref[rows, e * HEAD_DIM:(e + 1) * HEAD_DIM].astype(F32) for e in range(hpg)]
            hin = [ent_ref[0, s, e * HEAD_DIM:(e + 1) * HEAD_DIM, :] for e in range(hpg)]
            _, vjp = jax.vjp(lambda *a: _ssd_chunk(*a, rev, col0), xs, dt_ref[rows, :], b_ref[rows, :].astype(F32),
                             c_ref[rows, :].astype(F32), hin, a_ref[0, 0:1, :])
            dys = [dy_ref[rows, e * HEAD_DIM:(e + 1) * HEAD_DIM].astype(F32) for e in range(hpg)]
            dhn = [dhs[e * HEAD_DIM:(e + 1) * HEAD_DIM, :] for e in range(hpg)]
            dxs, ddt, db, dc, dh, da = vjp((dys, dhn))
            for e in range(hpg):
                dx_ref[rows, e * HEAD_DIM:(e + 1) * HEAD_DIM] = dxs[e].astype(dx_ref.dtype)
                dhs[e * HEAD_DIM:(e + 1) * HEAD_DIM, :] = dh[e]
            ddt_ref[rows, :] = ddt
            db_ref[rows, :] = db
            dc_ref[rows, :] = dc
            da_ref[0, 0:1, :] += da
        dh0_ref[0] = dhs[...]
        if carry:
            pl.when((g == SSD_GROUPS - 1) & (c == ns - 1))(lambda: carry.finish(c_ins, c_outs, sems))

    res = pl.pallas_call(
        body, name=name, grid=(SSD_GROUPS, ns),
        in_specs=[x_spec, dt_spec, b_spec, c_spec, a_spec, ent_spec, x_spec, st_spec] + [_HBM] * n_ci,
        out_specs=[x_spec, dt_spec, bc_out, bc_out, a_spec, st_spec] + [_HBM] * n_co,
        out_shape=[jax.ShapeDtypeStruct((L, w_ssd), BF16),
                   jax.ShapeDtypeStruct((L, SSD_GROUPS * LANES), F32),
                   jax.ShapeDtypeStruct((L, SSD_GROUPS * LANES), F32),
                   jax.ShapeDtypeStruct((L, SSD_GROUPS * LANES), F32),
                   jax.ShapeDtypeStruct((SSD_GROUPS, SUBLANES, LANES), F32),
                   jax.ShapeDtypeStruct((SSD_GROUPS, wx, LANES), F32)] + (carry.outs if carry else []),
        scratch_shapes=[pltpu.VMEM((wx, LANES), F32)] + (carry.sem_shapes() if carry else []),
        compiler_params=_cparams(("arbitrary", "arbitrary") if carry else ("parallel", "arbitrary")),
    )(xbc, dtg, xbc, xbc, a_g, ent, dy, dfin, *(carry.ins if carry else []))
    return (list(res[:6]), list(res[6:])) if carry else res


CONV_W = 4
CONV_LEFT = 2
CONV_CB = 128
CONV_ROWS = 256
HALO = 16


def _conv_window(ref, r, rows, nch, L):
    s = pl.multiple_of(r * rows, rows)
    cur = ref[pl.ds(s, rows), :].astype(F32)
    sp = pl.multiple_of(jnp.maximum(s - HALO, 0), HALO)
    sn = pl.multiple_of(jnp.minimum(s + rows, L - HALO), HALO)
    prev = jnp.where(r > 0, ref[pl.ds(sp, HALO), :].astype(F32), 0.0)
    nxt = jnp.where(r < nch - 1, ref[pl.ds(sn, HALO), :].astype(F32), 0.0)
    return jnp.concatenate([prev, cur, nxt], axis=0)


def _shifted(win, off, rows):
    n = win.shape[0]
    return pltpu.roll(win, (-off) % n, axis=0)[HALO:HALO + rows, :]


def _dsilu(p):
    s = _sigmoid(p)
    return s * (1.0 + p * (1.0 - s))


def conv_fwd(src, col_blk0, w, b, act, name, out_dtype=F32):
    L, C = src.shape[0], w.shape[1]
    rows = min(CONV_ROWS, L)
    nch = L // rows

    def body(x_ref, w_ref, b_ref, o_ref):
        def chunk(r, carry):
            win = _conv_window(x_ref, r, rows, nch, L)
            pre = b_ref[...] + sum(w_ref[k:k + 1, :] * _shifted(win, k - CONV_LEFT, rows) for k in range(CONV_W))
            o_ref[pl.ds(pl.multiple_of(r * rows, rows), rows), :] = (_silu(pre) if act else pre).astype(o_ref.dtype)
            return carry

        lax.fori_loop(0, nch, chunk, 0)

    return pl.pallas_call(
        body, name=name, grid=(C // CONV_CB,),
        in_specs=[pl.BlockSpec((L, CONV_CB), lambda j: (0, col_blk0 + j)),
                  pl.BlockSpec((CONV_W, CONV_CB), lambda j: (0, j)),
                  pl.BlockSpec((1, CONV_CB), lambda j: (0, j))],
        out_specs=pl.BlockSpec((L, CONV_CB), lambda j: (0, j)),
        out_shape=jax.ShapeDtypeStruct((L, C), out_dtype),
        compiler_params=_cparams(("parallel",)),
    )(src, w, b)


def conv_bwd(src, col_blk0, w, b, douts, act, name):
    L, C = src.shape[0], w.shape[1]
    rows = min(CONV_ROWS, L)
    nch = L // rows
    nd = len(douts)

    def body(*refs):
        x_ref, w_ref, b_ref = refs[:3]
        d_refs = refs[3:3 + nd]
        dx_ref, dw_ref, db_ref, dp = refs[3 + nd:]

        def pass1(r, acc):
            sl = pl.ds(pl.multiple_of(r * rows, rows), rows)
            win = _conv_window(x_ref, r, rows, nch, L)
            taps = [_shifted(win, k - CONV_LEFT, rows) for k in range(CONV_W)]
            dpre = sum(d[sl, :].astype(F32) for d in d_refs)
            if act:
                pre = b_ref[...] + sum(w_ref[k:k + 1, :] * taps[k] for k in range(CONV_W))
                dpre = dpre * _dsilu(pre)
            dp[sl, :] = dpre
            new = [acc[k] + jnp.sum(dpre * taps[k], axis=0, keepdims=True) for k in range(CONV_W)]
            return tuple(new) + (acc[CONV_W] + jnp.sum(dpre, axis=0, keepdims=True),)

        zero = jnp.zeros((1, CONV_CB), F32)
        acc = lax.fori_loop(0, nch, pass1, (zero,) * (CONV_W + 1))
        for k in range(CONV_W):
            dw_ref[k:k + 1, :] = acc[k]
        db_ref[...] = acc[CONV_W]

        def pass2(r, carry):
            win = _conv_window(dp, r, rows, nch, L)
            dx = sum(w_ref[k:k + 1, :] * _shifted(win, CONV_LEFT - k, rows) for k in range(CONV_W))
            dx_ref[pl.ds(pl.multiple_of(r * rows, rows), rows), :] = dx.astype(dx_ref.dtype)
            return carry

        lax.fori_loop(0, nch, pass2, 0)

    col = pl.BlockSpec((L, CONV_CB), lambda j: (0, j))
    return pl.pallas_call(
        body, name=name, grid=(C // CONV_CB,),
        in_specs=[pl.BlockSpec((L, CONV_CB), lambda j: (0, col_blk0 + j)),
                  pl.BlockSpec((CONV_W, CONV_CB), lambda j: (0, j)),
                  pl.BlockSpec((1, CONV_CB), lambda j: (0, j))] + [col] * nd,
        out_specs=[col, pl.BlockSpec((CONV_W, CONV_CB), lambda j: (0, j)), pl.BlockSpec((1, CONV_CB), lambda j: (0, j))],
        out_shape=[jax.ShapeDtypeStruct((L, C), BF16), jax.ShapeDtypeStruct((CONV_W, C), F32), jax.ShapeDtypeStruct((1, C), F32)],
        scratch_shapes=[pltpu.VMEM((L, CONV_CB), F32)],
        compiler_params=_cparams(("parallel",)),
    )(src, w, b, *douts)


SCAN_ROWS = 512
SCAN_CB = 512


def _scan8(a, u, rev):
    sub = lax.broadcasted_iota(jnp.int32, a.shape, 0)
    for s in (1, 2, 4):
        if rev:
            ok = sub < SUBLANES - s
            a_sh, u_sh = pltpu.roll(a, SUBLANES - s, axis=0), pltpu.roll(u, SUBLANES - s, axis=0)
        else:
            ok = sub >= s
            a_sh, u_sh = pltpu.roll(a, s, axis=0), pltpu.roll(u, s, axis=0)
        u = a * jnp.where(ok, u_sh, 0.0) + u
        a = a * jnp.where(ok, a_sh, 1.0)
    return a, u


def _shift8(h, carry, rev):
    sub = lax.broadcasted_iota(jnp.int32, h.shape, 0)
    if rev:
        return jnp.where(sub == SUBLANES - 1, carry, pltpu.roll(h, SUBLANES - 1, axis=0))
    return jnp.where(sub == 0, carry, pltpu.roll(h, 1, axis=0))


def lru_scan(a, u, c0, rev, name, adjoint_of=None, add=None):
    L, C = a.shape
    tc, cb = _pick(L, SCAN_ROWS), _pick(C, SCAN_CB)
    nt = L // tc
    ng = tc // SUBLANES
    adj = adjoint_of is not None
    n_in = 3 + (1 if adj else 0) + (1 if add is not None else 0)

    def body(*refs):
        a_ref, u_ref, c0_ref = refs[:3]
        x_ref = refs[3] if (adj or add is not None) else None
        outs = refs[n_in:-1]
        st = refs[-1]

        @pl.when(pl.program_id(1) == 0)
        def _():
            st[...] = c0_ref[...]

        def group(i, carry):
            g = (ng - 1 - i) if rev else i
            sl = pl.ds(pl.multiple_of(g * SUBLANES, SUBLANES), SUBLANES)
            av, uv = a_ref[sl, :], u_ref[sl, :]
            pa, h = _scan8(av, av * uv if adj else uv, rev)
            h = pa * carry + h
            hs = _shift8(h, carry, rev)
            if adj:
                lam = uv + hs
                outs[0][sl, :] = lam
                outs[1][sl, :] = lam * x_ref[sl, :]
            else:
                outs[0][sl, :] = h
                outs[1][sl, :] = hs
                if add is not None:
                    outs[3][sl, :] = h + x_ref[sl, :]
            last = 0 if rev else SUBLANES - 1
            return jnp.broadcast_to(h[last:last + 1, :], h.shape)

        st[...] = lax.fori_loop(0, ng, group, st[...])
        outs[2][...] = st[...]

    def tmap(j, t):
        return ((nt - 1 - t) if rev else t, j)

    blk = pl.BlockSpec((tc, cb), tmap)
    vec = pl.BlockSpec((SUBLANES, cb), lambda j, t: (0, j))
    args = [a, u, c0] + ([adjoint_of] if adj else []) + ([add] if add is not None else [])
    n_big = 2 if (adj or add is None) else 3
    out_specs = [blk, blk, vec] + ([blk] if n_big == 3 else [])
    out_shape = [jax.ShapeDtypeStruct((L, C), F32), jax.ShapeDtypeStruct((L, C), F32), jax.ShapeDtypeStruct((SUBLANES, C), F32)]
    out_shape += [jax.ShapeDtypeStruct((L, C), F32)] if n_big == 3 else []
    return pl.pallas_call(
        body, name=name, grid=(C // cb, nt),
        in_specs=[blk, blk, vec] + [blk] * (n_in - 3),
        out_specs=out_specs, out_shape=out_shape,
        scratch_shapes=[pltpu.VMEM((SUBLANES, cb), F32)],
        compiler_params=_cparams(("parallel", "arbitrary")),
    )(*args)


def f_silu(v):
    return (_silu(v),)


def f_add_bias(v, b):
    return (v + b,)


def f_mod_in(x, gain, shift, scale):
    return _rms(x, gain) * (1.0 + scale) + shift, x


def f_dt(raw, bias):
    return (_softplus(raw + bias),)


def f_gnorm(yf, yb, xs, z, dexp, gain):
    return (_rms((yf + yb + dexp * xs) * _silu(z), gain),)


def f_gelu_gate(r, yr):
    return (r * _gelu_tanh(yr),)


def f_merge(gs, gr, o_s, o_r, bs, br):
    return (_sigmoid(gs + bs) * o_s + _sigmoid(gr + br) * o_r,)


def f_res_mod(x, out, gm, gain, shift, scale):
    x1 = x + gm * out
    return x1, _rms(x1, gain) * (1.0 + scale) + shift


def f_final(x1, f, t, gf, gain):
    e = _rms(x1 + gf * f, gain) - t
    return (jnp.broadcast_to(0.5 * jnp.mean(e * e, axis=-1, keepdims=True), (e.shape[0], LANES)),)


GATE_ROWS = 1024
GATE_SUB_ROWS = 256


def _lru_coeffs(xh, wa, ba, wx, bx, lam):
    r = _sigmoid(bdot(xh, wa, "nn") + ba)
    i = _sigmoid(bdot(xh, wx, "nn") + bx)
    log_a = -LRU_C * r * _softplus(-lam)
    a = jnp.exp(log_a)
    one_minus_a2 = -jnp.tanh(log_a) * (a * a + 1.0)
    return a, jnp.sqrt(one_minus_a2) * (i * xh)


def f_gates(xh, wa_f, ba_f, wx_f, bx_f, lam_f, wa_b, ba_b, wx_b, bx_b, lam_b):
    return _lru_coeffs(xh, wa_f, ba_f, wx_f, bx_f, lam_f) + _lru_coeffs(xh, wa_b, ba_b, wx_b, bx_b, lam_b)


def f_adam(w, g, m, v):
    m2 = ADAM_B1 * m + (1.0 - ADAM_B1) * g
    v2 = ADAM_B2 * v + (1.0 - ADAM_B2) * (g * g)
    m_hat = m2 / (1.0 - ADAM_B1 ** ADAM_STEP)
    v_hat = v2 / (1.0 - ADAM_B2 ** ADAM_STEP)
    return -ADAM_LR * (m_hat / (jnp.sqrt(v_hat) + ADAM_EPS) + ADAM_WD * w), m2, v2


def swiglu(gu, dact, name, tb=128, sub=16):
    L, f2 = gu.shape
    f = f2 // 2
    tb = _pick(L, tb)

    def body(*refs):
        gu_ref, o_ref = refs[0], refs[-1]

        def step(i, carry):
            sl = pl.ds(pl.multiple_of(i * sub, sub), sub)
            g, u = gu_ref[sl, :f].astype(F32), gu_ref[sl, f:].astype(F32)
            if dact is None:
                o_ref[sl, :] = (_silu(g) * u).astype(o_ref.dtype)
            else:
                d = refs[1][sl, :].astype(F32)
                o_ref[sl, :f] = (d * u * _dsilu(g)).astype(o_ref.dtype)
                o_ref[sl, f:] = (d * _silu(g)).astype(o_ref.dtype)
            return carry

        lax.fori_loop(0, tb // sub, step, 0)

    wout = f if dact is None else f2
    ins = [gu] + ([] if dact is None else [dact])
    return pl.pallas_call(
        body, name=name, grid=(L // tb,),
        in_specs=[pl.BlockSpec((tb, a.shape[1]), lambda i: (i, 0)) for a in ins],
        out_specs=pl.BlockSpec((tb, wout), lambda i: (i, 0)),
        out_shape=jax.ShapeDtypeStruct((L, wout), BF16),
        compiler_params=_cparams(("parallel",)),
    )(*ins)


N_DEV = 8
N_CHIP = 4
PACK_W = 1024
_HBM = pl.BlockSpec(memory_space=pltpu.HBM)


def _me():
    return lax.axis_index("x"), lax.axis_index("y"), lax.axis_index("c")


def _flip(pos, k):
    x, y, c = pos
    return (1 - x if k & 4 else x, 1 - y if k & 2 else y, 1 - c if k & 1 else c)


def _dev_index(pos):
    return 4 * pos[0] + 2 * pos[1] + pos[2]


def _chip_index(pos):
    return 2 * pos[0] + pos[1]


def _rcopy(src, dst, sems, k, to):
    send_sems, recv_sems = sems
    return pltpu.make_async_remote_copy(src_ref=src, dst_ref=dst, send_sem=send_sems.at[k], recv_sem=recv_sems.at[k],
                                        device_id=to, device_id_type=MESH)


def fill_own_device_slot(stack, mine):
    return lax.dynamic_update_slice(stack, mine[None], (_dev_index(_me()),) + (0,) * mine.ndim)


COPY_PIECES = 4
ROW_TILE_16BIT = 16


def _pieces(rows, n=COPY_PIECES, align=ROW_TILE_16BIT):
    while n > 1 and rows % (n * align):
        n -= 1
    return [(q * (rows // n), rows // n) for q in range(n)]


class Exchange:
    def __init__(self, ins, outs, n_sems, start, finish):
        self.ins, self.outs, self.n_sems, self.start, self.finish = list(ins), list(outs), n_sems, start, finish

    def sem_shapes(self):
        return [pltpu.SemaphoreType.DMA((self.n_sems,)), pltpu.SemaphoreType.DMA((self.n_sems,))]


def run_exchange(ex, name):
    n_in, n_out = len(ex.ins), len(ex.outs)

    def body(*refs):
        ins, outs, sems = refs[:n_in], refs[n_in:n_in + n_out], tuple(refs[n_in + n_out:])
        ex.start(ins, outs, sems)
        ex.finish(ins, outs, sems)

    return pl.pallas_call(body, name=name, in_specs=[_HBM] * n_in, out_specs=[_HBM] * n_out, out_shape=ex.outs,
                          scratch_shapes=ex.sem_shapes())(*ex.ins)


def allgather_exchange(v):
    def plan(ins, outs):
        me = _me()
        for k in range(1, N_DEV):
            peer = _flip(me, k)
            yield outs[0].at[_dev_index(me)], outs[0].at[_dev_index(peer)], k - 1, peer

    def start(ins, outs, sems):
        for mine, _, k, peer in plan(ins, outs):
            _rcopy(ins[0], mine, sems, k, peer).start()

    def finish(ins, outs, sems):
        for _, theirs, k, peer in plan(ins, outs):
            _rcopy(theirs, theirs, sems, k, peer).wait_recv()
        for mine, _, k, peer in plan(ins, outs):
            _rcopy(ins[0], mine, sems, k, peer).wait_send()

    return Exchange([v], [jax.ShapeDtypeStruct((N_DEV,) + v.shape, v.dtype)], N_DEV - 1, start, finish)


def allgather8(v, name):
    return fill_own_device_slot(run_exchange(allgather_exchange(v), name)[0], v)


def gather_exchange(shards):
    cuts = [_pieces(s.shape[0] // 2) for s in shards]
    base = [0]
    for cu in cuts:
        base.append(base[-1] + 6 * len(cu))

    def plan(ins, outs):
        me = _me()
        chips = [_flip(me, 4), _flip(me, 2), _flip(me, 6)]
        for w, cu in enumerate(cuts):
            half, nq = shards[w].shape[0] // 2, len(cu)
            for q, (s, n) in enumerate(cu):
                mine = pl.ds(pl.multiple_of(me[2] * half + s, ROW_TILE_16BIT), n)
                sibs = pl.ds(pl.multiple_of((1 - me[2]) * half + s, ROW_TILE_16BIT), n)
                for j, p in enumerate(chips):
                    yield w, p, mine, sibs, base[w] + j * nq + q, base[w] + (3 + j) * nq + q

    def start(ins, outs, sems):
        me = _me()
        for w, p, mine, _, k_ici, _ in plan(ins, outs):
            _rcopy(ins[w].at[mine], outs[w].at[_chip_index(me), mine], sems, k_ici, p).start()

    def finish(ins, outs, sems):
        me = _me()
        sibling = _flip(me, 1)
        for w, p, mine, _, k_ici, k_fwd in plan(ins, outs):
            got = outs[w].at[_chip_index(p), mine]
            _rcopy(got, got, sems, k_ici, p).wait_recv()
            _rcopy(got, got, sems, k_fwd, sibling).start()
        for w, p, _, sibs, _, k_fwd in plan(ins, outs):
            got = outs[w].at[_chip_index(p), sibs]
            _rcopy(got, got, sems, k_fwd, sibling).wait_recv()
        for w, p, mine, _, k_ici, k_fwd in plan(ins, outs):
            got = outs[w].at[_chip_index(p), mine]
            _rcopy(ins[w].at[mine], got, sems, k_ici, p).wait_send()
            _rcopy(got, got, sems, k_fwd, sibling).wait_send()

    outs = [jax.ShapeDtypeStruct((N_CHIP,) + s.shape, s.dtype) for s in shards]
    return Exchange(shards, outs, base[-1], start, finish)


def fill_own_slot(stack, mine):
    return lax.dynamic_update_slice(stack, mine[None], (_chip_index(_me()), 0, 0))


def scatter_exchange(parts):
    def plan(ins, outs):
        me = _me()
        for w in range(len(parts)):
            half = parts[w].shape[1] // 2
            for k in range(1, N_DEV):
                to = _flip(me, k)
                src = ins[w].at[_chip_index(to), pl.ds(pl.multiple_of(to[2] * half, ROW_TILE_16BIT), half)]
                yield src, outs[w].at[k - 1], (N_DEV - 1) * w + k - 1, to

    def start(ins, outs, sems):
        for src, dst, k, to in plan(ins, outs):
            _rcopy(src, dst, sems, k, to).start()

    def finish(ins, outs, sems):
        for src, dst, k, to in plan(ins, outs):
            _rcopy(dst, dst, sems, k, to).wait_recv()
        for src, dst, k, to in plan(ins, outs):
            _rcopy(src, dst, sems, k, to).wait_send()

    outs = [jax.ShapeDtypeStruct((N_DEV - 1, p.shape[1] // 2, p.shape[2]), p.dtype) for p in parts]
    return Exchange(parts, outs, (N_DEV - 1) * len(parts), start, finish)


def own_piece(parts):
    me = _me()
    half = parts.shape[1] // 2
    return lax.dynamic_slice(parts, (_chip_index(me), me[2] * half, 0), (1, half, parts.shape[2]))[0]


def join_halves(mine, name):
    half, C = mine.shape
    cuts = _pieces(half, 2 * COPY_PIECES, SUBLANES)
    nq = len(cuts)

    def body(m_ref, out_ref, send_sems, recv_sems):
        me = _me()
        sems = (send_sems, recv_sems)
        sibling = _flip(me, 1)
        sends = [_rcopy(m_ref.at[pl.ds(s, n)], out_ref.at[pl.ds(s, n)], sems, q, sibling) for q, (s, n) in enumerate(cuts)]
        for cp in sends:
            cp.start()
        for q, (s, n) in enumerate(cuts):
            got = out_ref.at[pl.ds(s, n)]
            _rcopy(got, got, sems, q, sibling).wait_recv()
        for cp in sends:
            cp.wait_send()

    other = pl.pallas_call(
        body, name=name, in_specs=[_HBM], out_specs=_HBM,
        out_shape=jax.ShapeDtypeStruct((half, C), mine.dtype),
        scratch_shapes=[pltpu.SemaphoreType.DMA((nq,)), pltpu.SemaphoreType.DMA((nq,))],
    )(mine)
    south = _me()[2] == 0
    return jnp.concatenate([jnp.where(south, mine, other), jnp.where(south, other, mine)], axis=0)


SUM_BLOCK_ELEMS = 256 * 1024


def sum_slots(a, name, first=None):
    n, R, C = a.shape
    tb = _pick(R, max(ROW_TILE_16BIT, SUM_BLOCK_ELEMS // C), ROW_TILE_16BIT)

    def body(a_ref, *rest):
        o_ref = rest[-1]
        acc = a_ref[0].astype(F32) if first is None else rest[0][...].astype(F32) + a_ref[0].astype(F32)
        for j in range(1, n):
            acc = acc + a_ref[j].astype(F32)
        o_ref[...] = acc

    row = pl.BlockSpec((tb, C), lambda i: (i, 0))
    return pl.pallas_call(
        body, name=name, grid=(R // tb,),
        in_specs=[pl.BlockSpec((n, tb, C), lambda i: (0, i, 0))] + ([] if first is None else [row]),
        out_specs=row,
        out_shape=jax.ShapeDtypeStruct((R, C), F32),
        compiler_params=_cparams(("parallel",)),
    )(*([a] if first is None else [a, first]))


def _pack(arrays, dtype, row_align):
    flat = jnp.concatenate([a.astype(dtype).reshape(-1) for a in arrays])
    per = PACK_W * row_align
    n = -(-flat.shape[0] // per) * per
    return jnp.pad(flat, (0, n - flat.shape[0])).reshape(-1, PACK_W)


def _unpack(buf, shapes):
    flat = buf.reshape(-1)
    out, at = [], 0
    for s in shapes:
        n = 1
        for d in s:
            n *= d
        out.append(flat[at:at + n].reshape(s))
        at += n
    return out


def _to_col_major(t, rows):
    L, C = t.shape
    return t.reshape(rows, GRID_W, C).transpose(1, 0, 2).reshape(L, C)


def _to_row_major(t, rows):
    L, C = t.shape
    return t.reshape(GRID_W, rows, C).transpose(1, 0, 2).reshape(L, C)


def _heads_to_groups(v, hpg):
    lead = v.shape[:-1]
    t = jnp.moveaxis(v.reshape(lead + (2, SSD_GROUPS, hpg)), -3, -2).reshape(lead + (SSD_GROUPS, 2 * hpg))
    t = jnp.pad(t, [(0, 0)] * (len(lead) + 1) + [(0, LANES - 2 * hpg)])
    return t.reshape(lead + (SSD_GROUPS * LANES,))


def _groups_to_heads(t, hpg):
    lead = t.shape[:-1]
    t = t.reshape(lead + (SSD_GROUPS, LANES))[..., :2 * hpg].reshape(lead + (SSD_GROUPS, 2, hpg))
    return jnp.moveaxis(t, -2, -3).reshape(lead + (2 * SSD_GROUPS * hpg,))


def _r1(v):
    return v.reshape(1, -1)


class _Layout:
    def __init__(self, D, W, hpg):
        assert W == D and D % (SSD_GROUPS * LANES) == 0
        self.D, self.W, self.hpg = D, W, hpg
        self.GN = SSD_GROUPS * SSD_STATE
        self.xbc0 = 5 * D
        self.dt0 = 5 * D + W + 2 * self.GN
        self.width = self.dt0 + SSD_GROUPS * LANES
        self.xbc_blk = self.xbc0 // CONV_CB
        self.dt_blk = self.dt0 // (SSD_GROUPS * LANES)
        assert self.dt0 % (SSD_GROUPS * LANES) == 0


def _seq_forward(hb, w_all, sp, lay, tag, grid_rows, init, carry=None):
    D, W = lay.D, lay.W
    gw = SSD_GROUPS * LANES
    pg = matmul(hb, w_all, "nn", BF16, f"proj_{tag}", carry=carry)
    pg, carried = pg if carry else (pg, [])
    xbc = conv_fwd(pg, lay.xbc_blk, sp["ssd_conv_w"], sp["ssd_conv_b"], True, f"ssd_conv_{tag}", out_dtype=BF16)
    dtg = rowwise(f_dt, [Cols(pg, gw, lay.dt_blk)], [sp["dt_bias_g"]], [(gw, F32)], f"ssd_dt_{tag}")[0]
    yf, ent_f, fin_f = ssd_fwd(xbc, dtg, sp["a_g"], init["ssd_f"], False, W, f"ssd_scan_f_{tag}")
    yb, ent_b, fin_b = ssd_fwd(xbc, dtg, sp["a_g"], init["ssd_b"], True, W, f"ssd_scan_b_{tag}")
    xr = pg[:, D:2 * D]
    if grid_rows:
        xr = _to_col_major(xr, grid_rows)
    xc = conv_fwd(xr, 0, sp["lru_conv_w"], sp["lru_conv_b"], False, f"lru_conv_{tag}")
    a_f, u_f, a_b, u_b = rowwise(f_gates, [xc], sp["gate_pars"], [(LRU_HEAD, F32)] * 4, f"lru_gates_{tag}",
                                 tb=GATE_ROWS, sub=GATE_SUB_ROWS, ncol=D // LRU_HEAD)
    h_f, hs_f, fl_f = lru_scan(a_f, u_f, init["lru_f"], False, f"lru_scan_f_{tag}")
    _, hs_b, fl_b, r = lru_scan(a_b, u_b, init["lru_b"], True, f"lru_scan_b_{tag}", add=h_f)
    saved = dict(pg=pg, xbc=xbc, dtg=dtg, yf=yf, yb=yb, ent_f=ent_f, ent_b=ent_b, xr=xr, xc=xc,
                 a_f=a_f, a_b=a_b, hs_f=hs_f, hs_b=hs_b, r=r)
    return saved, dict(ssd_f=fin_f, ssd_b=fin_b, lru_f=fl_f, lru_b=fl_b), carried


def _seq_backward(sv, sp, lay, tag, grid_rows, dy, dxs_extra, d_r, dfin, carries=(None, None)):
    D, W, GN = lay.D, lay.W, lay.GN
    gw = SSD_GROUPS * LANES
    pg = sv["pg"]
    res_f = ssd_bwd(sv["xbc"], sv["dtg"], sp["a_g"], sv["ent_f"], dy, dfin["ssd_f"], False, W, f"ssd_scan_f_bwd_{tag}", carry=carries[0])
    res_b = ssd_bwd(sv["xbc"], sv["dtg"], sp["a_g"], sv["ent_b"], dy, dfin["ssd_b"], True, W, f"ssd_scan_b_bwd_{tag}", carry=carries[1])
    (dxf, ddt_f, db_f, dc_f, da_f, dh0_f), carried_f = res_f if carries[0] else (res_f, [])
    (dxb, ddt_b, db_b, dc_b, da_b, dh0_b), carried_b = res_b if carries[1] else (res_b, [])
    cw, cb = sp["ssd_conv_w"], sp["ssd_conv_b"]
    d_xs, dw1, db1 = conv_bwd(pg, lay.xbc_blk, cw[:, :W], cb[:, :W], [dxf, dxb] + dxs_extra, True, f"ssd_conv_x_bwd_{tag}")
    d_b, dw2, db2 = conv_bwd(pg, lay.xbc_blk + W // CONV_CB, cw[:, W:W + GN], cb[:, W:W + GN], [db_f, db_b], True, f"ssd_conv_b_bwd_{tag}")
    d_c, dw3, db3 = conv_bwd(pg, lay.xbc_blk + (W + GN) // CONV_CB, cw[:, W + GN:], cb[:, W + GN:], [dc_f, dc_b], True, f"ssd_conv_c_bwd_{tag}")
    d_dtraw, d_dtbias = rowwise_vjp(f_dt, [Cols(pg, gw, lay.dt_blk)], [sp["dt_bias_g"]], [[ddt_f, ddt_b]], [BF16], f"ssd_dt_bwd_{tag}")
    du_b, dab, dl0_b = lru_scan(sv["a_b"], d_r, dfin["lru_b"], False, f"lru_scan_b_bwd_{tag}", adjoint_of=sv["hs_b"])
    du_f, daf, dl0_f = lru_scan(sv["a_f"], d_r, dfin["lru_f"], True, f"lru_scan_f_bwd_{tag}", adjoint_of=sv["hs_f"])
    res = rowwise_vjp(f_gates, [sv["xc"]], sp["gate_pars"], [daf, du_f, dab, du_b], [F32], f"lru_gates_bwd_{tag}",
                      tb=GATE_ROWS, sub=GATE_SUB_ROWS, ncol=D // LRU_HEAD)
    d_xc, gate_grads = res[0], res[1:]
    d_xr, g_lcw, g_lcb = conv_bwd(sv["xr"], 0, sp["lru_conv_w"], sp["lru_conv_b"], [d_xc], False, f"lru_conv_bwd_{tag}")
    if grid_rows:
        d_xr = _to_row_major(d_xr, grid_rows)
    grads = dict(ssd_conv_w=jnp.concatenate([dw1, dw2, dw3], axis=1), ssd_conv_b=jnp.concatenate([db1, db2, db3], axis=1),
                 dt_bias_g=d_dtbias, a_g=da_f + da_b, lru_conv_w=g_lcw, lru_conv_b=g_lcb, gate_pars=list(gate_grads))
    pieces = dict(xr=d_xr, xs=d_xs, b=d_b, c=d_c, dt=d_dtraw)
    return pieces, grads, dict(ssd_f=dh0_f, ssd_b=dh0_b, lru_f=dl0_f, lru_b=dl0_b), (carried_f, carried_b)


def _proj_cotangent(lay, px, pc, Lc, dz, dyr, dgs, dgr):
    zero = jnp.zeros((Lc, lay.D), BF16)
    cols = [(dz, zero), (px["xr"], pc["xr"]), (dyr, zero), (dgs, zero), (dgr, zero),
            (px["xs"], pc["xs"]), (px["b"], pc["b"]), (px["c"], pc["c"]), (px["dt"], pc["dt"])]
    return jnp.concatenate([jnp.concatenate([a.astype(BF16), b.astype(BF16)], axis=0) for a, b in cols], axis=1)


EARLY = ["w_in", "w_gate"]
LATE = ["w_out_ssd", "w_out_lru", "w_o", "ffn_w13", "ffn_w2"]


def _local_step(x, ctx, target, mod_x, mod_c, wb, late, ws):
    L, D = x.shape
    Lc = ctx.shape[0]
    nh = ws["ssd_d"].shape[0]
    hpg = nh // SSD_GROUPS
    W = nh * HEAD_DIM
    lay = _Layout(D, W, hpg)
    GN = lay.GN
    grid_rows = L // GRID_W
    nlh = D // LRU_HEAD

    w_in = wb["w_in"]
    o_dt, o_xr = 2 * W + 2 * GN, 2 * W + 2 * GN + 2 * nh
    w_all = jnp.concatenate([w_in[:, :W], w_in[:, o_xr:o_xr + D], w_in[:, o_xr + D:], wb["w_gate"], w_in[:, W:o_dt],
                             _heads_to_groups(w_in[:, o_dt:o_xr], hpg)], axis=1)
    a_neg = -jnp.exp(ws["ssd_a_log"])
    a_g = jnp.pad(_heads_to_groups(a_neg.reshape(-1), hpg).reshape(SSD_GROUPS, 1, LANES), [(0, 0), (0, SUBLANES - 1), (0, 0)])
    gate_pars = []
    for d in range(2):
        gate_pars += [ws["lru_w_a"][d], ws["lru_b_a"][d].reshape(nlh, 1, LRU_HEAD), ws["lru_w_x"][d],
                      ws["lru_b_x"][d].reshape(nlh, 1, LRU_HEAD), ws["lru_lambda"][d].reshape(nlh, 1, LRU_HEAD)]
    sp = dict(ssd_conv_w=ws["ssd_conv_w"], ssd_conv_b=_r1(ws["ssd_conv_b"]), dt_bias_g=_r1(_heads_to_groups(ws["ssd_dt_bias"].reshape(-1), hpg)),
              a_g=a_g, lru_conv_w=ws["lru_conv_w"], lru_conv_b=_r1(ws["lru_conv_b"]), gate_pars=gate_pars)
    dexp = _r1(jnp.repeat(ws["ssd_d"], HEAD_DIM))
    norm_mix, norm_ffn, ssd_norm, final_norm = _r1(ws["norm_mix"]), _r1(ws["norm_ffn"]), _r1(ws["ssd_norm"]), _r1(ws["final_norm"])
    bg_s, bg_r = _r1(ws["b_gate"][:D]), _r1(ws["b_gate"][D:])
    sh_m, sc_m, g_m, sh_f, sc_f, g_f = [mod_x[:, k * D:(k + 1) * D] for k in range(6)]
    csh_m, csc_m = mod_c[:, :D], mod_c[:, D:2 * D]

    zst = dict(ssd_f=jnp.zeros((SSD_GROUPS, hpg * HEAD_DIM, SSD_STATE), F32), ssd_b=jnp.zeros((SSD_GROUPS, hpg * HEAD_DIM, SSD_STATE), F32),
               lru_f=jnp.zeros((SUBLANES, D), F32), lru_b=jnp.zeros((SUBLANES, D), F32))
    hcb = rowwise(f_mod_in, [ctx], [norm_mix, csh_m, csc_m], [(D, BF16)], "norm_mix_ctx")[0]
    hb = rowwise(f_mod_in, [x], [norm_mix, sh_m, sc_m], [(D, BF16)], "norm_mix_x")[0]
    svc, fin_c, _ = _seq_forward(hcb, w_all, sp, lay, "ctx", None, zst)
    svx, _, stacks = _seq_forward(hb, w_all, sp, lay, "x", grid_rows, fin_c, carry=gather_exchange([late[n] for n in LATE]))
    wb = dict(wb)
    for n, st in zip(LATE, stacks):
        wb[n] = _whole_from_shards(fill_own_slot(st, late[n]), n in BIG_BY_COLUMNS)
    pg = svx["pg"]
    r_rm = _to_row_major(svx["r"], grid_rows)
    gn_rows = [svx["yf"], svx["yb"], Cols(svx["xbc"], W, 0), Cols(pg, D, 0)]
    yn = rowwise(f_gnorm, gn_rows, [dexp, ssd_norm], [(W, BF16)], "ssd_gnorm")[0]
    o_s = matmul(yn, wb["w_out_ssd"], "nn", BF16, "out_ssd")
    o_in = rowwise(f_gelu_gate, [r_rm, Cols(pg, D, 2)], [], [(D, BF16)], "lru_gelu")[0]
    o_r = matmul(o_in, wb["w_out_lru"], "nn", BF16, "out_lru")
    mg_rows = [Cols(pg, D, 3), Cols(pg, D, 4), o_s, o_r]
    mixed = rowwise(f_merge, mg_rows, [bg_s, bg_r], [(D, BF16)], "merge")[0]
    out = matmul(mixed, wb["w_o"], "nn", BF16, "out_proj")
    x1, h2 = rowwise(f_res_mod, [x, out], [g_m, norm_ffn, sh_f, sc_f], [(D, F32), (D, BF16)], "res_norm_ffn")
    gu = matmul(h2, wb["ffn_w13"], "nn", BF16, "ffn_in")
    act = swiglu(gu, None, "ffn_act")
    f = matmul(act, wb["ffn_w2"], "nn", BF16, "ffn_out")

    ones = jnp.full((L, LANES), 1.0 / LANES, F32)
    dx1a, df, d_gf, d_fnorm, lsum = rowwise_vjp(f_final, [x1, f, target], [g_f, final_norm], [ones], [F32, BF16, None],
                                                "final_loss", out_sums=[(0, LANES)])
    loss = lsum[0, 0]
    d_act = matmul(df, wb["ffn_w2"], "nt", BF16, "ffn_out_dx")
    g_w2 = matmul(act, df, "tn", BF16, "ffn_out_dw")
    d_gu = swiglu(gu, d_act, "ffn_act_bwd")
    dh2 = matmul(d_gu, wb["ffn_w13"], "nt", BF16, "ffn_in_dx")
    g_w13 = matmul(h2, d_gu, "tn", BF16, "ffn_in_dw")
    dxa, d_out, d_gm, d_nffn, d_shf, d_scf = rowwise_vjp(f_res_mod, [x, out], [g_m, norm_ffn, sh_f, sc_f], [dx1a, dh2],
                                                         [F32, BF16], "res_norm_ffn_bwd")
    d_mixed = matmul(d_out, wb["w_o"], "nt", BF16, "out_proj_dx")
    g_wo = matmul(mixed, d_out, "tn", BF16, "out_proj_dw")
    dgs, dgr, do_s, do_r, d_bgs, d_bgr = rowwise_vjp(f_merge, mg_rows, [bg_s, bg_r], [d_mixed], [BF16] * 4, "merge_bwd")
    d_yn = matmul(do_s, wb["w_out_ssd"], "nt", BF16, "out_ssd_dx")
    g_wos = matmul(yn, do_s, "tn", BF16, "out_ssd_dw")
    d_oin = matmul(do_r, wb["w_out_lru"], "nt", BF16, "out_lru_dx")
    g_wol = matmul(o_in, do_r, "tn", BF16, "out_lru_dw")
    d_r_rm, d_yr = rowwise_vjp(f_gelu_gate, [r_rm, Cols(pg, D, 2)], [], [d_oin], [F32, BF16], "lru_gelu_bwd")
    dy, dxs_skip, dz, d_dexp, d_ssdn = rowwise_vjp(f_gnorm, gn_rows, [dexp, ssd_norm], [d_yn], [BF16, None, BF16, BF16], "ssd_gnorm_bwd")
    zfin = dict(zst)
    parts = {n: _shards_from_whole(g, n in BIG_BY_COLUMNS).astype(BF16)
             for n, g in (("ffn_w2", g_w2), ("ffn_w13", g_w13), ("w_o", g_wo), ("w_out_ssd", g_wos), ("w_out_lru", g_wol))}
    ride_f, ride_b = ["ffn_w2", "ffn_w13"], ["w_o", "w_out_ssd", "w_out_lru"]
    px, gx, dst, (land_f, land_b) = _seq_backward(
        svx, sp, lay, "x", grid_rows, dy, [dxs_skip], _to_col_major(d_r_rm, grid_rows), zfin,
        carries=(scatter_exchange([parts[n] for n in ride_f]), scatter_exchange([parts[n] for n in ride_b])))
    pc, gc, _, _ = _seq_backward(svc, sp, lay, "ctx", None, jnp.zeros((Lc, W), F32), [], jnp.zeros((Lc, D), F32), dst)
    dpg = _proj_cotangent(lay, px, pc, Lc, dz, d_yr, dgs, dgr)
    gg = [a + b for a, b in zip(gx["gate_pars"], gc["gate_pars"])]
    lru_w = jnp.concatenate([gg[0], gg[5], gg[2], gg[7]], axis=0).reshape(-1, PACK_W)
    g_wall, (lru_w_all,) = matmul(jnp.concatenate([hb, hcb], axis=0), dpg, "tn", BF16, "proj_dw", carry=allgather_exchange(lru_w))
    lru_w_all = fill_own_device_slot(lru_w_all, lru_w)
    xb0 = lay.xbc0
    g_w_in = jnp.concatenate([g_wall[:, :W], g_wall[:, xb0:xb0 + W + 2 * GN], _groups_to_heads(g_wall[:, lay.dt0:], hpg),
                              g_wall[:, D:2 * D], g_wall[:, 2 * D:3 * D]], axis=1)
    ride_p = ["w_in", "w_gate"]
    parts.update({n: _shards_from_whole(g, True).astype(BF16) for n, g in (("w_in", g_w_in), ("w_gate", g_wall[:, 3 * D:5 * D]))})
    dh, land_p = matmul(dpg, w_all, "nt", BF16, "proj_dx_x", a_rows=(0, L), carry=scatter_exchange([parts[n] for n in ride_p]))
    scattered = {n: (ld, own_piece(parts[n])) for n, ld in zip(ride_f + ride_b + ride_p, land_f + land_b + land_p)}
    dhc = matmul(dpg, w_all, "nt", BF16, "proj_dx_ctx", a_rows=(L, Lc))
    grad_x, d_nmix_x, d_shm, d_scm = rowwise_vjp(f_mod_in, [x], [norm_mix, sh_m, sc_m], [dh, dxa], [F32], "norm_mix_x_bwd")
    d_nmix_c, d_cshm, d_cscm = rowwise_vjp(f_mod_in, [ctx], [norm_mix, csh_m, csc_m], [dhc, jnp.zeros((Lc, D), F32)], [None], "norm_mix_ctx_bwd")

    dmod_x = jnp.concatenate([d_shm, d_scm, d_gm, d_shf, d_scf, d_gf], axis=1)
    dmod_c = jnp.concatenate([d_cshm, d_cscm, jnp.zeros((1, 4 * D), F32)], axis=1)

    d_a = _groups_to_heads((gx["a_g"] + gc["a_g"])[:, 0, :].reshape(-1), hpg).reshape(2, nh)

    def gate(k, shape):
        return jnp.stack([gg[k].reshape(shape), gg[5 + k].reshape(shape)], axis=0)

    grads = dict(
        norm_mix=(d_nmix_x + d_nmix_c).reshape(-1), norm_ffn=d_nffn.reshape(-1),
        ssd_conv_w=gx["ssd_conv_w"] + gc["ssd_conv_w"], ssd_conv_b=(gx["ssd_conv_b"] + gc["ssd_conv_b"]).reshape(-1),
        ssd_dt_bias=_groups_to_heads((gx["dt_bias_g"] + gc["dt_bias_g"]).reshape(-1), hpg).reshape(2, nh),
        ssd_a_log=d_a * a_neg, ssd_d=d_dexp.reshape(nh, HEAD_DIM).sum(axis=1), ssd_norm=d_ssdn.reshape(-1),
        lru_conv_w=gx["lru_conv_w"] + gc["lru_conv_w"], lru_conv_b=(gx["lru_conv_b"] + gc["lru_conv_b"]).reshape(-1),
        lru_b_a=gate(1, (D,)), lru_b_x=gate(3, (D,)), lru_lambda=gate(4, (D,)),
        b_gate=jnp.concatenate([d_bgs, d_bgr], axis=1).reshape(-1), final_norm=d_fnorm.reshape(-1))
    return loss, grad_x, grads, scattered, lru_w_all, dmod_x, dmod_c


WEIGHTS = ["c_ctx", "w_ada", "b_ada", "norm_mix", "norm_ffn", "w_in", "ssd_conv_w", "ssd_conv_b", "ssd_dt_bias", "ssd_a_log",
           "ssd_d", "ssd_norm", "w_out_ssd", "lru_conv_w", "lru_conv_b", "lru_w_a", "lru_b_a", "lru_w_x", "lru_b_x", "lru_lambda",
           "w_out_lru", "w_gate", "b_gate", "w_o", "ffn_w13", "ffn_w2", "final_norm"]
BIG = ["w_in", "w_out_ssd", "w_out_lru", "w_gate", "w_o", "ffn_w13", "ffn_w2"]
BIG_BY_COLUMNS = {"w_in", "w_gate", "ffn_w13"}
MOD_ROWS = 16
MOD_CTX_ROW = N_DEV
SMALL_SHARDED = ["ssd_conv_w", "lru_conv_w", "lru_b_a", "lru_b_x", "lru_lambda"]
ADAM_ROWS = 64


def _whole_from_shards(stack, by_columns):
    if by_columns:
        return stack.transpose(1, 0, 2).reshape(stack.shape[1], -1)
    return stack.reshape(-1, stack.shape[2])


def _shards_from_whole(g, by_columns):
    if by_columns:
        return g.reshape(g.shape[0], N_CHIP, -1).transpose(1, 0, 2)
    return g.reshape(N_CHIP, -1, g.shape[1])


def _adam(w, g, m, v, name):
    shape = w.shape
    cols = shape[-1]
    w2, g2, m2, v2 = [t.reshape(-1, cols) for t in (w, g, m, v)]
    d, nm, nv = rowwise(f_adam, [w2, g2, m2, v2], [], [(cols, F32)] * 3, name, tb=ADAM_ROWS, sub=8)
    return d.reshape(shape), nm.reshape(shape), nv.reshape(shape)


def kernel(x, c, ctx, c_ctx, w_ada, b_ada, norm_mix, norm_ffn, w_in, ssd_conv_w, ssd_conv_b, ssd_dt_bias, ssd_a_log, ssd_d, ssd_norm, w_out_ssd, lru_conv_w, lru_conv_b, lru_w_a, lru_b_a, lru_w_x, lru_b_x, lru_lambda, w_out_lru, w_gate, b_gate, w_o, ffn_w13, ffn_w2, final_norm, loss_target, m_c_ctx, m_w_ada, m_b_ada, m_norm_mix, m_norm_ffn, m_w_in, m_ssd_conv_w, m_ssd_conv_b, m_ssd_dt_bias, m_ssd_a_log, m_ssd_d, m_ssd_norm, m_w_out_ssd, m_lru_conv_w, m_lru_conv_b, m_lru_w_a, m_lru_b_a, m_lru_w_x, m_lru_b_x, m_lru_lambda, m_w_out_lru, m_w_gate, m_b_gate, m_w_o, m_ffn_w13, m_ffn_w2, m_final_norm, v_c_ctx, v_w_ada, v_b_ada, v_norm_mix, v_norm_ffn, v_w_in, v_ssd_conv_w, v_ssd_conv_b, v_ssd_dt_bias, v_ssd_a_log, v_ssd_d, v_ssd_norm, v_w_out_ssd, v_lru_conv_w, v_lru_conv_b, v_lru_w_a, v_lru_b_a, v_lru_w_x, v_lru_b_x, v_lru_lambda, v_w_out_lru, v_w_gate, v_b_gate, v_w_o, v_ffn_w13, v_ffn_w2, v_final_norm):
    given = dict(locals())
    layered = {n for n in WEIGHTS if n not in ("c_ctx", "final_norm")}
    w_blk = {n: (given[n][0] if n in layered else given[n]) for n in WEIGHTS}
    me = _me()
    chip, dev = _chip_index(me), _dev_index(me)
    D = c.shape[1]

    ada_w = w_blk["w_ada"].astype(BF16)
    ada_n = ada_w.shape[1]
    c_all = allgather8(jnp.concatenate([c, jnp.zeros((SUBLANES - 1, D), F32)], axis=0), "gather_c")[:, 0, :]
    cc = jnp.concatenate([c_all, c_ctx[None], jnp.zeros((MOD_ROWS - N_DEV - 1, D), F32)], axis=0)
    s_cc = rowwise(f_silu, [cc], [], [(D, BF16)], "mod_silu", tb=MOD_ROWS)[0]
    ada_b = lax.dynamic_slice_in_dim(w_blk["b_ada"], chip * ada_n, ada_n)[None]
    mod_blk = rowwise(f_add_bias, [matmul(s_cc, ada_w, "nn", F32, "mod_proj")], [ada_b], [(ada_n, F32)], "mod_bias", tb=MOD_ROWS)[0]
    mod = allgather8(mod_blk, "gather_mod")[::2].transpose(1, 0, 2).reshape(MOD_ROWS, N_CHIP * ada_n)
    mod_x = lax.dynamic_slice_in_dim(mod, dev, 1, axis=0)
    mod_c = mod[MOD_CTX_ROW:MOD_CTX_ROW + 1]

    shard16 = {n: w_blk[n].astype(BF16) for n in BIG}
    stacks = run_exchange(gather_exchange([shard16[n] for n in EARLY]), "gather_early")
    wb = {n: _whole_from_shards(fill_own_slot(st, shard16[n]), n in BIG_BY_COLUMNS) for n, st in zip(EARLY, stacks)}

    sm_shapes = [w_blk[n].shape for n in SMALL_SHARDED]
    sm_all = allgather8(_pack([w_blk[n] for n in SMALL_SHARDED], F32, ROW_TILE_16BIT), "gather_small")[::2]
    ws = {n: w_blk[n] for n in WEIGHTS if n not in BIG and n != "w_ada"}
    for n, s in zip(SMALL_SHARDED, _unpack_stacked(sm_all, sm_shapes)):
        ws[n] = jnp.concatenate([s[k] for k in range(N_CHIP)], axis=-1)

    loss, grad_x, grads, scattered, lru_w_all, dmod_x, dmod_c = _local_step(x[0], ctx[0], loss_target[0], mod_x, mod_c, wb,
                                                                            {n: shard16[n] for n in LATE}, ws)
    loss = lax.psum(loss, ("x", "y", "c"))

    dm = allgather8(jnp.concatenate([dmod_x, dmod_c, jnp.zeros((SUBLANES - 2, 6 * D), F32)], axis=0), "gather_dmod")
    dm_ctx = sum_slots(dm, "sum_dmod")[1:2]
    dmod = jnp.concatenate([dm[:, 0, :], dm_ctx, jnp.zeros((MOD_ROWS - N_DEV - 1, 6 * D), F32)], axis=0)
    g_shard = {"b_ada": rowwise_vjp(f_add_bias, [dmod], [w_blk["b_ada"][None]], [dmod], [None], "mod_bias_bwd", tb=MOD_ROWS)[0].reshape(-1)}
    d_blk = lax.dynamic_slice_in_dim(dmod, chip * ada_n, ada_n, axis=1).astype(BF16)
    g_shard["w_ada"] = matmul(s_cc, d_blk, "tn", F32, "mod_proj_dw")
    d_silu_ctx = matmul(d_blk, ada_w, "nt", F32, "mod_proj_dx")[MOD_CTX_ROW]
    grads["c_ctx_silu"] = 0.5 * d_silu_ctx

    for n in BIG:
        landed, own = scattered[n]
        g_shard[n] = join_halves(sum_slots(landed, "sum_" + n, first=own), "join_" + n)

    small = [n for n in WEIGHTS if n not in BIG and n != "w_ada"]
    lru_w = sum_slots(lru_w_all, "sum_lru_w").reshape((4,) + w_blk["lru_w_a"].shape[1:])
    g_shard["lru_w_a"], g_shard["lru_w_x"] = lru_w[:2], lru_w[2:]
    reduced = [n for n in small if n not in ("b_ada", "c_ctx", "lru_w_a", "lru_w_x")] + ["c_ctx_silu"]
    sm_g = sum_slots(allgather8(_pack([grads[n] for n in reduced], F32, ROW_TILE_16BIT), "gather_small_grads"), "sum_small_grads")
    for n, g in zip(reduced, _unpack(sm_g, [grads[n].shape for n in reduced])):
        if n in SMALL_SHARDED:
            per = g.shape[-1] // N_CHIP
            g = lax.dynamic_slice_in_dim(g, chip * per, per, axis=g.ndim - 1)
        g_shard[n] = g
    ctx_rows = jnp.concatenate([c_ctx[None], jnp.zeros((SUBLANES - 1, D), F32)], axis=0)
    ctx_cot = jnp.concatenate([g_shard.pop("c_ctx_silu")[None], jnp.zeros((SUBLANES - 1, D), F32)], axis=0)
    g_shard["c_ctx"] = rowwise_vjp(f_silu, [ctx_rows], [], [ctx_cot], [F32], "mod_silu_bwd", tb=SUBLANES, sub=SUBLANES)[0][0]

    out_g, out_d, out_m, out_v = {}, {}, {}, {}
    for n in BIG + ["w_ada"]:
        out_g[n] = g_shard[n]
        out_d[n], out_m[n], out_v[n] = _adam(w_blk[n], g_shard[n], given["m_" + n][0], given["v_" + n][0], "adamw_" + n)
    sm_blk_shapes = [w_blk[n].shape for n in small]
    packed = [_pack([t[n].reshape(w_blk[n].shape) for n in small], F32, ADAM_ROWS)
              for t in (w_blk, g_shard, {n: given["m_" + n] for n in small}, {n: given["v_" + n] for n in small})]
    res = _adam(*packed, "adamw_small")
    for tgt, buf in zip((out_d, out_m, out_v), res):
        tgt.update(zip(small, _unpack(buf, sm_blk_shapes)))
    for n in small:
        out_g[n] = g_shard[n].reshape(w_blk[n].shape)

    def full(n, t):
        return t.reshape(given[n].shape)

    return (loss, grad_x[None], *[full(n, out_g[n]) for n in WEIGHTS], *[full(n, out_d[n]) for n in WEIGHTS],
            *[full(n, out_m[n]) for n in WEIGHTS], *[full(n, out_v[n]) for n in WEIGHTS])


def _unpack_stacked(buf, shapes):
    k = buf.shape[0]
    flat = buf.reshape(k, -1)
    out, at = [], 0
    for s in shapes:
        n = 1
        for d in s:
            n *= d
        out.append(flat[:, at:at + n].reshape((k,) + tuple(s)))
        at += n
    return out
```

```python
import functools

import jax
import jax.numpy as jnp
from jax import lax
from jax.experimental import pallas as pl
from jax.experimental.pallas import tpu as pltpu

F32 = jnp.float32
BF16 = jnp.bfloat16
MESH = pl.DeviceIdType.MESH

V7X_VMEM_LIMIT_BYTES = 56 * 1024 * 1024
LANES = 128
SUBLANES = 8

HEAD_DIM = 64
SSD_STATE = 128
SSD_GROUPS = 4
SSD_CHUNK = 128
GRID_W = 64
LRU_HEAD = 128
LRU_C = 8.0
EPS = 1e-6
NEG = -1e30

ADAM_LR = 0.001
ADAM_B1 = 0.9
ADAM_B2 = 0.999
ADAM_EPS = 1e-08
ADAM_WD = 0.01
ADAM_STEP = 10


def _cparams(sem=None, **kw):
    if sem is not None:
        kw["dimension_semantics"] = sem
    return pltpu.CompilerParams(vmem_limit_bytes=V7X_VMEM_LIMIT_BYTES, **kw)


_DN = {"nn": (((1,), (0,)), ((), ())), "nt": (((1,), (1,)), ((), ())), "tn": (((0,), (0,)), ((), ()))}


def _raw_dot(a, b, form, precision=None):
    return lax.dot_general(a, b, _DN[form], precision=precision, preferred_element_type=F32)


def _dot_bwd_forms(form):
    return {"nn": (("g", "b", "nt"), ("a", "g", "tn")),
            "nt": (("g", "b", "nn"), ("g", "a", "tn")),
            "tn": (("b", "g", "nt"), ("a", "g", "nn"))}[form]


def _make_dot(rounding):
    @functools.partial(jax.custom_vjp, nondiff_argnums=(2,))
    def dot(a, b, form):
        if rounding is None:
            return _raw_dot(a, b, form, precision=lax.Precision.HIGHEST)
        return _raw_dot(a.astype(rounding), b.astype(rounding), form)

    def fwd(a, b, form):
        return dot(a, b, form), (a, b)

    def bwd(form, res, g):
        a, b = res
        ops = {"a": a, "b": b, "g": g}
        (l1, r1, f1), (l2, r2, f2) = _dot_bwd_forms(form)
        return dot(ops[l1], ops[r1], f1).astype(a.dtype), dot(ops[l2], ops[r2], f2).astype(b.dtype)

    dot.defvjp(fwd, bwd)
    return dot


bdot = _make_dot(BF16)
hdot = _make_dot(None)


def _sigmoid(v):
    return 0.5 * jnp.tanh(0.5 * v) + 0.5


def _silu(v):
    return v * _sigmoid(v)


def _softplus(v):
    return jnp.maximum(v, 0.0) + jnp.log1p(jnp.exp(-jnp.abs(v)))


def _gelu_tanh(v):
    return 0.5 * v * (1.0 + jnp.tanh(0.7978845608028654 * (v + 0.044715 * v * v * v)))


def _rms(t, gain):
    return t * lax.rsqrt(jnp.mean(t * t, axis=-1, keepdims=True) + EPS) * gain


def _pick(dim, want, mult=1):
    if dim <= want:
        return dim
    for t in range(want - want % mult, 0, -mult):
        if dim % t == 0:
            return t
    t = min(dim, want)
    while dim % t:
        t //= 2
    return t


MM_TILE_M = 1024
MM_TILE_N = 1536
MM_VMEM_BYTES = 46 * 1024 * 1024


def matmul(a, b, form, out_dtype, name, tm=MM_TILE_M, tn=MM_TILE_N, tk=None, a_rows=None, carry=None):
    if form == "nn":
        (m, k), (k2, n) = a.shape, b.shape
    elif form == "nt":
        (m, k), (n, k2) = a.shape, b.shape
    else:
        (k, m), (k2, n) = a.shape, b.shape
    assert k == k2, (a.shape, b.shape, form)
    row0 = 0
    if a_rows is not None:
        assert form != "tn"
        row0, m = a_rows
    tm, tn = _pick(m, tm, LANES), _pick(n, tn, LANES)
    assert row0 % tm == 0
    blk0 = row0 // tm
    if tk is None:
        fixed = tm * tn * (2 * jnp.dtype(out_dtype).itemsize + 4)
        tk = (MM_VMEM_BYTES - fixed) // (2 * a.dtype.itemsize * (tm + tn))
    tk = _pick(k, tk, LANES)
    nk = k // tk

    grid = (m // tm, n // tn, nk)
    n_ci, n_co = (len(carry.ins), len(carry.outs)) if carry else (0, 0)

    def body(a_ref, b_ref, *rest):
        c_ins, o_ref, c_outs = rest[:n_ci], rest[n_ci], rest[n_ci + 1:n_ci + 1 + n_co]
        scratch = rest[n_ci + 1 + n_co:]
        kk = pl.program_id(2)
        if carry:
            at_step = [pl.program_id(d) for d in range(3)]
            sems = tuple(scratch[-2:])
            pl.when((at_step[0] == 0) & (at_step[1] == 0) & (at_step[2] == 0))(lambda: carry.start(c_ins, c_outs, sems))
        part = _raw_dot(a_ref[...], b_ref[...], form)
        if nk == 1:
            o_ref[...] = part.astype(o_ref.dtype)
        else:
            acc = scratch[0]

            @pl.when(kk == 0)
            def _():
                acc[...] = part

            @pl.when(kk > 0)
            def _():
                acc[...] += part

            @pl.when(kk == nk - 1)
            def _():
                o_ref[...] = acc[...].astype(o_ref.dtype)

        if carry:
            last = (at_step[0] == grid[0] - 1) & (at_step[1] == grid[1] - 1) & (at_step[2] == grid[2] - 1)
            pl.when(last)(lambda: carry.finish(c_ins, c_outs, sems))

    a_spec = pl.BlockSpec((tk, tm), lambda i, j, kk: (kk, i)) if form == "tn" else pl.BlockSpec((tm, tk), lambda i, j, kk: (blk0 + i, kk))
    b_spec = pl.BlockSpec((tn, tk), lambda i, j, kk: (j, kk)) if form == "nt" else pl.BlockSpec((tk, tn), lambda i, j, kk: (kk, j))
    res = pl.pallas_call(
        body, name=name, grid=grid, in_specs=[a_spec, b_spec] + [_HBM] * n_ci,
        out_specs=[pl.BlockSpec((tm, tn), lambda i, j, kk: (i, j))] + [_HBM] * n_co,
        out_shape=[jax.ShapeDtypeStruct((m, n), out_dtype)] + (carry.outs if carry else []),
        scratch_shapes=([pltpu.VMEM((tm, tn), F32)] if nk > 1 else []) + (carry.sem_shapes() if carry else []),
        compiler_params=_cparams(("arbitrary",) * 3 if carry else ("parallel", "parallel", "arbitrary")),
    )(a, b, *(carry.ins if carry else []))
    return (res[0], list(res[1:])) if carry else res[0]


class Cols:
    def __init__(self, arr, w, j):
        assert (j + 1) * w <= arr.shape[1]
        self.arr, self.w, self.j = arr, w, j


def _row_width(x, ncol):
    return x.w if isinstance(x, Cols) else x.shape[1] // ncol


def _row_spec(x, tb, ncol):
    if isinstance(x, Cols):
        j0 = x.j
        return x.arr, pl.BlockSpec((tb, x.w), lambda j, i: (i, j0 + j))
    return x, pl.BlockSpec((tb, x.shape[1] // ncol), lambda j, i: (i, j))


def _par_spec(shape, ncol):
    if ncol == 1:
        return pl.BlockSpec(shape, lambda j, i, nd=len(shape): (0,) * nd)
    assert len(shape) == 3 and shape[0] == ncol, shape
    return pl.BlockSpec((1,) + tuple(shape[1:]), lambda j, i: (j, 0, 0))


def _par_value(ref, ncol):
    return ref[...] if ncol == 1 else ref[0]


def rowwise(fn, rows, pars, outs, name, tb=256, sub=16, ncol=1):
    m = (rows[0].arr if isinstance(rows[0], Cols) else rows[0]).shape[0]
    tb = _pick(m, tb)
    sub = min(sub, tb)
    nr, npar = len(rows), len(pars)

    def body(*refs):
        row_refs, par_refs, out_refs = refs[:nr], refs[nr:nr + npar], refs[nr + npar:]
        pv = [_par_value(p, ncol) for p in par_refs]

        def step(i, carry):
            sl = pl.ds(pl.multiple_of(i * sub, sub), sub)
            res = fn(*[r[sl, :].astype(F32) for r in row_refs], *pv)
            for o, v in zip(out_refs, res):
                o[sl, :] = v.astype(o.dtype)
            return carry

        lax.fori_loop(0, tb // sub, step, 0)

    arrs, specs = zip(*[_row_spec(r, tb, ncol) for r in rows])
    return pl.pallas_call(
        body, name=name, grid=(ncol, m // tb),
        in_specs=list(specs) + [_par_spec(p.shape, ncol) for p in pars],
        out_specs=[pl.BlockSpec((tb, w), lambda j, i: (i, j)) for w, _ in outs],
        out_shape=[jax.ShapeDtypeStruct((m, w * ncol), dt) for w, dt in outs],
        compiler_params=_cparams(("parallel", "parallel")),
    )(*arrs, *pars)


def rowwise_vjp(fn, rows, pars, cots, drow_dtypes, name, tb=256, sub=16, out_sums=(), ncol=1):
    m = (rows[0].arr if isinstance(rows[0], Cols) else rows[0]).shape[0]
    tb = _pick(m, tb)
    sub = min(sub, tb)
    cots = [list(c) if isinstance(c, (list, tuple)) else [c] for c in cots]
    flat = [c for group in cots for c in group]
    nr, npar, nc = len(rows), len(pars), len(flat)
    want = [i for i, d in enumerate(drow_dtypes) if d is not None]
    assert ncol == 1 or not out_sums

    def body(*refs):
        row_refs, par_refs = refs[:nr], refs[nr:nr + npar]
        cot_refs = list(refs[nr + npar:nr + npar + nc])
        drow_refs = refs[nr + npar + nc:nr + npar + nc + len(want)]
        acc_refs = refs[nr + npar + nc + len(want):]
        pv = [_par_value(p, ncol) for p in par_refs]

        def step(i, acc):
            sl = pl.ds(pl.multiple_of(i * sub, sub), sub)
            rv = [r[sl, :].astype(F32) for r in row_refs]
            res, vjp = jax.vjp(lambda rr, pp: tuple(fn(*rr, *pp)), rv, pv)
            ct, at = [], 0
            for group in cots:
                ct.append(sum(c[sl, :].astype(F32) for c in cot_refs[at:at + len(group)]))
                at += len(group)
            d_rows, d_pars = vjp(tuple(ct))
            for o, idx in zip(drow_refs, want):
                o[sl, :] = d_rows[idx].astype(o.dtype)
            sums = [jnp.sum(res[k], axis=0, keepdims=True) for k, _ in out_sums]
            return tuple(a + d for a, d in zip(acc, list(d_pars) + sums))

        init = tuple(jnp.zeros(p.shape, F32) for p in pv) + tuple(jnp.zeros((1, w), F32) for _, w in out_sums)
        acc = lax.fori_loop(0, tb // sub, step, init)

        @pl.when(pl.program_id(1) == 0)
        def _():
            for o in acc_refs:
                o[...] = jnp.zeros_like(o)

        for o, a in zip(acc_refs, acc):
            if ncol == 1:
                o[...] += a
            else:
                o[0] += a

    arrs, specs = zip(*[_row_spec(r, tb, ncol) for r in rows])
    carrs, cspecs = zip(*[_row_spec(c, tb, ncol) for c in flat])
    widths = [_row_width(r, ncol) for r in rows]
    acc_shapes = [tuple(p.shape) for p in pars] + [(1, w) for _, w in out_sums]
    return pl.pallas_call(
        body, name=name, grid=(ncol, m // tb),
        in_specs=list(specs) + [_par_spec(p.shape, ncol) for p in pars] + list(cspecs),
        out_specs=[pl.BlockSpec((tb, widths[i]), lambda j, i_: (i_, j)) for i in want]
        + [_par_spec(s, ncol) for s in acc_shapes],
        out_shape=[jax.ShapeDtypeStruct((m, widths[i] * ncol), drow_dtypes[i]) for i in want]
        + [jax.ShapeDtypeStruct(s, F32) for s in acc_shapes],
        compiler_params=_cparams(("parallel", "arbitrary")),
    )(*arrs, *pars, *carrs)


def _col(v, c):
    lane = lax.broadcasted_iota(jnp.int32, v.shape, 1)
    return jnp.sum(jnp.where(lane == c, v, 0.0), axis=1, keepdims=True)


def _row(v, r):
    sub = lax.broadcasted_iota(jnp.int32, v.shape, 0)
    return jnp.sum(jnp.where(sub == r, v, 0.0), axis=0, keepdims=True)


def _ssd_chunk(xs, dt, bm, cm, hs, a_row, rev, col0):
    q = dt.shape[0]
    ii = lax.broadcasted_iota(jnp.int32, (q, q), 0)
    jj = lax.broadcasted_iota(jnp.int32, (q, q), 1)
    keep = (jj >= ii) if rev else (jj <= ii)
    tri = keep.astype(F32)
    dta = dt * a_row
    a_cum = hdot(tri, dta, "nn")
    tri_t = ((jj <= ii) if rev else (jj >= ii)).astype(F32)
    a_cum_t = hdot(dta, tri_t, "tn")
    cb = bdot(cm, bm, "nt")
    last = 0 if rev else q - 1
    ys, hn = [], []
    for e, (x, h) in enumerate(zip(xs, hs)):
        c = col0 + e
        ac = _col(a_cum, c)
        seg = ac - _row(a_cum_t, c)
        lm = jnp.exp(jnp.where(keep, seg, NEG))
        xdt = x * _col(dt, c)
        y_diag = bdot(cb * lm, xdt, "nn")
        y_off = bdot(cm * jnp.exp(ac), h, "nt")
        ys.append(y_diag + y_off)
        tot = _row(ac, last)
        states = bdot(xdt, bm * jnp.exp(tot - ac), "tn")
        hn.append(jnp.exp(tot) * h + states)
    return ys, hn


def _ssd_specs(nc, hpg, w_ssd, rev_order):
    q = SSD_CHUNK
    wx = hpg * HEAD_DIM
    boff = w_ssd // LANES

    def cidx(c):
        return nc - 1 - c if rev_order else c

    x_spec = pl.BlockSpec((q, wx), lambda g, c: (cidx(c), g))
    dt_spec = pl.BlockSpec((q, LANES), lambda g, c: (cidx(c), g))
    b_spec = pl.BlockSpec((q, LANES), lambda g, c: (cidx(c), boff + g))
    c_spec = pl.BlockSpec((q, LANES), lambda g, c: (cidx(c), boff + SSD_GROUPS + g))
    a_spec = pl.BlockSpec((1, SUBLANES, LANES), lambda g, c: (g, 0, 0))
    st_spec = pl.BlockSpec((1, wx, LANES), lambda g, c: (g, 0, 0))
    ent_spec = pl.BlockSpec((1, 1, wx, LANES), lambda g, c: (g, cidx(c), 0, 0))
    return x_spec, dt_spec, b_spec, c_spec, a_spec, st_spec, ent_spec


def ssd_fwd(xbc, dtg, a_g, h0, rev, w_ssd, name):
    L = xbc.shape[0]
    nc = L // SSD_CHUNK
    hpg = w_ssd // (SSD_GROUPS * HEAD_DIM)
    wx = hpg * HEAD_DIM
    col0 = hpg if rev else 0
    x_spec, dt_spec, b_spec, c_spec, a_spec, st_spec, ent_spec = _ssd_specs(nc, hpg, w_ssd, rev)

    def body(x_ref, dt_ref, b_ref, c_ref, a_ref, h0_ref, y_ref, ent_ref, fin_ref, hs):
        c = pl.program_id(1)

        @pl.when(c == 0)
        def _():
            hs[...] = h0_ref[0]

        ent_ref[0, 0] = hs[...]
        xs = [x_ref[:, e * HEAD_DIM:(e + 1) * HEAD_DIM].astype(F32) for e in range(hpg)]
        hin = [hs[e * HEAD_DIM:(e + 1) * HEAD_DIM, :] for e in range(hpg)]
        ys, hn = _ssd_chunk(xs, dt_ref[...], b_ref[...].astype(F32), c_ref[...].astype(F32), hin, a_ref[0, 0:1, :], rev, col0)
        for e in range(hpg):
            y_ref[:, e * HEAD_DIM:(e + 1) * HEAD_DIM] = ys[e].astype(y_ref.dtype)
            hs[e * HEAD_DIM:(e + 1) * HEAD_DIM, :] = hn[e]
        fin_ref[0] = hs[...]

    return pl.pallas_call(
        body, name=name, grid=(SSD_GROUPS, nc),
        in_specs=[x_spec, dt_spec, b_spec, c_spec, a_spec, st_spec],
        out_specs=[x_spec, ent_spec, st_spec],
        out_shape=[jax.ShapeDtypeStruct((L, w_ssd), BF16),
                   jax.ShapeDtypeStruct((SSD_GROUPS, nc, wx, LANES), F32),
                   jax.ShapeDtypeStruct((SSD_GROUPS, wx, LANES), F32)],
        scratch_shapes=[pltpu.VMEM((wx, LANES), F32)],
        compiler_params=_cparams(("parallel", "arbitrary")),
    )(xbc, dtg, xbc, xbc, a_g, h0)


def ssd_bwd(xbc, dtg, a_g, ent, dy, dfin, rev, w_ssd, name, carry=None):
    L = xbc.shape[0]
    nc = L // SSD_CHUNK
    hpg = w_ssd // (SSD_GROUPS * HEAD_DIM)
    wx = hpg * HEAD_DIM
    col0 = hpg if rev else 0
    x_spec, dt_spec, b_spec, c_spec, a_spec, st_spec, ent_spec = _ssd_specs(nc, hpg, w_ssd, not rev)
    bc_out = pl.BlockSpec((SSD_CHUNK, LANES), lambda g, c: ((c if rev else nc - 1 - c), g))
    n_ci, n_co = (len(carry.ins), len(carry.outs)) if carry else (0, 0)

    def body(x_ref, dt_ref, b_ref, c_ref, a_ref, ent_ref, dy_ref, dfin_ref, *rest):
        c_ins = rest[:n_ci]
        dx_ref, ddt_ref, db_ref, dc_ref, da_ref, dh0_ref = rest[n_ci:n_ci + 6]
        c_outs = rest[n_ci + 6:n_ci + 6 + n_co]
        dhs = rest[n_ci + 6 + n_co]
        sems = tuple(rest[n_ci + 7 + n_co:])
        g, c = pl.program_id(0), pl.program_id(1)
        if carry:
            pl.when((g == 0) & (c == 0))(lambda: carry.start(c_ins, c_outs, sems))

        @pl.when(c == 0)
        def _():
            dhs[...] = dfin_ref[0]
            da_ref[...] = jnp.zeros_like(da_ref)

        halves = [range(0, hpg // 2), range(hpg // 2, hpg)] if hpg % 2 == 0 else [range(hpg)]
        for part, heads in enumerate(halves):
            xs = [x_ref[:, e * HEAD_DIM:(e + 1) * HEAD_DIM].astype(F32) for e in heads]
            hin = [ent_ref[0, 0, e * HEAD_DIM:(e + 1) * HEAD_DIM, :] for e in heads]
            first = col0 + heads[0]
            _, vjp = jax.vjp(lambda *a: _ssd_chunk(*a, rev, first), xs, dt_ref[...], b_ref[...].astype(F32), c_ref[...].astype(F32), hin,
                             a_ref[0, 0:1, :])
            dys = [dy_ref[:, e * HEAD_DIM:(e + 1) * HEAD_DIM].astype(F32) for e in heads]
            dhn = [dhs[e * HEAD_DIM:(e + 1) * HEAD_DIM, :] for e in heads]
            dxs, ddt, db, dc, dh, da = vjp((dys, dhn))
            for k, e in enumerate(heads):
                dx_ref[:, e * HEAD_DIM:(e + 1) * HEAD_DIM] = dxs[k].astype(dx_ref.dtype)
                dhs[e * HEAD_DIM:(e + 1) * HEAD_DIM, :] = dh[k]
            if part == 0:
                ddt_ref[...] = ddt
                db_ref[...] = db
                dc_ref[...] = dc
            else:
                ddt_ref[...] += ddt
                db_ref[...] += db
                dc_ref[...] += dc
            da_ref[0, 0:1, :] += da
        dh0_ref[0] = dhs[...]
        if carry:
            pl.when((g == SSD_GROUPS - 1) & (c == nc - 1))(lambda: carry.finish(c_ins, c_outs, sems))

    res = pl.pallas_call(
        body, name=name, grid=(SSD_GROUPS, nc),
        in_specs=[x_spec, dt_spec, b_spec, c_spec, a_spec, ent_spec, x_spec, st_spec] + [_HBM] * n_ci,
        out_specs=[x_spec, dt_spec, bc_out, bc_out, a_spec, st_spec] + [_HBM] * n_co,
        out_shape=[jax.ShapeDtypeStruct((L, w_ssd), BF16),
                   jax.ShapeDtypeStruct((L, SSD_GROUPS * LANES), F32),
                   jax.ShapeDtypeStruct((L, SSD_GROUPS * LANES), F32),
                   jax.ShapeDtypeStruct((L, SSD_GROUPS * LANES), F32),
                   jax.ShapeDtypeStruct((SSD_GROUPS, SUBLANES, LANES), F32),
                   jax.ShapeDtypeStruct((SSD_GROUPS, wx, LANES), F32)] + (carry.outs if carry else []),
        scratch_shapes=[pltpu.VMEM((wx, LANES), F32)] + (carry.sem_shapes() if carry else []),
        compiler_params=_cparams(("arbitrary", "arbitrary") if carry else ("parallel", "arbitrary")),
    )(xbc, dtg, xbc, xbc, a_g, ent, dy, dfin, *(carry.ins if carry else []))
    return (list(res[:6]), list(res[6:])) if carry else res


CONV_W = 4
CONV_LEFT = 2
CONV_CB = 128
CONV_ROWS = 256
HALO = 16


def _conv_window(ref, r, rows, nch, L):
    s = pl.multiple_of(r * rows, rows)
    cur = ref[pl.ds(s, rows), :].astype(F32)
    sp = pl.multiple_of(jnp.maximum(s - HALO, 0), HALO)
    sn = pl.multiple_of(jnp.minimum(s + rows, L - HALO), HALO)
    prev = jnp.where(r > 0, ref[pl.ds(sp, HALO), :].astype(F32), 0.0)
    nxt = jnp.where(r < nch - 1, ref[pl.ds(sn, HALO), :].astype(F32), 0.0)
    return jnp.concatenate([prev, cur, nxt], axis=0)


def _shifted(win, off, rows):
    n = win.shape[0]
    return pltpu.roll(win, (-off) % n, axis=0)[HALO:HALO + rows, :]


def _dsilu(p):
    s = _sigmoid(p)
    return s * (1.0 + p * (1.0 - s))


def conv_fwd(src, col_blk0, w, b, act, name, out_dtype=F32):
    L, C = src.shape[0], w.shape[1]
    rows = min(CONV_ROWS, L)
    nch = L // rows

    def body(x_ref, w_ref, b_ref, o_ref):
        def chunk(r, carry):
            win = _conv_window(x_ref, r, rows, nch, L)
            pre = b_ref[...] + sum(w_ref[k:k + 1, :] * _shifted(win, k - CONV_LEFT, rows) for k in range(CONV_W))
            o_ref[pl.ds(pl.multiple_of(r * rows, rows), rows), :] = (_silu(pre) if act else pre).astype(o_ref.dtype)
            return carry

        lax.fori_loop(0, nch, chunk, 0)

    return pl.pallas_call(
        body, name=name, grid=(C // CONV_CB,),
        in_specs=[pl.BlockSpec((L, CONV_CB), lambda j: (0, col_blk0 + j)),
                  pl.BlockSpec((CONV_W, CONV_CB), lambda j: (0, j)),
                  pl.BlockSpec((1, CONV_CB), lambda j: (0, j))],
        out_specs=pl.BlockSpec((L, CONV_CB), lambda j: (0, j)),
        out_shape=jax.ShapeDtypeStruct((L, C), out_dtype),
        compiler_params=_cparams(("parallel",)),
    )(src, w, b)


def conv_bwd(src, col_blk0, w, b, douts, act, name):
    L, C = src.shape[0], w.shape[1]
    rows = min(CONV_ROWS, L)
    nch = L // rows
    nd = len(douts)

    def body(*refs):
        x_ref, w_ref, b_ref = refs[:3]
        d_refs = refs[3:3 + nd]
        dx_ref, dw_ref, db_ref, dp = refs[3 + nd:]

        def pass1(r, acc):
            sl = pl.ds(pl.multiple_of(r * rows, rows), rows)
            win = _conv_window(x_ref, r, rows, nch, L)
            taps = [_shifted(win, k - CONV_LEFT, rows) for k in range(CONV_W)]
            dpre = sum(d[sl, :].astype(F32) for d in d_refs)
            if act:
                pre = b_ref[...] + sum(w_ref[k:k + 1, :] * taps[k] for k in range(CONV_W))
                dpre = dpre * _dsilu(pre)
            dp[sl, :] = dpre
            new = [acc[k] + jnp.sum(dpre * taps[k], axis=0, keepdims=True) for k in range(CONV_W)]
            return tuple(new) + (acc[CONV_W] + jnp.sum(dpre, axis=0, keepdims=True),)

        zero = jnp.zeros((1, CONV_CB), F32)
        acc = lax.fori_loop(0, nch, pass1, (zero,) * (CONV_W + 1))
        for k in range(CONV_W):
            dw_ref[k:k + 1, :] = acc[k]
        db_ref[...] = acc[CONV_W]

        def pass2(r, carry):
            win = _conv_window(dp, r, rows, nch, L)
            dx = sum(w_ref[k:k + 1, :] * _shifted(win, CONV_LEFT - k, rows) for k in range(CONV_W))
            dx_ref[pl.ds(pl.multiple_of(r * rows, rows), rows), :] = dx.astype(dx_ref.dtype)
            return carry

        lax.fori_loop(0, nch, pass2, 0)

    col = pl.BlockSpec((L, CONV_CB), lambda j: (0, j))
    return pl.pallas_call(
        body, name=name, grid=(C // CONV_CB,),
        in_specs=[pl.BlockSpec((L, CONV_CB), lambda j: (0, col_blk0 + j)),
                  pl.BlockSpec((CONV_W, CONV_CB), lambda j: (0, j)),
                  pl.BlockSpec((1, CONV_CB), lambda j: (0, j))] + [col] * nd,
        out_specs=[col, pl.BlockSpec((CONV_W, CONV_CB), lambda j: (0, j)), pl.BlockSpec((1, CONV_CB), lambda j: (0, j))],
        out_shape=[jax.ShapeDtypeStruct((L, C), BF16), jax.ShapeDtypeStruct((CONV_W, C), F32), jax.ShapeDtypeStruct((1, C), F32)],
        scratch_shapes=[pltpu.VMEM((L, CONV_CB), F32)],
        compiler_params=_cparams(("parallel",)),
    )(src, w, b, *douts)


SCAN_ROWS = 512
SCAN_CB = 512


def _scan8(a, u, rev):
    sub = lax.broadcasted_iota(jnp.int32, a.shape, 0)
    for s in (1, 2, 4):
        if rev:
            ok = sub < SUBLANES - s
            a_sh, u_sh = pltpu.roll(a, SUBLANES - s, axis=0), pltpu.roll(u, SUBLANES - s, axis=0)
        else:
            ok = sub >= s
            a_sh, u_sh = pltpu.roll(a, s, axis=0), pltpu.roll(u, s, axis=0)
        u = a * jnp.where(ok, u_sh, 0.0) + u
        a = a * jnp.where(ok, a_sh, 1.0)
    return a, u


def _shift8(h, carry, rev):
    sub = lax.broadcasted_iota(jnp.int32, h.shape, 0)
    if rev:
        return jnp.where(sub == SUBLANES - 1, carry, pltpu.roll(h, SUBLANES - 1, axis=0))
    return jnp.where(sub == 0, carry, pltpu.roll(h, 1, axis=0))


def lru_scan(a, u, c0, rev, name, adjoint_of=None, add=None):
    L, C = a.shape
    tc, cb = _pick(L, SCAN_ROWS), _pick(C, SCAN_CB)
    nt = L // tc
    ng = tc // SUBLANES
    adj = adjoint_of is not None
    n_in = 3 + (1 if adj else 0) + (1 if add is not None else 0)

    def body(*refs):
        a_ref, u_ref, c0_ref = refs[:3]
        x_ref = refs[3] if (adj or add is not None) else None
        outs = refs[n_in:-1]
        st = refs[-1]

        @pl.when(pl.program_id(1) == 0)
        def _():
            st[...] = c0_ref[...]

        def group(i, carry):
            g = (ng - 1 - i) if rev else i
            sl = pl.ds(pl.multiple_of(g * SUBLANES, SUBLANES), SUBLANES)
            av, uv = a_ref[sl, :], u_ref[sl, :]
            pa, h = _scan8(av, av * uv if adj else uv, rev)
            h = pa * carry + h
            hs = _shift8(h, carry, rev)
            if adj:
                lam = uv + hs
                outs[0][sl, :] = lam
                outs[1][sl, :] = lam * x_ref[sl, :]
            else:
                outs[0][sl, :] = h
                outs[1][sl, :] = hs
                if add is not None:
                    outs[3][sl, :] = h + x_ref[sl, :]
            last = 0 if rev else SUBLANES - 1
            return jnp.broadcast_to(h[last:last + 1, :], h.shape)

        st[...] = lax.fori_loop(0, ng, group, st[...])
        outs[2][...] = st[...]

    def tmap(j, t):
        return ((nt - 1 - t) if rev else t, j)

    blk = pl.BlockSpec((tc, cb), tmap)
    vec = pl.BlockSpec((SUBLANES, cb), lambda j, t: (0, j))
    args = [a, u, c0] + ([adjoint_of] if adj else []) + ([add] if add is not None else [])
    n_big = 2 if (adj or add is None) else 3
    out_specs = [blk, blk, vec] + ([blk] if n_big == 3 else [])
    out_shape = [jax.ShapeDtypeStruct((L, C), F32), jax.ShapeDtypeStruct((L, C), F32), jax.ShapeDtypeStruct((SUBLANES, C), F32)]
    out_shape += [jax.ShapeDtypeStruct((L, C), F32)] if n_big == 3 else []
    return pl.pallas_call(
        body, name=name, grid=(C // cb, nt),
        in_specs=[blk, blk, vec] + [blk] * (n_in - 3),
        out_specs=out_specs, out_shape=out_shape,
        scratch_shapes=[pltpu.VMEM((SUBLANES, cb), F32)],
        compiler_params=_cparams(("parallel", "arbitrary")),
    )(*args)


def f_silu(v):
    return (_silu(v),)


def f_add_bias(v, b):
    return (v + b,)


def f_mod_in(x, gain, shift, scale):
    return _rms(x, gain) * (1.0 + scale) + shift, x


def f_dt(raw, bias):
    return (_softplus(raw + bias),)


def f_gnorm(yf, yb, xs, z, dexp, gain):
    return (_rms((yf + yb + dexp * xs) * _silu(z), gain),)


def f_gelu_gate(r, yr):
    return (r * _gelu_tanh(yr),)


def f_merge(gs, gr, o_s, o_r, bs, br):
    return (_sigmoid(gs + bs) * o_s + _sigmoid(gr + br) * o_r,)


def f_res_mod(x, out, gm, gain, shift, scale):
    x1 = x + gm * out
    return x1, _rms(x1, gain) * (1.0 + scale) + shift


def f_final(x1, f, t, gf, gain):
    e = _rms(x1 + gf * f, gain) - t
    return (jnp.broadcast_to(0.5 * jnp.mean(e * e, axis=-1, keepdims=True), (e.shape[0], LANES)),)


GATE_ROWS = 1024
GATE_SUB_ROWS = 256


def _lru_coeffs(xh, wa, ba, wx, bx, lam):
    r = _sigmoid(bdot(xh, wa, "nn") + ba)
    i = _sigmoid(bdot(xh, wx, "nn") + bx)
    log_a = -LRU_C * r * _softplus(-lam)
    a = jnp.exp(log_a)
    one_minus_a2 = -jnp.tanh(log_a) * (a * a + 1.0)
    return a, jnp.sqrt(one_minus_a2) * (i * xh)


def f_gates(xh, wa_f, ba_f, wx_f, bx_f, lam_f, wa_b, ba_b, wx_b, bx_b, lam_b):
    return _lru_coeffs(xh, wa_f, ba_f, wx_f, bx_f, lam_f) + _lru_coeffs(xh, wa_b, ba_b, wx_b, bx_b, lam_b)


def f_adam(w, g, m, v):
    m2 = ADAM_B1 * m + (1.0 - ADAM_B1) * g
    v2 = ADAM_B2 * v + (1.0 - ADAM_B2) * (g * g)
    m_hat = m2 / (1.0 - ADAM_B1 ** ADAM_STEP)
    v_hat = v2 / (1.0 - ADAM_B2 ** ADAM_STEP)
    return -ADAM_LR * (m_hat / (jnp.sqrt(v_hat) + ADAM_EPS) + ADAM_WD * w), m2, v2


def swiglu(gu, dact, name, tb=128, sub=16):
    L, f2 = gu.shape
    f = f2 // 2
    tb = _pick(L, tb)

    def body(*refs):
        gu_ref, o_ref = refs[0], refs[-1]

        def step(i, carry):
            sl = pl.ds(pl.multiple_of(i * sub, sub), sub)
            g, u = gu_ref[sl, :f].astype(F32), gu_ref[sl, f:].astype(F32)
            if dact is None:
                o_ref[sl, :] = (_silu(g) * u).astype(o_ref.dtype)
            else:
                d = refs[1][sl, :].astype(F32)
                o_ref[sl, :f] = (d * u * _dsilu(g)).astype(o_ref.dtype)
                o_ref[sl, f:] = (d * _silu(g)).astype(o_ref.dtype)
            return carry

        lax.fori_loop(0, tb // sub, step, 0)

    wout = f if dact is None else f2
    ins = [gu] + ([] if dact is None else [dact])
    return pl.pallas_call(
        body, name=name, grid=(L // tb,),
        in_specs=[pl.BlockSpec((tb, a.shape[1]), lambda i: (i, 0)) for a in ins],
        out_specs=pl.BlockSpec((tb, wout), lambda i: (i, 0)),
        out_shape=jax.ShapeDtypeStruct((L, wout), BF16),
        compiler_params=_cparams(("parallel",)),
    )(*ins)


N_DEV = 8
N_CHIP = 4
PACK_W = 1024
_HBM = pl.BlockSpec(memory_space=pltpu.HBM)


def _me():
    return lax.axis_index("x"), lax.axis_index("y"), lax.axis_index("c")


def _flip(pos, k):
    x, y, c = pos
    return (1 - x if k & 4 else x, 1 - y if k & 2 else y, 1 - c if k & 1 else c)


def _dev_index(pos):
    return 4 * pos[0] + 2 * pos[1] + pos[2]


def _chip_index(pos):
    return 2 * pos[0] + pos[1]


def _rcopy(src, dst, sems, k, to):
    send_sems, recv_sems = sems
    return pltpu.make_async_remote_copy(src_ref=src, dst_ref=dst, send_sem=send_sems.at[k], recv_sem=recv_sems.at[k],
                                        device_id=to, device_id_type=MESH)


def fill_own_device_slot(stack, mine):
    return lax.dynamic_update_slice(stack, mine[None], (_dev_index(_me()),) + (0,) * mine.ndim)


COPY_PIECES = 4
ROW_TILE_16BIT = 16


def _pieces(rows, n=COPY_PIECES, align=ROW_TILE_16BIT):
    while n > 1 and rows % (n * align):
        n -= 1
    return [(q * (rows // n), rows // n) for q in range(n)]


class Exchange:
    def __init__(self, ins, outs, n_sems, start, finish):
        self.ins, self.outs, self.n_sems, self.start, self.finish = list(ins), list(outs), n_sems, start, finish

    def sem_shapes(self):
        return [pltpu.SemaphoreType.DMA((self.n_sems,)), pltpu.SemaphoreType.DMA((self.n_sems,))]


def run_exchange(ex, name):
    n_in, n_out = len(ex.ins), len(ex.outs)

    def body(*refs):
        ins, outs, sems = refs[:n_in], refs[n_in:n_in + n_out], tuple(refs[n_in + n_out:])
        ex.start(ins, outs, sems)
        ex.finish(ins, outs, sems)

    return pl.pallas_call(body, name=name, in_specs=[_HBM] * n_in, out_specs=[_HBM] * n_out, out_shape=ex.outs,
                          scratch_shapes=ex.sem_shapes())(*ex.ins)


def allgather_exchange(v):
    def plan(ins, outs):
        me = _me()
        for k in range(1, N_DEV):
            peer = _flip(me, k)
            yield outs[0].at[_dev_index(me)], outs[0].at[_dev_index(peer)], k - 1, peer

    def start(ins, outs, sems):
        for mine, _, k, peer in plan(ins, outs):
            _rcopy(ins[0], mine, sems, k, peer).start()

    def finish(ins, outs, sems):
        for _, theirs, k, peer in plan(ins, outs):
            _rcopy(theirs, theirs, sems, k, peer).wait_recv()
        for mine, _, k, peer in plan(ins, outs):
            _rcopy(ins[0], mine, sems, k, peer).wait_send()

    return Exchange([v], [jax.ShapeDtypeStruct((N_DEV,) + v.shape, v.dtype)], N_DEV - 1, start, finish)


def allgather8(v, name):
    return fill_own_device_slot(run_exchange(allgather_exchange(v), name)[0], v)


def gather_exchange(shards):
    cuts = [_pieces(s.shape[0] // 2) for s in shards]
    base = [0]
    for cu in cuts:
        base.append(base[-1] + 6 * len(cu))

    def plan(ins, outs):
        me = _me()
        chips = [_flip(me, 4), _flip(me, 2), _flip(me, 6)]
        for w, cu in enumerate(cuts):
            half, nq = shards[w].shape[0] // 2, len(cu)
            for q, (s, n) in enumerate(cu):
                mine = pl.ds(pl.multiple_of(me[2] * half + s, ROW_TILE_16BIT), n)
                sibs = pl.ds(pl.multiple_of((1 - me[2]) * half + s, ROW_TILE_16BIT), n)
                for j, p in enumerate(chips):
                    yield w, p, mine, sibs, base[w] + j * nq + q, base[w] + (3 + j) * nq + q

    def start(ins, outs, sems):
        me = _me()
        for w, p, mine, _, k_ici, _ in plan(ins, outs):
            _rcopy(ins[w].at[mine], outs[w].at[_chip_index(me), mine], sems, k_ici, p).start()

    def finish(ins, outs, sems):
        me = _me()
        sibling = _flip(me, 1)
        for w, p, mine, _, k_ici, k_fwd in plan(ins, outs):
            got = outs[w].at[_chip_index(p), mine]
            _rcopy(got, got, sems, k_ici, p).wait_recv()
            _rcopy(got, got, sems, k_fwd, sibling).start()
        for w, p, _, sibs, _, k_fwd in plan(ins, outs):
            got = outs[w].at[_chip_index(p), sibs]
            _rcopy(got, got, sems, k_fwd, sibling).wait_recv()
        for w, p, mine, _, k_ici, k_fwd in plan(ins, outs):
            got = outs[w].at[_chip_index(p), mine]
            _rcopy(ins[w].at[mine], got, sems, k_ici, p).wait_send()
            _rcopy(got, got, sems, k_fwd, sibling).wait_send()

    outs = [jax.ShapeDtypeStruct((N_CHIP,) + s.shape, s.dtype) for s in shards]
    return Exchange(shards, outs, base[-1], start, finish)


def fill_own_slot(stack, mine):
    return lax.dynamic_update_slice(stack, mine[None], (_chip_index(_me()), 0, 0))


def scatter_exchange(parts):
    def plan(ins, outs):
        me = _me()
        for w in range(len(parts)):
            half = parts[w].shape[1] // 2
            for k in range(1, N_DEV):
                to = _flip(me, k)
                src = ins[w].at[_chip_index(to), pl.ds(pl.multiple_of(to[2] * half, ROW_TILE_16BIT), half)]
                yield src, outs[w].at[k - 1], (N_DEV - 1) * w + k - 1, to

    def start(ins, outs, sems):
        for src, dst, k, to in plan(ins, outs):
            _rcopy(src, dst, sems, k, to).start()

    def finish(ins, outs, sems):
        for src, dst, k, to in plan(ins, outs):
            _rcopy(dst, dst, sems, k, to).wait_recv()
        for src, dst, k, to in plan(ins, outs):
            _rcopy(src, dst, sems, k, to).wait_send()

    outs = [jax.ShapeDtypeStruct((N_DEV - 1, p.shape[1] // 2, p.shape[2]), p.dtype) for p in parts]
    return Exchange(parts, outs, (N_DEV - 1) * len(parts), start, finish)


def own_piece(parts):
    me = _me()
    half = parts.shape[1] // 2
    return lax.dynamic_slice(parts, (_chip_index(me), me[2] * half, 0), (1, half, parts.shape[2]))[0]


def join_halves(mine, name):
    half, C = mine.shape
    cuts = _pieces(half, 2 * COPY_PIECES, SUBLANES)
    nq = len(cuts)

    def body(m_ref, out_ref, send_sems, recv_sems):
        me = _me()
        sems = (send_sems, recv_sems)
        sibling = _flip(me, 1)
        sends = [_rcopy(m_ref.at[pl.ds(s, n)], out_ref.at[pl.ds(s, n)], sems, q, sibling) for q, (s, n) in enumerate(cuts)]
        for cp in sends:
            cp.start()
        for q, (s, n) in enumerate(cuts):
            got = out_ref.at[pl.ds(s, n)]
            _rcopy(got, got, sems, q, sibling).wait_recv()
        for cp in sends:
            cp.wait_send()

    other = pl.pallas_call(
        body, name=name, in_specs=[_HBM], out_specs=_HBM,
        out_shape=jax.ShapeDtypeStruct((half, C), mine.dtype),
        scratch_shapes=[pltpu.SemaphoreType.DMA((nq,)), pltpu.SemaphoreType.DMA((nq,))],
    )(mine)
    south = _me()[2] == 0
    return jnp.concatenate([jnp.where(south, mine, other), jnp.where(south, other, mine)], axis=0)


SUM_BLOCK_ELEMS = 256 * 1024


def sum_slots(a, name, first=None):
    n, R, C = a.shape
    tb = _pick(R, max(ROW_TILE_16BIT, SUM_BLOCK_ELEMS // C), ROW_TILE_16BIT)

    def body(a_ref, *rest):
        o_ref = rest[-1]
        acc = a_ref[0].astype(F32) if first is None else rest[0][...].astype(F32) + a_ref[0].astype(F32)
        for j in range(1, n):
            acc = acc + a_ref[j].astype(F32)
        o_ref[...] = acc

    row = pl.BlockSpec((tb, C), lambda i: (i, 0))
    return pl.pallas_call(
        body, name=name, grid=(R // tb,),
        in_specs=[pl.BlockSpec((n, tb, C), lambda i: (0, i, 0))] + ([] if first is None else [row]),
        out_specs=row,
        out_shape=jax.ShapeDtypeStruct((R, C), F32),
        compiler_params=_cparams(("parallel",)),
    )(*([a] if first is None else [a, first]))


def _pack(arrays, dtype, row_align):
    flat = jnp.concatenate([a.astype(dtype).reshape(-1) for a in arrays])
    per = PACK_W * row_align
    n = -(-flat.shape[0] // per) * per
    return jnp.pad(flat, (0, n - flat.shape[0])).reshape(-1, PACK_W)


def _unpack(buf, shapes):
    flat = buf.reshape(-1)
    out, at = [], 0
    for s in shapes:
        n = 1
        for d in s:
            n *= d
        out.append(flat[at:at + n].reshape(s))
        at += n
    return out


def _to_col_major(t, rows):
    L, C = t.shape
    return t.reshape(rows, GRID_W, C).transpose(1, 0, 2).reshape(L, C)


def _to_row_major(t, rows):
    L, C = t.shape
    return t.reshape(GRID_W, rows, C).transpose(1, 0, 2).reshape(L, C)


def _heads_to_groups(v, hpg):
    lead = v.shape[:-1]
    t = jnp.moveaxis(v.reshape(lead + (2, SSD_GROUPS, hpg)), -3, -2).reshape(lead + (SSD_GROUPS, 2 * hpg))
    t = jnp.pad(t, [(0, 0)] * (len(lead) + 1) + [(0, LANES - 2 * hpg)])
    return t.reshape(lead + (SSD_GROUPS * LANES,))


def _groups_to_heads(t, hpg):
    lead = t.shape[:-1]
    t = t.reshape(lead + (SSD_GROUPS, LANES))[..., :2 * hpg].reshape(lead + (SSD_GROUPS, 2, hpg))
    return jnp.moveaxis(t, -2, -3).reshape(lead + (2 * SSD_GROUPS * hpg,))


def _r1(v):
    return v.reshape(1, -1)


class _Layout:
    def __init__(self, D, W, hpg):
        assert W == D and D % (SSD_GROUPS * LANES) == 0
        self.D, self.W, self.hpg = D, W, hpg
        self.GN = SSD_GROUPS * SSD_STATE
        self.xbc0 = 5 * D
        self.dt0 = 5 * D + W + 2 * self.GN
        self.width = self.dt0 + SSD_GROUPS * LANES
        self.xbc_blk = self.xbc0 // CONV_CB
        self.dt_blk = self.dt0 // (SSD_GROUPS * LANES)
        assert self.dt0 % (SSD_GROUPS * LANES) == 0


def _seq_forward(hb, w_all, sp, lay, tag, grid_rows, init, carry=None):
    D, W = lay.D, lay.W
    gw = SSD_GROUPS * LANES
    pg = matmul(hb, w_all, "nn", BF16, f"proj_{tag}", carry=carry)
    pg, carried = pg if carry else (pg, [])
    xbc = conv_fwd(pg, lay.xbc_blk, sp["ssd_conv_w"], sp["ssd_conv_b"], True, f"ssd_conv_{tag}", out_dtype=BF16)
    dtg = rowwise(f_dt, [Cols(pg, gw, lay.dt_blk)], [sp["dt_bias_g"]], [(gw, F32)], f"ssd_dt_{tag}")[0]
    yf, ent_f, fin_f = ssd_fwd(xbc, dtg, sp["a_g"], init["ssd_f"], False, W, f"ssd_scan_f_{tag}")
    yb, ent_b, fin_b = ssd_fwd(xbc, dtg, sp["a_g"], init["ssd_b"], True, W, f"ssd_scan_b_{tag}")
    xr = pg[:, D:2 * D]
    if grid_rows:
        xr = _to_col_major(xr, grid_rows)
    xc = conv_fwd(xr, 0, sp["lru_conv_w"], sp["lru_conv_b"], False, f"lru_conv_{tag}")
    a_f, u_f, a_b, u_b = rowwise(f_gates, [xc], sp["gate_pars"], [(LRU_HEAD, F32)] * 4, f"lru_gates_{tag}",
                                 tb=GATE_ROWS, sub=GATE_SUB_ROWS, ncol=D // LRU_HEAD)
    h_f, hs_f, fl_f = lru_scan(a_f, u_f, init["lru_f"], False, f"lru_scan_f_{tag}")
    _, hs_b, fl_b, r = lru_scan(a_b, u_b, init["lru_b"], True, f"lru_scan_b_{tag}", add=h_f)
    saved = dict(pg=pg, xbc=xbc, dtg=dtg, yf=yf, yb=yb, ent_f=ent_f, ent_b=ent_b, xr=xr, xc=xc,
                 a_f=a_f, a_b=a_b, hs_f=hs_f, hs_b=hs_b, r=r)
    return saved, dict(ssd_f=fin_f, ssd_b=fin_b, lru_f=fl_f, lru_b=fl_b), carried


def _seq_backward(sv, sp, lay, tag, grid_rows, dy, dxs_extra, d_r, dfin, carries=(None, None)):
    D, W, GN = lay.D, lay.W, lay.GN
    gw = SSD_GROUPS * LANES
    pg = sv["pg"]
    res_f = ssd_bwd(sv["xbc"], sv["dtg"], sp["a_g"], sv["ent_f"], dy, dfin["ssd_f"], False, W, f"ssd_scan_f_bwd_{tag}", carry=carries[0])
    res_b = ssd_bwd(sv["xbc"], sv["dtg"], sp["a_g"], sv["ent_b"], dy, dfin["ssd_b"], True, W, f"ssd_scan_b_bwd_{tag}", carry=carries[1])
    (dxf, ddt_f, db_f, dc_f, da_f, dh0_f), carried_f = res_f if carries[0] else (res_f, [])
    (dxb, ddt_b, db_b, dc_b, da_b, dh0_b), carried_b = res_b if carries[1] else (res_b, [])
    cw, cb = sp["ssd_conv_w"], sp["ssd_conv_b"]
    d_xs, dw1, db1 = conv_bwd(pg, lay.xbc_blk, cw[:, :W], cb[:, :W], [dxf, dxb] + dxs_extra, True, f"ssd_conv_x_bwd_{tag}")
    d_b, dw2, db2 = conv_bwd(pg, lay.xbc_blk + W // CONV_CB, cw[:, W:W + GN], cb[:, W:W + GN], [db_f, db_b], True, f"ssd_conv_b_bwd_{tag}")
    d_c, dw3, db3 = conv_bwd(pg, lay.xbc_blk + (W + GN) // CONV_CB, cw[:, W + GN:], cb[:, W + GN:], [dc_f, dc_b], True, f"ssd_conv_c_bwd_{tag}")
    d_dtraw, d_dtbias = rowwise_vjp(f_dt, [Cols(pg, gw, lay.dt_blk)], [sp["dt_bias_g"]], [[ddt_f, ddt_b]], [BF16], f"ssd_dt_bwd_{tag}")
    du_b, dab, dl0_b = lru_scan(sv["a_b"], d_r, dfin["lru_b"], False, f"lru_scan_b_bwd_{tag}", adjoint_of=sv["hs_b"])
    du_f, daf, dl0_f = lru_scan(sv["a_f"], d_r, dfin["lru_f"], True, f"lru_scan_f_bwd_{tag}", adjoint_of=sv["hs_f"])
    res = rowwise_vjp(f_gates, [sv["xc"]], sp["gate_pars"], [daf, du_f, dab, du_b], [F32], f"lru_gates_bwd_{tag}",
                      tb=GATE_ROWS, sub=GATE_SUB_ROWS, ncol=D // LRU_HEAD)
    d_xc, gate_grads = res[0], res[1:]
    d_xr, g_lcw, g_lcb = conv_bwd(sv["xr"], 0, sp["lru_conv_w"], sp["lru_conv_b"], [d_xc], False, f"lru_conv_bwd_{tag}")
    if grid_rows:
        d_xr = _to_row_major(d_xr, grid_rows)
    grads = dict(ssd_conv_w=jnp.concatenate([dw1, dw2, dw3], axis=1), ssd_conv_b=jnp.concatenate([db1, db2, db3], axis=1),
                 dt_bias_g=d_dtbias, a_g=da_f + da_b, lru_conv_w=g_lcw, lru_conv_b=g_lcb, gate_pars=list(gate_grads))
    pieces = dict(xr=d_xr, xs=d_xs, b=d_b, c=d_c, dt=d_dtraw)
    return pieces, grads, dict(ssd_f=dh0_f, ssd_b=dh0_b, lru_f=dl0_f, lru_b=dl0_b), (carried_f, carried_b)


def _proj_cotangent(lay, px, pc, Lc, dz, dyr, dgs, dgr):
    zero = jnp.zeros((Lc, lay.D), BF16)
    cols = [(dz, zero), (px["xr"], pc["xr"]), (dyr, zero), (dgs, zero), (dgr, zero),
            (px["xs"], pc["xs"]), (px["b"], pc["b"]), (px["c"], pc["c"]), (px["dt"], pc["dt"])]
    return jnp.concatenate([jnp.concatenate([a.astype(BF16), b.astype(BF16)], axis=0) for a, b in cols], axis=1)


EARLY = ["w_in", "w_gate"]
LATE = ["w_out_ssd", "w_out_lru", "w_o", "ffn_w13", "ffn_w2"]


def _local_step(x, ctx, target, mod_x, mod_c, wb, late, ws):
    L, D = x.shape
    Lc = ctx.shape[0]
    nh = ws["ssd_d"].shape[0]
    hpg = nh // SSD_GROUPS
    W = nh * HEAD_DIM
    lay = _Layout(D, W, hpg)
    GN = lay.GN
    grid_rows = L // GRID_W
    nlh = D // LRU_HEAD

    w_in = wb["w_in"]
    o_dt, o_xr = 2 * W + 2 * GN, 2 * W + 2 * GN + 2 * nh
    w_all = jnp.concatenate([w_in[:, :W], w_in[:, o_xr:o_xr + D], w_in[:, o_xr + D:], wb["w_gate"], w_in[:, W:o_dt],
                             _heads_to_groups(w_in[:, o_dt:o_xr], hpg)], axis=1)
    a_neg = -jnp.exp(ws["ssd_a_log"])
    a_g = jnp.pad(_heads_to_groups(a_neg.reshape(-1), hpg).reshape(SSD_GROUPS, 1, LANES), [(0, 0), (0, SUBLANES - 1), (0, 0)])
    gate_pars = []
    for d in range(2):
        gate_pars += [ws["lru_w_a"][d], ws["lru_b_a"][d].reshape(nlh, 1, LRU_HEAD), ws["lru_w_x"][d],
                      ws["lru_b_x"][d].reshape(nlh, 1, LRU_HEAD), ws["lru_lambda"][d].reshape(nlh, 1, LRU_HEAD)]
    sp = dict(ssd_conv_w=ws["ssd_conv_w"], ssd_conv_b=_r1(ws["ssd_conv_b"]), dt_bias_g=_r1(_heads_to_groups(ws["ssd_dt_bias"].reshape(-1), hpg)),
              a_g=a_g, lru_conv_w=ws["lru_conv_w"], lru_conv_b=_r1(ws["lru_conv_b"]), gate_pars=gate_pars)
    dexp = _r1(jnp.repeat(ws["ssd_d"], HEAD_DIM))
    norm_mix, norm_ffn, ssd_norm, final_norm = _r1(ws["norm_mix"]), _r1(ws["norm_ffn"]), _r1(ws["ssd_norm"]), _r1(ws["final_norm"])
    bg_s, bg_r = _r1(ws["b_gate"][:D]), _r1(ws["b_gate"][D:])
    sh_m, sc_m, g_m, sh_f, sc_f, g_f = [mod_x[:, k * D:(k + 1) * D] for k in range(6)]
    csh_m, csc_m = mod_c[:, :D], mod_c[:, D:2 * D]

    zst = dict(ssd_f=jnp.zeros((SSD_GROUPS, hpg * HEAD_DIM, SSD_STATE), F32), ssd_b=jnp.zeros((SSD_GROUPS, hpg * HEAD_DIM, SSD_STATE), F32),
               lru_f=jnp.zeros((SUBLANES, D), F32), lru_b=jnp.zeros((SUBLANES, D), F32))
    hcb = rowwise(f_mod_in, [ctx], [norm_mix, csh_m, csc_m], [(D, BF16)], "norm_mix_ctx")[0]
    hb = rowwise(f_mod_in, [x], [norm_mix, sh_m, sc_m], [(D, BF16)], "norm_mix_x")[0]
    svc, fin_c, _ = _seq_forward(hcb, w_all, sp, lay, "ctx", None, zst)
    svx, _, stacks = _seq_forward(hb, w_all, sp, lay, "x", grid_rows, fin_c, carry=gather_exchange([late[n] for n in LATE]))
    wb = dict(wb)
    for n, st in zip(LATE, stacks):
        wb[n] = _whole_from_shards(fill_own_slot(st, late[n]), n in BIG_BY_COLUMNS)
    pg = svx["pg"]
    r_rm = _to_row_major(svx["r"], grid_rows)
    gn_rows = [svx["yf"], svx["yb"], Cols(svx["xbc"], W, 0), Cols(pg, D, 0)]
    yn = rowwise(f_gnorm, gn_rows, [dexp, ssd_norm], [(W, BF16)], "ssd_gnorm")[0]
    o_s = matmul(yn, wb["w_out_ssd"], "nn", BF16, "out_ssd")
    o_in = rowwise(f_gelu_gate, [r_rm, Cols(pg, D, 2)], [], [(D, BF16)], "lru_gelu")[0]
    o_r = matmul(o_in, wb["w_out_lru"], "nn", BF16, "out_lru")
    mg_rows = [Cols(pg, D, 3), Cols(pg, D, 4), o_s, o_r]
    mixed = rowwise(f_merge, mg_rows, [bg_s, bg_r], [(D, BF16)], "merge")[0]
    out = matmul(mixed, wb["w_o"], "nn", BF16, "out_proj")
    x1, h2 = rowwise(f_res_mod, [x, out], [g_m, norm_ffn, sh_f, sc_f], [(D, F32), (D, BF16)], "res_norm_ffn")
    gu = matmul(h2, wb["ffn_w13"], "nn", BF16, "ffn_in")
    act = swiglu(gu, None, "ffn_act")
    f = matmul(act, wb["ffn_w2"], "nn", BF16, "ffn_out")

    ones = jnp.full((L, LANES), 1.0 / LANES, F32)
    dx1a, df, d_gf, d_fnorm, lsum = rowwise_vjp(f_final, [x1, f, target], [g_f, final_norm], [ones], [F32, BF16, None],
                                                "final_loss", out_sums=[(0, LANES)])
    loss = lsum[0, 0]
    d_act = matmul(df, wb["ffn_w2"], "nt", BF16, "ffn_out_dx")
    g_w2 = matmul(act, df, "tn", BF16, "ffn_out_dw")
    d_gu = swiglu(gu, d_act, "ffn_act_bwd")
    dh2 = matmul(d_gu, wb["ffn_w13"], "nt", BF16, "ffn_in_dx")
    g_w13 = matmul(h2, d_gu, "tn", BF16, "ffn_in_dw")
    dxa, d_out, d_gm, d_nffn, d_shf, d_scf = rowwise_vjp(f_res_mod, [x, out], [g_m, norm_ffn, sh_f, sc_f], [dx1a, dh2],
                                                         [F32, BF16], "res_norm_ffn_bwd")
    d_mixed = matmul(d_out, wb["w_o"], "nt", BF16, "out_proj_dx")
    g_wo = matmul(mixed, d_out, "tn", BF16, "out_proj_dw")
    dgs, dgr, do_s, do_r, d_bgs, d_bgr = rowwise_vjp(f_merge, mg_rows, [bg_s, bg_r], [d_mixed], [BF16] * 4, "merge_bwd")
    d_yn = matmul(do_s, wb["w_out_ssd"], "nt", BF16, "out_ssd_dx")
    g_wos = matmul(yn, do_s, "tn", BF16, "out_ssd_dw")
    d_oin = matmul(do_r, wb["w_out_lru"], "nt", BF16, "out_lru_dx")
    g_wol = matmul(o_in, do_r, "tn", BF16, "out_lru_dw")
    d_r_rm, d_yr = rowwise_vjp(f_gelu_gate, [r_rm, Cols(pg, D, 2)], [], [d_oin], [F32, BF16], "lru_gelu_bwd")
    dy, dxs_skip, dz, d_dexp, d_ssdn = rowwise_vjp(f_gnorm, gn_rows, [dexp, ssd_norm], [d_yn], [BF16, None, BF16, BF16], "ssd_gnorm_bwd")
    zfin = dict(zst)
    parts = {n: _shards_from_whole(g, n in BIG_BY_COLUMNS).astype(BF16)
             for n, g in (("ffn_w2", g_w2), ("ffn_w13", g_w13), ("w_o", g_wo), ("w_out_ssd", g_wos), ("w_out_lru", g_wol))}
    ride_f, ride_b = ["ffn_w2", "ffn_w13"], ["w_o", "w_out_ssd", "w_out_lru"]
    px, gx, dst, (land_f, land_b) = _seq_backward(
        svx, sp, lay, "x", grid_rows, dy, [dxs_skip], _to_col_major(d_r_rm, grid_rows), zfin,
        carries=(scatter_exchange([parts[n] for n in ride_f]), scatter_exchange([parts[n] for n in ride_b])))
    pc, gc, _, _ = _seq_backward(svc, sp, lay, "ctx", None, jnp.zeros((Lc, W), F32), [], jnp.zeros((Lc, D), F32), dst)
    dpg = _proj_cotangent(lay, px, pc, Lc, dz, d_yr, dgs, dgr)
    gg = [a + b for a, b in zip(gx["gate_pars"], gc["gate_pars"])]
    lru_w = jnp.concatenate([gg[0], gg[5], gg[2], gg[7]], axis=0).reshape(-1, PACK_W)
    g_wall, (lru_w_all,) = matmul(jnp.concatenate([hb, hcb], axis=0), dpg, "tn", BF16, "proj_dw", carry=allgather_exchange(lru_w))
    lru_w_all = fill_own_device_slot(lru_w_all, lru_w)
    xb0 = lay.xbc0
    g_w_in = jnp.concatenate([g_wall[:, :W], g_wall[:, xb0:xb0 + W + 2 * GN], _groups_to_heads(g_wall[:, lay.dt0:], hpg),
                              g_wall[:, D:2 * D], g_wall[:, 2 * D:3 * D]], axis=1)
    ride_p = ["w_in", "w_gate"]
    parts.update({n: _shards_from_whole(g, True).astype(BF16) for n, g in (("w_in", g_w_in), ("w_gate", g_wall[:, 3 * D:5 * D]))})
    dh, land_p = matmul(dpg, w_all, "nt", BF16, "proj_dx_x", a_rows=(0, L), carry=scatter_exchange([parts[n] for n in ride_p]))
    scattered = {n: (ld, own_piece(parts[n])) for n, ld in zip(ride_f + ride_b + ride_p, land_f + land_b + land_p)}
    dhc = matmul(dpg, w_all, "nt", BF16, "proj_dx_ctx", a_rows=(L, Lc))
    grad_x, d_nmix_x, d_shm, d_scm = rowwise_vjp(f_mod_in, [x], [norm_mix, sh_m, sc_m], [dh, dxa], [F32], "norm_mix_x_bwd")
    d_nmix_c, d_cshm, d_cscm = rowwise_vjp(f_mod_in, [ctx], [norm_mix, csh_m, csc_m], [dhc, jnp.zeros((Lc, D), F32)], [None], "norm_mix_ctx_bwd")

    dmod_x = jnp.concatenate([d_shm, d_scm, d_gm, d_shf, d_scf, d_gf], axis=1)
    dmod_c = jnp.concatenate([d_cshm, d_cscm, jnp.zeros((1, 4 * D), F32)], axis=1)

    d_a = _groups_to_heads((gx["a_g"] + gc["a_g"])[:, 0, :].reshape(-1), hpg).reshape(2, nh)

    def gate(k, shape):
        return jnp.stack([gg[k].reshape(shape), gg[5 + k].reshape(shape)], axis=0)

    grads = dict(
        norm_mix=(d_nmix_x + d_nmix_c).reshape(-1), norm_ffn=d_nffn.reshape(-1),
        ssd_conv_w=gx["ssd_conv_w"] + gc["ssd_conv_w"], ssd_conv_b=(gx["ssd_conv_b"] + gc["ssd_conv_b"]).reshape(-1),
        ssd_dt_bias=_groups_to_heads((gx["dt_bias_g"] + gc["dt_bias_g"]).reshape(-1), hpg).reshape(2, nh),
        ssd_a_log=d_a * a_neg, ssd_d=d_dexp.reshape(nh, HEAD_DIM).sum(axis=1), ssd_norm=d_ssdn.reshape(-1),
        lru_conv_w=gx["lru_conv_w"] + gc["lru_conv_w"], lru_conv_b=(gx["lru_conv_b"] + gc["lru_conv_b"]).reshape(-1),
        lru_b_a=gate(1, (D,)), lru_b_x=gate(3, (D,)), lru_lambda=gate(4, (D,)),
        b_gate=jnp.concatenate([d_bgs, d_bgr], axis=1).reshape(-1), final_norm=d_fnorm.reshape(-1))
    return loss, grad_x, grads, scattered, lru_w_all, dmod_x, dmod_c


WEIGHTS = ["c_ctx", "w_ada", "b_ada", "norm_mix", "norm_ffn", "w_in", "ssd_conv_w", "ssd_conv_b", "ssd_dt_bias", "ssd_a_log",
           "ssd_d", "ssd_norm", "w_out_ssd", "lru_conv_w", "lru_conv_b", "lru_w_a", "lru_b_a", "lru_w_x", "lru_b_x", "lru_lambda",
           "w_out_lru", "w_gate", "b_gate", "w_o", "ffn_w13", "ffn_w2", "final_norm"]
BIG = ["w_in", "w_out_ssd", "w_out_lru", "w_gate", "w_o", "ffn_w13", "ffn_w2"]
BIG_BY_COLUMNS = {"w_in", "w_gate", "ffn_w13"}
MOD_ROWS = 16
MOD_CTX_ROW = N_DEV
SMALL_SHARDED = ["ssd_conv_w", "lru_conv_w", "lru_b_a", "lru_b_x", "lru_lambda"]
ADAM_ROWS = 64


def _whole_from_shards(stack, by_columns):
    if by_columns:
        return stack.transpose(1, 0, 2).reshape(stack.shape[1], -1)
    return stack.reshape(-1, stack.shape[2])


def _shards_from_whole(g, by_columns):
    if by_columns:
        return g.reshape(g.shape[0], N_CHIP, -1).transpose(1, 0, 2)
    return g.reshape(N_CHIP, -1, g.shape[1])


def _adam(w, g, m, v, name):
    shape = w.shape
    cols = shape[-1]
    w2, g2, m2, v2 = [t.reshape(-1, cols) for t in (w, g, m, v)]
    d, nm, nv = rowwise(f_adam, [w2, g2, m2, v2], [], [(cols, F32)] * 3, name, tb=ADAM_ROWS, sub=8)
    return d.reshape(shape), nm.reshape(shape), nv.reshape(shape)


def kernel(x, c, ctx, c_ctx, w_ada, b_ada, norm_mix, norm_ffn, w_in, ssd_conv_w, ssd_conv_b, ssd_dt_bias, ssd_a_log, ssd_d, ssd_norm, w_out_ssd, lru_conv_w, lru_conv_b, lru_w_a, lru_b_a, lru_w_x, lru_b_x, lru_lambda, w_out_lru, w_gate, b_gate, w_o, ffn_w13, ffn_w2, final_norm, loss_target, m_c_ctx, m_w_ada, m_b_ada, m_norm_mix, m_norm_ffn, m_w_in, m_ssd_conv_w, m_ssd_conv_b, m_ssd_dt_bias, m_ssd_a_log, m_ssd_d, m_ssd_norm, m_w_out_ssd, m_lru_conv_w, m_lru_conv_b, m_lru_w_a, m_lru_b_a, m_lru_w_x, m_lru_b_x, m_lru_lambda, m_w_out_lru, m_w_gate, m_b_gate, m_w_o, m_ffn_w13, m_ffn_w2, m_final_norm, v_c_ctx, v_w_ada, v_b_ada, v_norm_mix, v_norm_ffn, v_w_in, v_ssd_conv_w, v_ssd_conv_b, v_ssd_dt_bias, v_ssd_a_log, v_ssd_d, v_ssd_norm, v_w_out_ssd, v_lru_conv_w, v_lru_conv_b, v_lru_w_a, v_lru_b_a, v_lru_w_x, v_lru_b_x, v_lru_lambda, v_w_out_lru, v_w_gate, v_b_gate, v_w_o, v_ffn_w13, v_ffn_w2, v_final_norm):
    given = dict(locals())
    layered = {n for n in WEIGHTS if n not in ("c_ctx", "final_norm")}
    w_blk = {n: (given[n][0] if n in layered else given[n]) for n in WEIGHTS}
    me = _me()
    chip, dev = _chip_index(me), _dev_index(me)
    D = c.shape[1]

    ada_w = w_blk["w_ada"].astype(BF16)
    ada_n = ada_w.shape[1]
    c_all = allgather8(jnp.concatenate([c, jnp.zeros((SUBLANES - 1, D), F32)], axis=0), "gather_c")[:, 0, :]
    cc = jnp.concatenate([c_all, c_ctx[None], jnp.zeros((MOD_ROWS - N_DEV - 1, D), F32)], axis=0)
    s_cc = rowwise(f_silu, [cc], [], [(D, BF16)], "mod_silu", tb=MOD_ROWS)[0]
    ada_b = lax.dynamic_slice_in_dim(w_blk["b_ada"], chip * ada_n, ada_n)[None]
    mod_blk = rowwise(f_add_bias, [matmul(s_cc, ada_w, "nn", F32, "mod_proj")], [ada_b], [(ada_n, F32)], "mod_bias", tb=MOD_ROWS)[0]
    mod = allgather8(mod_blk, "gather_mod")[::2].transpose(1, 0, 2).reshape(MOD_ROWS, N_CHIP * ada_n)
    mod_x = lax.dynamic_slice_in_dim(mod, dev, 1, axis=0)
    mod_c = mod[MOD_CTX_ROW:MOD_CTX_ROW + 1]

    shard16 = {n: w_blk[n].astype(BF16) for n in BIG}
    stacks = run_exchange(gather_exchange([shard16[n] for n in EARLY]), "gather_early")
    wb = {n: _whole_from_shards(fill_own_slot(st, shard16[n]), n in BIG_BY_COLUMNS) for n, st in zip(EARLY, stacks)}

    sm_shapes = [w_blk[n].shape for n in SMALL_SHARDED]
    sm_all = allgather8(_pack([w_blk[n] for n in SMALL_SHARDED], F32, ROW_TILE_16BIT), "gather_small")[::2]
    ws = {n: w_blk[n] for n in WEIGHTS if n not in BIG and n != "w_ada"}
    for n, s in zip(SMALL_SHARDED, _unpack_stacked(sm_all, sm_shapes)):
        ws[n] = jnp.concatenate([s[k] for k in range(N_CHIP)], axis=-1)

    loss, grad_x, grads, scattered, lru_w_all, dmod_x, dmod_c = _local_step(x[0], ctx[0], loss_target[0], mod_x, mod_c, wb,
                                                                            {n: shard16[n] for n in LATE}, ws)
    loss = lax.psum(loss, ("x", "y", "c"))

    dm = allgather8(jnp.concatenate([dmod_x, dmod_c, jnp.zeros((SUBLANES - 2, 6 * D), F32)], axis=0), "gather_dmod")
    dm_ctx = sum_slots(dm, "sum_dmod")[1:2]
    dmod = jnp.concatenate([dm[:, 0, :], dm_ctx, jnp.zeros((MOD_ROWS - N_DEV - 1, 6 * D), F32)], axis=0)
    g_shard = {"b_ada": rowwise_vjp(f_add_bias, [dmod], [w_blk["b_ada"][None]], [dmod], [None], "mod_bias_bwd", tb=MOD_ROWS)[0].reshape(-1)}
    d_blk = lax.dynamic_slice_in_dim(dmod, chip * ada_n, ada_n, axis=1).astype(BF16)
    g_shard["w_ada"] = matmul(s_cc, d_blk, "tn", F32, "mod_proj_dw")
    d_silu_ctx = matmul(d_blk, ada_w, "nt", F32, "mod_proj_dx")[MOD_CTX_ROW]
    grads["c_ctx_silu"] = 0.5 * d_silu_ctx

    for n in BIG:
        landed, own = scattered[n]
        g_shard[n] = join_halves(sum_slots(landed, "sum_" + n, first=own), "join_" + n)

    small = [n for n in WEIGHTS if n not in BIG and n != "w_ada"]
    lru_w = sum_slots(lru_w_all, "sum_lru_w").reshape((4,) + w_blk["lru_w_a"].shape[1:])
    g_shard["lru_w_a"], g_shard["lru_w_x"] = lru_w[:2], lru_w[2:]
    reduced = [n for n in small if n not in ("b_ada", "c_ctx", "lru_w_a", "lru_w_x")] + ["c_ctx_silu"]
    sm_g = sum_slots(allgather8(_pack([grads[n] for n in reduced], F32, ROW_TILE_16BIT), "gather_small_grads"), "sum_small_grads")
    for n, g in zip(reduced, _unpack(sm_g, [grads[n].shape for n in reduced])):
        if n in SMALL_SHARDED:
            per = g.shape[-1] // N_CHIP
            g = lax.dynamic_slice_in_dim(g, chip * per, per, axis=g.ndim - 1)
        g_shard[n] = g
    ctx_rows = jnp.concatenate([c_ctx[None], jnp.zeros((SUBLANES - 1, D), F32)], axis=0)
    ctx_cot = jnp.concatenate([g_shard.pop("c_ctx_silu")[None], jnp.zeros((SUBLANES - 1, D), F32)], axis=0)
    g_shard["c_ctx"] = rowwise_vjp(f_silu, [ctx_rows], [], [ctx_cot], [F32], "mod_silu_bwd", tb=SUBLANES, sub=SUBLANES)[0][0]

    out_g, out_d, out_m, out_v = {}, {}, {}, {}
    for n in BIG + ["w_ada"]:
        out_g[n] = g_shard[n]
        out_d[n], out_m[n], out_v[n] = _adam(w_blk[n], g_shard[n], given["m_" + n][0], given["v_" + n][0], "adamw_" + n)
    sm_blk_shapes = [w_blk[n].shape for n in small]
    packed = [_pack([t[n].reshape(w_blk[n].shape) for n in small], F32, ADAM_ROWS)
              for t in (w_blk, g_shard, {n: given["m_" + n] for n in small}, {n: given["v_" + n] for n in small})]
    res = _adam(*packed, "adamw_small")
    for tgt, buf in zip((out_d, out_m, out_v), res):
        tgt.update(zip(small, _unpack(buf, sm_blk_shapes)))
    for n in small:
        out_g[n] = g_shard[n].reshape(w_blk[n].shape)

    def full(n, t):
        return t.reshape(given[n].shape)

    return (loss, grad_x[None], *[full(n, out_g[n]) for n in WEIGHTS], *[full(n, out_d[n]) for n in WEIGHTS],
            *[full(n, out_m[n]) for n in WEIGHTS], *[full(n, out_v[n]) for n in WEIGHTS])


def _unpack_stacked(buf, shapes):
    k = buf.shape[0]
    flat = buf.reshape(k, -1)
    out, at = [], 0
    for s in shapes:
        n = 1
        for d in s:
            n *= d
        out.append(flat[:, at:at + n].reshape((k,) + tuple(s)))
        at += n
    return out
```
